```python
import math
import jax, jax.numpy as jnp
from jax import lax
import numpy as np

D_MODEL = 1024
BATCH = 4
SEQ = 8192
DEPTH = 4

GRID_W = 64
CTX_LEN = 256
N_MIXERS = 3
D_FF = 2816
N_MOD = 9
EPS = 1e-6

SSD_D_INNER = 2 * D_MODEL
SSD_HEAD_DIM = 64
SSD_N_HEADS = SSD_D_INNER // SSD_HEAD_DIM
SSD_N_GROUPS = 4
SSD_D_STATE = 128
SSD_CONV_W = 5
SSD_CHUNK = 128
SSD_GN = SSD_N_GROUPS * SSD_D_STATE
SSD_CONV_DIM = SSD_D_INNER + 2 * SSD_GN
SSD_IN_DIM = SSD_D_INNER + SSD_CONV_DIM + 2 * SSD_N_HEADS

POOL_WINDOWS = (2, 4, 8, 16)
POOL_GROUPS = 4
POOL_GROUP_DIM = D_MODEL // POOL_GROUPS

HY_ORDER = 2
HY_SHORT_W = 3
HY_EMB_DIM = 33
HY_BANDS = (HY_EMB_DIM - 1) // 2
HY_FILTER_HIDDEN = 64
HY_MAX_DECAY = math.log(1e-2) / 0.3
HY_MIN_DECAY = math.log(1e-2) / 1.5

kernel_name = "hybrid_ssd_pool_hyena_prefix_dit"


def rmsnorm(x, g):
    xf = x.astype(jnp.float32)
    y = xf * lax.rsqrt(jnp.mean(xf * xf, axis=-1, keepdims=True) + EPS)
    return (y * g.astype(jnp.float32)).astype(x.dtype)


def modulate(x, g, shift, scale):
    return rmsnorm(x, g) * (1 + scale) + shift


def swiglu(u, wg, wu, wd):
    return (jax.nn.silu(u @ wg) * (u @ wu)) @ wd


def dwconv_centred(u, w, b):
    K = w.shape[0]
    P = K // 2
    L = u.shape[1]
    up = jnp.pad(u, ((0, 0), (P, P), (0, 0)))
    out = b
    for k in range(K):
        out = out + w[k] * up[:, k:k + L]
    return out


def ssd_scan(x, dt, A, B, C, h0, with_output):
    b, l, H, P = x.shape
    G, N = B.shape[2], B.shape[3]
    E = H // G
    Q = SSD_CHUNK
    nc = l // Q
    xdt = (x.astype(jnp.float32) * dt[..., None]).reshape(b, nc, Q, G, E, P)
    Bc = B.astype(jnp.float32).reshape(b, nc, Q, G, N)
    Cc = C.astype(jnp.float32).reshape(b, nc, Q, G, N)
    a_cum = jnp.cumsum((dt * A).reshape(b, nc, Q, G, E), axis=2)
    a_last = a_cum[:, :, -1]
    decay_to_end = jnp.exp(a_last[:, :, None] - a_cum)
    states = jnp.einsum('bcsgn,bcsge,bcsgep->bcgepn', Bc, decay_to_end, xdt)

    def step(h, inp):
        st, al = inp
        return h * jnp.exp(al)[..., None, None] + st, h

    h_final, h_in = lax.scan(step, h0, (jnp.moveaxis(states, 1, 0), jnp.moveaxis(a_last, 1, 0)))
    if not with_output:
        return None, h_final
    h_in = jnp.moveaxis(h_in, 0, 1)
    seg = a_cum[:, :, :, None] - a_cum[:, :, None, :]
    mask = jnp.tril(jnp.ones((Q, Q), dtype=bool))[:, :, None, None]
    Lmat = jnp.exp(jnp.where(mask, seg, -jnp.inf))
    cb = jnp.einsum('bclgn,bcsgn->bclsg', Cc, Bc)
    y_diag = jnp.einsum('bclsg,bclsge,bcsgep->bclgep', cb, Lmat, xdt)
    y_off = jnp.einsum('bclgn,bcgepn,bclge->bclgep', Cc, h_in, jnp.exp(a_cum))
    y = (y_diag + y_off).reshape(b, l, H, P)
    return y.astype(x.dtype), h_final


def ssd_mixer(u_lat, u_ctx, w_in, conv_w, conv_b, a_log, dt_bias, d_skip, norm_g, w_out, need_ctx_out):
    H, P, G, N = SSD_N_HEADS, SSD_HEAD_DIM, SSD_N_GROUPS, SSD_D_STATE
    A = -jnp.exp(a_log.astype(jnp.float32))

    def project(u):
        b, l, _ = u.shape
        zxbcdt = u @ w_in
        z = zxbcdt[..., :SSD_D_INNER]
        xbc = jax.nn.silu(dwconv_centred(zxbcdt[..., SSD_D_INNER:SSD_D_INNER + SSD_CONV_DIM], conv_w, conv_b))
        dt_raw = zxbcdt[..., SSD_D_INNER + SSD_CONV_DIM:]
        xs = xbc[..., :SSD_D_INNER].reshape(b, l, H, P)
        Bm = xbc[..., SSD_D_INNER:SSD_D_INNER + SSD_GN].reshape(b, l, G, N)
        Cm = xbc[..., SSD_D_INNER + SSD_GN:].reshape(b, l, G, N)
        dt = jax.nn.softplus(dt_raw.reshape(b, l, 2, H).astype(jnp.float32) + dt_bias.astype(jnp.float32))
        return z, xs, Bm, Cm, dt

    def flip(t, d):
        return t[:, ::-1] if d == 1 else t

    zc, xc, Bc, Cc, dtc = project(u_ctx)
    zl, xl, Bl, Cl, dtl = project(u_lat)
    b = u_lat.shape[0]
    h0 = jnp.zeros((b, G, H // G, P, N), jnp.float32)
    y_lat = d_skip[:, None] * xl
    y_ctx = d_skip[:, None] * xc if need_ctx_out else None
    for d in range(2):
        yc, hc = ssd_scan(flip(xc, d), flip(dtc[:, :, d], d), A[d], flip(Bc, d), flip(Cc, d), h0, need_ctx_out)
        yl, _ = ssd_scan(flip(xl, d), flip(dtl[:, :, d], d), A[d], flip(Bl, d), flip(Cl, d), hc, True)
        y_lat = y_lat + flip(yl, d)
        if need_ctx_out:
            y_ctx = y_ctx + flip(yc, d)

    def gate_norm_out(y, z):
        b_, l_ = y.shape[:2]
        gy = (y.reshape(b_, l_, SSD_D_INNER) * jax.nn.silu(z)).reshape(b_, l_, G, SSD_D_INNER // G)
        gy = rmsnorm(gy, norm_g.reshape(G, SSD_D_INNER // G))
        return gy.reshape(b_, l_, SSD_D_INNER) @ w_out

    out_lat = gate_norm_out(y_lat, zl)
    out_ctx = gate_norm_out(y_ctx, zc) if need_ctx_out else None
    return out_lat, out_ctx


def box_sum(v, w, axis):
    L = v.shape[axis]
    cs = jnp.cumsum(v, axis=axis)
    pad = [(0, 0)] * v.ndim
    pad[axis] = (1, 0)
    cs = jnp.pad(cs, pad)
    pos = jnp.arange(L)
    hi = jnp.minimum(pos + (w - w // 2), L)
    lo = jnp.maximum(pos - w // 2, 0)
    s = jnp.take(cs, hi, axis=axis) - jnp.take(cs, lo, axis=axis)
    return s, (hi - lo).astype(jnp.float32)


def pool_grid(v, w):
    b, l, C = v.shape
    rows = l // GRID_W
    v4 = v.astype(jnp.float32).reshape(b, rows, GRID_W, C)
    s, cnt_c = box_sum(v4, w, 2)
    s, cnt_r = box_sum(s, w, 1)
    mean = s / (cnt_r[:, None, None] * cnt_c[None, :, None])
    return mean.reshape(b, l, C).astype(v.dtype) - v


def pool_seq(v, w):
    s, cnt = box_sum(v.astype(jnp.float32), w, 1)
    return (s / cnt[None, :, None]).astype(v.dtype) - v


def pool_mixer(u_lat, u_ctx, w, bias, scale, need_ctx_out):
    def mix(u, pool_fn):
        b, l, _ = u.shape
        grp = u.reshape(b, l, POOL_GROUPS, POOL_GROUP_DIM)
        pooled = jnp.stack([pool_fn(grp[:, :, g], POOL_WINDOWS[g]) for g in range(POOL_GROUPS)], axis=2)
        y = jnp.einsum('blgc,gcd->blgd', pooled, w) + bias
        return y.reshape(b, l, D_MODEL) * scale

    out_lat = mix(u_lat, pool_grid)
    out_ctx = mix(u_ctx, pool_seq) if need_ctx_out else None
    return out_lat, out_ctx


def hyena_filter_fft(L, fw1, fb1, fw2, fb2, fw3, ffreq):
    f32 = jnp.float32
    pos = jnp.arange(L, dtype=f32)
    t = jnp.linspace(0.0, 1.0, L, dtype=f32)
    wpos = 2.0 * math.pi * pos / L
    f = jnp.linspace(1e-4, HY_BANDS - 1, HY_BANDS, dtype=f32)
    ang = wpos[:, None] * f[None, :]
    z = jnp.concatenate([t[:, None], jnp.cos(ang), -jnp.sin(ang)], axis=-1)
    h = jnp.sin(ffreq[0].astype(f32) * (z @ fw1.astype(f32) + fb1.astype(f32)))
    h = jnp.sin(ffreq[1].astype(f32) * (h @ fw2.astype(f32) + fb2.astype(f32)))
    h = (h @ fw3.astype(f32)).reshape(L, HY_ORDER, 2, D_MODEL)
    deltas = jnp.abs(jnp.linspace(HY_MIN_DECAY, HY_MAX_DECAY, D_MODEL, dtype=f32))
    h = h * jnp.exp(-t[:, None, None, None] * deltas)
    h = h * lax.rsqrt(jnp.sum(h * h, axis=(0, 2), keepdims=True) + EPS)
    kern = jnp.concatenate([h[:, :, 0], jnp.zeros((1, HY_ORDER, D_MODEL), f32), h[:0:-1, :, 1]], axis=0)
    return jnp.fft.rfft(kern, axis=0)


def fftconv(u, kf, bias):
    L = u.shape[1]
    uf = u.astype(jnp.float32)
    y = jnp.fft.irfft(jnp.fft.rfft(uf, n=2 * L, axis=1) * kf, n=2 * L, axis=1)[:, :L]
    return (y + uf * bias.astype(jnp.float32)).astype(u.dtype)


def hyena_mixer(u_lat, u_ctx, w_in, conv_w, conv_b, fw1, fb1, fw2, fb2, fw3, ffreq, hbias, w_out, need_ctx_out):
    def run(u):
        kf = hyena_filter_fft(u.shape[1], fw1, fb1, fw2, fb2, fw3, ffreq)
        p = dwconv_centred(u @ w_in, conv_w, conv_b)
        v, x1, x2 = jnp.split(p, 3, axis=-1)
        z = x1 * fftconv(v, kf[:, 0], hbias[0])
        y = x2 * fftconv(z, kf[:, 1], hbias[1])
        return y @ w_out

    out_lat = run(u_lat)
    out_ctx = run(u_ctx) if need_ctx_out else None
    return out_lat, out_ctx


def n_layers_of(kind):
    return len(range(kind, DEPTH, N_MIXERS))


def setup_inputs(seed: int = 0) -> dict:
    key = jax.random.key(seed)
    ks = jax.random.split(key, 32)
    f32 = jnp.float32

    def nrm(k, shape, scale):
        return jax.random.normal(k, shape, f32) * scale

    nA, nB, nC = n_layers_of(0), n_layers_of(1), n_layers_of(2)
    D, F = D_MODEL, D_FF
    H = SSD_N_HEADS
    dt0 = jnp.exp(jax.random.uniform(ks[12], (nA, 2, H), f32, math.log(1e-3), math.log(1e-1)))
    return {
        "x": nrm(ks[0], (BATCH, SEQ, D), 1.0),
        "c": nrm(ks[1], (BATCH, D), 1.0),
        "ctx": nrm(ks[2], (BATCH, CTX_LEN, D), 1.0),
        "c_ctx": nrm(ks[3], (D,), 1.0),
        "ada_w": nrm(ks[4], (DEPTH, D, N_MOD * D), 0.5 * D ** -0.5),
        "ada_b": nrm(ks[5], (DEPTH, N_MOD * D), 0.02),
        "norm_g": 1.0 + nrm(ks[6], (DEPTH, 3, D), 0.05),
        "ffn_w_gate": nrm(ks[7], (DEPTH, 2, D, F), D ** -0.5),
        "ffn_w_up": nrm(ks[8], (DEPTH, 2, D, F), D ** -0.5),
        "ffn_w_down": nrm(ks[9], (DEPTH, 2, F, D), F ** -0.5),
        "ssd_w_in": nrm(ks[10], (nA, D, SSD_IN_DIM), D ** -0.5),
        "ssd_conv_w": nrm(ks[11], (nA, SSD_CONV_W, SSD_CONV_DIM), SSD_CONV_W ** -0.5),
        "ssd_conv_b": nrm(ks[13], (nA, SSD_CONV_DIM), 0.02),
        "ssd_a_log": jnp.log(jax.random.uniform(ks[14], (nA, 2, H), f32, 1.0, 16.0)),
        "ssd_dt_bias": dt0 + jnp.log(-jnp.expm1(-dt0)),
        "ssd_d": 1.0 + nrm(ks[15], (nA, H), 0.1),
        "ssd_norm_g": 1.0 + nrm(ks[16], (nA, SSD_D_INNER), 0.05),
        "ssd_w_out": nrm(ks[17], (nA, SSD_D_INNER, D), SSD_D_INNER ** -0.5),
        "pool_w": nrm(ks[18], (nB, POOL_GROUPS, POOL_GROUP_DIM, POOL_GROUP_DIM), POOL_GROUP_DIM ** -0.5),
        "pool_b": nrm(ks[19], (nB, POOL_GROUPS, POOL_GROUP_DIM), 0.02),
        "pool_scale": 1.0 + nrm(ks[20], (nB, D), 0.1),
        "hy_w_in": nrm(ks[21], (nC, D, 3 * D), D ** -0.5),
        "hy_conv_w": nrm(ks[22], (nC, HY_SHORT_W, 3 * D), HY_SHORT_W ** -0.5),
        "hy_conv_b": nrm(ks[23], (nC, 3 * D), 0.02),
        "hy_filt_w1": nrm(ks[24], (nC, HY_EMB_DIM, HY_FILTER_HIDDEN), HY_EMB_DIM ** -0.5),
        "hy_filt_b1": nrm(ks[25], (nC, HY_FILTER_HIDDEN), 0.1),
        "hy_filt_w2": nrm(ks[26], (nC, HY_FILTER_HIDDEN, HY_FILTER_HIDDEN), HY_FILTER_HIDDEN ** -0.5),
        "hy_filt_b2": nrm(ks[27], (nC, HY_FILTER_HIDDEN), 0.1),
        "hy_filt_w3": nrm(ks[28], (nC, HY_FILTER_HIDDEN, HY_ORDER * 2 * D), HY_FILTER_HIDDEN ** -0.5),
        "hy_filt_freq": 1.0 + nrm(ks[29], (nC, 2, HY_FILTER_HIDDEN), 0.1),
        "hy_bias": nrm(ks[30], (nC, HY_ORDER, D), 0.1),
        "hy_w_out": nrm(ks[31], (nC, D, D), D ** -0.5),
        "final_g": 1.0 + nrm(jax.random.fold_in(key, 99), (D,), 0.05),
    }


def reference(x, c, ctx, c_ctx, ada_w, ada_b, norm_g, ffn_w_gate, ffn_w_up, ffn_w_down,
              ssd_w_in, ssd_conv_w, ssd_conv_b, ssd_a_log, ssd_dt_bias, ssd_d, ssd_norm_g, ssd_w_out,
              pool_w, pool_b, pool_scale,
              hy_w_in, hy_conv_w, hy_conv_b, hy_filt_w1, hy_filt_b1, hy_filt_w2, hy_filt_b2, hy_filt_w3,
              hy_filt_freq, hy_bias, hy_w_out, final_g):
    h = ctx
    sc = jax.nn.silu(c)
    scc = jax.nn.silu(c_ctx)
    for i in range(DEPTH):
        kind = i % N_MIXERS
        j = i // N_MIXERS
        last = i == DEPTH - 1
        ctx_in_needed = (not last) or kind == 0
        m = (sc @ ada_w[i] + ada_b[i]).reshape(-1, N_MOD, D_MODEL)[:, :, None, :]
        mc = (scc @ ada_w[i] + ada_b[i]).reshape(N_MOD, D_MODEL)[:, None, None, :]

        x = x + 0.5 * m[:, 2] * swiglu(modulate(x, norm_g[i, 0], m[:, 0], m[:, 1]),
                                       ffn_w_gate[i, 0], ffn_w_up[i, 0], ffn_w_down[i, 0])
        if ctx_in_needed:
            h = h + 0.5 * mc[2] * swiglu(modulate(h, norm_g[i, 0], mc[0], mc[1]),
                                         ffn_w_gate[i, 0], ffn_w_up[i, 0], ffn_w_down[i, 0])

        u = modulate(x, norm_g[i, 1], m[:, 3], m[:, 4])
        uc = modulate(h, norm_g[i, 1], mc[3], mc[4]) if ctx_in_needed else None
        if kind == 0:
            y, yc = ssd_mixer(u, uc, ssd_w_in[j], ssd_conv_w[j], ssd_conv_b[j], ssd_a_log[j], ssd_dt_bias[j],
                              ssd_d[j], ssd_norm_g[j], ssd_w_out[j], not last)
        elif kind == 1:
            y, yc = pool_mixer(u, uc, pool_w[j], pool_b[j], pool_scale[j], not last)
        else:
            y, yc = hyena_mixer(u, uc, hy_w_in[j], hy_conv_w[j], hy_conv_b[j], hy_filt_w1[j], hy_filt_b1[j],
                                hy_filt_w2[j], hy_filt_b2[j], hy_filt_w3[j], hy_filt_freq[j], hy_bias[j],
                                hy_w_out[j], not last)
        x = x + m[:, 5] * y
        if not last:
            h = h + mc[5] * yc

        x = x + 0.5 * m[:, 8] * swiglu(modulate(x, norm_g[i, 2], m[:, 6], m[:, 7]),
                                       ffn_w_gate[i, 1], ffn_w_up[i, 1], ffn_w_down[i, 1])
        if not last:
            h = h + 0.5 * mc[8] * swiglu(modulate(h, norm_g[i, 2], mc[6], mc[7]),
                                         ffn_w_gate[i, 1], ffn_w_up[i, 1], ffn_w_down[i, 1])
    return rmsnorm(x, final_g)
```

```python
import functools
import math

import jax
import jax.numpy as jnp
from jax import lax
from jax.experimental import pallas as pl
from jax.experimental.pallas import tpu as pltpu

F32 = jnp.float32
BF16 = jnp.bfloat16

D_MODEL = 1024
DEPTH = 4
GRID_W = 64
N_MIXERS = 3
D_FF = 2816
N_MOD = 9
EPS = 1e-6

SSD_D_INNER = 2 * D_MODEL
SSD_HEAD_DIM = 64
SSD_N_HEADS = SSD_D_INNER // SSD_HEAD_DIM
SSD_N_GROUPS = 4
SSD_D_STATE = 128
SSD_CHUNK = 128
SSD_GN = SSD_N_GROUPS * SSD_D_STATE
SSD_CONV_DIM = SSD_D_INNER + 2 * SSD_GN

POOL_WINDOWS = (2, 4, 8, 16)
POOL_GROUPS = 4
POOL_GROUP_DIM = D_MODEL // POOL_GROUPS

HY_ORDER = 2
HY_EMB_DIM = 33
HY_BANDS = (HY_EMB_DIM - 1) // 2
HY_MAX_DECAY = math.log(1e-2) / 0.3
HY_MIN_DECAY = math.log(1e-2) / 1.5

VMEM_LIMIT_BYTES = 56 * 1024 * 1024


def _params(sem):
    return pltpu.CompilerParams(dimension_semantics=sem, vmem_limit_bytes=VMEM_LIMIT_BYTES)


def _resident(shape):
    nd = len(shape)
    return pl.BlockSpec(shape, lambda *_: (0,) * nd, pipeline_mode=pl.Buffered(1))


def _modulated(x, g, shift, scale):
    ms = jnp.mean(x * x, axis=-1, keepdims=True)
    return (x * lax.rsqrt(ms + EPS)) * g * (1.0 + scale) + shift


def _ada_kernel(s_ref, w_ref, b_ref, o_ref):
    o_ref[0] = jnp.dot(s_ref[...], w_ref[0], preferred_element_type=F32,
                       precision=lax.Precision.HIGHEST) + b_ref[0]


def ada_modulation(s, ada_w, ada_b, tn=1024):
    depth, d, n = ada_w.shape
    r = s.shape[0]
    return pl.pallas_call(
        _ada_kernel,
        grid=(depth, n // tn),
        in_specs=[pl.BlockSpec((r, d), lambda i, j: (0, 0)),
                  pl.BlockSpec((1, d, tn), lambda i, j: (i, 0, j)),
                  pl.BlockSpec((1, 1, tn), lambda i, j: (i, 0, j))],
        out_specs=pl.BlockSpec((1, r, tn), lambda i, j: (i, 0, j)),
        out_shape=jax.ShapeDtypeStruct((depth, r, n), F32),
        compiler_params=_params(("parallel", "parallel")),
        name="ada_modulation",
    )(s, ada_w, ada_b.reshape(depth, 1, n))


def _ffn_kernel(x_ref, sh_ref, sc_ref, gt_ref, g_ref, wg_ref, wu_ref, wd_ref, *rest, f_chunk, final):
    if final:
        fg_ref, o_ref = rest
    else:
        (o_ref,) = rest
    x = x_ref[0]
    u = _modulated(x, g_ref[...], sh_ref[0], sc_ref[0]).astype(BF16)
    acc = jnp.zeros(x.shape, F32)
    d_ff = wg_ref.shape[1]
    for f0 in range(0, d_ff, f_chunk):
        a = jnp.dot(u, wg_ref[:, f0:f0 + f_chunk], preferred_element_type=F32)
        b = jnp.dot(u, wu_ref[:, f0:f0 + f_chunk], preferred_element_type=F32)
        h = (a * jax.nn.sigmoid(a) * b).astype(BF16)
        acc = acc + jnp.dot(h, wd_ref[f0:f0 + f_chunk, :], preferred_element_type=F32)
    y = x + (0.5 * gt_ref[0]) * acc
    if final:
        ms = jnp.mean(y * y, axis=-1, keepdims=True)
        y = y * lax.rsqrt(ms + EPS) * fg_ref[...]
    o_ref[0] = y


def ffn_step(x, shift, scale, gate, g, wg, wu, wd, final_g=None, tm=512, f_chunk=256):
    b, l, d = x.shape
    tm = min(tm, l)
    f = wg.shape[1]
    final = final_g is not None
    mod_spec = pl.BlockSpec((1, 1, d), lambda i, j: (i, 0, 0))
    in_specs = [pl.BlockSpec((1, tm, d), lambda i, j: (i, j, 0)),
                mod_spec, mod_spec, mod_spec,
                _resident((1, d)), _resident((d, f)), _resident((d, f)), _resident((f, d))]
    args = [x, shift, scale, gate, g, wg, wu, wd]
    if final:
        in_specs.append(_resident((1, d)))
        args.append(final_g)
    return pl.pallas_call(
        functools.partial(_ffn_kernel, f_chunk=f_chunk, final=final),
        grid=(b, l // tm),
        in_specs=in_specs,
        out_specs=pl.BlockSpec((1, tm, d), lambda i, j: (i, j, 0)),
        out_shape=jax.ShapeDtypeStruct((b, l, d), F32),
        compiler_params=_params(("parallel", "parallel")),
        name="ffn_step",
    )(*args)


def _modproj_kernel(x_ref, sh_ref, sc_ref, g_ref, *rest, n_w):
    w_refs, o_refs = rest[:n_w], rest[n_w:]
    u = _modulated(x_ref[0], g_ref[...], sh_ref[0], sc_ref[0])
    if n_w == 0:
        o_refs[0][0] = u
        return
    u = u.astype(BF16)
    for w_ref, o_ref in zip(w_refs, o_refs):
        o_ref[0] = jnp.dot(u, w_ref[...], preferred_element_type=F32)


def modulate_project(x, shift, scale, g, weights, tm=256):
    b, l, d = x.shape
    tm = min(tm, l)
    mod_spec = pl.BlockSpec((1, 1, d), lambda i, j: (i, 0, 0))
    in_specs = [pl.BlockSpec((1, tm, d), lambda i, j: (i, j, 0)), mod_spec, mod_spec, _resident((1, d))]
    in_specs += [_resident(w.shape) for w in weights]
    widths = [w.shape[1] for w in weights] or [d]
    return pl.pallas_call(
        functools.partial(_modproj_kernel, n_w=len(weights)),
        grid=(b, l // tm),
        in_specs=in_specs,
        out_specs=[pl.BlockSpec((1, tm, n), lambda i, j: (i, j, 0)) for n in widths],
        out_shape=[jax.ShapeDtypeStruct((b, l, n), F32) for n in widths],
        compiler_params=_params(("parallel", "parallel")),
        name="modulate_project",
    )(x, shift, scale, g, *weights)


def _outproj_kernel(x_ref, a_ref, gt_ref, w_ref, o_ref):
    y = jnp.dot(a_ref[0].astype(BF16), w_ref[...], preferred_element_type=F32)
    o_ref[0] = x_ref[0] + gt_ref[0] * y


def project_residual(x, a, gate, w, tm=512):
    b, l, d = x.shape
    k = a.shape[-1]
    tm = min(tm, l)
    return pl.pallas_call(
        _outproj_kernel,
        grid=(b, l // tm),
        in_specs=[pl.BlockSpec((1, tm, d), lambda i, j: (i, j, 0)),
                  pl.BlockSpec((1, tm, k), lambda i, j: (i, j, 0)),
                  pl.BlockSpec((1, 1, d), lambda i, j: (i, 0, 0)),
                  _resident((k, d))],
        out_specs=pl.BlockSpec((1, tm, d), lambda i, j: (i, j, 0)),
        out_shape=jax.ShapeDtypeStruct((b, l, d), F32),
        compiler_params=_params(("parallel", "parallel")),
        name="project_residual",
    )(x, a, gate, w)


def _rmsnorm(x, g):
    y = x * lax.rsqrt(jnp.mean(x * x, axis=-1, keepdims=True) + EPS)
    return y * g


def _dwconv_centred(u, w, b):
    k_w = w.shape[0]
    p = k_w // 2
    l = u.shape[1]
    up = jnp.pad(u, ((0, 0), (p, p), (0, 0)))
    out = b
    for k in range(k_w):
        out = out + w[k] * up[:, k:k + l]
    return out


def _ssd_scan(x, dt, a_neg, bm, cm, h0, with_output):
    b, l, h, p = x.shape
    g, n = bm.shape[2], bm.shape[3]
    e = h // g
    q = SSD_CHUNK
    nc = l // q
    xdt = (x * dt[..., None]).reshape(b, nc, q, g, e, p)
    bc = bm.reshape(b, nc, q, g, n)
    cc = cm.reshape(b, nc, q, g, n)
    a_cum = jnp.cumsum((dt * a_neg).reshape(b, nc, q, g, e), axis=2)
    a_last = a_cum[:, :, -1]
    decay_to_end = jnp.exp(a_last[:, :, None] - a_cum)
    states = jnp.einsum('bcsgn,bcsge,bcsgep->bcgepn', bc, decay_to_end, xdt)

    def step(hh, inp):
        st, al = inp
        return hh * jnp.exp(al)[..., None, None] + st, hh

    h_final, h_in = lax.scan(step, h0, (jnp.moveaxis(states, 1, 0), jnp.moveaxis(a_last, 1, 0)))
    if not with_output:
        return None, h_final
    h_in = jnp.moveaxis(h_in, 0, 1)
    seg = a_cum[:, :, :, None] - a_cum[:, :, None, :]
    mask = jnp.tril(jnp.ones((q, q), dtype=bool))[:, :, None, None]
    lmat = jnp.exp(jnp.where(mask, seg, -jnp.inf))
    cb = jnp.einsum('bclgn,bcsgn->bclsg', cc, bc)
    y_diag = jnp.einsum('bclsg,bclsge,bcsgep->bclgep', cb, lmat, xdt)
    y_off = jnp.einsum('bclgn,bcgepn,bclge->bclgep', cc, h_in, jnp.exp(a_cum))
    return (y_diag + y_off).reshape(b, l, h, p), h_final


def _ssd_core(parts_lat, parts_ctx, conv_w, conv_b, a_log, dt_bias, d_skip, norm_g, need_ctx_out):
    hh, pp, gg, nn = SSD_N_HEADS, SSD_HEAD_DIM, SSD_N_GROUPS, SSD_D_STATE
    a_neg = -jnp.exp(a_log)

    def prep(parts):
        z, xbc_raw, dt_raw = parts
        b, l, _ = z.shape
        xbc = jax.nn.silu(_dwconv_centred(xbc_raw, conv_w, conv_b))
        xs = xbc[..., :SSD_D_INNER].reshape(b, l, hh, pp)
        bm = xbc[..., SSD_D_INNER:SSD_D_INNER + SSD_GN].reshape(b, l, gg, nn)
        cm = xbc[..., SSD_D_INNER + SSD_GN:].reshape(b, l, gg, nn)
        dt = jax.nn.softplus(dt_raw.reshape(b, l, 2, hh) + dt_bias)
        return z, xs, bm, cm, dt

    def flip(t, d):
        return t[:, ::-1] if d == 1 else t

    zc, xc, bc, cc, dtc = prep(parts_ctx)
    zl, xl, bl, cl, dtl = prep(parts_lat)
    b = zl.shape[0]
    h0 = jnp.zeros((b, gg, hh // gg, pp, nn), F32)
    y_lat = d_skip[:, None] * xl
    y_ctx = d_skip[:, None] * xc if need_ctx_out else None
    for d in range(2):
        yc, hc = _ssd_scan(flip(xc, d), flip(dtc[:, :, d], d), a_neg[d], flip(bc, d), flip(cc, d), h0, need_ctx_out)
        yl, _ = _ssd_scan(flip(xl, d), flip(dtl[:, :, d], d), a_neg[d], flip(bl, d), flip(cl, d), hc, True)
        y_lat = y_lat + flip(yl, d)
        if need_ctx_out:
            y_ctx = y_ctx + flip(yc, d)

    def gate_norm(y, z):
        b_, l_ = y.shape[:2]
        gy = (y.reshape(b_, l_, SSD_D_INNER) * jax.nn.silu(z)).reshape(b_, l_, gg, SSD_D_INNER // gg)
        gy = _rmsnorm(gy, norm_g.reshape(gg, SSD_D_INNER // gg))
        return gy.reshape(b_, l_, SSD_D_INNER)

    return gate_norm(y_lat, zl), (gate_norm(y_ctx, zc) if need_ctx_out else None)


def _box_sum(v, w, axis):
    l = v.shape[axis]
    cs = jnp.cumsum(v, axis=axis)
    pad = [(0, 0)] * v.ndim
    pad[axis] = (1, 0)
    cs = jnp.pad(cs, pad)
    pos = jnp.arange(l)
    hi = jnp.minimum(pos + (w - w // 2), l)
    lo = jnp.maximum(pos - w // 2, 0)
    s = jnp.take(cs, hi, axis=axis) - jnp.take(cs, lo, axis=axis)
    return s, (hi - lo).astype(F32)


def _pool_grid(v, w):
    b, l, c = v.shape
    rows = l // GRID_W
    v4 = v.reshape(b, rows, GRID_W, c)
    s, cnt_c = _box_sum(v4, w, 2)
    s, cnt_r = _box_sum(s, w, 1)
    mean = s / (cnt_r[:, None, None] * cnt_c[None, :, None])
    return mean.reshape(b, l, c) - v


def _pool_seq(v, w):
    s, cnt = _box_sum(v, w, 1)
    return s / cnt[None, :, None] - v


def _pool_core(u, pool_fn):
    b, l, _ = u.shape
    grp = u.reshape(b, l, POOL_GROUPS, POOL_GROUP_DIM)
    pooled = jnp.stack([pool_fn(grp[:, :, g], POOL_WINDOWS[g]) for g in range(POOL_GROUPS)], axis=2)
    return pooled.reshape(b, l, D_MODEL)


def _hyena_filter_fft(l, fw1, fb1, fw2, fb2, fw3, ffreq):
    pos = jnp.arange(l, dtype=F32)
    t = jnp.linspace(0.0, 1.0, l, dtype=F32)
    wpos = 2.0 * math.pi * pos / l
    f = jnp.linspace(1e-4, HY_BANDS - 1, HY_BANDS, dtype=F32)
    ang = wpos[:, None] * f[None, :]
    z = jnp.concatenate([t[:, None], jnp.cos(ang), -jnp.sin(ang)], axis=-1)
    h = jnp.sin(ffreq[0] * (z @ fw1 + fb1))
    h = jnp.sin(ffreq[1] * (h @ fw2 + fb2))
    h = (h @ fw3).reshape(l, HY_ORDER, 2, D_MODEL)
    deltas = jnp.abs(jnp.linspace(HY_MIN_DECAY, HY_MAX_DECAY, D_MODEL, dtype=F32))
    h = h * jnp.exp(-t[:, None, None, None] * deltas)
    h = h * lax.rsqrt(jnp.sum(h * h, axis=(0, 2), keepdims=True) + EPS)
    kern = jnp.concatenate([h[:, :, 0], jnp.zeros((1, HY_ORDER, D_MODEL), F32), h[:0:-1, :, 1]], axis=0)
    return jnp.fft.rfft(kern, axis=0)


def _fftconv(u, kf, bias):
    l = u.shape[1]
    y = jnp.fft.irfft(jnp.fft.rfft(u, n=2 * l, axis=1) * kf, n=2 * l, axis=1)[:, :l]
    return y + u * bias


def _hyena_core(p_raw, conv_w, conv_b, fw1, fb1, fw2, fb2, fw3, ffreq, hbias):
    kf = _hyena_filter_fft(p_raw.shape[1], fw1, fb1, fw2, fb2, fw3, ffreq)
    p = _dwconv_centred(p_raw, conv_w, conv_b)
    v, x1, x2 = jnp.split(p, 3, axis=-1)
    z = x1 * _fftconv(v, kf[:, 0], hbias[0])
    return x2 * _fftconv(z, kf[:, 1], hbias[1])


def kernel(x, c, ctx, c_ctx, ada_w, ada_b, norm_g, ffn_w_gate, ffn_w_up, ffn_w_down, ssd_w_in, ssd_conv_w, ssd_conv_b, ssd_a_log, ssd_dt_bias, ssd_d, ssd_norm_g, ssd_w_out, pool_w, pool_b, pool_scale, hy_w_in, hy_conv_w, hy_conv_b, hy_filt_w1, hy_filt_b1, hy_filt_w2, hy_filt_b2, hy_filt_w3, hy_filt_freq, hy_bias, hy_w_out, final_g):
    batch = x.shape[0]
    d = D_MODEL
    h = ctx

    s = jnp.concatenate([jax.nn.silu(c), jax.nn.silu(c_ctx)[None], jnp.zeros((7 - batch, d), F32)], axis=0)
    mods = ada_modulation(s, ada_w, ada_b).reshape(DEPTH, 8, N_MOD, d)

    wg_bf, wu_bf, wd_bf = ffn_w_gate.astype(BF16), ffn_w_up.astype(BF16), ffn_w_down.astype(BF16)
    fg = final_g.reshape(1, d)

    for i in range(DEPTH):
        kind = i % N_MIXERS
        j = i // N_MIXERS
        last = i == DEPTH - 1
        ctx_in_needed = (not last) or kind == 0
        m = [mods[i, :batch, k][:, None, :] for k in range(N_MOD)]
        mc = [jnp.broadcast_to(mods[i, batch, k][None, None, :], (batch, 1, d)) for k in range(N_MOD)]
        g0, g1, g2 = (norm_g[i, k].reshape(1, d) for k in range(3))

        x = ffn_step(x, m[0], m[1], m[2], g0, wg_bf[i, 0], wu_bf[i, 0], wd_bf[i, 0])
        if ctx_in_needed:
            h = ffn_step(h, mc[0], mc[1], mc[2], g0, wg_bf[i, 0], wu_bf[i, 0], wd_bf[i, 0])

        if kind == 0:
            w_in = ssd_w_in[j].astype(BF16)
            ws = [w_in[:, :SSD_D_INNER], w_in[:, SSD_D_INNER:SSD_D_INNER + SSD_CONV_DIM],
                  w_in[:, SSD_D_INNER + SSD_CONV_DIM:]]
            parts_lat = modulate_project(x, m[3], m[4], g1, ws)
            parts_ctx = modulate_project(h, mc[3], mc[4], g1, ws)
            a_lat, a_ctx = _ssd_core(parts_lat, parts_ctx, ssd_conv_w[j], ssd_conv_b[j], ssd_a_log[j],
                                     ssd_dt_bias[j], ssd_d[j], ssd_norm_g[j], not last)
            w_out = ssd_w_out[j].astype(BF16)
            x = project_residual(x, a_lat, m[5], w_out)
            if not last:
                h = project_residual(h, a_ctx, mc[5], w_out)
        elif kind == 1:
            (u,) = modulate_project(x, m[3], m[4], g1, [])
            pooled = _pool_core(u, _pool_grid)
            y = jnp.einsum('blgc,gcd->blgd', pooled.reshape(batch, -1, POOL_GROUPS, POOL_GROUP_DIM), pool_w[j]) + pool_b[j]
            x = x + m[5] * (y.reshape(batch, -1, d) * pool_scale[j])
            if not last:
                (uc,) = modulate_project(h, mc[3], mc[4], g1, [])
                pooled_c = _pool_core(uc, _pool_seq)
                yc = jnp.einsum('blgc,gcd->blgd', pooled_c.reshape(batch, -1, POOL_GROUPS, POOL_GROUP_DIM), pool_w[j]) + pool_b[j]
                h = h + mc[5] * (yc.reshape(batch, -1, d) * pool_scale[j])
        else:
            w_in = hy_w_in[j].astype(BF16)
            w_out = hy_w_out[j].astype(BF16)
            filt = (hy_filt_w1[j], hy_filt_b1[j], hy_filt_w2[j], hy_filt_b2[j], hy_filt_w3[j], hy_filt_freq[j])
            (p_lat,) = modulate_project(x, m[3], m[4], g1, [w_in])
            x = project_residual(x, _hyena_core(p_lat, hy_conv_w[j], hy_conv_b[j], *filt, hy_bias[j]), m[5], w_out)
            if not last:
                (p_ctx,) = modulate_project(h, mc[3], mc[4], g1, [w_in])
                h = project_residual(h, _hyena_core(p_ctx, hy_conv_w[j], hy_conv_b[j], *filt, hy_bias[j]), mc[5], w_out)

        x = ffn_step(x, m[6], m[7], m[8], g2, wg_bf[i, 1], wu_bf[i, 1], wd_bf[i, 1],
                     final_g=fg if last else None)
        if not last:
            h = ffn_step(h, mc[6], mc[7], mc[8], g2, wg_bf[i, 1], wu_bf[i, 1], wd_bf[i, 1])
    return x
```

```python
import functools
import math

import jax
import jax.numpy as jnp
from jax import lax
from jax.experimental import pallas as pl
from jax.experimental.pallas import tpu as pltpu

F32 = jnp.float32
BF16 = jnp.bfloat16
HIGHEST = lax.Precision.HIGHEST

D_MODEL = 1024
DEPTH = 4
GRID_W = 64
N_MIXERS = 3
D_FF = 2816
N_MOD = 9
EPS = 1e-6

SSD_D_INNER = 2 * D_MODEL
SSD_HEAD_DIM = 64
SSD_N_HEADS = SSD_D_INNER // SSD_HEAD_DIM
SSD_N_GROUPS = 4
SSD_HEADS_PER_GROUP = SSD_N_HEADS // SSD_N_GROUPS
SSD_D_STATE = 128
SSD_CHUNK = 128
SSD_GN = SSD_N_GROUPS * SSD_D_STATE
SSD_CONV_DIM = SSD_D_INNER + 2 * SSD_GN
SSD_GROUP_WIDTH = SSD_D_INNER // SSD_N_GROUPS

POOL_WINDOWS = (2, 4, 8, 16)
POOL_GROUPS = 4
POOL_GROUP_DIM = D_MODEL // POOL_GROUPS

HY_ORDER = 2
HY_EMB_DIM = 33
HY_BANDS = (HY_EMB_DIM - 1) // 2
HY_MAX_DECAY = math.log(1e-2) / 0.3
HY_MIN_DECAY = math.log(1e-2) / 1.5

VMEM_LIMIT_BYTES = 56 * 1024 * 1024
SUBLANES = 8
LANES = 128


def _params(sem):
    return pltpu.CompilerParams(dimension_semantics=sem, vmem_limit_bytes=VMEM_LIMIT_BYTES)


def _resident(shape):
    nd = len(shape)
    return pl.BlockSpec(shape, lambda *_: (0,) * nd, pipeline_mode=pl.Buffered(1))


def _mod_spec(d, n0_tiles):
    if n0_tiles == 0:
        return pl.BlockSpec((1, 1, 1, d), lambda i, j: (i, 0, 0, 0))
    return pl.BlockSpec((1, 1, 1, d), lambda i, j: (i, jnp.where(j < n0_tiles, 0, 1), 0, 0))


def _modulated(x, g, shift, scale):
    ms = jnp.mean(x * x, axis=-1, keepdims=True)
    return (x * lax.rsqrt(ms + EPS)) * g * (1.0 + scale) + shift


def _silu(v):
    return v * jax.nn.sigmoid(v)


def _ada_kernel(s_ref, w_ref, b_ref, o_ref):
    o_ref[0] = jnp.dot(s_ref[...], w_ref[0], preferred_element_type=F32, precision=HIGHEST) + b_ref[0]


def ada_modulation(s, ada_w, ada_b, tn=1024):
    depth, d, n = ada_w.shape
    r = s.shape[0]
    return pl.pallas_call(
        _ada_kernel,
        grid=(depth, n // tn),
        in_specs=[pl.BlockSpec((r, d), lambda i, j: (0, 0)),
                  pl.BlockSpec((1, d, tn), lambda i, j: (i, 0, j)),
                  pl.BlockSpec((1, 1, tn), lambda i, j: (i, 0, j))],
        out_specs=pl.BlockSpec((1, r, tn), lambda i, j: (i, 0, j)),
        out_shape=jax.ShapeDtypeStruct((depth, r, n), F32),
        compiler_params=_params(("parallel", "parallel")),
        name="ada_modulation",
    )(s, ada_w, ada_b.reshape(depth, 1, n))


def _ffn_kernel(x_ref, sh_ref, sc_ref, gt_ref, g_ref, wg_ref, wu_ref, wd_ref, *rest, f_chunk, final):
    if final:
        fg_ref, o_ref = rest
    else:
        (o_ref,) = rest
    x = x_ref[0]
    u = _modulated(x, g_ref[...], sh_ref[0, 0], sc_ref[0, 0]).astype(BF16)
    acc = jnp.zeros(x.shape, F32)
    d_ff = wg_ref.shape[1]
    for f0 in range(0, d_ff, f_chunk):
        a = jnp.dot(u, wg_ref[:, f0:f0 + f_chunk], preferred_element_type=F32)
        b = jnp.dot(u, wu_ref[:, f0:f0 + f_chunk], preferred_element_type=F32)
        h = (_silu(a) * b).astype(BF16)
        acc = acc + jnp.dot(h, wd_ref[f0:f0 + f_chunk, :], preferred_element_type=F32)
    y = x + (0.5 * gt_ref[0, 0]) * acc
    if final:
        ms = jnp.mean(y * y, axis=-1, keepdims=True)
        y = y * lax.rsqrt(ms + EPS) * fg_ref[...]
    o_ref[0] = y


def ffn_step(x, shift, scale, gate, g, wg, wu, wd, final_g=None, tm=512, f_chunk=256):
    b, l, d = x.shape
    tm = min(tm, l)
    f = wg.shape[1]
    final = final_g is not None
    mod_spec = _mod_spec(d, 0)
    in_specs = [pl.BlockSpec((1, tm, d), lambda i, j: (i, j, 0)),
                mod_spec, mod_spec, mod_spec,
                _resident((1, d)), _resident((d, f)), _resident((d, f)), _resident((f, d))]
    args = [x, shift, scale, gate, g, wg, wu, wd]
    if final:
        in_specs.append(_resident((1, d)))
        args.append(final_g)
    return pl.pallas_call(
        functools.partial(_ffn_kernel, f_chunk=f_chunk, final=final),
        grid=(b, l // tm),
        in_specs=in_specs,
        out_specs=pl.BlockSpec((1, tm, d), lambda i, j: (i, j, 0)),
        out_shape=jax.ShapeDtypeStruct((b, l, d), F32),
        compiler_params=_params(("parallel", "parallel")),
        name="ffn_step",
    )(*args)


def _modproj_kernel(x_ref, sh_ref, sc_ref, g_ref, *rest, n_w):
    w_refs, o_refs = rest[:n_w], rest[n_w:]
    u = _modulated(x_ref[0], g_ref[...], sh_ref[0, 0], sc_ref[0, 0])
    if n_w == 0:
        o_refs[0][0] = u
        return
    u = u.astype(BF16)
    for w_ref, o_ref in zip(w_refs, o_refs):
        o_ref[0] = jnp.dot(u, w_ref[...], preferred_element_type=F32)


def modulate_project(x, shift, scale, g, weights, tm=256, n0_tiles=0):
    b, l, d = x.shape
    tm = min(tm, l)
    mod_spec = _mod_spec(d, n0_tiles)
    in_specs = [pl.BlockSpec((1, tm, d), lambda i, j: (i, j, 0)), mod_spec, mod_spec, _resident((1, d))]
    in_specs += [_resident(w.shape) for w in weights]
    widths = [w.shape[1] for w in weights] or [d]
    return pl.pallas_call(
        functools.partial(_modproj_kernel, n_w=len(weights)),
        grid=(b, l // tm),
        in_specs=in_specs,
        out_specs=[pl.BlockSpec((1, tm, n), lambda i, j: (i, j, 0)) for n in widths],
        out_shape=[jax.ShapeDtypeStruct((b, l, n), F32) for n in widths],
        compiler_params=_params(("parallel", "parallel")),
        name="modulate_project",
    )(x, shift, scale, g, *weights)


def _outproj_kernel(x_ref, a_ref, gt_ref, w_ref, o_ref):
    y = jnp.dot(a_ref[0].astype(BF16), w_ref[...], preferred_element_type=F32)
    o_ref[0] = x_ref[0] + gt_ref[0, 0] * y


def project_residual(x, a, gate, w, tm=512):
    b, l, d = x.shape
    k = a.shape[-1]
    tm = min(tm, l)
    return pl.pallas_call(
        _outproj_kernel,
        grid=(b, l // tm),
        in_specs=[pl.BlockSpec((1, tm, d), lambda i, j: (i, j, 0)),
                  pl.BlockSpec((1, tm, k), lambda i, j: (i, j, 0)),
                  _mod_spec(d, 0),
                  _resident((k, d))],
        out_specs=pl.BlockSpec((1, tm, d), lambda i, j: (i, j, 0)),
        out_shape=jax.ShapeDtypeStruct((b, l, d), F32),
        compiler_params=_params(("parallel", "parallel")),
        name="project_residual",
    )(x, a, gate, w)


def _dwconv_kernel(prev_ref, x_ref, next_ref, w_ref, b_ref, o_ref, *, k_w, n_rows, seg_rows, silu):
    tr, c = x_ref.shape[1], x_ref.shape[2]
    p = k_w // 2
    ext = jnp.concatenate([prev_ref[0], x_ref[0], next_ref[0]], axis=0)
    row = pl.program_id(2) * tr + lax.broadcasted_iota(jnp.int32, (tr, c), 0)
    acc = jnp.broadcast_to(b_ref[...], (tr, c))
    for k in range(k_w):
        off = k - p
        tap = ext[SUBLANES + off:SUBLANES + off + tr, :]
        if off != 0:
            src = row + off
            valid = (src >= 0) & (src < n_rows)
            if seg_rows:
                lo, hi = (src, row) if off < 0 else (row, src)
                valid = valid & jnp.logical_not((lo < seg_rows) & (hi >= seg_rows))
            tap = jnp.where(valid, tap, 0.0)
        acc = acc + w_ref[k:k + 1, :] * tap
    o_ref[0] = _silu(acc) if silu else acc


def dwconv_tokens(x, w, bias, seg_rows=0, silu=False, tr=256, ct=512):
    b, t, c = x.shape
    k_w = w.shape[0]
    tr = min(tr, t)
    ct = min(ct, c)
    rb = tr // SUBLANES
    last_halo = t // SUBLANES - 1
    return pl.pallas_call(
        functools.partial(_dwconv_kernel, k_w=k_w, n_rows=t, seg_rows=seg_rows, silu=silu),
        grid=(b, c // ct, t // tr),
        in_specs=[pl.BlockSpec((1, SUBLANES, ct), lambda i, j, r: (i, jnp.maximum(r * rb - 1, 0), j)),
                  pl.BlockSpec((1, tr, ct), lambda i, j, r: (i, r, j)),
                  pl.BlockSpec((1, SUBLANES, ct), lambda i, j, r: (i, jnp.minimum((r + 1) * rb, last_halo), j)),
                  pl.BlockSpec((k_w, ct), lambda i, j, r: (0, j)),
                  pl.BlockSpec((1, ct), lambda i, j, r: (0, j))],
        out_specs=pl.BlockSpec((1, tr, ct), lambda i, j, r: (i, r, j)),
        out_shape=jax.ShapeDtypeStruct((b, t, c), F32),
        compiler_params=_params(("parallel", "parallel", "parallel")),
        name="dwconv_tokens",
    )(x, x, x, w, bias.reshape(1, c))


def _ssd_scan_kernel(xbc_ref, dt_ref, dtt_ref, dtb_ref, dtbt_ref, a_ref, at_ref, ex_ref, o_ref, h_ref):
    nh, q, hp = SSD_N_HEADS, SSD_CHUNK, SSD_HEAD_DIM
    d = pl.program_id(1)
    fwd = d == 0

    @pl.when(pl.program_id(2) == 0)
    def _():
        h_ref[...] = jnp.zeros(h_ref.shape, F32)

    dt2 = jax.nn.softplus(dt_ref[0] + dtb_ref[...])
    dt = jnp.where(fwd, dt2[:, :nh], dt2[:, nh:])
    a = dt * jnp.where(fwd, a_ref[:, :nh], a_ref[:, nh:])
    dtt2 = jax.nn.softplus(dtt_ref[0] + dtbt_ref[...])
    a_t = jnp.where(fwd, dtt2[:nh], dtt2[nh:]) * jnp.where(fwd, at_ref[:nh], at_ref[nh:])

    r_i = lax.broadcasted_iota(jnp.int32, (q, q), 0)
    c_i = lax.broadcasted_iota(jnp.int32, (q, q), 1)
    ahead = jnp.where(fwd, c_i - r_i, r_i - c_i)
    seen = ahead <= 0
    seen_t = ahead >= 0
    cum = jnp.dot(seen.astype(F32), a, preferred_element_type=F32, precision=HIGHEST)
    cum_t = jnp.dot(a_t, seen_t.astype(F32), preferred_element_type=F32, precision=HIGHEST)
    total = jnp.where(fwd, cum[q - 1:q, :], cum[0:1, :])

    def widen(v):
        return jnp.dot(v, ex_ref[...], preferred_element_type=F32, precision=HIGHEST)

    dt_x = widen(dt)
    dec_x = widen(jnp.exp(total - cum))
    ecum_x = widen(jnp.exp(cum))
    etot_x = jnp.where(fwd, ecum_x[q - 1:q, :], ecum_x[0:1, :])

    xs = xbc_ref[0, :, 0:SSD_D_INNER]
    xdt = xs * dt_x
    xdec = (xdt * dec_x).astype(BF16)
    xdt = xdt.astype(BF16)
    gw = SSD_GROUP_WIDTH
    for g in range(SSD_N_GROUPS):
        b_g = xbc_ref[0, :, SSD_D_INNER + g * SSD_D_STATE:SSD_D_INNER + (g + 1) * SSD_D_STATE].astype(BF16)
        c_lo = SSD_D_INNER + SSD_GN + g * SSD_D_STATE
        c_g = xbc_ref[0, :, c_lo:c_lo + SSD_D_STATE].astype(BF16)
        cb = lax.dot_general(c_g, b_g, (((1,), (1,)), ((), ())), preferred_element_type=F32)
        h_g = h_ref[g]
        y_off = jnp.dot(c_g, h_g.astype(BF16), preferred_element_type=F32) * ecum_x[:, g * gw:(g + 1) * gw]
        ys = []
        for e in range(SSD_HEADS_PER_GROUP):
            hd = g * SSD_HEADS_PER_GROUP + e
            seg = cum[:, hd:hd + 1] - cum_t[hd:hd + 1, :]
            w = (cb * jnp.exp(jnp.where(seen, seg, -jnp.inf))).astype(BF16)
            ys.append(jnp.dot(w, xdt[:, hd * hp:(hd + 1) * hp], preferred_element_type=F32))
        o_ref[0, 0, :, g * gw:(g + 1) * gw] = jnp.concatenate(ys, axis=1) + y_off
        s_g = lax.dot_general(b_g, xdec[:, g * gw:(g + 1) * gw], (((0,), (0,)), ((), ())),
                              preferred_element_type=F32)
        h_ref[g] = h_g * etot_x[:, g * gw:(g + 1) * gw] + s_g


def ssd_scan(xbc, dt_raw, dt_bias, a_log, n_lead_chunks):
    b, t, _ = xbc.shape
    q, nh = SSD_CHUNK, SSD_N_HEADS
    nc = t // q
    dt_t = jnp.swapaxes(dt_raw, 1, 2)
    a_neg = -jnp.exp(a_log.astype(F32)).reshape(1, 2 * nh)
    dtb = dt_bias.astype(F32).reshape(1, 2 * nh)
    expand = jnp.repeat(jnp.eye(nh, dtype=F32), SSD_HEAD_DIM, axis=1)

    def chunk_of(d, c):
        back = jnp.where(c < n_lead_chunks, n_lead_chunks - 1 - c, nc - 1 + n_lead_chunks - c)
        return jnp.where(d == 0, c, back)

    return pl.pallas_call(
        _ssd_scan_kernel,
        grid=(b, 2, nc),
        in_specs=[pl.BlockSpec((1, q, SSD_CONV_DIM), lambda i, d, c: (i, chunk_of(d, c), 0)),
                  pl.BlockSpec((1, q, 2 * nh), lambda i, d, c: (i, chunk_of(d, c), 0)),
                  pl.BlockSpec((1, 2 * nh, q), lambda i, d, c: (i, 0, chunk_of(d, c))),
                  _resident((1, 2 * nh)), _resident((2 * nh, 1)),
                  _resident((1, 2 * nh)), _resident((2 * nh, 1)),
                  _resident((nh, SSD_D_INNER))],
        out_specs=pl.BlockSpec((1, 1, q, SSD_D_INNER), lambda i, d, c: (i, d, chunk_of(d, c), 0)),
        out_shape=jax.ShapeDtypeStruct((b, 2, t, SSD_D_INNER), F32),
        scratch_shapes=[pltpu.VMEM((SSD_N_GROUPS, SSD_D_STATE, SSD_GROUP_WIDTH), F32)],
        compiler_params=_params(("parallel", "parallel", "arbitrary")),
        name="ssd_scan",
    )(xbc, dt_raw, dt_t, dtb, dtb.reshape(2 * nh, 1), a_neg, a_neg.reshape(2 * nh, 1), expand)


def _ssd_out_kernel(x_ref, y0_ref, y1_ref, xs0_ref, xs1_ref, z_ref, dsk_ref, ng_ref, gt_ref, w_ref, o_ref):
    xs = jnp.concatenate([xs0_ref[0], xs1_ref[0]], axis=1)
    y = y0_ref[0, 0] + y1_ref[0, 0] + dsk_ref[...] * xs
    gy = y * _silu(z_ref[0])
    parts = []
    for g in range(SSD_N_GROUPS):
        blk = gy[:, g * SSD_GROUP_WIDTH:(g + 1) * SSD_GROUP_WIDTH]
        parts.append(blk * lax.rsqrt(jnp.mean(blk * blk, axis=-1, keepdims=True) + EPS))
    a = (jnp.concatenate(parts, axis=1) * ng_ref[...]).astype(BF16)
    o_ref[0] = x_ref[0] + gt_ref[0, 0] * jnp.dot(a, w_ref[...], preferred_element_type=F32)


def ssd_gate_project(x, y2, xbc, z, d_skip, norm_g, gate, w_out, n0_tiles, tm=256):
    b, t, d = x.shape
    half = SSD_D_INNER // 2
    dsk = jnp.repeat(d_skip.astype(F32), SSD_HEAD_DIM).reshape(1, SSD_D_INNER)
    return pl.pallas_call(
        _ssd_out_kernel,
        grid=(b, t // tm),
        in_specs=[pl.BlockSpec((1, tm, d), lambda i, j: (i, j, 0)),
                  pl.BlockSpec((1, 1, tm, SSD_D_INNER), lambda i, j: (i, 0, j, 0)),
                  pl.BlockSpec((1, 1, tm, SSD_D_INNER), lambda i, j: (i, 1, j, 0)),
                  pl.BlockSpec((1, tm, half), lambda i, j: (i, j, 0)),
                  pl.BlockSpec((1, tm, half), lambda i, j: (i, j, 1)),
                  pl.BlockSpec((1, tm, SSD_D_INNER), lambda i, j: (i, j, 0)),
                  _resident((1, SSD_D_INNER)), _resident((1, SSD_D_INNER)),
                  _mod_spec(d, n0_tiles),
                  _resident((SSD_D_INNER, d))],
        out_specs=pl.BlockSpec((1, tm, d), lambda i, j: (i, j, 0)),
        out_shape=jax.ShapeDtypeStruct((b, t, d), F32),
        compiler_params=_params(("parallel", "parallel")),
        name="ssd_gate_project",
    )(x, y2, y2, xbc, xbc, z, dsk, norm_g.astype(F32).reshape(1, SSD_D_INNER), gate, w_out)


def ssd_mixer(x, h, m_lat, m_ctx, g1, w_in, conv_w, conv_b, a_log, dt_bias, d_skip, norm_g, w_out, tm=256):
    lc = h.shape[1]
    xx = jnp.concatenate([h, x], axis=1)
    sh, sc, gt = (jnp.concatenate([mc, ml], axis=1) for mc, ml in zip(m_ctx, m_lat))
    n0 = lc // tm
    w_bf = w_in.astype(BF16)
    ws = [w_bf[:, :SSD_D_INNER], w_bf[:, SSD_D_INNER:SSD_D_INNER + SSD_CONV_DIM], w_bf[:, SSD_D_INNER + SSD_CONV_DIM:]]
    z, xbc_raw, dt_raw = modulate_project(xx, sh, sc, g1, ws, tm=tm, n0_tiles=n0)
    xbc = dwconv_tokens(xbc_raw, conv_w, conv_b, seg_rows=lc, silu=True)
    y2 = ssd_scan(xbc, dt_raw, dt_bias, a_log, lc // SSD_CHUNK)
    out = ssd_gate_project(xx, y2, xbc, z, d_skip, norm_g, gt, w_out.astype(BF16), n0, tm=tm)
    return out[:, lc:], out[:, :lc]


def _dwconv_centred(u, w, b):
    k_w = w.shape[0]
    p = k_w // 2
    l = u.shape[1]
    up = jnp.pad(u, ((0, 0), (p, p), (0, 0)))
    out = b
    for k in range(k_w):
        out = out + w[k] * up[:, k:k + l]
    return out


def _box_sum(v, w, axis):
    l = v.shape[axis]
    cs = jnp.cumsum(v, axis=axis)
    pad = [(0, 0)] * v.ndim
    pad[axis] = (1, 0)
    cs = jnp.pad(cs, pad)
    pos = jnp.arange(l)
    hi = jnp.minimum(pos + (w - w // 2), l)
    lo = jnp.maximum(pos - w // 2, 0)
    s = jnp.take(cs, hi, axis=axis) - jnp.take(cs, lo, axis=axis)
    return s, (hi - lo).astype(F32)


def _pool_grid(v, w):
    b, l, c = v.shape
    rows = l // GRID_W
    v4 = v.reshape(b, rows, GRID_W, c)
    s, cnt_c = _box_sum(v4, w, 2)
    s, cnt_r = _box_sum(s, w, 1)
    mean = s / (cnt_r[:, None, None] * cnt_c[None, :, None])
    return mean.reshape(b, l, c) - v


def _pool_seq(v, w):
    s, cnt = _box_sum(v, w, 1)
    return s / cnt[None, :, None] - v


def _pool_core(u, pool_fn):
    b, l, _ = u.shape
    grp = u.reshape(b, l, POOL_GROUPS, POOL_GROUP_DIM)
    pooled = jnp.stack([pool_fn(grp[:, :, g], POOL_WINDOWS[g]) for g in range(POOL_GROUPS)], axis=2)
    return pooled.reshape(b, l, D_MODEL)


def _hyena_filter_fft(l, fw1, fb1, fw2, fb2, fw3, ffreq):
    pos = jnp.arange(l, dtype=F32)
    t = jnp.linspace(0.0, 1.0, l, dtype=F32)
    wpos = 2.0 * math.pi * pos / l
    f = jnp.linspace(1e-4, HY_BANDS - 1, HY_BANDS, dtype=F32)
    ang = wpos[:, None] * f[None, :]
    z = jnp.concatenate([t[:, None], jnp.cos(ang), -jnp.sin(ang)], axis=-1)
    h = jnp.sin(ffreq[0] * (z @ fw1 + fb1))
    h = jnp.sin(ffreq[1] * (h @ fw2 + fb2))
    h = (h @ fw3).reshape(l, HY_ORDER, 2, D_MODEL)
    deltas = jnp.abs(jnp.linspace(HY_MIN_DECAY, HY_MAX_DECAY, D_MODEL, dtype=F32))
    h = h * jnp.exp(-t[:, None, None, None] * deltas)
    h = h * lax.rsqrt(jnp.sum(h * h, axis=(0, 2), keepdims=True) + EPS)
    kern = jnp.concatenate([h[:, :, 0], jnp.zeros((1, HY_ORDER, D_MODEL), F32), h[:0:-1, :, 1]], axis=0)
    return jnp.fft.rfft(kern, axis=0)


def _fftconv(u, kf, bias):
    l = u.shape[1]
    y = jnp.fft.irfft(jnp.fft.rfft(u, n=2 * l, axis=1) * kf, n=2 * l, axis=1)[:, :l]
    return y + u * bias


def _hyena_core(p_raw, conv_w, conv_b, fw1, fb1, fw2, fb2, fw3, ffreq, hbias):
    kf = _hyena_filter_fft(p_raw.shape[1], fw1, fb1, fw2, fb2, fw3, ffreq)
    p = _dwconv_centred(p_raw, conv_w, conv_b)
    v, x1, x2 = jnp.split(p, 3, axis=-1)
    z = x1 * _fftconv(v, kf[:, 0], hbias[0])
    return x2 * _fftconv(z, kf[:, 1], hbias[1])


def kernel(x, c, ctx, c_ctx, ada_w, ada_b, norm_g, ffn_w_gate, ffn_w_up, ffn_w_down, ssd_w_in, ssd_conv_w, ssd_conv_b, ssd_a_log, ssd_dt_bias, ssd_d, ssd_norm_g, ssd_w_out, pool_w, pool_b, pool_scale, hy_w_in, hy_conv_w, hy_conv_b, hy_filt_w1, hy_filt_b1, hy_filt_w2, hy_filt_b2, hy_filt_w3, hy_filt_freq, hy_bias, hy_w_out, final_g):
    batch = x.shape[0]
    d = D_MODEL
    h = ctx

    s = jnp.concatenate([jax.nn.silu(c), jax.nn.silu(c_ctx)[None], jnp.zeros((7 - batch, d), F32)], axis=0)
    mods = ada_modulation(s, ada_w, ada_b).reshape(DEPTH, 8, N_MOD, d)

    wg_bf, wu_bf, wd_bf = ffn_w_gate.astype(BF16), ffn_w_up.astype(BF16), ffn_w_down.astype(BF16)
    fg = final_g.reshape(1, d)

    for i in range(DEPTH):
        kind = i % N_MIXERS
        j = i // N_MIXERS
        last = i == DEPTH - 1
        ctx_in_needed = (not last) or kind == 0
        m = [mods[i, :batch, k].reshape(batch, 1, 1, d) for k in range(N_MOD)]
        mc = [jnp.broadcast_to(mods[i, batch, k].reshape(1, 1, 1, d), (batch, 1, 1, d)) for k in range(N_MOD)]
        g0, g1, g2 = (norm_g[i, k].reshape(1, d) for k in range(3))

        x = ffn_step(x, m[0], m[1], m[2], g0, wg_bf[i, 0], wu_bf[i, 0], wd_bf[i, 0])
        if ctx_in_needed:
            h = ffn_step(h, mc[0], mc[1], mc[2], g0, wg_bf[i, 0], wu_bf[i, 0], wd_bf[i, 0])

        if kind == 0:
            x, h_new = ssd_mixer(x, h, (m[3], m[4], m[5]), (mc[3], mc[4], mc[5]), g1, ssd_w_in[j], ssd_conv_w[j],
                                 ssd_conv_b[j], ssd_a_log[j], ssd_dt_bias[j], ssd_d[j], ssd_norm_g[j], ssd_w_out[j])
            if not last:
                h = h_new
        elif kind == 1:
            (u,) = modulate_project(x, m[3], m[4], g1, [])
            pooled = _pool_core(u, _pool_grid)
            y = jnp.einsum('blgc,gcd->blgd', pooled.reshape(batch, -1, POOL_GROUPS, POOL_GROUP_DIM), pool_w[j]) + pool_b[j]
            x = x + m[5][:, 0] * (y.reshape(batch, -1, d) * pool_scale[j])
            if not last:
                (uc,) = modulate_project(h, mc[3], mc[4], g1, [])
                pooled_c = _pool_core(uc, _pool_seq)
                yc = jnp.einsum('blgc,gcd->blgd', pooled_c.reshape(batch, -1, POOL_GROUPS, POOL_GROUP_DIM), pool_w[j]) + pool_b[j]
                h = h + mc[5][:, 0] * (yc.reshape(batch, -1, d) * pool_scale[j])
        else:
            w_in = hy_w_in[j].astype(BF16)
            w_out = hy_w_out[j].astype(BF16)
            filt = (hy_filt_w1[j], hy_filt_b1[j], hy_filt_w2[j], hy_filt_b2[j], hy_filt_w3[j], hy_filt_freq[j])
            (p_lat,) = modulate_project(x, m[3], m[4], g1, [w_in])
            x = project_residual(x, _hyena_core(p_lat, hy_conv_w[j], hy_conv_b[j], *filt, hy_bias[j]), m[5], w_out)
            if not last:
                (p_ctx,) = modulate_project(h, mc[3], mc[4], g1, [w_in])
                h = project_residual(h, _hyena_core(p_ctx, hy_conv_w[j], hy_conv_b[j], *filt, hy_bias[j]), mc[5], w_out)

        x = ffn_step(x, m[6], m[7], m[8], g2, wg_bf[i, 1], wu_bf[i, 1], wd_bf[i, 1],
                     final_g=fg if last else None)
        if not last:
            h = ffn_step(h, mc[6], mc[7], mc[8], g2, wg_bf[i, 1], wu_bf[i, 1], wd_bf[i, 1])
    return x
```

```python
import functools
import math

import jax
import jax.numpy as jnp
from jax import lax
from jax.experimental import pallas as pl
from jax.experimental.pallas import tpu as pltpu

F32 = jnp.float32
BF16 = jnp.bfloat16
HIGHEST = lax.Precision.HIGHEST

D_MODEL = 1024
DEPTH = 4
GRID_W = 64
N_MIXERS = 3
D_FF = 2816
N_MOD = 9
EPS = 1e-6

SSD_D_INNER = 2 * D_MODEL
SSD_HEAD_DIM = 64
SSD_N_HEADS = SSD_D_INNER // SSD_HEAD_DIM
SSD_N_GROUPS = 4
SSD_HEADS_PER_GROUP = SSD_N_HEADS // SSD_N_GROUPS
SSD_D_STATE = 128
SSD_CHUNK = 128
SSD_GN = SSD_N_GROUPS * SSD_D_STATE
SSD_CONV_DIM = SSD_D_INNER + 2 * SSD_GN
SSD_GROUP_WIDTH = SSD_D_INNER // SSD_N_GROUPS

POOL_WINDOWS = (2, 4, 8, 16)
POOL_GROUPS = 4
POOL_GROUP_DIM = D_MODEL // POOL_GROUPS

HY_ORDER = 2
HY_EMB_DIM = 33
HY_BANDS = (HY_EMB_DIM - 1) // 2
HY_MAX_DECAY = math.log(1e-2) / 0.3
HY_MIN_DECAY = math.log(1e-2) / 1.5

VMEM_LIMIT_BYTES = 56 * 1024 * 1024
SUBLANES = 8
LANES = 128


def _params(sem):
    return pltpu.CompilerParams(dimension_semantics=sem, vmem_limit_bytes=VMEM_LIMIT_BYTES)


def _resident(shape):
    nd = len(shape)
    return pl.BlockSpec(shape, lambda *_: (0,) * nd, pipeline_mode=pl.Buffered(1))


def _mod_spec(d, n0_tiles):
    if n0_tiles == 0:
        return pl.BlockSpec((1, 1, 1, d), lambda i, j: (i, 0, 0, 0))
    return pl.BlockSpec((1, 1, 1, d), lambda i, j: (i, jnp.where(j < n0_tiles, 0, 1), 0, 0))


def _modulated(x, g, shift, scale):
    ms = jnp.mean(x * x, axis=-1, keepdims=True)
    return (x * lax.rsqrt(ms + EPS)) * g * (1.0 + scale) + shift


def _silu(v):
    return v * jax.nn.sigmoid(v)


def _ada_kernel(s_ref, w_ref, b_ref, o_ref):
    o_ref[0] = jnp.dot(s_ref[...], w_ref[0], preferred_element_type=F32, precision=HIGHEST) + b_ref[0]


def ada_modulation(s, ada_w, ada_b, tn=1024):
    depth, d, n = ada_w.shape
    r = s.shape[0]
    return pl.pallas_call(
        _ada_kernel,
        grid=(depth, n // tn),
        in_specs=[pl.BlockSpec((r, d), lambda i, j: (0, 0)),
                  pl.BlockSpec((1, d, tn), lambda i, j: (i, 0, j)),
                  pl.BlockSpec((1, 1, tn), lambda i, j: (i, 0, j))],
        out_specs=pl.BlockSpec((1, r, tn), lambda i, j: (i, 0, j)),
        out_shape=jax.ShapeDtypeStruct((depth, r, n), F32),
        compiler_params=_params(("parallel", "parallel")),
        name="ada_modulation",
    )(s, ada_w, ada_b.reshape(depth, 1, n))


def _ffn_kernel(x_ref, sh_ref, sc_ref, gt_ref, g_ref, wg_ref, wu_ref, wd_ref, *rest, f_chunk, final):
    if final:
        fg_ref, o_ref = rest
    else:
        (o_ref,) = rest
    x = x_ref[0]
    u = _modulated(x, g_ref[...], sh_ref[0, 0], sc_ref[0, 0]).astype(BF16)
    acc = jnp.zeros(x.shape, F32)
    d_ff = wg_ref.shape[1]
    for f0 in range(0, d_ff, f_chunk):
        a = jnp.dot(u, wg_ref[:, f0:f0 + f_chunk], preferred_element_type=F32)
        b = jnp.dot(u, wu_ref[:, f0:f0 + f_chunk], preferred_element_type=F32)
        h = (_silu(a) * b).astype(BF16)
        acc = acc + jnp.dot(h, wd_ref[f0:f0 + f_chunk, :], preferred_element_type=F32)
    y = x + (0.5 * gt_ref[0, 0]) * acc
    if final:
        ms = jnp.mean(y * y, axis=-1, keepdims=True)
        y = y * lax.rsqrt(ms + EPS) * fg_ref[...]
    o_ref[0] = y


def ffn_step(x, shift, scale, gate, g, wg, wu, wd, final_g=None, tm=512, f_chunk=256):
    b, l, d = x.shape
    tm = min(tm, l)
    f = wg.shape[1]
    final = final_g is not None
    mod_spec = _mod_spec(d, 0)
    in_specs = [pl.BlockSpec((1, tm, d), lambda i, j: (i, j, 0)),
                mod_spec, mod_spec, mod_spec,
                _resident((1, d)), _resident((d, f)), _resident((d, f)), _resident((f, d))]
    args = [x, shift, scale, gate, g, wg, wu, wd]
    if final:
        in_specs.append(_resident((1, d)))
        args.append(final_g)
    return pl.pallas_call(
        functools.partial(_ffn_kernel, f_chunk=f_chunk, final=final),
        grid=(b, l // tm),
        in_specs=in_specs,
        out_specs=pl.BlockSpec((1, tm, d), lambda i, j: (i, j, 0)),
        out_shape=jax.ShapeDtypeStruct((b, l, d), F32),
        compiler_params=_params(("parallel", "parallel")),
        name="ffn_step",
    )(*args)


def _modproj_kernel(x_ref, sh_ref, sc_ref, g_ref, *rest, n_w):
    w_refs, o_refs = rest[:n_w], rest[n_w:]
    u = _modulated(x_ref[0], g_ref[...], sh_ref[0, 0], sc_ref[0, 0])
    if n_w == 0:
        o_refs[0][0] = u
        return
    u = u.astype(BF16)
    for w_ref, o_ref in zip(w_refs, o_refs):
        o_ref[0] = jnp.dot(u, w_ref[...], preferred_element_type=F32)


def modulate_project(x, shift, scale, g, weights, tm=256, n0_tiles=0):
    b, l, d = x.shape
    tm = min(tm, l)
    mod_spec = _mod_spec(d, n0_tiles)
    in_specs = [pl.BlockSpec((1, tm, d), lambda i, j: (i, j, 0)), mod_spec, mod_spec, _resident((1, d))]
    in_specs += [_resident(w.shape) for w in weights]
    widths = [w.shape[1] for w in weights] or [d]
    return pl.pallas_call(
        functools.partial(_modproj_kernel, n_w=len(weights)),
        grid=(b, l // tm),
        in_specs=in_specs,
        out_specs=[pl.BlockSpec((1, tm, n), lambda i, j: (i, j, 0)) for n in widths],
        out_shape=[jax.ShapeDtypeStruct((b, l, n), F32) for n in widths],
        compiler_params=_params(("parallel", "parallel")),
        name="modulate_project",
    )(x, shift, scale, g, *weights)


def _outproj_kernel(x_ref, a_ref, gt_ref, w_ref, o_ref):
    y = jnp.dot(a_ref[0].astype(BF16), w_ref[...], preferred_element_type=F32)
    o_ref[0] = x_ref[0] + gt_ref[0, 0] * y


def project_residual(x, a, gate, w, tm=512):
    b, l, d = x.shape
    k = a.shape[-1]
    tm = min(tm, l)
    return pl.pallas_call(
        _outproj_kernel,
        grid=(b, l // tm),
        in_specs=[pl.BlockSpec((1, tm, d), lambda i, j: (i, j, 0)),
                  pl.BlockSpec((1, tm, k), lambda i, j: (i, j, 0)),
                  _mod_spec(d, 0),
                  _resident((k, d))],
        out_specs=pl.BlockSpec((1, tm, d), lambda i, j: (i, j, 0)),
        out_shape=jax.ShapeDtypeStruct((b, l, d), F32),
        compiler_params=_params(("parallel", "parallel")),
        name="project_residual",
    )(x, a, gate, w)


def _dwconv_kernel(prev_ref, x_ref, next_ref, w_ref, b_ref, o_ref, *, k_w, n_rows, seg_rows, silu):
    tr, c = x_ref.shape[1], x_ref.shape[2]
    p = k_w // 2
    ext = jnp.concatenate([prev_ref[0], x_ref[0], next_ref[0]], axis=0)
    row = pl.program_id(2) * tr + lax.broadcasted_iota(jnp.int32, (tr, c), 0)
    acc = jnp.broadcast_to(b_ref[...], (tr, c))
    for k in range(k_w):
        off = k - p
        tap = ext[SUBLANES + off:SUBLANES + off + tr, :]
        if off != 0:
            src = row + off
            valid = (src >= 0) & (src < n_rows)
            if seg_rows:
                lo, hi = (src, row) if off < 0 else (row, src)
                valid = valid & jnp.logical_not((lo < seg_rows) & (hi >= seg_rows))
            tap = jnp.where(valid, tap, 0.0)
        acc = acc + w_ref[k:k + 1, :] * tap
    o_ref[0] = _silu(acc) if silu else acc


def dwconv_tokens(x, w, bias, seg_rows=0, silu=False, tr=256, ct=512):
    b, t, c = x.shape
    k_w = w.shape[0]
    tr = min(tr, t)
    ct = min(ct, c)
    rb = tr // SUBLANES
    last_halo = t // SUBLANES - 1
    return pl.pallas_call(
        functools.partial(_dwconv_kernel, k_w=k_w, n_rows=t, seg_rows=seg_rows, silu=silu),
        grid=(b, c // ct, t // tr),
        in_specs=[pl.BlockSpec((1, SUBLANES, ct), lambda i, j, r: (i, jnp.maximum(r * rb - 1, 0), j)),
                  pl.BlockSpec((1, tr, ct), lambda i, j, r: (i, r, j)),
                  pl.BlockSpec((1, SUBLANES, ct), lambda i, j, r: (i, jnp.minimum((r + 1) * rb, last_halo), j)),
                  pl.BlockSpec((k_w, ct), lambda i, j, r: (0, j)),
                  pl.BlockSpec((1, ct), lambda i, j, r: (0, j))],
        out_specs=pl.BlockSpec((1, tr, ct), lambda i, j, r: (i, r, j)),
        out_shape=jax.ShapeDtypeStruct((b, t, c), F32),
        compiler_params=_params(("parallel", "parallel", "parallel")),
        name="dwconv_tokens",
    )(x, x, x, w, bias.reshape(1, c))


def _ssd_scan_kernel(xbc_ref, dt_ref, dtt_ref, dtb_ref, dtbt_ref, a_ref, at_ref, ex_ref, o_ref, h_ref):
    nh, q, hp = SSD_N_HEADS, SSD_CHUNK, SSD_HEAD_DIM
    d = pl.program_id(1)
    fwd = d == 0

    @pl.when(pl.program_id(2) == 0)
    def _():
        h_ref[...] = jnp.zeros(h_ref.shape, F32)

    dt2 = jax.nn.softplus(dt_ref[0] + dtb_ref[...])
    dt = jnp.where(fwd, dt2[:, :nh], dt2[:, nh:])
    a = dt * jnp.where(fwd, a_ref[:, :nh], a_ref[:, nh:])
    dtt2 = jax.nn.softplus(dtt_ref[0] + dtbt_ref[...])
    a_t = jnp.where(fwd, dtt2[:nh], dtt2[nh:]) * jnp.where(fwd, at_ref[:nh], at_ref[nh:])

    r_i = lax.broadcasted_iota(jnp.int32, (q, q), 0)
    c_i = lax.broadcasted_iota(jnp.int32, (q, q), 1)
    ahead = jnp.where(fwd, c_i - r_i, r_i - c_i)
    seen = ahead <= 0
    seen_t = ahead >= 0
    cum = jnp.dot(seen.astype(F32), a, preferred_element_type=F32, precision=HIGHEST)
    cum_t = jnp.dot(a_t, seen_t.astype(F32), preferred_element_type=F32, precision=HIGHEST)
    total = jnp.where(fwd, cum[q - 1:q, :], cum[0:1, :])

    def widen(v):
        return jnp.dot(v, ex_ref[...], preferred_element_type=F32, precision=HIGHEST)

    dt_x = widen(dt)
    dec_x = widen(jnp.exp(total - cum))
    ecum_x = widen(jnp.exp(cum))
    etot_x = jnp.where(fwd, ecum_x[q - 1:q, :], ecum_x[0:1, :])

    xs = xbc_ref[0, :, 0:SSD_D_INNER]
    xdt = xs * dt_x
    xdec = (xdt * dec_x).astype(BF16)
    xdt = xdt.astype(BF16)
    gw = SSD_GROUP_WIDTH
    for g in range(SSD_N_GROUPS):
        b_g = xbc_ref[0, :, SSD_D_INNER + g * SSD_D_STATE:SSD_D_INNER + (g + 1) * SSD_D_STATE].astype(BF16)
        c_lo = SSD_D_INNER + SSD_GN + g * SSD_D_STATE
        c_g = xbc_ref[0, :, c_lo:c_lo + SSD_D_STATE].astype(BF16)
        cb = lax.dot_general(c_g, b_g, (((1,), (1,)), ((), ())), preferred_element_type=F32)
        h_g = h_ref[g]
        y_off = jnp.dot(c_g, h_g.astype(BF16), preferred_element_type=F32) * ecum_x[:, g * gw:(g + 1) * gw]
        ys = []
        for e in range(SSD_HEADS_PER_GROUP):
            hd = g * SSD_HEADS_PER_GROUP + e
            seg = cum[:, hd:hd + 1] - cum_t[hd:hd + 1, :]
            w = (cb * jnp.exp(jnp.where(seen, seg, -jnp.inf))).astype(BF16)
            ys.append(jnp.dot(w, xdt[:, hd * hp:(hd + 1) * hp], preferred_element_type=F32))
        o_ref[0, 0, :, g * gw:(g + 1) * gw] = jnp.concatenate(ys, axis=1) + y_off
        s_g = lax.dot_general(b_g, xdec[:, g * gw:(g + 1) * gw], (((0,), (0,)), ((), ())),
                              preferred_element_type=F32)
        h_ref[g] = h_g * etot_x[:, g * gw:(g + 1) * gw] + s_g


def ssd_scan(xbc, dt_raw, dt_bias, a_log, n_lead_chunks):
    b, t, _ = xbc.shape
    q, nh = SSD_CHUNK, SSD_N_HEADS
    nc = t // q
    dt_t = jnp.swapaxes(dt_raw, 1, 2)
    a_neg = -jnp.exp(a_log.astype(F32)).reshape(1, 2 * nh)
    dtb = dt_bias.astype(F32).reshape(1, 2 * nh)
    expand = jnp.repeat(jnp.eye(nh, dtype=F32), SSD_HEAD_DIM, axis=1)

    def chunk_of(d, c):
        back = jnp.where(c < n_lead_chunks, n_lead_chunks - 1 - c, nc - 1 + n_lead_chunks - c)
        return jnp.where(d == 0, c, back)

    return pl.pallas_call(
        _ssd_scan_kernel,
        grid=(b, 2, nc),
        in_specs=[pl.BlockSpec((1, q, SSD_CONV_DIM), lambda i, d, c: (i, chunk_of(d, c), 0)),
                  pl.BlockSpec((1, q, 2 * nh), lambda i, d, c: (i, chunk_of(d, c), 0)),
                  pl.BlockSpec((1, 2 * nh, q), lambda i, d, c: (i, 0, chunk_of(d, c))),
                  _resident((1, 2 * nh)), _resident((2 * nh, 1)),
                  _resident((1, 2 * nh)), _resident((2 * nh, 1)),
                  _resident((nh, SSD_D_INNER))],
        out_specs=pl.BlockSpec((1, 1, q, SSD_D_INNER), lambda i, d, c: (i, d, chunk_of(d, c), 0)),
        out_shape=jax.ShapeDtypeStruct((b, 2, t, SSD_D_INNER), F32),
        scratch_shapes=[pltpu.VMEM((SSD_N_GROUPS, SSD_D_STATE, SSD_GROUP_WIDTH), F32)],
        compiler_params=_params(("parallel", "parallel", "arbitrary")),
        name="ssd_scan",
    )(xbc, dt_raw, dt_t, dtb, dtb.reshape(2 * nh, 1), a_neg, a_neg.reshape(2 * nh, 1), expand)


def _ssd_out_kernel(x_ref, y0_ref, y1_ref, xs0_ref, xs1_ref, z_ref, dsk_ref, ng_ref, gt_ref, w_ref, o_ref):
    xs = jnp.concatenate([xs0_ref[0], xs1_ref[0]], axis=1)
    y = y0_ref[0, 0] + y1_ref[0, 0] + dsk_ref[...] * xs
    gy = y * _silu(z_ref[0])
    parts = []
    for g in range(SSD_N_GROUPS):
        blk = gy[:, g * SSD_GROUP_WIDTH:(g + 1) * SSD_GROUP_WIDTH]
        parts.append(blk * lax.rsqrt(jnp.mean(blk * blk, axis=-1, keepdims=True) + EPS))
    a = (jnp.concatenate(parts, axis=1) * ng_ref[...]).astype(BF16)
    o_ref[0] = x_ref[0] + gt_ref[0, 0] * jnp.dot(a, w_ref[...], preferred_element_type=F32)


def ssd_gate_project(x, y2, xbc, z, d_skip, norm_g, gate, w_out, n0_tiles, tm=256):
    b, t, d = x.shape
    half = SSD_D_INNER // 2
    dsk = jnp.repeat(d_skip.astype(F32), SSD_HEAD_DIM).reshape(1, SSD_D_INNER)
    return pl.pallas_call(
        _ssd_out_kernel,
        grid=(b, t // tm),
        in_specs=[pl.BlockSpec((1, tm, d), lambda i, j: (i, j, 0)),
                  pl.BlockSpec((1, 1, tm, SSD_D_INNER), lambda i, j: (i, 0, j, 0)),
                  pl.BlockSpec((1, 1, tm, SSD_D_INNER), lambda i, j: (i, 1, j, 0)),
                  pl.BlockSpec((1, tm, half), lambda i, j: (i, j, 0)),
                  pl.BlockSpec((1, tm, half), lambda i, j: (i, j, 1)),
                  pl.BlockSpec((1, tm, SSD_D_INNER), lambda i, j: (i, j, 0)),
                  _resident((1, SSD_D_INNER)), _resident((1, SSD_D_INNER)),
                  _mod_spec(d, n0_tiles),
                  _resident((SSD_D_INNER, d))],
        out_specs=pl.BlockSpec((1, tm, d), lambda i, j: (i, j, 0)),
        out_shape=jax.ShapeDtypeStruct((b, t, d), F32),
        compiler_params=_params(("parallel", "parallel")),
        name="ssd_gate_project",
    )(x, y2, y2, xbc, xbc, z, dsk, norm_g.astype(F32).reshape(1, SSD_D_INNER), gate, w_out)


def ssd_mixer(x, h, m_lat, m_ctx, g1, w_in, conv_w, conv_b, a_log, dt_bias, d_skip, norm_g, w_out, tm=256):
    lc = h.shape[1]
    xx = jnp.concatenate([h, x], axis=1)
    sh, sc, gt = (jnp.concatenate([mc, ml], axis=1) for mc, ml in zip(m_ctx, m_lat))
    n0 = lc // tm
    w_bf = w_in.astype(BF16)
    ws = [w_bf[:, :SSD_D_INNER], w_bf[:, SSD_D_INNER:SSD_D_INNER + SSD_CONV_DIM], w_bf[:, SSD_D_INNER + SSD_CONV_DIM:]]
    z, xbc_raw, dt_raw = modulate_project(xx, sh, sc, g1, ws, tm=tm, n0_tiles=n0)
    xbc = dwconv_tokens(xbc_raw, conv_w, conv_b, seg_rows=lc, silu=True)
    y2 = ssd_scan(xbc, dt_raw, dt_bias, a_log, lc // SSD_CHUNK)
    out = ssd_gate_project(xx, y2, xbc, z, d_skip, norm_g, gt, w_out.astype(BF16), n0, tm=tm)
    return out[:, lc:], out[:, :lc]


def _dwconv_centred(u, w, b):
    k_w = w.shape[0]
    p = k_w // 2
    l = u.shape[1]
    up = jnp.pad(u, ((0, 0), (p, p), (0, 0)))
    out = b
    for k in range(k_w):
        out = out + w[k] * up[:, k:k + l]
    return out


def _box_sum(v, w, axis):
    l = v.shape[axis]
    cs = jnp.cumsum(v, axis=axis)
    pad = [(0, 0)] * v.ndim
    pad[axis] = (1, 0)
    cs = jnp.pad(cs, pad)
    pos = jnp.arange(l)
    hi = jnp.minimum(pos + (w - w // 2), l)
    lo = jnp.maximum(pos - w // 2, 0)
    s = jnp.take(cs, hi, axis=axis) - jnp.take(cs, lo, axis=axis)
    return s, (hi - lo).astype(F32)


def _pool_grid(v, w):
    b, l, c = v.shape
    rows = l // GRID_W
    v4 = v.reshape(b, rows, GRID_W, c)
    s, cnt_c = _box_sum(v4, w, 2)
    s, cnt_r = _box_sum(s, w, 1)
    mean = s / (cnt_r[:, None, None] * cnt_c[None, :, None])
    return mean.reshape(b, l, c) - v


def _pool_seq(v, w):
    s, cnt = _box_sum(v, w, 1)
    return s / cnt[None, :, None] - v


def _pool_core(u, pool_fn):
    b, l, _ = u.shape
    grp = u.reshape(b, l, POOL_GROUPS, POOL_GROUP_DIM)
    pooled = jnp.stack([pool_fn(grp[:, :, g], POOL_WINDOWS[g]) for g in range(POOL_GROUPS)], axis=2)
    return pooled.reshape(b, l, D_MODEL)


SLAB_PAD = 8


def _round_up(v, m):
    return (v + m - 1) // m * m


def _fft_plan(l):
    n = 2 * l
    na = 1 << (n.bit_length() // 2)
    nb = n // na
    ns = na // 2 + 1
    return dict(l=l, n=n, na=na, nb=nb, ns=ns, nsp=_round_up(ns + 1, SUBLANES), pitch=2 * nb + SLAB_PAD,
                n_pairs=(ns + 1) // 2)


def _fft_tables(l):
    import numpy as np
    p = _fft_plan(l)
    n, na, nb, ns, nsp = p["n"], p["na"], p["nb"], p["ns"], p["nsp"]
    half = na // 2
    ka = np.arange(ns)[None, :, None]
    tb = np.arange(nb)[:, None, None]

    def stage1(ta):
        th = 2.0 * np.pi * (ta[None, None, :] * ka / na + tb * ka / n)
        m = np.zeros((nb, 2 * nsp, ta.shape[0]))
        m[:, :ns] = np.cos(th)
        m[:, nsp:nsp + ns] = -np.sin(th)
        return m

    f1 = stage1(np.arange(half))
    f1k = np.zeros((nb, 2 * nsp, na))
    f1k[:, :, :half] = f1
    f1k[1:, :, half:] = stage1(na - 1 - np.arange(half))[1:]
    tb0 = stage1(na - np.arange(half))[0]
    tb0[:, 0] = 0.0
    f1k[0, :, half:] = tb0

    k2 = np.arange(nb)
    ang = 2.0 * np.pi * np.outer(k2, k2) / nb
    c, s = np.cos(ang), np.sin(ang)
    f2 = np.block([[c, s], [-s, c]])
    g2 = np.block([[c, -s], [s, c]])

    ta = np.arange(half)[None, :, None]
    kk = np.arange(ns)[None, None, :]
    tbb = np.arange(nb)[:, None, None]
    ph = 2.0 * np.pi * (ta * kk / na + tbb * kk / n)
    wgt = np.where((kk == 0) | (kk == na // 2), 1.0, 2.0) / n
    g1 = np.zeros((nb, half, 2 * nsp))
    g1[:, :, :ns] = wgt * np.cos(ph)
    g1[:, :, nsp:nsp + ns] = -wgt * np.sin(ph)
    f32 = np.float32
    return p, f1.astype(f32), f1k.astype(f32), f2.astype(f32), g2.astype(f32), g1.astype(f32)


def _fft_stage1(gather, f1_ref, s_ref, p):
    nb, nsp, pitch = p["nb"], p["nsp"], p["pitch"]

    def body(tb, carry):
        a = jnp.dot(f1_ref[tb], gather(tb), preferred_element_type=F32)
        s_ref[pl.ds(tb, nsp, stride=pitch), :] = a[:nsp]
        s_ref[pl.ds(nb + tb, nsp, stride=pitch), :] = a[nsp:]
        return carry

    lax.fori_loop(0, nb, body, 0)


def _slab_pair(s_ref, i, p):
    nb, pitch = p["nb"], p["pitch"]
    r0 = pl.multiple_of(2 * i * pitch, SUBLANES)
    r1 = pl.multiple_of(2 * i * pitch + pitch, SUBLANES)
    return jnp.concatenate([s_ref[pl.ds(r0, 2 * nb), :], s_ref[pl.ds(r1, 2 * nb), :]], axis=1)


def _filter_mlp_kernel(z_ref, w1_ref, b1_ref, w2_ref, b2_ref, fr_ref, w3f_ref, w3b_ref, dl_ref, o_ref):
    z = z_ref[...]
    h = jnp.sin(fr_ref[0:1, :] * (jnp.dot(z, w1_ref[...], preferred_element_type=F32, precision=HIGHEST) + b1_ref[...]))
    h = jnp.sin(fr_ref[1:2, :] * (jnp.dot(h, w2_ref[...], preferred_element_type=F32, precision=HIGHEST) + b2_ref[...]))
    decay = jnp.exp(-z[:, 0:1] * dl_ref[...])
    hf = jnp.dot(h, w3f_ref[...], preferred_element_type=F32, precision=HIGHEST) * decay
    hb = jnp.dot(h, w3b_ref[...], preferred_element_type=F32, precision=HIGHEST) * decay
    scale = lax.rsqrt(jnp.sum(hf * hf + hb * hb, axis=0, keepdims=True) + EPS)
    o_ref[0, 0] = hf * scale
    o_ref[0, 1] = hb * scale


def hyena_filters(l, fw1, fb1, fw2, fb2, fw3, ffreq, tn=LANES):
    d = D_MODEL
    pos = jnp.arange(l, dtype=F32)
    t = jnp.linspace(0.0, 1.0, l, dtype=F32)
    wpos = 2.0 * math.pi * pos / l
    f = jnp.linspace(1e-4, HY_BANDS - 1, HY_BANDS, dtype=F32)
    ang = wpos[:, None] * f[None, :]
    emb_pad = _round_up(HY_EMB_DIM, SUBLANES)
    z = jnp.concatenate([t[:, None], jnp.cos(ang), -jnp.sin(ang), jnp.zeros((l, emb_pad - HY_EMB_DIM), F32)], axis=-1)
    w1 = jnp.concatenate([fw1.astype(F32), jnp.zeros((emb_pad - HY_EMB_DIM, fw1.shape[1]), F32)], axis=0)
    deltas = jnp.abs(jnp.linspace(HY_MIN_DECAY, HY_MAX_DECAY, d, dtype=F32)).reshape(1, d)
    hid = fw2.shape[0]
    nt = d // tn
    w3 = fw3.astype(F32)
    return pl.pallas_call(
        _filter_mlp_kernel,
        grid=(HY_ORDER, nt),
        in_specs=[_resident((l, emb_pad)), _resident((emb_pad, hid)), _resident((1, hid)), _resident((hid, hid)),
                  _resident((1, hid)), _resident((2, hid)),
                  pl.BlockSpec((hid, tn), lambda o, j: (0, (2 * o) * nt + j)),
                  pl.BlockSpec((hid, tn), lambda o, j: (0, (2 * o + 1) * nt + j)),
                  pl.BlockSpec((1, tn), lambda o, j: (0, j))],
        out_specs=pl.BlockSpec((1, 2, l, tn), lambda o, j: (o, 0, 0, j)),
        out_shape=jax.ShapeDtypeStruct((HY_ORDER, 2, l, d), F32),
        compiler_params=_params(("parallel", "parallel")),
        name="hyena_filters",
    )(z, w1, fb1.astype(F32).reshape(1, hid), fw2.astype(F32), fb2.astype(F32).reshape(1, hid), ffreq.astype(F32), w3, w3, deltas)


def _filter_spectrum_kernel(hf_ref, hb_ref, f1_ref, f2_ref, o_ref, s_ref, *, p):
    na, nb = p["na"], p["nb"]
    half = na // 2

    def gather(tb):
        fwd = hf_ref[pl.ds(tb, half, stride=nb), :]
        bwd = hb_ref[pl.ds(jnp.where(tb == 0, 0, nb - tb), half, stride=nb), :]
        return jnp.concatenate([fwd, bwd], axis=0).astype(BF16)

    _fft_stage1(gather, f1_ref, s_ref, p)

    def body(i, carry):
        spec = jnp.dot(f2_ref[...], _slab_pair(s_ref, i, p).astype(BF16), preferred_element_type=F32)
        r = pl.multiple_of(i * 4 * nb, SUBLANES)
        o_ref[pl.ds(r, 2 * nb), :] = spec[:, :LANES].astype(BF16)
        o_ref[pl.ds(r + 2 * nb, 2 * nb), :] = spec[:, LANES:].astype(BF16)
        return carry

    lax.fori_loop(0, p["n_pairs"], body, 0)


def hyena_filter_spectrum(filt, tables):
    p, _, f1k, f2, _, _ = tables
    order, _, l, d = filt.shape
    rows = 2 * p["n_pairs"] * 2 * p["nb"]
    return pl.pallas_call(
        functools.partial(_filter_spectrum_kernel, p=p),
        grid=(order, d // LANES),
        in_specs=[pl.BlockSpec((None, None, l, LANES), lambda o, j: (o, 0, 0, j)),
                  pl.BlockSpec((None, None, l, LANES), lambda o, j: (o, 1, 0, j)),
                  _resident(f1k.shape), _resident(f2.shape)],
        out_specs=pl.BlockSpec((None, rows, LANES), lambda o, j: (o, 0, j)),
        out_shape=jax.ShapeDtypeStruct((order, rows, d), BF16),
        scratch_shapes=[pltpu.VMEM((p["nsp"] * p["pitch"], LANES), F32)],
        compiler_params=_params(("parallel", "parallel")),
        name="hyena_filter_spectrum",
    )(filt, filt, jnp.asarray(f1k).astype(BF16), jnp.asarray(f2).astype(BF16))


def _longconv_kernel(a_ref, m_ref, k_ref, bias_ref, f1_ref, f2_ref, g2_ref, g1_ref, o_ref, s_ref, *, p, row_chunk):
    na, nb, nsp, pitch = p["na"], p["nb"], p["nsp"], p["pitch"]
    half = na // 2

    _fft_stage1(lambda tb: a_ref[pl.ds(tb, half, stride=nb), :].astype(BF16), f1_ref, s_ref, p)

    def mid(i, carry):
        x = jnp.dot(f2_ref[...], _slab_pair(s_ref, i, p).astype(BF16), preferred_element_type=F32)
        r = pl.multiple_of(i * 4 * nb, SUBLANES)
        kk = jnp.concatenate([k_ref[pl.ds(r, 2 * nb), :], k_ref[pl.ds(r + 2 * nb, 2 * nb), :]], axis=1).astype(F32)
        xr, xi, kr, ki = x[:nb], x[nb:], kk[:nb], kk[nb:]
        y = jnp.concatenate([xr * kr - xi * ki, xr * ki + xi * kr], axis=0).astype(BF16)
        bq = jnp.dot(g2_ref[...], y, preferred_element_type=F32)
        r0 = pl.multiple_of(2 * i * pitch, SUBLANES)
        r1 = pl.multiple_of(2 * i * pitch + pitch, SUBLANES)
        s_ref[pl.ds(r0, 2 * nb), :] = bq[:, :LANES]
        s_ref[pl.ds(r1, 2 * nb), :] = bq[:, LANES:]
        return carry

    lax.fori_loop(0, p["n_pairs"], mid, 0)

    def last(tb, carry):
        bq = jnp.concatenate([s_ref[pl.ds(tb, nsp, stride=pitch), :], s_ref[pl.ds(nb + tb, nsp, stride=pitch), :]], axis=0)
        o_ref[pl.ds(tb, half, stride=nb), :] = jnp.dot(g1_ref[tb], bq.astype(BF16), preferred_element_type=F32)
        return carry

    lax.fori_loop(0, nb, last, 0)

    def finish(i, carry):
        r = pl.multiple_of(i * row_chunk, SUBLANES)
        a = a_ref[pl.ds(r, row_chunk), :]
        o_ref[pl.ds(r, row_chunk), :] = m_ref[pl.ds(r, row_chunk), :] * (o_ref[pl.ds(r, row_chunk), :] + bias_ref[...] * a)
        return carry

    lax.fori_loop(0, p["l"] // row_chunk, finish, 0)


def hyena_longconv(a, a_col, m, m_col, kspec, bias, tables):
    p, f1, _, f2, g2, g1 = tables
    b, l, _ = a.shape
    d = D_MODEL
    nt = d // LANES
    rows = kspec.shape[0]
    tabs = [jnp.asarray(t).astype(BF16) for t in (f1, f2, g2, g1)]
    return pl.pallas_call(
        functools.partial(_longconv_kernel, p=p, row_chunk=min(512, l)),
        grid=(nt, b),
        in_specs=[pl.BlockSpec((None, l, LANES), lambda j, i: (i, 0, a_col * nt + j)),
                  pl.BlockSpec((None, l, LANES), lambda j, i: (i, 0, m_col * nt + j)),
                  pl.BlockSpec((rows, LANES), lambda j, i: (0, j)),
                  pl.BlockSpec((1, LANES), lambda j, i: (0, j))] + [_resident(t.shape) for t in tabs],
        out_specs=pl.BlockSpec((None, l, LANES), lambda j, i: (i, 0, j)),
        out_shape=jax.ShapeDtypeStruct((b, l, d), F32),
        scratch_shapes=[pltpu.VMEM((p["nsp"] * p["pitch"], LANES), F32)],
        compiler_params=_params(("parallel", "parallel")),
        name="hyena_longconv",
    )(a, m, kspec, bias.astype(F32).reshape(1, d), *tabs)


def hyena_core(p_raw, conv_w, conv_b, fw1, fb1, fw2, fb2, fw3, ffreq, hbias):
    l = p_raw.shape[1]
    tables = _fft_tables(l)
    filt = hyena_filters(l, fw1, fb1, fw2, fb2, fw3, ffreq)
    kspec = hyena_filter_spectrum(filt, tables)
    pc = dwconv_tokens(p_raw, conv_w, conv_b)
    z = hyena_longconv(pc, 0, pc, 1, kspec[0], hbias[0], tables)
    return hyena_longconv(z, 0, pc, 2, kspec[1], hbias[1], tables)


def kernel(x, c, ctx, c_ctx, ada_w, ada_b, norm_g, ffn_w_gate, ffn_w_up, ffn_w_down, ssd_w_in, ssd_conv_w, ssd_conv_b, ssd_a_log, ssd_dt_bias, ssd_d, ssd_norm_g, ssd_w_out, pool_w, pool_b, pool_scale, hy_w_in, hy_conv_w, hy_conv_b, hy_filt_w1, hy_filt_b1, hy_filt_w2, hy_filt_b2, hy_filt_w3, hy_filt_freq, hy_bias, hy_w_out, final_g):
    batch = x.shape[0]
    d = D_MODEL
    h = ctx

    s = jnp.concatenate([jax.nn.silu(c), jax.nn.silu(c_ctx)[None], jnp.zeros((7 - batch, d), F32)], axis=0)
    mods = ada_modulation(s, ada_w, ada_b).reshape(DEPTH, 8, N_MOD, d)

    wg_bf, wu_bf, wd_bf = ffn_w_gate.astype(BF16), ffn_w_up.astype(BF16), ffn_w_down.astype(BF16)
    fg = final_g.reshape(1, d)

    for i in range(DEPTH):
        kind = i % N_MIXERS
        j = i // N_MIXERS
        last = i == DEPTH - 1
        ctx_in_needed = (not last) or kind == 0
        m = [mods[i, :batch, k].reshape(batch, 1, 1, d) for k in range(N_MOD)]
        mc = [jnp.broadcast_to(mods[i, batch, k].reshape(1, 1, 1, d), (batch, 1, 1, d)) for k in range(N_MOD)]
        g0, g1, g2 = (norm_g[i, k].reshape(1, d) for k in range(3))

        x = ffn_step(x, m[0], m[1], m[2], g0, wg_bf[i, 0], wu_bf[i, 0], wd_bf[i, 0])
        if ctx_in_needed:
            h = ffn_step(h, mc[0], mc[1], mc[2], g0, wg_bf[i, 0], wu_bf[i, 0], wd_bf[i, 0])

        if kind == 0:
            x, h_new = ssd_mixer(x, h, (m[3], m[4], m[5]), (mc[3], mc[4], mc[5]), g1, ssd_w_in[j], ssd_conv_w[j],
                                 ssd_conv_b[j], ssd_a_log[j], ssd_dt_bias[j], ssd_d[j], ssd_norm_g[j], ssd_w_out[j])
            if not last:
                h = h_new
        elif kind == 1:
            (u,) = modulate_project(x, m[3], m[4], g1, [])
            pooled = _pool_core(u, _pool_grid)
            y = jnp.einsum('blgc,gcd->blgd', pooled.reshape(batch, -1, POOL_GROUPS, POOL_GROUP_DIM), pool_w[j]) + pool_b[j]
            x = x + m[5][:, 0] * (y.reshape(batch, -1, d) * pool_scale[j])
            if not last:
                (uc,) = modulate_project(h, mc[3], mc[4], g1, [])
                pooled_c = _pool_core(uc, _pool_seq)
                yc = jnp.einsum('blgc,gcd->blgd', pooled_c.reshape(batch, -1, POOL_GROUPS, POOL_GROUP_DIM), pool_w[j]) + pool_b[j]
                h = h + mc[5][:, 0] * (yc.reshape(batch, -1, d) * pool_scale[j])
        else:
            w_in = hy_w_in[j].astype(BF16)
            w_out = hy_w_out[j].astype(BF16)
            filt = (hy_filt_w1[j], hy_filt_b1[j], hy_filt_w2[j], hy_filt_b2[j], hy_filt_w3[j], hy_filt_freq[j])
            (p_lat,) = modulate_project(x, m[3], m[4], g1, [w_in])
            x = project_residual(x, hyena_core(p_lat, hy_conv_w[j], hy_conv_b[j], *filt, hy_bias[j]), m[5], w_out)
            if not last:
                (p_ctx,) = modulate_project(h, mc[3], mc[4], g1, [w_in])
                h = project_residual(h, hyena_core(p_ctx, hy_conv_w[j], hy_conv_b[j], *filt, hy_bias[j]), mc[5], w_out)

        x = ffn_step(x, m[6], m[7], m[8], g2, wg_bf[i, 1], wu_bf[i, 1], wd_bf[i, 1],
                     final_g=fg if last else None)
        if not last:
            h = ffn_step(h, mc[6], mc[7], mc[8], g2, wg_bf[i, 1], wu_bf[i, 1], wd_bf[i, 1])
    return x
```

```python
import functools
import math

import jax
import jax.numpy as jnp
from jax import lax
from jax.experimental import pallas as pl
from jax.experimental.pallas import tpu as pltpu

F32 = jnp.float32
BF16 = jnp.bfloat16
HIGHEST = lax.Precision.HIGHEST

D_MODEL = 1024
DEPTH = 4
GRID_W = 64
N_MIXERS = 3
D_FF = 2816
N_MOD = 9
EPS = 1e-6

SSD_D_INNER = 2 * D_MODEL
SSD_HEAD_DIM = 64
SSD_N_HEADS = SSD_D_INNER // SSD_HEAD_DIM
SSD_N_GROUPS = 4
SSD_HEADS_PER_GROUP = SSD_N_HEADS // SSD_N_GROUPS
SSD_D_STATE = 128
SSD_CHUNK = 128
SSD_GN = SSD_N_GROUPS * SSD_D_STATE
SSD_CONV_DIM = SSD_D_INNER + 2 * SSD_GN
SSD_GROUP_WIDTH = SSD_D_INNER // SSD_N_GROUPS

POOL_WINDOWS = (2, 4, 8, 16)
POOL_GROUPS = 4
POOL_GROUP_DIM = D_MODEL // POOL_GROUPS

HY_ORDER = 2
HY_EMB_DIM = 33
HY_BANDS = (HY_EMB_DIM - 1) // 2
HY_MAX_DECAY = math.log(1e-2) / 0.3
HY_MIN_DECAY = math.log(1e-2) / 1.5

VMEM_LIMIT_BYTES = 56 * 1024 * 1024
SUBLANES = 8
LANES = 128


def _params(sem):
    return pltpu.CompilerParams(dimension_semantics=sem, vmem_limit_bytes=VMEM_LIMIT_BYTES)


def _resident(shape):
    nd = len(shape)
    return pl.BlockSpec(shape, lambda *_: (0,) * nd, pipeline_mode=pl.Buffered(1))


def _mod_spec(d, n0_tiles):
    if n0_tiles == 0:
        return pl.BlockSpec((1, 1, 1, d), lambda i, j: (i, 0, 0, 0))
    return pl.BlockSpec((1, 1, 1, d), lambda i, j: (i, jnp.where(j < n0_tiles, 0, 1), 0, 0))


def _modulated(x, g, shift, scale):
    ms = jnp.mean(x * x, axis=-1, keepdims=True)
    return (x * lax.rsqrt(ms + EPS)) * g * (1.0 + scale) + shift


def _silu(v):
    return v * jax.nn.sigmoid(v)


def _ada_kernel(s_ref, w_ref, b_ref, o_ref):
    o_ref[0] = jnp.dot(s_ref[...], w_ref[0], preferred_element_type=F32, precision=HIGHEST) + b_ref[0]


def ada_modulation(s, ada_w, ada_b, tn=1024):
    depth, d, n = ada_w.shape
    r = s.shape[0]
    return pl.pallas_call(
        _ada_kernel,
        grid=(depth, n // tn),
        in_specs=[pl.BlockSpec((r, d), lambda i, j: (0, 0)),
                  pl.BlockSpec((1, d, tn), lambda i, j: (i, 0, j)),
                  pl.BlockSpec((1, 1, tn), lambda i, j: (i, 0, j))],
        out_specs=pl.BlockSpec((1, r, tn), lambda i, j: (i, 0, j)),
        out_shape=jax.ShapeDtypeStruct((depth, r, n), F32),
        compiler_params=_params(("parallel", "parallel")),
        name="ada_modulation",
    )(s, ada_w, ada_b.reshape(depth, 1, n))


def _ffn_kernel(x_ref, sh_ref, sc_ref, gt_ref, g_ref, wg_ref, wu_ref, wd_ref, *rest, f_chunk, final):
    if final:
        fg_ref, o_ref = rest
    else:
        (o_ref,) = rest
    x = x_ref[0]
    u = _modulated(x, g_ref[...], sh_ref[0, 0], sc_ref[0, 0]).astype(BF16)
    acc = jnp.zeros(x.shape, F32)
    d_ff = wg_ref.shape[1]
    for f0 in range(0, d_ff, f_chunk):
        a = jnp.dot(u, wg_ref[:, f0:f0 + f_chunk], preferred_element_type=F32)
        b = jnp.dot(u, wu_ref[:, f0:f0 + f_chunk], preferred_element_type=F32)
        h = (_silu(a) * b).astype(BF16)
        acc = acc + jnp.dot(h, wd_ref[f0:f0 + f_chunk, :], preferred_element_type=F32)
    y = x + (0.5 * gt_ref[0, 0]) * acc
    if final:
        ms = jnp.mean(y * y, axis=-1, keepdims=True)
        y = y * lax.rsqrt(ms + EPS) * fg_ref[...]
    o_ref[0] = y


def ffn_step(x, shift, scale, gate, g, wg, wu, wd, final_g=None, tm=512, f_chunk=256):
    b, l, d = x.shape
    tm = min(tm, l)
    f = wg.shape[1]
    final = final_g is not None
    mod_spec = _mod_spec(d, 0)
    in_specs = [pl.BlockSpec((1, tm, d), lambda i, j: (i, j, 0)),
                mod_spec, mod_spec, mod_spec,
                _resident((1, d)), _resident((d, f)), _resident((d, f)), _resident((f, d))]
    args = [x, shift, scale, gate, g, wg, wu, wd]
    if final:
        in_specs.append(_resident((1, d)))
        args.append(final_g)
    return pl.pallas_call(
        functools.partial(_ffn_kernel, f_chunk=f_chunk, final=final),
        grid=(b, l // tm),
        in_specs=in_specs,
        out_specs=pl.BlockSpec((1, tm, d), lambda i, j: (i, j, 0)),
        out_shape=jax.ShapeDtypeStruct((b, l, d), F32),
        compiler_params=_params(("parallel", "parallel")),
        name="ffn_step",
    )(*args)


def _modproj_kernel(x_ref, sh_ref, sc_ref, g_ref, *rest, n_w):
    w_refs, o_refs = rest[:n_w], rest[n_w:]
    u = _modulated(x_ref[0], g_ref[...], sh_ref[0, 0], sc_ref[0, 0])
    if n_w == 0:
        o_refs[0][0] = u
        return
    u = u.astype(BF16)
    for w_ref, o_ref in zip(w_refs, o_refs):
        o_ref[0] = jnp.dot(u, w_ref[...], preferred_element_type=F32)


def modulate_project(x, shift, scale, g, weights, tm=256, n0_tiles=0):
    b, l, d = x.shape
    tm = min(tm, l)
    mod_spec = _mod_spec(d, n0_tiles)
    in_specs = [pl.BlockSpec((1, tm, d), lambda i, j: (i, j, 0)), mod_spec, mod_spec, _resident((1, d))]
    in_specs += [_resident(w.shape) for w in weights]
    widths = [w.shape[1] for w in weights] or [d]
    return pl.pallas_call(
        functools.partial(_modproj_kernel, n_w=len(weights)),
        grid=(b, l // tm),
        in_specs=in_specs,
        out_specs=[pl.BlockSpec((1, tm, n), lambda i, j: (i, j, 0)) for n in widths],
        out_shape=[jax.ShapeDtypeStruct((b, l, n), F32) for n in widths],
        compiler_params=_params(("parallel", "parallel")),
        name="modulate_project",
    )(x, shift, scale, g, *weights)


def _outproj_kernel(x_ref, a_ref, gt_ref, w_ref, o_ref):
    y = jnp.dot(a_ref[0].astype(BF16), w_ref[...], preferred_element_type=F32)
    o_ref[0] = x_ref[0] + gt_ref[0, 0] * y


def project_residual(x, a, gate, w, tm=512):
    b, l, d = x.shape
    k = a.shape[-1]
    tm = min(tm, l)
    return pl.pallas_call(
        _outproj_kernel,
        grid=(b, l // tm),
        in_specs=[pl.BlockSpec((1, tm, d), lambda i, j: (i, j, 0)),
                  pl.BlockSpec((1, tm, k), lambda i, j: (i, j, 0)),
                  _mod_spec(d, 0),
                  _resident((k, d))],
        out_specs=pl.BlockSpec((1, tm, d), lambda i, j: (i, j, 0)),
        out_shape=jax.ShapeDtypeStruct((b, l, d), F32),
        compiler_params=_params(("parallel", "parallel")),
        name="project_residual",
    )(x, a, gate, w)


def _dwconv_kernel(prev_ref, x_ref, next_ref, w_ref, b_ref, o_ref, *, k_w, n_rows, seg_rows, silu):
    tr, c = x_ref.shape[1], x_ref.shape[2]
    p = k_w // 2
    ext = jnp.concatenate([prev_ref[0], x_ref[0], next_ref[0]], axis=0)
    row = pl.program_id(2) * tr + lax.broadcasted_iota(jnp.int32, (tr, c), 0)
    acc = jnp.broadcast_to(b_ref[...], (tr, c))
    for k in range(k_w):
        off = k - p
        tap = ext[SUBLANES + off:SUBLANES + off + tr, :]
        if off != 0:
            src = row + off
            valid = (src >= 0) & (src < n_rows)
            if seg_rows:
                lo, hi = (src, row) if off < 0 else (row, src)
                valid = valid & jnp.logical_not((lo < seg_rows) & (hi >= seg_rows))
            tap = jnp.where(valid, tap, 0.0)
        acc = acc + w_ref[k:k + 1, :] * tap
    o_ref[0] = _silu(acc) if silu else acc


def dwconv_tokens(x, w, bias, seg_rows=0, silu=False, tr=256, ct=512):
    b, t, c = x.shape
    k_w = w.shape[0]
    tr = min(tr, t)
    ct = min(ct, c)
    rb = tr // SUBLANES
    last_halo = t // SUBLANES - 1
    return pl.pallas_call(
        functools.partial(_dwconv_kernel, k_w=k_w, n_rows=t, seg_rows=seg_rows, silu=silu),
        grid=(b, c // ct, t // tr),
        in_specs=[pl.BlockSpec((1, SUBLANES, ct), lambda i, j, r: (i, jnp.maximum(r * rb - 1, 0), j)),
                  pl.BlockSpec((1, tr, ct), lambda i, j, r: (i, r, j)),
                  pl.BlockSpec((1, SUBLANES, ct), lambda i, j, r: (i, jnp.minimum((r + 1) * rb, last_halo), j)),
                  pl.BlockSpec((k_w, ct), lambda i, j, r: (0, j)),
                  pl.BlockSpec((1, ct), lambda i, j, r: (0, j))],
        out_specs=pl.BlockSpec((1, tr, ct), lambda i, j, r: (i, r, j)),
        out_shape=jax.ShapeDtypeStruct((b, t, c), F32),
        compiler_params=_params(("parallel", "parallel", "parallel")),
        name="dwconv_tokens",
    )(x, x, x, w, bias.reshape(1, c))


def _ssd_scan_kernel(xbc_ref, dt_ref, dtt_ref, dtb_ref, dtbt_ref, a_ref, at_ref, ex_ref, o_ref, h_ref):
    nh, q, hp = SSD_N_HEADS, SSD_CHUNK, SSD_HEAD_DIM
    d = pl.program_id(1)
    fwd = d == 0

    @pl.when(pl.program_id(2) == 0)
    def _():
        h_ref[...] = jnp.zeros(h_ref.shape, F32)

    dt2 = jax.nn.softplus(dt_ref[0] + dtb_ref[...])
    dt = jnp.where(fwd, dt2[:, :nh], dt2[:, nh:])
    a = dt * jnp.where(fwd, a_ref[:, :nh], a_ref[:, nh:])
    dtt2 = jax.nn.softplus(dtt_ref[0] + dtbt_ref[...])
    a_t = jnp.where(fwd, dtt2[:nh], dtt2[nh:]) * jnp.where(fwd, at_ref[:nh], at_ref[nh:])

    r_i = lax.broadcasted_iota(jnp.int32, (q, q), 0)
    c_i = lax.broadcasted_iota(jnp.int32, (q, q), 1)
    ahead = jnp.where(fwd, c_i - r_i, r_i - c_i)
    seen = ahead <= 0
    seen_t = ahead >= 0
    cum = jnp.dot(seen.astype(F32), a, preferred_element_type=F32, precision=HIGHEST)
    cum_t = jnp.dot(a_t, seen_t.astype(F32), preferred_element_type=F32, precision=HIGHEST)
    total = jnp.where(fwd, cum[q - 1:q, :], cum[0:1, :])

    def widen(v):
        return jnp.dot(v, ex_ref[...], preferred_element_type=F32, precision=HIGHEST)

    dt_x = widen(dt)
    dec_x = widen(jnp.exp(total - cum))
    ecum_x = widen(jnp.exp(cum))
    etot_x = jnp.where(fwd, ecum_x[q - 1:q, :], ecum_x[0:1, :])

    xs = xbc_ref[0, :, 0:SSD_D_INNER]
    xdt = xs * dt_x
    xdec = (xdt * dec_x).astype(BF16)
    xdt = xdt.astype(BF16)
    gw = SSD_GROUP_WIDTH
    for g in range(SSD_N_GROUPS):
        b_g = xbc_ref[0, :, SSD_D_INNER + g * SSD_D_STATE:SSD_D_INNER + (g + 1) * SSD_D_STATE].astype(BF16)
        c_lo = SSD_D_INNER + SSD_GN + g * SSD_D_STATE
        c_g = xbc_ref[0, :, c_lo:c_lo + SSD_D_STATE].astype(BF16)
        cb = lax.dot_general(c_g, b_g, (((1,), (1,)), ((), ())), preferred_element_type=F32)
        h_g = h_ref[g]
        y_off = jnp.dot(c_g, h_g.astype(BF16), preferred_element_type=F32) * ecum_x[:, g * gw:(g + 1) * gw]
        ys = []
        for e in range(SSD_HEADS_PER_GROUP):
            hd = g * SSD_HEADS_PER_GROUP + e
            seg = cum[:, hd:hd + 1] - cum_t[hd:hd + 1, :]
            w = (cb * jnp.exp(jnp.where(seen, seg, -jnp.inf))).astype(BF16)
            ys.append(jnp.dot(w, xdt[:, hd * hp:(hd + 1) * hp], preferred_element_type=F32))
        o_ref[0, 0, :, g * gw:(g + 1) * gw] = jnp.concatenate(ys, axis=1) + y_off
        s_g = lax.dot_general(b_g, xdec[:, g * gw:(g + 1) * gw], (((0,), (0,)), ((), ())),
                              preferred_element_type=F32)
        h_ref[g] = h_g * etot_x[:, g * gw:(g + 1) * gw] + s_g


def ssd_scan(xbc, dt_raw, dt_bias, a_log, n_lead_chunks):
    b, t, _ = xbc.shape
    q, nh = SSD_CHUNK, SSD_N_HEADS
    nc = t // q
    dt_t = jnp.swapaxes(dt_raw, 1, 2)
    a_neg = -jnp.exp(a_log.astype(F32)).reshape(1, 2 * nh)
    dtb = dt_bias.astype(F32).reshape(1, 2 * nh)
    expand = jnp.repeat(jnp.eye(nh, dtype=F32), SSD_HEAD_DIM, axis=1)

    def chunk_of(d, c):
        back = jnp.where(c < n_lead_chunks, n_lead_chunks - 1 - c, nc - 1 + n_lead_chunks - c)
        return jnp.where(d == 0, c, back)

    return pl.pallas_call(
        _ssd_scan_kernel,
        grid=(b, 2, nc),
        in_specs=[pl.BlockSpec((1, q, SSD_CONV_DIM), lambda i, d, c: (i, chunk_of(d, c), 0)),
                  pl.BlockSpec((1, q, 2 * nh), lambda i, d, c: (i, chunk_of(d, c), 0)),
                  pl.BlockSpec((1, 2 * nh, q), lambda i, d, c: (i, 0, chunk_of(d, c))),
                  _resident((1, 2 * nh)), _resident((2 * nh, 1)),
                  _resident((1, 2 * nh)), _resident((2 * nh, 1)),
                  _resident((nh, SSD_D_INNER))],
        out_specs=pl.BlockSpec((1, 1, q, SSD_D_INNER), lambda i, d, c: (i, d, chunk_of(d, c), 0)),
        out_shape=jax.ShapeDtypeStruct((b, 2, t, SSD_D_INNER), F32),
        scratch_shapes=[pltpu.VMEM((SSD_N_GROUPS, SSD_D_STATE, SSD_GROUP_WIDTH), F32)],
        compiler_params=_params(("parallel", "parallel", "arbitrary")),
        name="ssd_scan",
    )(xbc, dt_raw, dt_t, dtb, dtb.reshape(2 * nh, 1), a_neg, a_neg.reshape(2 * nh, 1), expand)


def _ssd_out_kernel(x_ref, y0_ref, y1_ref, xs0_ref, xs1_ref, z_ref, dsk_ref, ng_ref, gt_ref, w_ref, o_ref):
    xs = jnp.concatenate([xs0_ref[0], xs1_ref[0]], axis=1)
    y = y0_ref[0, 0] + y1_ref[0, 0] + dsk_ref[...] * xs
    gy = y * _silu(z_ref[0])
    parts = []
    for g in range(SSD_N_GROUPS):
        blk = gy[:, g * SSD_GROUP_WIDTH:(g + 1) * SSD_GROUP_WIDTH]
        parts.append(blk * lax.rsqrt(jnp.mean(blk * blk, axis=-1, keepdims=True) + EPS))
    a = (jnp.concatenate(parts, axis=1) * ng_ref[...]).astype(BF16)
    o_ref[0] = x_ref[0] + gt_ref[0, 0] * jnp.dot(a, w_ref[...], preferred_element_type=F32)


def ssd_gate_project(x, y2, xbc, z, d_skip, norm_g, gate, w_out, n0_tiles, tm=256):
    b, t, d = x.shape
    half = SSD_D_INNER // 2
    dsk = jnp.repeat(d_skip.astype(F32), SSD_HEAD_DIM).reshape(1, SSD_D_INNER)
    return pl.pallas_call(
        _ssd_out_kernel,
        grid=(b, t // tm),
        in_specs=[pl.BlockSpec((1, tm, d), lambda i, j: (i, j, 0)),
                  pl.BlockSpec((1, 1, tm, SSD_D_INNER), lambda i, j: (i, 0, j, 0)),
                  pl.BlockSpec((1, 1, tm, SSD_D_INNER), lambda i, j: (i, 1, j, 0)),
                  pl.BlockSpec((1, tm, half), lambda i, j: (i, j, 0)),
                  pl.BlockSpec((1, tm, half), lambda i, j: (i, j, 1)),
                  pl.BlockSpec((1, tm, SSD_D_INNER), lambda i, j: (i, j, 0)),
                  _resident((1, SSD_D_INNER)), _resident((1, SSD_D_INNER)),
                  _mod_spec(d, n0_tiles),
                  _resident((SSD_D_INNER, d))],
        out_specs=pl.BlockSpec((1, tm, d), lambda i, j: (i, j, 0)),
        out_shape=jax.ShapeDtypeStruct((b, t, d), F32),
        compiler_params=_params(("parallel", "parallel")),
        name="ssd_gate_project",
    )(x, y2, y2, xbc, xbc, z, dsk, norm_g.astype(F32).reshape(1, SSD_D_INNER), gate, w_out)


def ssd_mixer(x, h, m_lat, m_ctx, g1, w_in, conv_w, conv_b, a_log, dt_bias, d_skip, norm_g, w_out, tm=256):
    lc = h.shape[1]
    xx = jnp.concatenate([h, x], axis=1)
    sh, sc, gt = (jnp.concatenate([mc, ml], axis=1) for mc, ml in zip(m_ctx, m_lat))
    n0 = lc // tm
    w_bf = w_in.astype(BF16)
    ws = [w_bf[:, :SSD_D_INNER], w_bf[:, SSD_D_INNER:SSD_D_INNER + SSD_CONV_DIM], w_bf[:, SSD_D_INNER + SSD_CONV_DIM:]]
    z, xbc_raw, dt_raw = modulate_project(xx, sh, sc, g1, ws, tm=tm, n0_tiles=n0)
    xbc = dwconv_tokens(xbc_raw, conv_w, conv_b, seg_rows=lc, silu=True)
    y2 = ssd_scan(xbc, dt_raw, dt_bias, a_log, lc // SSD_CHUNK)
    out = ssd_gate_project(xx, y2, xbc, z, d_skip, norm_g, gt, w_out.astype(BF16), n0, tm=tm)
    return out[:, lc:], out[:, :lc]


POOL_CHUNK = 256
POOL_MAX_HALF = max(POOL_WINDOWS) // 2


def _pool_band_tables(width):
    import numpy as np
    pos = np.arange(POOL_CHUNK)
    line, col = pos // width, pos % width
    out = []
    for w in POOL_WINDOWS:
        inside = (col[None, :] >= col[:, None] - w // 2) & (col[None, :] <= col[:, None] + (w - w // 2) - 1)
        out.append((inside & (line[None, :] == line[:, None])).astype(np.float32))
    return np.stack(out)


def _pool_kernel(u_ref, band_ref, o_ref, cs_ref, *, width, n_lines):
    l, c = u_ref.shape
    n_chunks = l // POOL_CHUNK
    assert POOL_WINDOWS == tuple(2 << g for g in range(POOL_GROUPS)) and width & (width - 1) == 0
    half = lax.shift_left(jnp.int32(1), pl.program_id(1) // (POOL_GROUP_DIM // LANES))
    log_w = width.bit_length() - 1
    pad = POOL_MAX_HALF * width if n_lines > 1 else 0
    if pad:
        cs_ref[0:pad, :] = jnp.zeros((pad, c), F32)
        cs_ref[pad + l:2 * pad + l, :] = jnp.zeros((pad, c), F32)

    def col_pass(i, carry):
        r = pl.multiple_of(i * POOL_CHUNK, POOL_CHUNK)
        u = u_ref[pl.ds(r, POOL_CHUNK), :]
        hi = u.astype(BF16)
        lo = (u - hi.astype(F32)).astype(BF16)
        cs_ref[pl.ds(pad + r, POOL_CHUNK), :] = (jnp.dot(band_ref[0], hi, preferred_element_type=F32)
                                                 + jnp.dot(band_ref[0], lo, preferred_element_type=F32))
        return carry

    lax.fori_loop(0, n_chunks, col_pass, 0)

    def out_pass(i, carry):
        r = pl.multiple_of(i * POOL_CHUNK, POOL_CHUNK)
        pos = r + lax.broadcasted_iota(jnp.int32, (POOL_CHUNK, c), 0)
        col = pos & (width - 1)
        cnt = jnp.minimum(col + half, width) - jnp.maximum(col - half, 0)
        if n_lines > 1:
            def add_line(k, acc):
                return acc + cs_ref[pl.ds(pl.multiple_of(pad + r + k * width, SUBLANES), POOL_CHUNK), :]
            s = lax.fori_loop(-half, half, add_line, jnp.zeros((POOL_CHUNK, c), F32))
            line = lax.shift_right_logical(pos, log_w)
            cnt_l = jnp.minimum(line + half, n_lines) - jnp.maximum(line - half, 0)
            mean = s / (cnt_l.astype(F32) * cnt.astype(F32))
        else:
            mean = cs_ref[pl.ds(r, POOL_CHUNK), :] / cnt.astype(F32)
        o_ref[pl.ds(r, POOL_CHUNK), :] = (mean - u_ref[pl.ds(r, POOL_CHUNK), :]).astype(o_ref.dtype)
        return carry

    lax.fori_loop(0, n_chunks, out_pass, 0)


def pool_tokens(u, width):
    b, l, d = u.shape
    n_lines = l // width
    bands = jnp.asarray(_pool_band_tables(width)).astype(BF16)
    tiles_per_group = POOL_GROUP_DIM // LANES
    pad = POOL_MAX_HALF * width if n_lines > 1 else 0
    return pl.pallas_call(
        functools.partial(_pool_kernel, width=width, n_lines=n_lines),
        grid=(b, d // LANES),
        in_specs=[pl.BlockSpec((None, l, LANES), lambda i, j: (i, 0, j)),
                  pl.BlockSpec((1, POOL_CHUNK, POOL_CHUNK), lambda i, j: (j // tiles_per_group, 0, 0))],
        out_specs=pl.BlockSpec((None, l, LANES), lambda i, j: (i, 0, j)),
        out_shape=jax.ShapeDtypeStruct((b, l, d), BF16),
        scratch_shapes=[pltpu.VMEM((l + 2 * pad, LANES), F32)],
        compiler_params=_params(("parallel", "parallel")),
        name="pool_tokens",
    )(u, bands)


def _pool_mix_kernel(x_ref, p_ref, w_ref, b_ref, sc_ref, gt_ref, o_ref):
    ys = [jnp.dot(p_ref[0, :, g * POOL_GROUP_DIM:(g + 1) * POOL_GROUP_DIM], w_ref[g], preferred_element_type=F32)
          for g in range(POOL_GROUPS)]
    y = (jnp.concatenate(ys, axis=1) + b_ref[...]) * sc_ref[...]
    o_ref[0] = x_ref[0] + gt_ref[0, 0] * y


def pool_mix_residual(x, pooled, w, bias, scale, gate, tm=512):
    b, l, d = x.shape
    tm = min(tm, l)
    return pl.pallas_call(
        _pool_mix_kernel,
        grid=(b, l // tm),
        in_specs=[pl.BlockSpec((1, tm, d), lambda i, j: (i, j, 0)),
                  pl.BlockSpec((1, tm, d), lambda i, j: (i, j, 0)),
                  _resident(w.shape), _resident((1, d)), _resident((1, d)), _mod_spec(d, 0)],
        out_specs=pl.BlockSpec((1, tm, d), lambda i, j: (i, j, 0)),
        out_shape=jax.ShapeDtypeStruct((b, l, d), F32),
        compiler_params=_params(("parallel", "parallel")),
        name="pool_mix_residual",
    )(x, pooled, w.astype(BF16), bias.astype(F32).reshape(1, d), scale.astype(F32).reshape(1, d), gate)


def pool_mixer(x, shift, scale_mod, gate, g1, w, bias, scale, width):
    (u,) = modulate_project(x, shift, scale_mod, g1, [])
    return pool_mix_residual(x, pool_tokens(u, width), w, bias, scale, gate)


SLAB_PAD = 8


def _round_up(v, m):
    return (v + m - 1) // m * m


def _fft_plan(l):
    n = 2 * l
    na = 1 << (n.bit_length() // 2)
    nb = n // na
    ns = na // 2 + 1
    return dict(l=l, n=n, na=na, nb=nb, ns=ns, nsp=_round_up(ns + 1, SUBLANES), pitch=2 * nb + SLAB_PAD,
                n_pairs=(ns + 1) // 2)


def _fft_tables(l):
    import numpy as np
    p = _fft_plan(l)
    n, na, nb, ns, nsp = p["n"], p["na"], p["nb"], p["ns"], p["nsp"]
    half = na // 2
    ka = np.arange(ns)[None, :, None]
    tb = np.arange(nb)[:, None, None]

    def stage1(ta):
        th = 2.0 * np.pi * (ta[None, None, :] * ka / na + tb * ka / n)
        m = np.zeros((nb, 2 * nsp, ta.shape[0]))
        m[:, :ns] = np.cos(th)
        m[:, nsp:nsp + ns] = -np.sin(th)
        return m

    f1 = stage1(np.arange(half))
    f1k = np.zeros((nb, 2 * nsp, na))
    f1k[:, :, :half] = f1
    f1k[1:, :, half:] = stage1(na - 1 - np.arange(half))[1:]
    tb0 = stage1(na - np.arange(half))[0]
    tb0[:, 0] = 0.0
    f1k[0, :, half:] = tb0

    k2 = np.arange(nb)
    ang = 2.0 * np.pi * np.outer(k2, k2) / nb
    c, s = np.cos(ang), np.sin(ang)
    f2 = np.block([[c, s], [-s, c]])
    g2 = np.block([[c, -s], [s, c]])

    ta = np.arange(half)[None, :, None]
    kk = np.arange(ns)[None, None, :]
    tbb = np.arange(nb)[:, None, None]
    ph = 2.0 * np.pi * (ta * kk / na + tbb * kk / n)
    wgt = np.where((kk == 0) | (kk == na // 2), 1.0, 2.0) / n
    g1 = np.zeros((nb, half, 2 * nsp))
    g1[:, :, :ns] = wgt * np.cos(ph)
    g1[:, :, nsp:nsp + ns] = -wgt * np.sin(ph)
    f32 = np.float32
    return p, f1.astype(f32), f1k.astype(f32), f2.astype(f32), g2.astype(f32), g1.astype(f32)


def _fft_stage1(gather, f1_ref, s_ref, p):
    nb, nsp, pitch = p["nb"], p["nsp"], p["pitch"]

    def body(tb, carry):
        a = jnp.dot(f1_ref[tb], gather(tb), preferred_element_type=F32)
        s_ref[pl.ds(tb, nsp, stride=pitch), :] = a[:nsp]
        s_ref[pl.ds(nb + tb, nsp, stride=pitch), :] = a[nsp:]
        return carry

    lax.fori_loop(0, nb, body, 0)


def _slab_pair(s_ref, i, p):
    nb, pitch = p["nb"], p["pitch"]
    r0 = pl.multiple_of(2 * i * pitch, SUBLANES)
    r1 = pl.multiple_of(2 * i * pitch + pitch, SUBLANES)
    return jnp.concatenate([s_ref[pl.ds(r0, 2 * nb), :], s_ref[pl.ds(r1, 2 * nb), :]], axis=1)


def _filter_mlp_kernel(z_ref, w1_ref, b1_ref, w2_ref, b2_ref, fr_ref, w3f_ref, w3b_ref, dl_ref, o_ref):
    z = z_ref[...]
    h = jnp.sin(fr_ref[0:1, :] * (jnp.dot(z, w1_ref[...], preferred_element_type=F32, precision=HIGHEST) + b1_ref[...]))
    h = jnp.sin(fr_ref[1:2, :] * (jnp.dot(h, w2_ref[...], preferred_element_type=F32, precision=HIGHEST) + b2_ref[...]))
    decay = jnp.exp(-z[:, 0:1] * dl_ref[...])
    hf = jnp.dot(h, w3f_ref[...], preferred_element_type=F32, precision=HIGHEST) * decay
    hb = jnp.dot(h, w3b_ref[...], preferred_element_type=F32, precision=HIGHEST) * decay
    scale = lax.rsqrt(jnp.sum(hf * hf + hb * hb, axis=0, keepdims=True) + EPS)
    o_ref[0, 0] = hf * scale
    o_ref[0, 1] = hb * scale


def hyena_filters(l, fw1, fb1, fw2, fb2, fw3, ffreq, tn=LANES):
    d = D_MODEL
    pos = jnp.arange(l, dtype=F32)
    t = jnp.linspace(0.0, 1.0, l, dtype=F32)
    wpos = 2.0 * math.pi * pos / l
    f = jnp.linspace(1e-4, HY_BANDS - 1, HY_BANDS, dtype=F32)
    ang = wpos[:, None] * f[None, :]
    emb_pad = _round_up(HY_EMB_DIM, SUBLANES)
    z = jnp.concatenate([t[:, None], jnp.cos(ang), -jnp.sin(ang), jnp.zeros((l, emb_pad - HY_EMB_DIM), F32)], axis=-1)
    w1 = jnp.concatenate([fw1.astype(F32), jnp.zeros((emb_pad - HY_EMB_DIM, fw1.shape[1]), F32)], axis=0)
    deltas = jnp.abs(jnp.linspace(HY_MIN_DECAY, HY_MAX_DECAY, d, dtype=F32)).reshape(1, d)
    hid = fw2.shape[0]
    nt = d // tn
    w3 = fw3.astype(F32)
    return pl.pallas_call(
        _filter_mlp_kernel,
        grid=(HY_ORDER, nt),
        in_specs=[_resident((l, emb_pad)), _resident((emb_pad, hid)), _resident((1, hid)), _resident((hid, hid)),
                  _resident((1, hid)), _resident((2, hid)),
                  pl.BlockSpec((hid, tn), lambda o, j: (0, (2 * o) * nt + j)),
                  pl.BlockSpec((hid, tn), lambda o, j: (0, (2 * o + 1) * nt + j)),
                  pl.BlockSpec((1, tn), lambda o, j: (0, j))],
        out_specs=pl.BlockSpec((1, 2, l, tn), lambda o, j: (o, 0, 0, j)),
        out_shape=jax.ShapeDtypeStruct((HY_ORDER, 2, l, d), F32),
        compiler_params=_params(("parallel", "parallel")),
        name="hyena_filters",
    )(z, w1, fb1.astype(F32).reshape(1, hid), fw2.astype(F32), fb2.astype(F32).reshape(1, hid), ffreq.astype(F32), w3, w3, deltas)


def _filter_spectrum_kernel(hf_ref, hb_ref, f1_ref, f2_ref, o_ref, s_ref, *, p):
    na, nb = p["na"], p["nb"]
    half = na // 2

    def gather(tb):
        fwd = hf_ref[pl.ds(tb, half, stride=nb), :]
        bwd = hb_ref[pl.ds(jnp.where(tb == 0, 0, nb - tb), half, stride=nb), :]
        return jnp.concatenate([fwd, bwd], axis=0).astype(BF16)

    _fft_stage1(gather, f1_ref, s_ref, p)

    def body(i, carry):
        spec = jnp.dot(f2_ref[...], _slab_pair(s_ref, i, p).astype(BF16), preferred_element_type=F32)
        r = pl.multiple_of(i * 4 * nb, SUBLANES)
        o_ref[pl.ds(r, 2 * nb), :] = spec[:, :LANES].astype(BF16)
        o_ref[pl.ds(r + 2 * nb, 2 * nb), :] = spec[:, LANES:].astype(BF16)
        return carry

    lax.fori_loop(0, p["n_pairs"], body, 0)


def hyena_filter_spectrum(filt, tables):
    p, _, f1k, f2, _, _ = tables
    order, _, l, d = filt.shape
    rows = 2 * p["n_pairs"] * 2 * p["nb"]
    return pl.pallas_call(
        functools.partial(_filter_spectrum_kernel, p=p),
        grid=(order, d // LANES),
        in_specs=[pl.BlockSpec((None, None, l, LANES), lambda o, j: (o, 0, 0, j)),
                  pl.BlockSpec((None, None, l, LANES), lambda o, j: (o, 1, 0, j)),
                  _resident(f1k.shape), _resident(f2.shape)],
        out_specs=pl.BlockSpec((None, rows, LANES), lambda o, j: (o, 0, j)),
        out_shape=jax.ShapeDtypeStruct((order, rows, d), BF16),
        scratch_shapes=[pltpu.VMEM((p["nsp"] * p["pitch"], LANES), F32)],
        compiler_params=_params(("parallel", "parallel")),
        name="hyena_filter_spectrum",
    )(filt, filt, jnp.asarray(f1k).astype(BF16), jnp.asarray(f2).astype(BF16))


def _longconv_kernel(a_ref, m_ref, k_ref, bias_ref, f1_ref, f2_ref, g2_ref, g1_ref, o_ref, s_ref, *, p, row_chunk):
    na, nb, nsp, pitch = p["na"], p["nb"], p["nsp"], p["pitch"]
    half = na // 2

    _fft_stage1(lambda tb: a_ref[pl.ds(tb, half, stride=nb), :].astype(BF16), f1_ref, s_ref, p)

    def mid(i, carry):
        x = jnp.dot(f2_ref[...], _slab_pair(s_ref, i, p).astype(BF16), preferred_element_type=F32)
        r = pl.multiple_of(i * 4 * nb, SUBLANES)
        kk = jnp.concatenate([k_ref[pl.ds(r, 2 * nb), :], k_ref[pl.ds(r + 2 * nb, 2 * nb), :]], axis=1).astype(F32)
        xr, xi, kr, ki = x[:nb], x[nb:], kk[:nb], kk[nb:]
        y = jnp.concatenate([xr * kr - xi * ki, xr * ki + xi * kr], axis=0).astype(BF16)
        bq = jnp.dot(g2_ref[...], y, preferred_element_type=F32)
        r0 = pl.multiple_of(2 * i * pitch, SUBLANES)
        r1 = pl.multiple_of(2 * i * pitch + pitch, SUBLANES)
        s_ref[pl.ds(r0, 2 * nb), :] = bq[:, :LANES]
        s_ref[pl.ds(r1, 2 * nb), :] = bq[:, LANES:]
        return carry

    lax.fori_loop(0, p["n_pairs"], mid, 0)

    def last(tb, carry):
        bq = jnp.concatenate([s_ref[pl.ds(tb, nsp, stride=pitch), :], s_ref[pl.ds(nb + tb, nsp, stride=pitch), :]], axis=0)
        o_ref[pl.ds(tb, half, stride=nb), :] = jnp.dot(g1_ref[tb], bq.astype(BF16), preferred_element_type=F32)
        return carry

    lax.fori_loop(0, nb, last, 0)

    def finish(i, carry):
        r = pl.multiple_of(i * row_chunk, SUBLANES)
        a = a_ref[pl.ds(r, row_chunk), :]
        o_ref[pl.ds(r, row_chunk), :] = m_ref[pl.ds(r, row_chunk), :] * (o_ref[pl.ds(r, row_chunk), :] + bias_ref[...] * a)
        return carry

    lax.fori_loop(0, p["l"] // row_chunk, finish, 0)


def hyena_longconv(a, a_col, m, m_col, kspec, bias, tables):
    p, f1, _, f2, g2, g1 = tables
    b, l, _ = a.shape
    d = D_MODEL
    nt = d // LANES
    rows = kspec.shape[0]
    tabs = [jnp.asarray(t).astype(BF16) for t in (f1, f2, g2, g1)]
    return pl.pallas_call(
        functools.partial(_longconv_kernel, p=p, row_chunk=min(512, l)),
        grid=(nt, b),
        in_specs=[pl.BlockSpec((None, l, LANES), lambda j, i: (i, 0, a_col * nt + j)),
                  pl.BlockSpec((None, l, LANES), lambda j, i: (i, 0, m_col * nt + j)),
                  pl.BlockSpec((rows, LANES), lambda j, i: (0, j)),
                  pl.BlockSpec((1, LANES), lambda j, i: (0, j))] + [_resident(t.shape) for t in tabs],
        out_specs=pl.BlockSpec((None, l, LANES), lambda j, i: (i, 0, j)),
        out_shape=jax.ShapeDtypeStruct((b, l, d), F32),
        scratch_shapes=[pltpu.VMEM((p["nsp"] * p["pitch"], LANES), F32)],
        compiler_params=_params(("parallel", "parallel")),
        name="hyena_longconv",
    )(a, m, kspec, bias.astype(F32).reshape(1, d), *tabs)


def hyena_core(p_raw, conv_w, conv_b, fw1, fb1, fw2, fb2, fw3, ffreq, hbias):
    l = p_raw.shape[1]
    tables = _fft_tables(l)
    filt = hyena_filters(l, fw1, fb1, fw2, fb2, fw3, ffreq)
    kspec = hyena_filter_spectrum(filt, tables)
    pc = dwconv_tokens(p_raw, conv_w, conv_b)
    z = hyena_longconv(pc, 0, pc, 1, kspec[0], hbias[0], tables)
    return hyena_longconv(z, 0, pc, 2, kspec[1], hbias[1], tables)


def kernel(x, c, ctx, c_ctx, ada_w, ada_b, norm_g, ffn_w_gate, ffn_w_up, ffn_w_down, ssd_w_in, ssd_conv_w, ssd_conv_b, ssd_a_log, ssd_dt_bias, ssd_d, ssd_norm_g, ssd_w_out, pool_w, pool_b, pool_scale, hy_w_in, hy_conv_w, hy_conv_b, hy_filt_w1, hy_filt_b1, hy_filt_w2, hy_filt_b2, hy_filt_w3, hy_filt_freq, hy_bias, hy_w_out, final_g):
    batch = x.shape[0]
    d = D_MODEL
    h = ctx

    s = jnp.concatenate([jax.nn.silu(c), jax.nn.silu(c_ctx)[None], jnp.zeros((7 - batch, d), F32)], axis=0)
    mods = ada_modulation(s, ada_w, ada_b).reshape(DEPTH, 8, N_MOD, d)

    wg_bf, wu_bf, wd_bf = ffn_w_gate.astype(BF16), ffn_w_up.astype(BF16), ffn_w_down.astype(BF16)
    fg = final_g.reshape(1, d)

    for i in range(DEPTH):
        kind = i % N_MIXERS
        j = i // N_MIXERS
        last = i == DEPTH - 1
        ctx_in_needed = (not last) or kind == 0
        m = [mods[i, :batch, k].reshape(batch, 1, 1, d) for k in range(N_MOD)]
        mc = [jnp.broadcast_to(mods[i, batch, k].reshape(1, 1, 1, d), (batch, 1, 1, d)) for k in range(N_MOD)]
        g0, g1, g2 = (norm_g[i, k].reshape(1, d) for k in range(3))

        x = ffn_step(x, m[0], m[1], m[2], g0, wg_bf[i, 0], wu_bf[i, 0], wd_bf[i, 0])
        if ctx_in_needed:
            h = ffn_step(h, mc[0], mc[1], mc[2], g0, wg_bf[i, 0], wu_bf[i, 0], wd_bf[i, 0])

        if kind == 0:
            x, h_new = ssd_mixer(x, h, (m[3], m[4], m[5]), (mc[3], mc[4], mc[5]), g1, ssd_w_in[j], ssd_conv_w[j],
                                 ssd_conv_b[j], ssd_a_log[j], ssd_dt_bias[j], ssd_d[j], ssd_norm_g[j], ssd_w_out[j])
            if not last:
                h = h_new
        elif kind == 1:
            x = pool_mixer(x, m[3], m[4], m[5], g1, pool_w[j], pool_b[j].reshape(-1), pool_scale[j], GRID_W)
            if not last:
                h = pool_mixer(h, mc[3], mc[4], mc[5], g1, pool_w[j], pool_b[j].reshape(-1), pool_scale[j], h.shape[1])
        else:
            w_in = hy_w_in[j].astype(BF16)
            w_out = hy_w_out[j].astype(BF16)
            filt = (hy_filt_w1[j], hy_filt_b1[j], hy_filt_w2[j], hy_filt_b2[j], hy_filt_w3[j], hy_filt_freq[j])
            (p_lat,) = modulate_project(x, m[3], m[4], g1, [w_in])
            x = project_residual(x, hyena_core(p_lat, hy_conv_w[j], hy_conv_b[j], *filt, hy_bias[j]), m[5], w_out)
            if not last:
                (p_ctx,) = modulate_project(h, mc[3], mc[4], g1, [w_in])
                h = project_residual(h, hyena_core(p_ctx, hy_conv_w[j], hy_conv_b[j], *filt, hy_bias[j]), mc[5], w_out)

        x = ffn_step(x, m[6], m[7], m[8], g2, wg_bf[i, 1], wu_bf[i, 1], wd_bf[i, 1],
                     final_g=fg if last else None)
        if not last:
            h = ffn_step(h, mc[6], mc[7], mc[8], g2, wg_bf[i, 1], wu_bf[i, 1], wd_bf[i, 1])
    return x
```

```python
import functools
import math

import jax
import jax.numpy as jnp
from jax import lax
from jax.experimental import pallas as pl
from jax.experimental.pallas import tpu as pltpu

F32 = jnp.float32
BF16 = jnp.bfloat16
HIGHEST = lax.Precision.HIGHEST

D_MODEL = 1024
DEPTH = 4
GRID_W = 64
N_MIXERS = 3
D_FF = 2816
N_MOD = 9
EPS = 1e-6

SSD_D_INNER = 2 * D_MODEL
SSD_HEAD_DIM = 64
SSD_N_HEADS = SSD_D_INNER // SSD_HEAD_DIM
SSD_N_GROUPS = 4
SSD_HEADS_PER_GROUP = SSD_N_HEADS // SSD_N_GROUPS
SSD_D_STATE = 128
SSD_CHUNK = 128
SSD_GN = SSD_N_GROUPS * SSD_D_STATE
SSD_CONV_DIM = SSD_D_INNER + 2 * SSD_GN
SSD_GROUP_WIDTH = SSD_D_INNER // SSD_N_GROUPS

POOL_WINDOWS = (2, 4, 8, 16)
POOL_GROUPS = 4
POOL_GROUP_DIM = D_MODEL // POOL_GROUPS

HY_ORDER = 2
HY_EMB_DIM = 33
HY_BANDS = (HY_EMB_DIM - 1) // 2
HY_MAX_DECAY = math.log(1e-2) / 0.3
HY_MIN_DECAY = math.log(1e-2) / 1.5

VMEM_LIMIT_BYTES = 56 * 1024 * 1024
SUBLANES = 8
LANES = 128


def _params(sem):
    return pltpu.CompilerParams(dimension_semantics=sem, vmem_limit_bytes=VMEM_LIMIT_BYTES)


def _resident(shape):
    nd = len(shape)
    return pl.BlockSpec(shape, lambda *_: (0,) * nd, pipeline_mode=pl.Buffered(1))


def _mod_spec(d, n0_tiles):
    if n0_tiles == 0:
        return pl.BlockSpec((1, 1, 1, d), lambda i, j: (i, 0, 0, 0))
    return pl.BlockSpec((1, 1, 1, d), lambda i, j: (i, jnp.where(j < n0_tiles, 0, 1), 0, 0))


def _modulated(x, g, shift, scale):
    ms = jnp.mean(x * x, axis=-1, keepdims=True)
    return (x * lax.rsqrt(ms + EPS)) * g * (1.0 + scale) + shift


def _silu(v):
    return v * jax.nn.sigmoid(v)


def _ada_kernel(s_ref, w_ref, b_ref, o_ref):
    o_ref[0] = jnp.dot(s_ref[...], w_ref[0], preferred_element_type=F32, precision=HIGHEST) + b_ref[0]


def ada_modulation(s, ada_w, ada_b, tn=1024):
    depth, d, n = ada_w.shape
    r = s.shape[0]
    return pl.pallas_call(
        _ada_kernel,
        grid=(depth, n // tn),
        in_specs=[pl.BlockSpec((r, d), lambda i, j: (0, 0)),
                  pl.BlockSpec((1, d, tn), lambda i, j: (i, 0, j)),
                  pl.BlockSpec((1, 1, tn), lambda i, j: (i, 0, j))],
        out_specs=pl.BlockSpec((1, r, tn), lambda i, j: (i, 0, j)),
        out_shape=jax.ShapeDtypeStruct((depth, r, n), F32),
        compiler_params=_params(("parallel", "parallel")),
        name="ada_modulation",
    )(s, ada_w, ada_b.reshape(depth, 1, n))


def _ffn_kernel(x_ref, sh_ref, sc_ref, gt_ref, g_ref, wg_ref, wu_ref, wd_ref, *rest, f_chunk, final):
    if final:
        fg_ref, o_ref = rest
    else:
        (o_ref,) = rest
    x = x_ref[0]
    u = _modulated(x, g_ref[...], sh_ref[0, 0], sc_ref[0, 0]).astype(BF16)
    acc = jnp.zeros(x.shape, F32)
    d_ff = wg_ref.shape[1]
    for f0 in range(0, d_ff, f_chunk):
        a = jnp.dot(u, wg_ref[:, f0:f0 + f_chunk], preferred_element_type=F32)
        b = jnp.dot(u, wu_ref[:, f0:f0 + f_chunk], preferred_element_type=F32)
        h = (_silu(a) * b).astype(BF16)
        acc = acc + jnp.dot(h, wd_ref[f0:f0 + f_chunk, :], preferred_element_type=F32)
    y = x + (0.5 * gt_ref[0, 0]) * acc
    if final:
        ms = jnp.mean(y * y, axis=-1, keepdims=True)
        y = y * lax.rsqrt(ms + EPS) * fg_ref[...]
    o_ref[0] = y


def ffn_step(x, shift, scale, gate, g, wg, wu, wd, final_g=None, tm=512, f_chunk=256):
    b, l, d = x.shape
    tm = min(tm, l)
    f = wg.shape[1]
    final = final_g is not None
    mod_spec = _mod_spec(d, 0)
    in_specs = [pl.BlockSpec((1, tm, d), lambda i, j: (i, j, 0)),
                mod_spec, mod_spec, mod_spec,
                _resident((1, d)), _resident((d, f)), _resident((d, f)), _resident((f, d))]
    args = [x, shift, scale, gate, g, wg, wu, wd]
    if final:
        in_specs.append(_resident((1, d)))
        args.append(final_g)
    return pl.pallas_call(
        functools.partial(_ffn_kernel, f_chunk=f_chunk, final=final),
        grid=(b, l // tm),
        in_specs=in_specs,
        out_specs=pl.BlockSpec((1, tm, d), lambda i, j: (i, j, 0)),
        out_shape=jax.ShapeDtypeStruct((b, l, d), F32),
        compiler_params=_params(("parallel", "parallel")),
        name="ffn_step",
    )(*args)


def _modproj_kernel(x_ref, sh_ref, sc_ref, g_ref, *rest, n_w):
    w_refs, o_refs = rest[:n_w], rest[n_w:]
    u = _modulated(x_ref[0], g_ref[...], sh_ref[0, 0], sc_ref[0, 0])
    if n_w == 0:
        o_refs[0][0] = u
        return
    u = u.astype(BF16)
    for w_ref, o_ref in zip(w_refs, o_refs):
        o_ref[0] = jnp.dot(u, w_ref[...], preferred_element_type=F32)


def modulate_project(x, shift, scale, g, weights, tm=256, n0_tiles=0):
    b, l, d = x.shape
    tm = min(tm, l)
    mod_spec = _mod_spec(d, n0_tiles)
    in_specs = [pl.BlockSpec((1, tm, d), lambda i, j: (i, j, 0)), mod_spec, mod_spec, _resident((1, d))]
    in_specs += [_resident(w.shape) for w in weights]
    widths = [w.shape[1] for w in weights] or [d]
    return pl.pallas_call(
        functools.partial(_modproj_kernel, n_w=len(weights)),
        grid=(b, l // tm),
        in_specs=in_specs,
        out_specs=[pl.BlockSpec((1, tm, n), lambda i, j: (i, j, 0)) for n in widths],
        out_shape=[jax.ShapeDtypeStruct((b, l, n), F32) for n in widths],
        compiler_params=_params(("parallel", "parallel")),
        name="modulate_project",
    )(x, shift, scale, g, *weights)


def _outproj_kernel(x_ref, a_ref, gt_ref, w_ref, o_ref):
    y = jnp.dot(a_ref[0].astype(BF16), w_ref[...], preferred_element_type=F32)
    o_ref[0] = x_ref[0] + gt_ref[0, 0] * y


def project_residual(x, a, gate, w, tm=512):
    b, l, d = x.shape
    k = a.shape[-1]
    tm = min(tm, l)
    return pl.pallas_call(
        _outproj_kernel,
        grid=(b, l // tm),
        in_specs=[pl.BlockSpec((1, tm, d), lambda i, j: (i, j, 0)),
                  pl.BlockSpec((1, tm, k), lambda i, j: (i, j, 0)),
                  _mod_spec(d, 0),
                  _resident((k, d))],
        out_specs=pl.BlockSpec((1, tm, d), lambda i, j: (i, j, 0)),
        out_shape=jax.ShapeDtypeStruct((b, l, d), F32),
        compiler_params=_params(("parallel", "parallel")),
        name="project_residual",
    )(x, a, gate, w)


def _dwconv_kernel(prev_ref, x_ref, next_ref, w_ref, b_ref, o_ref, *, k_w, n_rows, seg_rows, silu):
    tr, c = x_ref.shape[1], x_ref.shape[2]
    p = k_w // 2
    act = _silu if silu else (lambda v: v)
    ext = jnp.concatenate([prev_ref[0], x_ref[0], next_ref[0]], axis=0)
    acc = jnp.broadcast_to(b_ref[...], (tr, c))
    for k in range(k_w):
        acc = acc + w_ref[k:k + 1, :] * ext[SUBLANES + k - p:SUBLANES + k - p + tr, :]
    o_ref[0] = act(acc)

    edges = sorted({0, n_rows} | ({seg_rows} if seg_rows else set()))
    n_tiles = n_rows // SUBLANES
    special = sorted({t for e in edges for t in ((e - p) // SUBLANES, (e + p - 1) // SUBLANES) if 0 <= t < n_tiles})
    tiles_per_block = tr // SUBLANES
    for tile in special:
        lo = (tile % tiles_per_block) * SUBLANES

        @pl.when(pl.program_id(2) == tile // tiles_per_block)
        def _(tile=tile, lo=lo):
            row = tile * SUBLANES + lax.broadcasted_iota(jnp.int32, (SUBLANES, c), 0)
            acc_t = jnp.broadcast_to(b_ref[...], (SUBLANES, c))
            for k in range(k_w):
                src = row + (k - p)
                valid = (src >= 0) & (src < n_rows)
                for e in edges[1:-1]:
                    valid = valid & jnp.logical_not((jnp.minimum(row, src) < e) & (jnp.maximum(row, src) >= e))
                tap = ext[SUBLANES + lo + k - p:2 * SUBLANES + lo + k - p, :]
                acc_t = acc_t + w_ref[k:k + 1, :] * jnp.where(valid, tap, 0.0)
            o_ref[0, lo:lo + SUBLANES, :] = act(acc_t)


def dwconv_tokens(x, w, bias, seg_rows=0, silu=False, tr=1024, ct=512):
    b, t, c = x.shape
    k_w = w.shape[0]
    tr = next(r for r in range(min(tr, t), 0, -SUBLANES) if t % r == 0)
    ct = min(ct, c)
    rb = tr // SUBLANES
    last_halo = t // SUBLANES - 1
    return pl.pallas_call(
        functools.partial(_dwconv_kernel, k_w=k_w, n_rows=t, seg_rows=seg_rows, silu=silu),
        grid=(b, c // ct, t // tr),
        in_specs=[pl.BlockSpec((1, SUBLANES, ct), lambda i, j, r: (i, jnp.maximum(r * rb - 1, 0), j)),
                  pl.BlockSpec((1, tr, ct), lambda i, j, r: (i, r, j)),
                  pl.BlockSpec((1, SUBLANES, ct), lambda i, j, r: (i, jnp.minimum((r + 1) * rb, last_halo), j)),
                  pl.BlockSpec((k_w, ct), lambda i, j, r: (0, j)),
                  pl.BlockSpec((1, ct), lambda i, j, r: (0, j))],
        out_specs=pl.BlockSpec((1, tr, ct), lambda i, j, r: (i, r, j)),
        out_shape=jax.ShapeDtypeStruct((b, t, c), F32),
        compiler_params=_params(("parallel", "parallel", "parallel")),
        name="dwconv_tokens",
    )(x, x, x, w, bias.reshape(1, c))


def _bf16_terms(v):
    hi = v.astype(BF16)
    r = v - hi.astype(F32)
    mid = r.astype(BF16)
    return hi, mid, (r - mid.astype(F32)).astype(BF16)


def _dot_f32_rhs(m, v):
    return sum(jnp.dot(m, t, preferred_element_type=F32) for t in _bf16_terms(v))


def _dot_f32_lhs(v, m):
    return sum(jnp.dot(t, m, preferred_element_type=F32) for t in _bf16_terms(v))


def _dot_split(a, b):
    a_hi, a_lo, _ = _bf16_terms(a)
    b_hi, b_lo, _ = _bf16_terms(b)
    return (jnp.dot(a_hi, b_hi, preferred_element_type=F32) + jnp.dot(a_lo, b_hi, preferred_element_type=F32)
            + jnp.dot(a_hi, b_lo, preferred_element_type=F32))


def _ssd_scan_kernel(xbc_ref, dt_ref, dtt_ref, dtb_ref, dtbt_ref, a_ref, at_ref, o_ref, h_ref):
    nh, q, hp = SSD_N_HEADS, SSD_CHUNK, SSD_HEAD_DIM
    d = pl.program_id(1)
    fwd = d == 0

    @pl.when(pl.program_id(2) == 0)
    def _():
        h_ref[...] = jnp.zeros(h_ref.shape, F32)

    dt2 = jax.nn.softplus(dt_ref[0] + dtb_ref[...])
    dt = jnp.where(fwd, dt2[:, :nh], dt2[:, nh:])
    a = dt * jnp.where(fwd, a_ref[:, :nh], a_ref[:, nh:])
    dtt2 = jax.nn.softplus(dtt_ref[0] + dtbt_ref[...])
    a_t = jnp.where(fwd, dtt2[:nh], dtt2[nh:]) * jnp.where(fwd, at_ref[:nh], at_ref[nh:])

    r_i = lax.broadcasted_iota(jnp.int32, (q, q), 0)
    c_i = lax.broadcasted_iota(jnp.int32, (q, q), 1)
    ahead = jnp.where(fwd, c_i - r_i, r_i - c_i)
    seen = ahead <= 0
    seen_t = ahead >= 0
    cum = _dot_f32_rhs(seen.astype(BF16), a)
    cum_t = _dot_f32_lhs(a_t, seen_t.astype(BF16))
    total = jnp.where(fwd, cum[q - 1:q, :], cum[0:1, :])
    ecum = jnp.exp(cum)
    dtdec = dt * jnp.exp(total - cum)
    etot = jnp.exp(total)
    dt_t = jnp.where(fwd, dtt2[:nh], dtt2[nh:])

    pw = 2 * hp
    left = lax.broadcasted_iota(jnp.int32, (q, pw), 1) < hp
    left_row = lax.broadcasted_iota(jnp.int32, (1, pw), 1) < hp
    pairs = SSD_HEADS_PER_GROUP // 2
    for g in range(SSD_N_GROUPS):
        b_g = xbc_ref[0, :, SSD_D_INNER + g * SSD_D_STATE:SSD_D_INNER + (g + 1) * SSD_D_STATE].astype(BF16)
        c_lo = SSD_D_INNER + SSD_GN + g * SSD_D_STATE
        c_g = xbc_ref[0, :, c_lo:c_lo + SSD_D_STATE].astype(BF16)
        cb = lax.dot_general(c_g, b_g, (((1,), (1,)), ((), ())), preferred_element_type=F32)
        h_g = h_ref[g]
        ch = jnp.dot(c_g, h_g.astype(BF16), preferred_element_type=F32)
        xdecs, etots = [], []
        for pr in range(pairs):
            e1 = g * SSD_HEADS_PER_GROUP + 2 * pr
            e2 = e1 + 1
            x_pair = xbc_ref[0, :, e1 * hp:e1 * hp + pw]
            ws = []
            for e in (e1, e2):
                seg = cum[:, e:e + 1] - cum_t[e:e + 1, :]
                ws.append((cb * jnp.exp(jnp.where(seen, seg, -jnp.inf)) * dt_t[e:e + 1, :]).astype(BF16))
            rhs = jnp.concatenate([jnp.where(left, x_pair, 0.0), jnp.where(left, 0.0, x_pair)], axis=0).astype(BF16)
            y_diag = jnp.dot(jnp.concatenate(ws, axis=1), rhs, preferred_element_type=F32)
            y_off = ch[:, pr * pw:(pr + 1) * pw] * jnp.where(left, ecum[:, e1:e1 + 1], ecum[:, e2:e2 + 1])
            o_ref[0, 0, :, e1 * hp:e1 * hp + pw] = y_diag + y_off
            xdecs.append((x_pair * jnp.where(left, dtdec[:, e1:e1 + 1], dtdec[:, e2:e2 + 1])).astype(BF16))
            etots.append(jnp.where(left_row, etot[:, e1:e1 + 1], etot[:, e2:e2 + 1]))
        s_g = lax.dot_general(b_g, jnp.concatenate(xdecs, axis=1), (((0,), (0,)), ((), ())),
                              preferred_element_type=F32)
        h_ref[g] = h_g * jnp.concatenate(etots, axis=1) + s_g


def ssd_scan(xbc, dt_raw, dt_bias, a_log, n_lead_chunks):
    b, t, _ = xbc.shape
    q, nh = SSD_CHUNK, SSD_N_HEADS
    nc = t // q
    dt_t = jnp.swapaxes(dt_raw, 1, 2)
    a_neg = -jnp.exp(a_log.astype(F32)).reshape(1, 2 * nh)
    dtb = dt_bias.astype(F32).reshape(1, 2 * nh)

    def chunk_of(d, c):
        back = jnp.where(c < n_lead_chunks, n_lead_chunks - 1 - c, nc - 1 + n_lead_chunks - c)
        return jnp.where(d == 0, c, back)

    return pl.pallas_call(
        _ssd_scan_kernel,
        grid=(b, 2, nc),
        in_specs=[pl.BlockSpec((1, q, SSD_CONV_DIM), lambda i, d, c: (i, chunk_of(d, c), 0)),
                  pl.BlockSpec((1, q, 2 * nh), lambda i, d, c: (i, chunk_of(d, c), 0)),
                  pl.BlockSpec((1, 2 * nh, q), lambda i, d, c: (i, 0, chunk_of(d, c))),
                  _resident((1, 2 * nh)), _resident((2 * nh, 1)),
                  _resident((1, 2 * nh)), _resident((2 * nh, 1))],
        out_specs=pl.BlockSpec((1, 1, q, SSD_D_INNER), lambda i, d, c: (i, d, chunk_of(d, c), 0)),
        out_shape=jax.ShapeDtypeStruct((b, 2, t, SSD_D_INNER), F32),
        scratch_shapes=[pltpu.VMEM((SSD_N_GROUPS, SSD_D_STATE, SSD_GROUP_WIDTH), F32)],
        compiler_params=_params(("parallel", "parallel", "arbitrary")),
        name="ssd_scan",
    )(xbc, dt_raw, dt_t, dtb, dtb.reshape(2 * nh, 1), a_neg, a_neg.reshape(2 * nh, 1))


def _ssd_out_kernel(x_ref, y0_ref, y1_ref, xs0_ref, xs1_ref, z_ref, dsk_ref, ng_ref, gt_ref, w_ref, o_ref):
    xs = jnp.concatenate([xs0_ref[0], xs1_ref[0]], axis=1)
    y = y0_ref[0, 0] + y1_ref[0, 0] + dsk_ref[...] * xs
    gy = y * _silu(z_ref[0])
    parts = []
    for g in range(SSD_N_GROUPS):
        blk = gy[:, g * SSD_GROUP_WIDTH:(g + 1) * SSD_GROUP_WIDTH]
        parts.append(blk * lax.rsqrt(jnp.mean(blk * blk, axis=-1, keepdims=True) + EPS))
    a = (jnp.concatenate(parts, axis=1) * ng_ref[...]).astype(BF16)
    o_ref[0] = x_ref[0] + gt_ref[0, 0] * jnp.dot(a, w_ref[...], preferred_element_type=F32)


def ssd_gate_project(x, y2, xbc, z, d_skip, norm_g, gate, w_out, n0_tiles, tm=256):
    b, t, d = x.shape
    half = SSD_D_INNER // 2
    dsk = jnp.repeat(d_skip.astype(F32), SSD_HEAD_DIM).reshape(1, SSD_D_INNER)
    return pl.pallas_call(
        _ssd_out_kernel,
        grid=(b, t // tm),
        in_specs=[pl.BlockSpec((1, tm, d), lambda i, j: (i, j, 0)),
                  pl.BlockSpec((1, 1, tm, SSD_D_INNER), lambda i, j: (i, 0, j, 0)),
                  pl.BlockSpec((1, 1, tm, SSD_D_INNER), lambda i, j: (i, 1, j, 0)),
                  pl.BlockSpec((1, tm, half), lambda i, j: (i, j, 0)),
                  pl.BlockSpec((1, tm, half), lambda i, j: (i, j, 1)),
                  pl.BlockSpec((1, tm, SSD_D_INNER), lambda i, j: (i, j, 0)),
                  _resident((1, SSD_D_INNER)), _resident((1, SSD_D_INNER)),
                  _mod_spec(d, n0_tiles),
                  _resident((SSD_D_INNER, d))],
        out_specs=pl.BlockSpec((1, tm, d), lambda i, j: (i, j, 0)),
        out_shape=jax.ShapeDtypeStruct((b, t, d), F32),
        compiler_params=_params(("parallel", "parallel")),
        name="ssd_gate_project",
    )(x, y2, y2, xbc, xbc, z, dsk, norm_g.astype(F32).reshape(1, SSD_D_INNER), gate, w_out)


def ssd_mixer(x, h, m_lat, m_ctx, g1, w_in, conv_w, conv_b, a_log, dt_bias, d_skip, norm_g, w_out, tm=256):
    lc = h.shape[1]
    xx = jnp.concatenate([h, x], axis=1)
    sh, sc, gt = (jnp.concatenate([mc, ml], axis=1) for mc, ml in zip(m_ctx, m_lat))
    n0 = lc // tm
    w_bf = w_in.astype(BF16)
    ws = [w_bf[:, :SSD_D_INNER], w_bf[:, SSD_D_INNER:SSD_D_INNER + SSD_CONV_DIM], w_bf[:, SSD_D_INNER + SSD_CONV_DIM:]]
    z, xbc_raw, dt_raw = modulate_project(xx, sh, sc, g1, ws, tm=tm, n0_tiles=n0)
    xbc = dwconv_tokens(xbc_raw, conv_w, conv_b, seg_rows=lc, silu=True)
    y2 = ssd_scan(xbc, dt_raw, dt_bias, a_log, lc // SSD_CHUNK)
    out = ssd_gate_project(xx, y2, xbc, z, d_skip, norm_g, gt, w_out.astype(BF16), n0, tm=tm)
    return out[:, lc:], out[:, :lc]


POOL_CHUNK = 256
POOL_MAX_HALF = max(POOL_WINDOWS) // 2


def _pool_band_tables(width):
    import numpy as np
    pos = np.arange(POOL_CHUNK)
    line, col = pos // width, pos % width
    out = []
    for w in POOL_WINDOWS:
        inside = (col[None, :] >= col[:, None] - w // 2) & (col[None, :] <= col[:, None] + (w - w // 2) - 1)
        out.append((inside & (line[None, :] == line[:, None])).astype(np.float32))
    return np.stack(out)


def _pool_kernel(u_ref, band_ref, o_ref, cs_ref, *, width, n_lines):
    l, c = u_ref.shape
    n_chunks = l // POOL_CHUNK
    assert POOL_WINDOWS == tuple(2 << g for g in range(POOL_GROUPS)) and width & (width - 1) == 0
    half = lax.shift_left(jnp.int32(1), pl.program_id(1) // (POOL_GROUP_DIM // LANES))
    log_w = width.bit_length() - 1
    pad = POOL_MAX_HALF * width if n_lines > 1 else 0
    if pad:
        cs_ref[0:pad, :] = jnp.zeros((pad, c), F32)
        cs_ref[pad + l:2 * pad + l, :] = jnp.zeros((pad, c), F32)

    def col_pass(i, carry):
        r = pl.multiple_of(i * POOL_CHUNK, POOL_CHUNK)
        u = u_ref[pl.ds(r, POOL_CHUNK), :]
        hi = u.astype(BF16)
        lo = (u - hi.astype(F32)).astype(BF16)
        cs_ref[pl.ds(pad + r, POOL_CHUNK), :] = (jnp.dot(band_ref[0], hi, preferred_element_type=F32)
                                                 + jnp.dot(band_ref[0], lo, preferred_element_type=F32))
        return carry

    lax.fori_loop(0, n_chunks, col_pass, 0)

    def out_pass(i, carry):
        r = pl.multiple_of(i * POOL_CHUNK, POOL_CHUNK)
        pos = r + lax.broadcasted_iota(jnp.int32, (POOL_CHUNK, c), 0)
        col = pos & (width - 1)
        cnt = jnp.minimum(col + half, width) - jnp.maximum(col - half, 0)
        if n_lines > 1:
            def add_line(k, acc):
                return acc + cs_ref[pl.ds(pl.multiple_of(pad + r + k * width, SUBLANES), POOL_CHUNK), :]
            s = lax.fori_loop(-half, half, add_line, jnp.zeros((POOL_CHUNK, c), F32))
            line = lax.shift_right_logical(pos, log_w)
            cnt_l = jnp.minimum(line + half, n_lines) - jnp.maximum(line - half, 0)
            mean = s / (cnt_l.astype(F32) * cnt.astype(F32))
        else:
            mean = cs_ref[pl.ds(r, POOL_CHUNK), :] / cnt.astype(F32)
        o_ref[pl.ds(r, POOL_CHUNK), :] = (mean - u_ref[pl.ds(r, POOL_CHUNK), :]).astype(o_ref.dtype)
        return carry

    lax.fori_loop(0, n_chunks, out_pass, 0)


def pool_tokens(u, width):
    b, l, d = u.shape
    n_lines = l // width
    bands = jnp.asarray(_pool_band_tables(width)).astype(BF16)
    tiles_per_group = POOL_GROUP_DIM // LANES
    pad = POOL_MAX_HALF * width if n_lines > 1 else 0
    return pl.pallas_call(
        functools.partial(_pool_kernel, width=width, n_lines=n_lines),
        grid=(b, d // LANES),
        in_specs=[pl.BlockSpec((None, l, LANES), lambda i, j: (i, 0, j)),
                  pl.BlockSpec((1, POOL_CHUNK, POOL_CHUNK), lambda i, j: (j // tiles_per_group, 0, 0))],
        out_specs=pl.BlockSpec((None, l, LANES), lambda i, j: (i, 0, j)),
        out_shape=jax.ShapeDtypeStruct((b, l, d), BF16),
        scratch_shapes=[pltpu.VMEM((l + 2 * pad, LANES), F32)],
        compiler_params=_params(("parallel", "parallel")),
        name="pool_tokens",
    )(u, bands)


def _pool_mix_kernel(x_ref, p_ref, w_ref, b_ref, sc_ref, gt_ref, o_ref):
    ys = [jnp.dot(p_ref[0, :, g * POOL_GROUP_DIM:(g + 1) * POOL_GROUP_DIM], w_ref[g], preferred_element_type=F32)
          for g in range(POOL_GROUPS)]
    y = (jnp.concatenate(ys, axis=1) + b_ref[...]) * sc_ref[...]
    o_ref[0] = x_ref[0] + gt_ref[0, 0] * y


def pool_mix_residual(x, pooled, w, bias, scale, gate, tm=512):
    b, l, d = x.shape
    tm = min(tm, l)
    return pl.pallas_call(
        _pool_mix_kernel,
        grid=(b, l // tm),
        in_specs=[pl.BlockSpec((1, tm, d), lambda i, j: (i, j, 0)),
                  pl.BlockSpec((1, tm, d), lambda i, j: (i, j, 0)),
                  _resident(w.shape), _resident((1, d)), _resident((1, d)), _mod_spec(d, 0)],
        out_specs=pl.BlockSpec((1, tm, d), lambda i, j: (i, j, 0)),
        out_shape=jax.ShapeDtypeStruct((b, l, d), F32),
        compiler_params=_params(("parallel", "parallel")),
        name="pool_mix_residual",
    )(x, pooled, w.astype(BF16), bias.astype(F32).reshape(1, d), scale.astype(F32).reshape(1, d), gate)


def pool_mixer(x, shift, scale_mod, gate, g1, w, bias, scale, width):
    (u,) = modulate_project(x, shift, scale_mod, g1, [])
    return pool_mix_residual(x, pool_tokens(u, width), w, bias, scale, gate)


SLAB_PAD = 8


def _round_up(v, m):
    return (v + m - 1) // m * m


def _fft_plan(l):
    n = 2 * l
    na = 1 << (n.bit_length() // 2)
    nb = n // na
    ns = na // 2 + 1
    return dict(l=l, n=n, na=na, nb=nb, ns=ns, nsp=_round_up(ns + 1, SUBLANES), pitch=2 * nb + SLAB_PAD,
                n_pairs=(ns + 1) // 2)


def _fft_tables(l):
    import numpy as np
    p = _fft_plan(l)
    n, na, nb, ns, nsp = p["n"], p["na"], p["nb"], p["ns"], p["nsp"]
    half = na // 2
    ka = np.arange(ns)[None, :, None]
    tb = np.arange(nb)[:, None, None]

    def stage1(ta):
        th = 2.0 * np.pi * (ta[None, None, :] * ka / na + tb * ka / n)
        m = np.zeros((nb, 2 * nsp, ta.shape[0]))
        m[:, :ns] = np.cos(th)
        m[:, nsp:nsp + ns] = -np.sin(th)
        return m

    f1 = stage1(np.arange(half))
    f1k = np.zeros((nb, 2 * nsp, na))
    f1k[:, :, :half] = f1
    f1k[1:, :, half:] = stage1(na - 1 - np.arange(half))[1:]
    tb0 = stage1(na - np.arange(half))[0]
    tb0[:, 0] = 0.0
    f1k[0, :, half:] = tb0

    k2 = np.arange(nb)
    ang = 2.0 * np.pi * np.outer(k2, k2) / nb
    c, s = np.cos(ang), np.sin(ang)
    f2 = np.block([[c, s], [-s, c]])
    g2 = np.block([[c, -s], [s, c]])

    ta = np.arange(half)[None, :, None]
    kk = np.arange(ns)[None, None, :]
    tbb = np.arange(nb)[:, None, None]
    ph = 2.0 * np.pi * (ta * kk / na + tbb * kk / n)
    wgt = np.where((kk == 0) | (kk == na // 2), 1.0, 2.0) / n
    g1 = np.zeros((nb, half, 2 * nsp))
    g1[:, :, :ns] = wgt * np.cos(ph)
    g1[:, :, nsp:nsp + ns] = -wgt * np.sin(ph)
    f32 = np.float32
    return p, f1.astype(f32), f1k.astype(f32), f2.astype(f32), g2.astype(f32), g1.astype(f32)


def _fft_stage1(gather, f1_ref, s_ref, p):
    nb, nsp, pitch = p["nb"], p["nsp"], p["pitch"]

    def body(tb, carry):
        a = jnp.dot(f1_ref[tb], gather(tb), preferred_element_type=F32)
        s_ref[pl.ds(tb, nsp, stride=pitch), :] = a[:nsp]
        s_ref[pl.ds(nb + tb, nsp, stride=pitch), :] = a[nsp:]
        return carry

    lax.fori_loop(0, nb, body, 0)


def _slab_pair(s_ref, i, p):
    nb, pitch = p["nb"], p["pitch"]
    r0 = pl.multiple_of(2 * i * pitch, SUBLANES)
    r1 = pl.multiple_of(2 * i * pitch + pitch, SUBLANES)
    return jnp.concatenate([s_ref[pl.ds(r0, 2 * nb), :], s_ref[pl.ds(r1, 2 * nb), :]], axis=1)


def _filter_mlp_kernel(z_ref, w1_ref, b1_ref, w2_ref, b2_ref, fr_ref, w3f_ref, w3b_ref, dl_ref, o_ref):
    z = z_ref[...]
    h = jnp.sin(fr_ref[0:1, :] * (jnp.dot(z, w1_ref[...], preferred_element_type=F32, precision=HIGHEST) + b1_ref[...]))
    h = jnp.sin(fr_ref[1:2, :] * (jnp.dot(h, w2_ref[...], preferred_element_type=F32, precision=HIGHEST) + b2_ref[...]))
    decay = jnp.exp(-z[:, 0:1] * dl_ref[...])
    hf = _dot_split(h, w3f_ref[...]) * decay
    hb = _dot_split(h, w3b_ref[...]) * decay
    scale = lax.rsqrt(jnp.sum(hf * hf + hb * hb, axis=0, keepdims=True) + EPS)
    o_ref[0, 0] = hf * scale
    o_ref[0, 1] = hb * scale


def hyena_filters(l, fw1, fb1, fw2, fb2, fw3, ffreq, tn=LANES):
    d = D_MODEL
    pos = jnp.arange(l, dtype=F32)
    t = jnp.linspace(0.0, 1.0, l, dtype=F32)
    wpos = 2.0 * math.pi * pos / l
    f = jnp.linspace(1e-4, HY_BANDS - 1, HY_BANDS, dtype=F32)
    ang = wpos[:, None] * f[None, :]
    emb_pad = _round_up(HY_EMB_DIM, SUBLANES)
    z = jnp.concatenate([t[:, None], jnp.cos(ang), -jnp.sin(ang), jnp.zeros((l, emb_pad - HY_EMB_DIM), F32)], axis=-1)
    w1 = jnp.concatenate([fw1.astype(F32), jnp.zeros((emb_pad - HY_EMB_DIM, fw1.shape[1]), F32)], axis=0)
    deltas = jnp.abs(jnp.linspace(HY_MIN_DECAY, HY_MAX_DECAY, d, dtype=F32)).reshape(1, d)
    hid = fw2.shape[0]
    nt = d // tn
    w3 = fw3.astype(F32)
    return pl.pallas_call(
        _filter_mlp_kernel,
        grid=(HY_ORDER, nt),
        in_specs=[_resident((l, emb_pad)), _resident((emb_pad, hid)), _resident((1, hid)), _resident((hid, hid)),
                  _resident((1, hid)), _resident((2, hid)),
                  pl.BlockSpec((hid, tn), lambda o, j: (0, (2 * o) * nt + j)),
                  pl.BlockSpec((hid, tn), lambda o, j: (0, (2 * o + 1) * nt + j)),
                  pl.BlockSpec((1, tn), lambda o, j: (0, j))],
        out_specs=pl.BlockSpec((1, 2, l, tn), lambda o, j: (o, 0, 0, j)),
        out_shape=jax.ShapeDtypeStruct((HY_ORDER, 2, l, d), F32),
        compiler_params=_params(("parallel", "parallel")),
        name="hyena_filters",
    )(z, w1, fb1.astype(F32).reshape(1, hid), fw2.astype(F32), fb2.astype(F32).reshape(1, hid), ffreq.astype(F32), w3, w3, deltas)


def _filter_spectrum_kernel(hf_ref, hb_ref, f1_ref, f2_ref, o_ref, s_ref, *, p):
    na, nb = p["na"], p["nb"]
    half = na // 2

    def gather(tb):
        fwd = hf_ref[pl.ds(tb, half, stride=nb), :]
        bwd = hb_ref[pl.ds(jnp.where(tb == 0, 0, nb - tb), half, stride=nb), :]
        return jnp.concatenate([fwd, bwd], axis=0).astype(BF16)

    _fft_stage1(gather, f1_ref, s_ref, p)

    def body(i, carry):
        spec = jnp.dot(f2_ref[...], _slab_pair(s_ref, i, p).astype(BF16), preferred_element_type=F32)
        r = pl.multiple_of(i * 4 * nb, SUBLANES)
        o_ref[pl.ds(r, 2 * nb), :] = spec[:, :LANES].astype(BF16)
        o_ref[pl.ds(r + 2 * nb, 2 * nb), :] = spec[:, LANES:].astype(BF16)
        return carry

    lax.fori_loop(0, p["n_pairs"], body, 0)


def hyena_filter_spectrum(filt, tables):
    p, _, f1k, f2, _, _ = tables
    order, _, l, d = filt.shape
    rows = 2 * p["n_pairs"] * 2 * p["nb"]
    return pl.pallas_call(
        functools.partial(_filter_spectrum_kernel, p=p),
        grid=(order, d // LANES),
        in_specs=[pl.BlockSpec((None, None, l, LANES), lambda o, j: (o, 0, 0, j)),
                  pl.BlockSpec((None, None, l, LANES), lambda o, j: (o, 1, 0, j)),
                  _resident(f1k.shape), _resident(f2.shape)],
        out_specs=pl.BlockSpec((None, rows, LANES), lambda o, j: (o, 0, j)),
        out_shape=jax.ShapeDtypeStruct((order, rows, d), BF16),
        scratch_shapes=[pltpu.VMEM((p["nsp"] * p["pitch"], LANES), F32)],
        compiler_params=_params(("parallel", "parallel")),
        name="hyena_filter_spectrum",
    )(filt, filt, jnp.asarray(f1k).astype(BF16), jnp.asarray(f2).astype(BF16))


def _longconv_kernel(a_ref, m_ref, k_ref, bias_ref, f1_ref, f2_ref, g2_ref, g1_ref, o_ref, s_ref, *, p, row_chunk):
    na, nb, nsp, pitch = p["na"], p["nb"], p["nsp"], p["pitch"]
    half = na // 2

    _fft_stage1(lambda tb: a_ref[pl.ds(tb, half, stride=nb), :].astype(BF16), f1_ref, s_ref, p)

    def mid(i, carry):
        x = jnp.dot(f2_ref[...], _slab_pair(s_ref, i, p).astype(BF16), preferred_element_type=F32)
        r = pl.multiple_of(i * 4 * nb, SUBLANES)
        kk = jnp.concatenate([k_ref[pl.ds(r, 2 * nb), :], k_ref[pl.ds(r + 2 * nb, 2 * nb), :]], axis=1).astype(F32)
        xr, xi, kr, ki = x[:nb], x[nb:], kk[:nb], kk[nb:]
        y = jnp.concatenate([xr * kr - xi * ki, xr * ki + xi * kr], axis=0).astype(BF16)
        bq = jnp.dot(g2_ref[...], y, preferred_element_type=F32)
        r0 = pl.multiple_of(2 * i * pitch, SUBLANES)
        r1 = pl.multiple_of(2 * i * pitch + pitch, SUBLANES)
        s_ref[pl.ds(r0, 2 * nb), :] = bq[:, :LANES]
        s_ref[pl.ds(r1, 2 * nb), :] = bq[:, LANES:]
        return carry

    lax.fori_loop(0, p["n_pairs"], mid, 0)

    def last(tb, carry):
        bq = jnp.concatenate([s_ref[pl.ds(tb, nsp, stride=pitch), :], s_ref[pl.ds(nb + tb, nsp, stride=pitch), :]], axis=0)
        o_ref[pl.ds(tb, half, stride=nb), :] = jnp.dot(g1_ref[tb], bq.astype(BF16), preferred_element_type=F32)
        return carry

    lax.fori_loop(0, nb, last, 0)

    def finish(i, carry):
        r = pl.multiple_of(i * row_chunk, SUBLANES)
        a = a_ref[pl.ds(r, row_chunk), :]
        o_ref[pl.ds(r, row_chunk), :] = m_ref[pl.ds(r, row_chunk), :] * (o_ref[pl.ds(r, row_chunk), :] + bias_ref[...] * a)
        return carry

    lax.fori_loop(0, p["l"] // row_chunk, finish, 0)


def hyena_longconv(a, a_col, m, m_col, kspec, bias, tables):
    p, f1, _, f2, g2, g1 = tables
    b, l, _ = a.shape
    d = D_MODEL
    nt = d // LANES
    rows = kspec.shape[0]
    tabs = [jnp.asarray(t).astype(BF16) for t in (f1, f2, g2, g1)]
    return pl.pallas_call(
        functools.partial(_longconv_kernel, p=p, row_chunk=min(512, l)),
        grid=(nt, b),
        in_specs=[pl.BlockSpec((None, l, LANES), lambda j, i: (i, 0, a_col * nt + j)),
                  pl.BlockSpec((None, l, LANES), lambda j, i: (i, 0, m_col * nt + j)),
                  pl.BlockSpec((rows, LANES), lambda j, i: (0, j)),
                  pl.BlockSpec((1, LANES), lambda j, i: (0, j))] + [_resident(t.shape) for t in tabs],
        out_specs=pl.BlockSpec((None, l, LANES), lambda j, i: (i, 0, j)),
        out_shape=jax.ShapeDtypeStruct((b, l, d), F32),
        scratch_shapes=[pltpu.VMEM((p["nsp"] * p["pitch"], LANES), F32)],
        compiler_params=_params(("parallel", "parallel")),
        name="hyena_longconv",
    )(a, m, kspec, bias.astype(F32).reshape(1, d), *tabs)


def hyena_core(p_raw, conv_w, conv_b, fw1, fb1, fw2, fb2, fw3, ffreq, hbias):
    l = p_raw.shape[1]
    tables = _fft_tables(l)
    filt = hyena_filters(l, fw1, fb1, fw2, fb2, fw3, ffreq)
    kspec = hyena_filter_spectrum(filt, tables)
    pc = dwconv_tokens(p_raw, conv_w, conv_b)
    z = hyena_longconv(pc, 0, pc, 1, kspec[0], hbias[0], tables)
    return hyena_longconv(z, 0, pc, 2, kspec[1], hbias[1], tables)


def kernel(x, c, ctx, c_ctx, ada_w, ada_b, norm_g, ffn_w_gate, ffn_w_up, ffn_w_down, ssd_w_in, ssd_conv_w, ssd_conv_b, ssd_a_log, ssd_dt_bias, ssd_d, ssd_norm_g, ssd_w_out, pool_w, pool_b, pool_scale, hy_w_in, hy_conv_w, hy_conv_b, hy_filt_w1, hy_filt_b1, hy_filt_w2, hy_filt_b2, hy_filt_w3, hy_filt_freq, hy_bias, hy_w_out, final_g):
    batch = x.shape[0]
    d = D_MODEL
    h = ctx

    s = jnp.concatenate([jax.nn.silu(c), jax.nn.silu(c_ctx)[None], jnp.zeros((7 - batch, d), F32)], axis=0)
    mods = ada_modulation(s, ada_w, ada_b).reshape(DEPTH, 8, N_MOD, d)

    wg_bf, wu_bf, wd_bf = ffn_w_gate.astype(BF16), ffn_w_up.astype(BF16), ffn_w_down.astype(BF16)
    fg = final_g.reshape(1, d)

    for i in range(DEPTH):
        kind = i % N_MIXERS
        j = i // N_MIXERS
        last = i == DEPTH - 1
        ctx_in_needed = (not last) or kind == 0
        m = [mods[i, :batch, k].reshape(batch, 1, 1, d) for k in range(N_MOD)]
        mc = [jnp.broadcast_to(mods[i, batch, k].reshape(1, 1, 1, d), (batch, 1, 1, d)) for k in range(N_MOD)]
        g0, g1, g2 = (norm_g[i, k].reshape(1, d) for k in range(3))

        x = ffn_step(x, m[0], m[1], m[2], g0, wg_bf[i, 0], wu_bf[i, 0], wd_bf[i, 0])
        if ctx_in_needed:
            h = ffn_step(h, mc[0], mc[1], mc[2], g0, wg_bf[i, 0], wu_bf[i, 0], wd_bf[i, 0])

        if kind == 0:
            x, h_new = ssd_mixer(x, h, (m[3], m[4], m[5]), (mc[3], mc[4], mc[5]), g1, ssd_w_in[j], ssd_conv_w[j],
                                 ssd_conv_b[j], ssd_a_log[j], ssd_dt_bias[j], ssd_d[j], ssd_norm_g[j], ssd_w_out[j])
            if not last:
                h = h_new
        elif kind == 1:
            x = pool_mixer(x, m[3], m[4], m[5], g1, pool_w[j], pool_b[j].reshape(-1), pool_scale[j], GRID_W)
            if not last:
                h = pool_mixer(h, mc[3], mc[4], mc[5], g1, pool_w[j], pool_b[j].reshape(-1), pool_scale[j], h.shape[1])
        else:
            w_in = hy_w_in[j].astype(BF16)
            w_out = hy_w_out[j].astype(BF16)
            filt = (hy_filt_w1[j], hy_filt_b1[j], hy_filt_w2[j], hy_filt_b2[j], hy_filt_w3[j], hy_filt_freq[j])
            (p_lat,) = modulate_project(x, m[3], m[4], g1, [w_in])
            x = project_residual(x, hyena_core(p_lat, hy_conv_w[j], hy_conv_b[j], *filt, hy_bias[j]), m[5], w_out)
            if not last:
                (p_ctx,) = modulate_project(h, mc[3], mc[4], g1, [w_in])
                h = project_residual(h, hyena_core(p_ctx, hy_conv_w[j], hy_conv_b[j], *filt, hy_bias[j]), mc[5], w_out)

        x = ffn_step(x, m[6], m[7], m[8], g2, wg_bf[i, 1], wu_bf[i, 1], wd_bf[i, 1],
                     final_g=fg if last else None)
        if not last:
            h = ffn_step(h, mc[6], mc[7], mc[8], g2, wg_bf[i, 1], wu_bf[i, 1], wd_bf[i, 1])
    return x
```

```python
import functools
import math

import jax
import jax.numpy as jnp
from jax import lax
from jax.experimental import pallas as pl
from jax.experimental.pallas import tpu as pltpu

F32 = jnp.float32
BF16 = jnp.bfloat16
HIGHEST = lax.Precision.HIGHEST

D_MODEL = 1024
DEPTH = 4
GRID_W = 64
N_MIXERS = 3
D_FF = 2816
N_MOD = 9
EPS = 1e-6

SSD_D_INNER = 2 * D_MODEL
SSD_HEAD_DIM = 64
SSD_N_HEADS = SSD_D_INNER // SSD_HEAD_DIM
SSD_N_GROUPS = 4
SSD_HEADS_PER_GROUP = SSD_N_HEADS // SSD_N_GROUPS
SSD_D_STATE = 128
SSD_CHUNK = 128
SSD_GN = SSD_N_GROUPS * SSD_D_STATE
SSD_CONV_DIM = SSD_D_INNER + 2 * SSD_GN
SSD_GROUP_WIDTH = SSD_D_INNER // SSD_N_GROUPS

POOL_WINDOWS = (2, 4, 8, 16)
POOL_GROUPS = 4
POOL_GROUP_DIM = D_MODEL // POOL_GROUPS

HY_ORDER = 2
HY_EMB_DIM = 33
HY_BANDS = (HY_EMB_DIM - 1) // 2
HY_MAX_DECAY = math.log(1e-2) / 0.3
HY_MIN_DECAY = math.log(1e-2) / 1.5

VMEM_LIMIT_BYTES = 56 * 1024 * 1024
SUBLANES = 8
LANES = 128


def _params(sem):
    return pltpu.CompilerParams(dimension_semantics=sem, vmem_limit_bytes=VMEM_LIMIT_BYTES)


def _resident(shape):
    nd = len(shape)
    return pl.BlockSpec(shape, lambda *_: (0,) * nd, pipeline_mode=pl.Buffered(1))


def _mod_spec(d, n0_tiles):
    if n0_tiles == 0:
        return pl.BlockSpec((1, 1, 1, d), lambda i, j: (i, 0, 0, 0))
    return pl.BlockSpec((1, 1, 1, d), lambda i, j: (i, jnp.where(j < n0_tiles, 0, 1), 0, 0))


def _modulated(x, g, shift, scale):
    ms = jnp.mean(x * x, axis=-1, keepdims=True)
    return (x * lax.rsqrt(ms + EPS)) * g * (1.0 + scale) + shift


def _silu(v):
    return v * jax.nn.sigmoid(v)


def _ada_kernel(s_ref, w_ref, b_ref, o_ref):
    o_ref[0] = jnp.dot(s_ref[...], w_ref[0], preferred_element_type=F32, precision=HIGHEST) + b_ref[0]


def ada_modulation(s, ada_w, ada_b, tn=1024):
    depth, d, n = ada_w.shape
    r = s.shape[0]
    return pl.pallas_call(
        _ada_kernel,
        grid=(depth, n // tn),
        in_specs=[pl.BlockSpec((r, d), lambda i, j: (0, 0)),
                  pl.BlockSpec((1, d, tn), lambda i, j: (i, 0, j)),
                  pl.BlockSpec((1, 1, tn), lambda i, j: (i, 0, j))],
        out_specs=pl.BlockSpec((1, r, tn), lambda i, j: (i, 0, j)),
        out_shape=jax.ShapeDtypeStruct((depth, r, n), F32),
        compiler_params=_params(("parallel", "parallel")),
        name="ada_modulation",
    )(s, ada_w, ada_b.reshape(depth, 1, n))


def _ffn_kernel(x_ref, sh_ref, sc_ref, gt_ref, g_ref, wg_ref, wu_ref, wd_ref, *rest, f_chunk, final):
    if final:
        fg_ref, o_ref = rest
    else:
        (o_ref,) = rest
    x = x_ref[0]
    u = _modulated(x, g_ref[...], sh_ref[0, 0], sc_ref[0, 0]).astype(BF16)
    acc = jnp.zeros(x.shape, F32)
    d_ff = wg_ref.shape[1]
    for f0 in range(0, d_ff, f_chunk):
        a = jnp.dot(u, wg_ref[:, f0:f0 + f_chunk], preferred_element_type=F32)
        b = jnp.dot(u, wu_ref[:, f0:f0 + f_chunk], preferred_element_type=F32)
        h = (_silu(a) * b).astype(BF16)
        acc = acc + jnp.dot(h, wd_ref[f0:f0 + f_chunk, :], preferred_element_type=F32)
    y = x + (0.5 * gt_ref[0, 0]) * acc
    if final:
        ms = jnp.mean(y * y, axis=-1, keepdims=True)
        y = y * lax.rsqrt(ms + EPS) * fg_ref[...]
    o_ref[0] = y


def ffn_step(x, shift, scale, gate, g, wg, wu, wd, final_g=None, tm=512, f_chunk=256):
    b, l, d = x.shape
    tm = min(tm, l)
    f = wg.shape[1]
    final = final_g is not None
    mod_spec = _mod_spec(d, 0)
    in_specs = [pl.BlockSpec((1, tm, d), lambda i, j: (i, j, 0)),
                mod_spec, mod_spec, mod_spec,
                _resident((1, d)), _resident((d, f)), _resident((d, f)), _resident((f, d))]
    args = [x, shift, scale, gate, g, wg, wu, wd]
    if final:
        in_specs.append(_resident((1, d)))
        args.append(final_g)
    return pl.pallas_call(
        functools.partial(_ffn_kernel, f_chunk=f_chunk, final=final),
        grid=(b, l // tm),
        in_specs=in_specs,
        out_specs=pl.BlockSpec((1, tm, d), lambda i, j: (i, j, 0)),
        out_shape=jax.ShapeDtypeStruct((b, l, d), F32),
        compiler_params=_params(("parallel", "parallel")),
        name="ffn_step",
    )(*args)


def _modproj_kernel(x_ref, sh_ref, sc_ref, g_ref, *rest, n_w):
    w_refs, o_refs = rest[:n_w], rest[n_w:]
    u = _modulated(x_ref[0], g_ref[...], sh_ref[0, 0], sc_ref[0, 0])
    if n_w == 0:
        o_refs[0][0] = u
        return
    u = u.astype(BF16)
    for w_ref, o_ref in zip(w_refs, o_refs):
        o_ref[0] = jnp.dot(u, w_ref[...], preferred_element_type=F32)


def modulate_project(x, shift, scale, g, weights, tm=256, n0_tiles=0):
    b, l, d = x.shape
    tm = min(tm, l)
    mod_spec = _mod_spec(d, n0_tiles)
    in_specs = [pl.BlockSpec((1, tm, d), lambda i, j: (i, j, 0)), mod_spec, mod_spec, _resident((1, d))]
    in_specs += [_resident(w.shape) for w in weights]
    widths = [w.shape[1] for w in weights] or [d]
    return pl.pallas_call(
        functools.partial(_modproj_kernel, n_w=len(weights)),
        grid=(b, l // tm),
        in_specs=in_specs,
        out_specs=[pl.BlockSpec((1, tm, n), lambda i, j: (i, j, 0)) for n in widths],
        out_shape=[jax.ShapeDtypeStruct((b, l, n), F32) for n in widths],
        compiler_params=_params(("parallel", "parallel")),
        name="modulate_project",
    )(x, shift, scale, g, *weights)


def _outproj_kernel(x_ref, a_ref, gt_ref, w_ref, o_ref):
    y = jnp.dot(a_ref[0].astype(BF16), w_ref[...], preferred_element_type=F32)
    o_ref[0] = x_ref[0] + gt_ref[0, 0] * y


def project_residual(x, a, gate, w, tm=512):
    b, l, d = x.shape
    k = a.shape[-1]
    tm = min(tm, l)
    return pl.pallas_call(
        _outproj_kernel,
        grid=(b, l // tm),
        in_specs=[pl.BlockSpec((1, tm, d), lambda i, j: (i, j, 0)),
                  pl.BlockSpec((1, tm, k), lambda i, j: (i, j, 0)),
                  _mod_spec(d, 0),
                  _resident((k, d))],
        out_specs=pl.BlockSpec((1, tm, d), lambda i, j: (i, j, 0)),
        out_shape=jax.ShapeDtypeStruct((b, l, d), F32),
        compiler_params=_params(("parallel", "parallel")),
        name="project_residual",
    )(x, a, gate, w)


def _dwconv_kernel(prev_ref, x_ref, next_ref, w_ref, b_ref, o_ref, *, k_w, n_rows, seg_rows, silu):
    tr, c = x_ref.shape[1], x_ref.shape[2]
    p = k_w // 2
    act = _silu if silu else (lambda v: v)
    ext = jnp.concatenate([prev_ref[0], x_ref[0], next_ref[0]], axis=0)
    acc = jnp.broadcast_to(b_ref[...], (tr, c))
    for k in range(k_w):
        acc = acc + w_ref[k:k + 1, :] * ext[SUBLANES + k - p:SUBLANES + k - p + tr, :]
    o_ref[0] = act(acc)

    edges = sorted({0, n_rows} | ({seg_rows} if seg_rows else set()))
    n_tiles = n_rows // SUBLANES
    special = sorted({t for e in edges for t in ((e - p) // SUBLANES, (e + p - 1) // SUBLANES) if 0 <= t < n_tiles})
    tiles_per_block = tr // SUBLANES
    for tile in special:
        lo = (tile % tiles_per_block) * SUBLANES

        @pl.when(pl.program_id(2) == tile // tiles_per_block)
        def _(tile=tile, lo=lo):
            row = tile * SUBLANES + lax.broadcasted_iota(jnp.int32, (SUBLANES, c), 0)
            acc_t = jnp.broadcast_to(b_ref[...], (SUBLANES, c))
            for k in range(k_w):
                src = row + (k - p)
                valid = (src >= 0) & (src < n_rows)
                for e in edges[1:-1]:
                    valid = valid & jnp.logical_not((jnp.minimum(row, src) < e) & (jnp.maximum(row, src) >= e))
                tap = ext[SUBLANES + lo + k - p:2 * SUBLANES + lo + k - p, :]
                acc_t = acc_t + w_ref[k:k + 1, :] * jnp.where(valid, tap, 0.0)
            o_ref[0, lo:lo + SUBLANES, :] = act(acc_t)


def dwconv_tokens(x, w, bias, seg_rows=0, silu=False, tr=1024, ct=512):
    b, t, c = x.shape
    k_w = w.shape[0]
    tr = next(r for r in range(min(tr, t), 0, -SUBLANES) if t % r == 0)
    ct = min(ct, c)
    rb = tr // SUBLANES
    last_halo = t // SUBLANES - 1
    return pl.pallas_call(
        functools.partial(_dwconv_kernel, k_w=k_w, n_rows=t, seg_rows=seg_rows, silu=silu),
        grid=(b, c // ct, t // tr),
        in_specs=[pl.BlockSpec((1, SUBLANES, ct), lambda i, j, r: (i, jnp.maximum(r * rb - 1, 0), j)),
                  pl.BlockSpec((1, tr, ct), lambda i, j, r: (i, r, j)),
                  pl.BlockSpec((1, SUBLANES, ct), lambda i, j, r: (i, jnp.minimum((r + 1) * rb, last_halo), j)),
                  pl.BlockSpec((k_w, ct), lambda i, j, r: (0, j)),
                  pl.BlockSpec((1, ct), lambda i, j, r: (0, j))],
        out_specs=pl.BlockSpec((1, tr, ct), lambda i, j, r: (i, r, j)),
        out_shape=jax.ShapeDtypeStruct((b, t, c), F32),
        compiler_params=_params(("parallel", "parallel", "parallel")),
        name="dwconv_tokens",
    )(x, x, x, w, bias.reshape(1, c))


def _bf16_terms(v):
    hi = v.astype(BF16)
    r = v - hi.astype(F32)
    mid = r.astype(BF16)
    return hi, mid, (r - mid.astype(F32)).astype(BF16)


def _dot_f32_rhs(m, v):
    return sum(jnp.dot(m, t, preferred_element_type=F32) for t in _bf16_terms(v))


def _dot_f32_lhs(v, m):
    return sum(jnp.dot(t, m, preferred_element_type=F32) for t in _bf16_terms(v))


def _dot_split(a, b):
    a_hi, a_lo, _ = _bf16_terms(a)
    b_hi, b_lo, _ = _bf16_terms(b)
    return (jnp.dot(a_hi, b_hi, preferred_element_type=F32) + jnp.dot(a_lo, b_hi, preferred_element_type=F32)
            + jnp.dot(a_hi, b_lo, preferred_element_type=F32))


def _ssd_scan_kernel(xbc_ref, dt_ref, dtt_ref, dtb_ref, dtbt_ref, a_ref, at_ref, o_ref, h_ref):
    nh, q, hp = SSD_N_HEADS, SSD_CHUNK, SSD_HEAD_DIM
    d = pl.program_id(1)
    fwd = d == 0

    @pl.when(pl.program_id(2) == 0)
    def _():
        h_ref[...] = jnp.zeros(h_ref.shape, F32)

    dt2 = jax.nn.softplus(dt_ref[0] + dtb_ref[...])
    dt = jnp.where(fwd, dt2[:, :nh], dt2[:, nh:])
    a = dt * jnp.where(fwd, a_ref[:, :nh], a_ref[:, nh:])
    dtt2 = jax.nn.softplus(dtt_ref[0] + dtbt_ref[...])
    a_t = jnp.where(fwd, dtt2[:nh], dtt2[nh:]) * jnp.where(fwd, at_ref[:nh], at_ref[nh:])

    r_i = lax.broadcasted_iota(jnp.int32, (q, q), 0)
    c_i = lax.broadcasted_iota(jnp.int32, (q, q), 1)
    ahead = jnp.where(fwd, c_i - r_i, r_i - c_i)
    seen = ahead <= 0
    seen_t = ahead >= 0
    cum = _dot_f32_rhs(seen.astype(BF16), a)
    cum_t = _dot_f32_lhs(a_t, seen_t.astype(BF16))
    total = jnp.where(fwd, cum[q - 1:q, :], cum[0:1, :])
    ecum = jnp.exp(cum)
    dtdec = dt * jnp.exp(total - cum)
    etot = jnp.exp(total)
    dt_t = jnp.where(fwd, dtt2[:nh], dtt2[nh:])

    pw = 2 * hp
    left = lax.broadcasted_iota(jnp.int32, (q, pw), 1) < hp
    left_row = lax.broadcasted_iota(jnp.int32, (1, pw), 1) < hp
    pairs = SSD_HEADS_PER_GROUP // 2
    for g in range(SSD_N_GROUPS):
        b_g = xbc_ref[0, :, SSD_D_INNER + g * SSD_D_STATE:SSD_D_INNER + (g + 1) * SSD_D_STATE].astype(BF16)
        c_lo = SSD_D_INNER + SSD_GN + g * SSD_D_STATE
        c_g = xbc_ref[0, :, c_lo:c_lo + SSD_D_STATE].astype(BF16)
        cb = lax.dot_general(c_g, b_g, (((1,), (1,)), ((), ())), preferred_element_type=F32)
        h_g = h_ref[g]
        ch = jnp.dot(c_g, h_g.astype(BF16), preferred_element_type=F32)
        xdecs, etots = [], []
        for pr in range(pairs):
            e1 = g * SSD_HEADS_PER_GROUP + 2 * pr
            e2 = e1 + 1
            x_pair = xbc_ref[0, :, e1 * hp:e1 * hp + pw]
            ws = []
            for e in (e1, e2):
                seg = cum[:, e:e + 1] - cum_t[e:e + 1, :]
                ws.append((cb * jnp.exp(jnp.where(seen, seg, -jnp.inf)) * dt_t[e:e + 1, :]).astype(BF16))
            rhs = jnp.concatenate([jnp.where(left, x_pair, 0.0), jnp.where(left, 0.0, x_pair)], axis=0).astype(BF16)
            y_diag = jnp.dot(jnp.concatenate(ws, axis=1), rhs, preferred_element_type=F32)
            y_off = ch[:, pr * pw:(pr + 1) * pw] * jnp.where(left, ecum[:, e1:e1 + 1], ecum[:, e2:e2 + 1])
            o_ref[0, 0, :, e1 * hp:e1 * hp + pw] = y_diag + y_off
            xdecs.append((x_pair * jnp.where(left, dtdec[:, e1:e1 + 1], dtdec[:, e2:e2 + 1])).astype(BF16))
            etots.append(jnp.where(left_row, etot[:, e1:e1 + 1], etot[:, e2:e2 + 1]))
        s_g = lax.dot_general(b_g, jnp.concatenate(xdecs, axis=1), (((0,), (0,)), ((), ())),
                              preferred_element_type=F32)
        h_ref[g] = h_g * jnp.concatenate(etots, axis=1) + s_g


def ssd_scan(xbc, dt_raw, dt_bias, a_log, n_lead_chunks):
    b, t, _ = xbc.shape
    q, nh = SSD_CHUNK, SSD_N_HEADS
    nc = t // q
    dt_t = jnp.swapaxes(dt_raw, 1, 2)
    a_neg = -jnp.exp(a_log.astype(F32)).reshape(1, 2 * nh)
    dtb = dt_bias.astype(F32).reshape(1, 2 * nh)

    def chunk_of(d, c):
        back = jnp.where(c < n_lead_chunks, n_lead_chunks - 1 - c, nc - 1 + n_lead_chunks - c)
        return jnp.where(d == 0, c, back)

    return pl.pallas_call(
        _ssd_scan_kernel,
        grid=(b, 2, nc),
        in_specs=[pl.BlockSpec((1, q, SSD_CONV_DIM), lambda i, d, c: (i, chunk_of(d, c), 0)),
                  pl.BlockSpec((1, q, 2 * nh), lambda i, d, c: (i, chunk_of(d, c), 0)),
                  pl.BlockSpec((1, 2 * nh, q), lambda i, d, c: (i, 0, chunk_of(d, c))),
                  _resident((1, 2 * nh)), _resident((2 * nh, 1)),
                  _resident((1, 2 * nh)), _resident((2 * nh, 1))],
        out_specs=pl.BlockSpec((1, 1, q, SSD_D_INNER), lambda i, d, c: (i, d, chunk_of(d, c), 0)),
        out_shape=jax.ShapeDtypeStruct((b, 2, t, SSD_D_INNER), F32),
        scratch_shapes=[pltpu.VMEM((SSD_N_GROUPS, SSD_D_STATE, SSD_GROUP_WIDTH), F32)],
        compiler_params=_params(("parallel", "parallel", "arbitrary")),
        name="ssd_scan",
    )(xbc, dt_raw, dt_t, dtb, dtb.reshape(2 * nh, 1), a_neg, a_neg.reshape(2 * nh, 1))


def _ssd_out_kernel(x_ref, y0_ref, y1_ref, xs0_ref, xs1_ref, z_ref, dsk_ref, ng_ref, gt_ref, w_ref, o_ref):
    xs = jnp.concatenate([xs0_ref[0], xs1_ref[0]], axis=1)
    y = y0_ref[0, 0] + y1_ref[0, 0] + dsk_ref[...] * xs
    gy = y * _silu(z_ref[0])
    parts = []
    for g in range(SSD_N_GROUPS):
        blk = gy[:, g * SSD_GROUP_WIDTH:(g + 1) * SSD_GROUP_WIDTH]
        parts.append(blk * lax.rsqrt(jnp.mean(blk * blk, axis=-1, keepdims=True) + EPS))
    a = (jnp.concatenate(parts, axis=1) * ng_ref[...]).astype(BF16)
    o_ref[0] = x_ref[0] + gt_ref[0, 0] * jnp.dot(a, w_ref[...], preferred_element_type=F32)


def ssd_gate_project(x, y2, xbc, z, d_skip, norm_g, gate, w_out, n0_tiles, tm=256):
    b, t, d = x.shape
    half = SSD_D_INNER // 2
    dsk = jnp.repeat(d_skip.astype(F32), SSD_HEAD_DIM).reshape(1, SSD_D_INNER)
    return pl.pallas_call(
        _ssd_out_kernel,
        grid=(b, t // tm),
        in_specs=[pl.BlockSpec((1, tm, d), lambda i, j: (i, j, 0)),
                  pl.BlockSpec((1, 1, tm, SSD_D_INNER), lambda i, j: (i, 0, j, 0)),
                  pl.BlockSpec((1, 1, tm, SSD_D_INNER), lambda i, j: (i, 1, j, 0)),
                  pl.BlockSpec((1, tm, half), lambda i, j: (i, j, 0)),
                  pl.BlockSpec((1, tm, half), lambda i, j: (i, j, 1)),
                  pl.BlockSpec((1, tm, SSD_D_INNER), lambda i, j: (i, j, 0)),
                  _resident((1, SSD_D_INNER)), _resident((1, SSD_D_INNER)),
                  _mod_spec(d, n0_tiles),
                  _resident((SSD_D_INNER, d))],
        out_specs=pl.BlockSpec((1, tm, d), lambda i, j: (i, j, 0)),
        out_shape=jax.ShapeDtypeStruct((b, t, d), F32),
        compiler_params=_params(("parallel", "parallel")),
        name="ssd_gate_project",
    )(x, y2, y2, xbc, xbc, z, dsk, norm_g.astype(F32).reshape(1, SSD_D_INNER), gate, w_out)


def ssd_mixer(x, h, m_lat, m_ctx, g1, w_in, conv_w, conv_b, a_log, dt_bias, d_skip, norm_g, w_out, tm=256):
    lc = h.shape[1]
    xx = jnp.concatenate([h, x], axis=1)
    sh, sc, gt = (jnp.concatenate([mc, ml], axis=1) for mc, ml in zip(m_ctx, m_lat))
    n0 = lc // tm
    w_bf = w_in.astype(BF16)
    ws = [w_bf[:, :SSD_D_INNER], w_bf[:, SSD_D_INNER:SSD_D_INNER + SSD_CONV_DIM], w_bf[:, SSD_D_INNER + SSD_CONV_DIM:]]
    z, xbc_raw, dt_raw = modulate_project(xx, sh, sc, g1, ws, tm=tm, n0_tiles=n0)
    xbc = dwconv_tokens(xbc_raw, conv_w, conv_b, seg_rows=lc, silu=True)
    y2 = ssd_scan(xbc, dt_raw, dt_bias, a_log, lc // SSD_CHUNK)
    out = ssd_gate_project(xx, y2, xbc, z, d_skip, norm_g, gt, w_out.astype(BF16), n0, tm=tm)
    return out[:, lc:], out[:, :lc]


POOL_CHUNK = 256
POOL_MAX_HALF = max(POOL_WINDOWS) // 2


def _pool_band_tables(width):
    import numpy as np
    pos = np.arange(POOL_CHUNK)
    line, col = pos // width, pos % width
    out = []
    for w in POOL_WINDOWS:
        inside = (col[None, :] >= col[:, None] - w // 2) & (col[None, :] <= col[:, None] + (w - w // 2) - 1)
        out.append((inside & (line[None, :] == line[:, None])).astype(np.float32))
    return np.stack(out)


def _pool_kernel(u_ref, band_ref, o_ref, cs_ref, *, width, n_lines):
    l, c = u_ref.shape
    n_chunks = l // POOL_CHUNK
    assert POOL_WINDOWS == tuple(2 << g for g in range(POOL_GROUPS)) and width & (width - 1) == 0
    half = lax.shift_left(jnp.int32(1), pl.program_id(1) // (POOL_GROUP_DIM // LANES))
    log_w = width.bit_length() - 1
    pad = POOL_MAX_HALF * width if n_lines > 1 else 0
    if pad:
        cs_ref[0:pad, :] = jnp.zeros((pad, c), F32)
        cs_ref[pad + l:2 * pad + l, :] = jnp.zeros((pad, c), F32)

    def col_pass(i, carry):
        r = pl.multiple_of(i * POOL_CHUNK, POOL_CHUNK)
        u = u_ref[pl.ds(r, POOL_CHUNK), :]
        hi = u.astype(BF16)
        lo = (u - hi.astype(F32)).astype(BF16)
        cs_ref[pl.ds(pad + r, POOL_CHUNK), :] = (jnp.dot(band_ref[0], hi, preferred_element_type=F32)
                                                 + jnp.dot(band_ref[0], lo, preferred_element_type=F32))
        return carry

    lax.fori_loop(0, n_chunks, col_pass, 0)

    def out_pass(i, carry):
        r = pl.multiple_of(i * POOL_CHUNK, POOL_CHUNK)
        pos = r + lax.broadcasted_iota(jnp.int32, (POOL_CHUNK, c), 0)
        col = pos & (width - 1)
        cnt = jnp.minimum(col + half, width) - jnp.maximum(col - half, 0)
        if n_lines > 1:
            def add_line(k, acc):
                return acc + cs_ref[pl.ds(pl.multiple_of(pad + r + k * width, SUBLANES), POOL_CHUNK), :]
            s = lax.fori_loop(-half, half, add_line, jnp.zeros((POOL_CHUNK, c), F32))
            line = lax.shift_right_logical(pos, log_w)
            cnt_l = jnp.minimum(line + half, n_lines) - jnp.maximum(line - half, 0)
            mean = s / (cnt_l.astype(F32) * cnt.astype(F32))
        else:
            mean = cs_ref[pl.ds(r, POOL_CHUNK), :] / cnt.astype(F32)
        o_ref[pl.ds(r, POOL_CHUNK), :] = (mean - u_ref[pl.ds(r, POOL_CHUNK), :]).astype(o_ref.dtype)
        return carry

    lax.fori_loop(0, n_chunks, out_pass, 0)


def pool_tokens(u, width):
    b, l, d = u.shape
    n_lines = l // width
    bands = jnp.asarray(_pool_band_tables(width)).astype(BF16)
    tiles_per_group = POOL_GROUP_DIM // LANES
    pad = POOL_MAX_HALF * width if n_lines > 1 else 0
    return pl.pallas_call(
        functools.partial(_pool_kernel, width=width, n_lines=n_lines),
        grid=(b, d // LANES),
        in_specs=[pl.BlockSpec((None, l, LANES), lambda i, j: (i, 0, j)),
                  pl.BlockSpec((1, POOL_CHUNK, POOL_CHUNK), lambda i, j: (j // tiles_per_group, 0, 0))],
        out_specs=pl.BlockSpec((None, l, LANES), lambda i, j: (i, 0, j)),
        out_shape=jax.ShapeDtypeStruct((b, l, d), BF16),
        scratch_shapes=[pltpu.VMEM((l + 2 * pad, LANES), F32)],
        compiler_params=_params(("parallel", "parallel")),
        name="pool_tokens",
    )(u, bands)


def _pool_mix_kernel(x_ref, p_ref, w_ref, b_ref, sc_ref, gt_ref, o_ref):
    ys = [jnp.dot(p_ref[0, :, g * POOL_GROUP_DIM:(g + 1) * POOL_GROUP_DIM], w_ref[g], preferred_element_type=F32)
          for g in range(POOL_GROUPS)]
    y = (jnp.concatenate(ys, axis=1) + b_ref[...]) * sc_ref[...]
    o_ref[0] = x_ref[0] + gt_ref[0, 0] * y


def pool_mix_residual(x, pooled, w, bias, scale, gate, tm=512):
    b, l, d = x.shape
    tm = min(tm, l)
    return pl.pallas_call(
        _pool_mix_kernel,
        grid=(b, l // tm),
        in_specs=[pl.BlockSpec((1, tm, d), lambda i, j: (i, j, 0)),
                  pl.BlockSpec((1, tm, d), lambda i, j: (i, j, 0)),
                  _resident(w.shape), _resident((1, d)), _resident((1, d)), _mod_spec(d, 0)],
        out_specs=pl.BlockSpec((1, tm, d), lambda i, j: (i, j, 0)),
        out_shape=jax.ShapeDtypeStruct((b, l, d), F32),
        compiler_params=_params(("parallel", "parallel")),
        name="pool_mix_residual",
    )(x, pooled, w.astype(BF16), bias.astype(F32).reshape(1, d), scale.astype(F32).reshape(1, d), gate)


def pool_mixer(x, shift, scale_mod, gate, g1, w, bias, scale, width):
    (u,) = modulate_project(x, shift, scale_mod, g1, [])
    return pool_mix_residual(x, pool_tokens(u, width), w, bias, scale, gate)


SLAB_PAD = 8


def _round_up(v, m):
    return (v + m - 1) // m * m


def _fft_plan(l):
    n = 2 * l
    na = 1 << (n.bit_length() // 2)
    nb = n // na
    ns = na // 2 + 1
    n_pairs = (ns + 1) // 2
    return dict(l=l, n=n, na=na, nb=nb, ns=ns, nsp=_round_up(ns + 1, SUBLANES), pitch=2 * nb + SLAB_PAD,
                n_pairs=n_pairs, tb_unroll=min(nb, 16),
                pair_unroll=max(u for u in range(1, 12) if n_pairs % u == 0))


def _fft_tables(l):
    import numpy as np
    p = _fft_plan(l)
    n, na, nb, ns, nsp = p["n"], p["na"], p["nb"], p["ns"], p["nsp"]
    half = na // 2
    ka = np.arange(ns)[None, :, None]
    tb = np.arange(nb)[:, None, None]

    def stage1(ta):
        th = 2.0 * np.pi * (ta[None, None, :] * ka / na + tb * ka / n)
        m = np.zeros((nb, 2 * nsp, ta.shape[0]))
        m[:, :ns] = np.cos(th)
        m[:, nsp:nsp + ns] = -np.sin(th)
        return m

    f1 = stage1(np.arange(half))
    f1k = np.zeros((nb, 2 * nsp, na))
    f1k[:, :, :half] = f1
    f1k[1:, :, half:] = stage1(na - 1 - np.arange(half))[1:]
    tb0 = stage1(na - np.arange(half))[0]
    tb0[:, 0] = 0.0
    f1k[0, :, half:] = tb0

    k2 = np.arange(nb)
    ang = 2.0 * np.pi * np.outer(k2, k2) / nb
    c, s = np.cos(ang), np.sin(ang)
    f2 = np.block([[c, s], [-s, c]])
    g2 = np.block([[c, -s], [s, c]])

    ta = np.arange(half)[None, :, None]
    kk = np.arange(ns)[None, None, :]
    tbb = np.arange(nb)[:, None, None]
    ph = 2.0 * np.pi * (ta * kk / na + tbb * kk / n)
    wgt = np.where((kk == 0) | (kk == na // 2), 1.0, 2.0) / n
    g1 = np.zeros((nb, half, 2 * nsp))
    g1[:, :, :ns] = wgt * np.cos(ph)
    g1[:, :, nsp:nsp + ns] = -wgt * np.sin(ph)
    f32 = np.float32
    return p, f1.astype(f32), f1k.astype(f32), f2.astype(f32), g2.astype(f32), g1.astype(f32)


def _fft_stage1(gather, f1_ref, s_ref, p):
    nb, nsp, pitch = p["nb"], p["nsp"], p["pitch"]

    def body(tb, carry):
        a = jnp.dot(f1_ref[tb], gather(tb), preferred_element_type=F32)
        s_ref[pl.ds(tb, nsp, stride=pitch), :] = a[:nsp]
        s_ref[pl.ds(nb + tb, nsp, stride=pitch), :] = a[nsp:]
        return carry

    lax.fori_loop(0, nb, body, 0, unroll=p["tb_unroll"])


def _slab_pair(s_ref, i, p):
    nb, pitch = p["nb"], p["pitch"]
    r0 = pl.multiple_of(2 * i * pitch, SUBLANES)
    r1 = pl.multiple_of(2 * i * pitch + pitch, SUBLANES)
    return jnp.concatenate([s_ref[pl.ds(r0, 2 * nb), :], s_ref[pl.ds(r1, 2 * nb), :]], axis=1)


FILTER_ROWS = 512


def _filter_hidden_kernel(z_ref, w1_ref, b1_ref, w2_ref, b2_ref, fr_ref, o_ref):
    h = jnp.sin(fr_ref[0:1, :] * (jnp.dot(z_ref[...], w1_ref[...], preferred_element_type=F32, precision=HIGHEST)
                                  + b1_ref[...]))
    o_ref[...] = jnp.sin(fr_ref[1:2, :] * (jnp.dot(h, w2_ref[...], preferred_element_type=F32, precision=HIGHEST)
                                           + b2_ref[...]))


def _filter_out_kernel(h_ref, t_ref, w3f_ref, w3b_ref, dl_ref, o_ref, *, rows):
    l, tn = o_ref.shape[2], o_ref.shape[3]
    wf, wb, dl = w3f_ref[...], w3b_ref[...], dl_ref[...]

    def fill(i, ss):
        r = pl.multiple_of(i * rows, rows)
        h = h_ref[pl.ds(r, rows), :]
        decay = jnp.exp(-t_ref[pl.ds(r, rows), :] * dl)
        hf = _dot_split(h, wf) * decay
        hb = _dot_split(h, wb) * decay
        o_ref[0, 0, pl.ds(r, rows), :] = hf
        o_ref[0, 1, pl.ds(r, rows), :] = hb
        return ss + jnp.sum(hf * hf + hb * hb, axis=0, keepdims=True)

    ss = lax.fori_loop(0, l // rows, fill, jnp.zeros((1, tn), F32))
    scale = lax.rsqrt(ss + EPS)

    def rescale(i, carry):
        r = pl.multiple_of(i * rows, rows)
        o_ref[0, 0, pl.ds(r, rows), :] = o_ref[0, 0, pl.ds(r, rows), :] * scale
        o_ref[0, 1, pl.ds(r, rows), :] = o_ref[0, 1, pl.ds(r, rows), :] * scale
        return carry

    lax.fori_loop(0, l // rows, rescale, 0)


def hyena_filters(l, fw1, fb1, fw2, fb2, fw3, ffreq, tn=LANES):
    d = D_MODEL
    pos = jnp.arange(l, dtype=F32)
    t = jnp.linspace(0.0, 1.0, l, dtype=F32)
    wpos = 2.0 * math.pi * pos / l
    f = jnp.linspace(1e-4, HY_BANDS - 1, HY_BANDS, dtype=F32)
    ang = wpos[:, None] * f[None, :]
    emb_pad = _round_up(HY_EMB_DIM, SUBLANES)
    z = jnp.concatenate([t[:, None], jnp.cos(ang), -jnp.sin(ang), jnp.zeros((l, emb_pad - HY_EMB_DIM), F32)], axis=-1)
    w1 = jnp.concatenate([fw1.astype(F32), jnp.zeros((emb_pad - HY_EMB_DIM, fw1.shape[1]), F32)], axis=0)
    deltas = jnp.abs(jnp.linspace(HY_MIN_DECAY, HY_MAX_DECAY, d, dtype=F32)).reshape(1, d)
    hid = fw2.shape[0]
    rows = min(FILTER_ROWS, l)
    hidden = pl.pallas_call(
        _filter_hidden_kernel,
        grid=(l // rows,),
        in_specs=[pl.BlockSpec((rows, emb_pad), lambda r: (r, 0)), _resident((emb_pad, hid)), _resident((1, hid)),
                  _resident((hid, hid)), _resident((1, hid)), _resident((2, hid))],
        out_specs=pl.BlockSpec((rows, hid), lambda r: (r, 0)),
        out_shape=jax.ShapeDtypeStruct((l, hid), F32),
        compiler_params=_params(("parallel",)),
        name="hyena_filter_hidden",
    )(z, w1, fb1.astype(F32).reshape(1, hid), fw2.astype(F32), fb2.astype(F32).reshape(1, hid), ffreq.astype(F32))
    nt = d // tn
    w3 = fw3.astype(F32)
    return pl.pallas_call(
        functools.partial(_filter_out_kernel, rows=rows),
        grid=(HY_ORDER, nt),
        in_specs=[_resident((l, hid)), _resident((l, 1)),
                  pl.BlockSpec((hid, tn), lambda o, j: (0, (2 * o) * nt + j)),
                  pl.BlockSpec((hid, tn), lambda o, j: (0, (2 * o + 1) * nt + j)),
                  pl.BlockSpec((1, tn), lambda o, j: (0, j))],
        out_specs=pl.BlockSpec((1, 2, l, tn), lambda o, j: (o, 0, 0, j)),
        out_shape=jax.ShapeDtypeStruct((HY_ORDER, 2, l, d), F32),
        compiler_params=_params(("parallel", "parallel")),
        name="hyena_filters",
    )(hidden, t.reshape(l, 1), w3, w3, deltas)


def _filter_spectrum_kernel(hf_ref, hb_ref, f1_ref, f2_ref, o_ref, s_ref, *, p):
    na, nb = p["na"], p["nb"]
    half = na // 2

    def gather(tb):
        fwd = hf_ref[pl.ds(tb, half, stride=nb), :]
        bwd = hb_ref[pl.ds(jnp.where(tb == 0, 0, nb - tb), half, stride=nb), :]
        return jnp.concatenate([fwd, bwd], axis=0).astype(BF16)

    _fft_stage1(gather, f1_ref, s_ref, p)

    def body(i, carry):
        spec = jnp.dot(f2_ref[...], _slab_pair(s_ref, i, p).astype(BF16), preferred_element_type=F32)
        r = pl.multiple_of(i * 4 * nb, SUBLANES)
        o_ref[pl.ds(r, 2 * nb), :] = spec[:, :LANES].astype(BF16)
        o_ref[pl.ds(r + 2 * nb, 2 * nb), :] = spec[:, LANES:].astype(BF16)
        return carry

    lax.fori_loop(0, p["n_pairs"], body, 0, unroll=p["pair_unroll"])


def hyena_filter_spectrum(filt, tables):
    p, _, f1k, f2, _, _ = tables
    order, _, l, d = filt.shape
    rows = 2 * p["n_pairs"] * 2 * p["nb"]
    return pl.pallas_call(
        functools.partial(_filter_spectrum_kernel, p=p),
        grid=(order, d // LANES),
        in_specs=[pl.BlockSpec((None, None, l, LANES), lambda o, j: (o, 0, 0, j)),
                  pl.BlockSpec((None, None, l, LANES), lambda o, j: (o, 1, 0, j)),
                  _resident(f1k.shape), _resident(f2.shape)],
        out_specs=pl.BlockSpec((None, rows, LANES), lambda o, j: (o, 0, j)),
        out_shape=jax.ShapeDtypeStruct((order, rows, d), BF16),
        scratch_shapes=[pltpu.VMEM((p["nsp"] * p["pitch"], LANES), F32)],
        compiler_params=_params(("parallel", "parallel")),
        name="hyena_filter_spectrum",
    )(filt, filt, jnp.asarray(f1k).astype(BF16), jnp.asarray(f2).astype(BF16))


def _longconv_kernel(a_ref, m_ref, k_ref, bias_ref, f1_ref, f2_ref, g2_ref, g1_ref, o_ref, s_ref, *, p, row_chunk):
    na, nb, nsp, pitch = p["na"], p["nb"], p["nsp"], p["pitch"]
    half = na // 2

    _fft_stage1(lambda tb: a_ref[pl.ds(tb, half, stride=nb), :].astype(BF16), f1_ref, s_ref, p)

    def mid(i, carry):
        x = jnp.dot(f2_ref[...], _slab_pair(s_ref, i, p).astype(BF16), preferred_element_type=F32)
        r = pl.multiple_of(i * 4 * nb, SUBLANES)
        kk = jnp.concatenate([k_ref[pl.ds(r, 2 * nb), :], k_ref[pl.ds(r + 2 * nb, 2 * nb), :]], axis=1).astype(F32)
        xr, xi, kr, ki = x[:nb], x[nb:], kk[:nb], kk[nb:]
        y = jnp.concatenate([xr * kr - xi * ki, xr * ki + xi * kr], axis=0).astype(BF16)
        bq = jnp.dot(g2_ref[...], y, preferred_element_type=F32)
        r0 = pl.multiple_of(2 * i * pitch, SUBLANES)
        r1 = pl.multiple_of(2 * i * pitch + pitch, SUBLANES)
        s_ref[pl.ds(r0, 2 * nb), :] = bq[:, :LANES]
        s_ref[pl.ds(r1, 2 * nb), :] = bq[:, LANES:]
        return carry

    lax.fori_loop(0, p["n_pairs"], mid, 0, unroll=p["pair_unroll"])

    def last(tb, carry):
        bq = jnp.concatenate([s_ref[pl.ds(tb, nsp, stride=pitch), :], s_ref[pl.ds(nb + tb, nsp, stride=pitch), :]], axis=0)
        o_ref[pl.ds(tb, half, stride=nb), :] = jnp.dot(g1_ref[tb], bq.astype(BF16), preferred_element_type=F32)
        return carry

    lax.fori_loop(0, nb, last, 0, unroll=p["tb_unroll"])

    def finish(i, carry):
        r = pl.multiple_of(i * row_chunk, SUBLANES)
        a = a_ref[pl.ds(r, row_chunk), :]
        o_ref[pl.ds(r, row_chunk), :] = m_ref[pl.ds(r, row_chunk), :] * (o_ref[pl.ds(r, row_chunk), :] + bias_ref[...] * a)
        return carry

    lax.fori_loop(0, p["l"] // row_chunk, finish, 0)


def hyena_longconv(a, a_col, m, m_col, kspec, bias, tables):
    p, f1, _, f2, g2, g1 = tables
    b, l, _ = a.shape
    d = D_MODEL
    nt = d // LANES
    rows = kspec.shape[0]
    tabs = [jnp.asarray(t).astype(BF16) for t in (f1, f2, g2, g1)]
    return pl.pallas_call(
        functools.partial(_longconv_kernel, p=p, row_chunk=min(512, l)),
        grid=(nt, b),
        in_specs=[pl.BlockSpec((None, l, LANES), lambda j, i: (i, 0, a_col * nt + j)),
                  pl.BlockSpec((None, l, LANES), lambda j, i: (i, 0, m_col * nt + j)),
                  pl.BlockSpec((rows, LANES), lambda j, i: (0, j)),
                  pl.BlockSpec((1, LANES), lambda j, i: (0, j))] + [_resident(t.shape) for t in tabs],
        out_specs=pl.BlockSpec((None, l, LANES), lambda j, i: (i, 0, j)),
        out_shape=jax.ShapeDtypeStruct((b, l, d), F32),
        scratch_shapes=[pltpu.VMEM((p["nsp"] * p["pitch"], LANES), F32)],
        compiler_params=_params(("parallel", "parallel")),
        name="hyena_longconv",
    )(a, m, kspec, bias.astype(F32).reshape(1, d), *tabs)


def hyena_core(p_raw, conv_w, conv_b, fw1, fb1, fw2, fb2, fw3, ffreq, hbias):
    l = p_raw.shape[1]
    tables = _fft_tables(l)
    filt = hyena_filters(l, fw1, fb1, fw2, fb2, fw3, ffreq)
    kspec = hyena_filter_spectrum(filt, tables)
    pc = dwconv_tokens(p_raw, conv_w, conv_b)
    z = hyena_longconv(pc, 0, pc, 1, kspec[0], hbias[0], tables)
    return hyena_longconv(z, 0, pc, 2, kspec[1], hbias[1], tables)


def kernel(x, c, ctx, c_ctx, ada_w, ada_b, norm_g, ffn_w_gate, ffn_w_up, ffn_w_down, ssd_w_in, ssd_conv_w, ssd_conv_b, ssd_a_log, ssd_dt_bias, ssd_d, ssd_norm_g, ssd_w_out, pool_w, pool_b, pool_scale, hy_w_in, hy_conv_w, hy_conv_b, hy_filt_w1, hy_filt_b1, hy_filt_w2, hy_filt_b2, hy_filt_w3, hy_filt_freq, hy_bias, hy_w_out, final_g):
    batch = x.shape[0]
    d = D_MODEL
    h = ctx

    s = jnp.concatenate([jax.nn.silu(c), jax.nn.silu(c_ctx)[None], jnp.zeros((7 - batch, d), F32)], axis=0)
    mods = ada_modulation(s, ada_w, ada_b).reshape(DEPTH, 8, N_MOD, d)

    wg_bf, wu_bf, wd_bf = ffn_w_gate.astype(BF16), ffn_w_up.astype(BF16), ffn_w_down.astype(BF16)
    fg = final_g.reshape(1, d)

    for i in range(DEPTH):
        kind = i % N_MIXERS
        j = i // N_MIXERS
        last = i == DEPTH - 1
        ctx_in_needed = (not last) or kind == 0
        m = [mods[i, :batch, k].reshape(batch, 1, 1, d) for k in range(N_MOD)]
        mc = [jnp.broadcast_to(mods[i, batch, k].reshape(1, 1, 1, d), (batch, 1, 1, d)) for k in range(N_MOD)]
        g0, g1, g2 = (norm_g[i, k].reshape(1, d) for k in range(3))

        x = ffn_step(x, m[0], m[1], m[2], g0, wg_bf[i, 0], wu_bf[i, 0], wd_bf[i, 0])
        if ctx_in_needed:
            h = ffn_step(h, mc[0], mc[1], mc[2], g0, wg_bf[i, 0], wu_bf[i, 0], wd_bf[i, 0])

        if kind == 0:
            x, h_new = ssd_mixer(x, h, (m[3], m[4], m[5]), (mc[3], mc[4], mc[5]), g1, ssd_w_in[j], ssd_conv_w[j],
                                 ssd_conv_b[j], ssd_a_log[j], ssd_dt_bias[j], ssd_d[j], ssd_norm_g[j], ssd_w_out[j])
            if not last:
                h = h_new
        elif kind == 1:
            x = pool_mixer(x, m[3], m[4], m[5], g1, pool_w[j], pool_b[j].reshape(-1), pool_scale[j], GRID_W)
            if not last:
                h = pool_mixer(h, mc[3], mc[4], mc[5], g1, pool_w[j], pool_b[j].reshape(-1), pool_scale[j], h.shape[1])
        else:
            w_in = hy_w_in[j].astype(BF16)
            w_out = hy_w_out[j].astype(BF16)
            filt = (hy_filt_w1[j], hy_filt_b1[j], hy_filt_w2[j], hy_filt_b2[j], hy_filt_w3[j], hy_filt_freq[j])
            (p_lat,) = modulate_project(x, m[3], m[4], g1, [w_in])
            x = project_residual(x, hyena_core(p_lat, hy_conv_w[j], hy_conv_b[j], *filt, hy_bias[j]), m[5], w_out)
            if not last:
                (p_ctx,) = modulate_project(h, mc[3], mc[4], g1, [w_in])
                h = project_residual(h, hyena_core(p_ctx, hy_conv_w[j], hy_conv_b[j], *filt, hy_bias[j]), mc[5], w_out)

        x = ffn_step(x, m[6], m[7], m[8], g2, wg_bf[i, 1], wu_bf[i, 1], wd_bf[i, 1],
                     final_g=fg if last else None)
        if not last:
            h = ffn_step(h, mc[6], mc[7], mc[8], g2, wg_bf[i, 1], wu_bf[i, 1], wd_bf[i, 1])
    return x
```

```python
import functools
import math

import jax
import jax.numpy as jnp
from jax import lax
from jax.experimental import pallas as pl
from jax.experimental.pallas import tpu as pltpu

F32 = jnp.float32
BF16 = jnp.bfloat16
HIGHEST = lax.Precision.HIGHEST

D_MODEL = 1024
DEPTH = 4
GRID_W = 64
N_MIXERS = 3
D_FF = 2816
N_MOD = 9
EPS = 1e-6

SSD_D_INNER = 2 * D_MODEL
SSD_HEAD_DIM = 64
SSD_N_HEADS = SSD_D_INNER // SSD_HEAD_DIM
SSD_N_GROUPS = 4
SSD_HEADS_PER_GROUP = SSD_N_HEADS // SSD_N_GROUPS
SSD_D_STATE = 128
SSD_CHUNK = 128
SSD_GN = SSD_N_GROUPS * SSD_D_STATE
SSD_CONV_DIM = SSD_D_INNER + 2 * SSD_GN
SSD_GROUP_WIDTH = SSD_D_INNER // SSD_N_GROUPS

POOL_WINDOWS = (2, 4, 8, 16)
POOL_GROUPS = 4
POOL_GROUP_DIM = D_MODEL // POOL_GROUPS

HY_ORDER = 2
HY_EMB_DIM = 33
HY_BANDS = (HY_EMB_DIM - 1) // 2
HY_MAX_DECAY = math.log(1e-2) / 0.3
HY_MIN_DECAY = math.log(1e-2) / 1.5

VMEM_LIMIT_BYTES = 56 * 1024 * 1024
SUBLANES = 8
LANES = 128


def _params(sem):
    return pltpu.CompilerParams(dimension_semantics=sem, vmem_limit_bytes=VMEM_LIMIT_BYTES)


def _resident(shape):
    nd = len(shape)
    return pl.BlockSpec(shape, lambda *_: (0,) * nd, pipeline_mode=pl.Buffered(1))


def _mod_spec(d, _unused=0):
    return pl.BlockSpec((1, 1, 1, d), lambda i, j: (i, 0, 0, 0))


def _modulated(x, g, shift, scale):
    ms = jnp.mean(x * x, axis=-1, keepdims=True)
    return (x * lax.rsqrt(ms + EPS)) * g * (1.0 + scale) + shift


def _silu(v):
    return v * jax.nn.sigmoid(v)


def _ada_kernel(s_ref, w_ref, b_ref, o_ref):
    o_ref[0] = jnp.dot(s_ref[...], w_ref[0], preferred_element_type=F32, precision=HIGHEST) + b_ref[0]


def ada_modulation(s, ada_w, ada_b, tn=1024):
    depth, d, n = ada_w.shape
    r = s.shape[0]
    return pl.pallas_call(
        _ada_kernel,
        grid=(depth, n // tn),
        in_specs=[pl.BlockSpec((r, d), lambda i, j: (0, 0)),
                  pl.BlockSpec((1, d, tn), lambda i, j: (i, 0, j)),
                  pl.BlockSpec((1, 1, tn), lambda i, j: (i, 0, j))],
        out_specs=pl.BlockSpec((1, r, tn), lambda i, j: (i, 0, j)),
        out_shape=jax.ShapeDtypeStruct((depth, r, n), F32),
        compiler_params=_params(("parallel", "parallel")),
        name="ada_modulation",
    )(s, ada_w, ada_b.reshape(depth, 1, n))


def _ffn_kernel(x_ref, sh_ref, sc_ref, gt_ref, g_ref, wg_ref, wu_ref, wd_ref, *rest, f_chunk, final):
    if final:
        fg_ref, o_ref = rest
    else:
        (o_ref,) = rest
    x = x_ref[0]
    u = _modulated(x, g_ref[...], sh_ref[0, 0], sc_ref[0, 0]).astype(BF16)
    acc = jnp.zeros(x.shape, F32)
    d_ff = wg_ref.shape[1]
    for f0 in range(0, d_ff, f_chunk):
        a = jnp.dot(u, wg_ref[:, f0:f0 + f_chunk], preferred_element_type=F32)
        b = jnp.dot(u, wu_ref[:, f0:f0 + f_chunk], preferred_element_type=F32)
        h = (_silu(a) * b).astype(BF16)
        acc = acc + jnp.dot(h, wd_ref[f0:f0 + f_chunk, :], preferred_element_type=F32)
    y = x + (0.5 * gt_ref[0, 0]) * acc
    if final:
        ms = jnp.mean(y * y, axis=-1, keepdims=True)
        y = y * lax.rsqrt(ms + EPS) * fg_ref[...]
    o_ref[0] = y


def ffn_step(x, shift, scale, gate, g, wg, wu, wd, final_g=None, tm=512, f_chunk=256):
    b, l, d = x.shape
    tm = min(tm, l)
    f = wg.shape[1]
    final = final_g is not None
    mod_spec = _mod_spec(d, 0)
    in_specs = [pl.BlockSpec((1, tm, d), lambda i, j: (i, j, 0)),
                mod_spec, mod_spec, mod_spec,
                _resident((1, d)), _resident((d, f)), _resident((d, f)), _resident((f, d))]
    args = [x, shift, scale, gate, g, wg, wu, wd]
    if final:
        in_specs.append(_resident((1, d)))
        args.append(final_g)
    return pl.pallas_call(
        functools.partial(_ffn_kernel, f_chunk=f_chunk, final=final),
        grid=(b, l // tm),
        in_specs=in_specs,
        out_specs=pl.BlockSpec((1, tm, d), lambda i, j: (i, j, 0)),
        out_shape=jax.ShapeDtypeStruct((b, l, d), F32),
        compiler_params=_params(("parallel", "parallel")),
        name="ffn_step",
    )(*args)


def _modproj_kernel(x_ref, sh_ref, sc_ref, g_ref, *rest, n_w, n_dest):
    w_refs, o_refs = rest[:n_w], rest[n_w + n_dest:]
    u = _modulated(x_ref[0], g_ref[...], sh_ref[0, 0], sc_ref[0, 0])
    if n_w == 0:
        o_refs[0][0] = u
        return
    u = u.astype(BF16)
    for w_ref, o_ref in zip(w_refs, o_refs):
        o_ref[0] = jnp.dot(u, w_ref[...], preferred_element_type=F32).astype(o_ref.dtype)


def modulate_project(x, shift, scale, g, weights, tm=256, out_dtypes=None, out_rows=None, row_block=0, dest=None):
    b, l, d = x.shape
    tm = min(tm, l)
    mod_spec = _mod_spec(d)
    in_specs = [pl.BlockSpec((1, tm, d), lambda i, j: (i, j, 0)), mod_spec, mod_spec, _resident((1, d))]
    in_specs += [_resident(w.shape) for w in weights]
    widths = [w.shape[1] for w in weights] or [d]
    out_dtypes = out_dtypes or [F32] * len(widths)
    dest = list(dest or [])
    in_specs += [pl.BlockSpec(memory_space=pl.ANY)] * len(dest)
    n_in = 4 + len(weights)
    return pl.pallas_call(
        functools.partial(_modproj_kernel, n_w=len(weights), n_dest=len(dest)),
        grid=(b, l // tm),
        in_specs=in_specs,
        out_specs=[pl.BlockSpec((1, tm, n), lambda i, j: (i, j + row_block, 0)) for n in widths],
        out_shape=[jax.ShapeDtypeStruct((b, out_rows or l, n), dt) for n, dt in zip(widths, out_dtypes)],
        input_output_aliases={n_in + k: k for k in range(len(dest))},
        compiler_params=_params(("parallel", "parallel")),
        name="modulate_project",
    )(x, shift, scale, g, *weights, *dest)


def _outproj_kernel(x_ref, a_ref, gt_ref, w_ref, o_ref):
    y = jnp.dot(a_ref[0].astype(BF16), w_ref[...], preferred_element_type=F32)
    o_ref[0] = x_ref[0] + gt_ref[0, 0] * y


def project_residual(x, a, gate, w, tm=512):
    b, l, d = x.shape
    k = a.shape[-1]
    tm = min(tm, l)
    return pl.pallas_call(
        _outproj_kernel,
        grid=(b, l // tm),
        in_specs=[pl.BlockSpec((1, tm, d), lambda i, j: (i, j, 0)),
                  pl.BlockSpec((1, tm, k), lambda i, j: (i, j, 0)),
                  _mod_spec(d, 0),
                  _resident((k, d))],
        out_specs=pl.BlockSpec((1, tm, d), lambda i, j: (i, j, 0)),
        out_shape=jax.ShapeDtypeStruct((b, l, d), F32),
        compiler_params=_params(("parallel", "parallel")),
        name="project_residual",
    )(x, a, gate, w)


def _dwconv_kernel(prev_ref, x_ref, next_ref, w_ref, b_ref, o_ref, *, k_w, n_rows, seg_rows, silu):
    tr, c = x_ref.shape[1], x_ref.shape[2]
    halo = prev_ref.shape[1]
    p = k_w // 2
    act = _silu if silu else (lambda v: v)
    ext = jnp.concatenate([prev_ref[0], x_ref[0], next_ref[0]], axis=0).astype(F32)
    acc = jnp.broadcast_to(b_ref[...], (tr, c))
    for k in range(k_w):
        acc = acc + w_ref[k:k + 1, :] * ext[halo + k - p:halo + k - p + tr, :]
    o_ref[0] = act(acc).astype(o_ref.dtype)

    edges = sorted({0, n_rows} | ({seg_rows} if seg_rows else set()))
    n_tiles = n_rows // halo
    special = sorted({t for e in edges for t in ((e - p) // halo, (e + p - 1) // halo) if 0 <= t < n_tiles})
    tiles_per_block = tr // halo
    for tile in special:
        lo = (tile % tiles_per_block) * halo

        @pl.when(pl.program_id(2) == tile // tiles_per_block)
        def _(tile=tile, lo=lo):
            row = tile * halo + lax.broadcasted_iota(jnp.int32, (halo, c), 0)
            acc_t = jnp.broadcast_to(b_ref[...], (halo, c))
            for k in range(k_w):
                src = row + (k - p)
                valid = (src >= 0) & (src < n_rows)
                for e in edges[1:-1]:
                    valid = valid & jnp.logical_not((jnp.minimum(row, src) < e) & (jnp.maximum(row, src) >= e))
                tap = ext[halo + lo + k - p:2 * halo + lo + k - p, :]
                acc_t = acc_t + w_ref[k:k + 1, :] * jnp.where(valid, tap, 0.0)
            o_ref[0, lo:lo + halo, :] = act(acc_t).astype(o_ref.dtype)


def dwconv_tokens(x, w, bias, seg_rows=0, silu=False, tr=1024, ct=512):
    b, t, c = x.shape
    k_w = w.shape[0]
    halo = SUBLANES * (4 // x.dtype.itemsize)
    tr = next(r for r in range(min(tr, t), 0, -halo) if t % r == 0)
    ct = min(ct, c)
    rb = tr // halo
    last_halo = t // halo - 1
    return pl.pallas_call(
        functools.partial(_dwconv_kernel, k_w=k_w, n_rows=t, seg_rows=seg_rows, silu=silu),
        grid=(b, c // ct, t // tr),
        in_specs=[pl.BlockSpec((1, halo, ct), lambda i, j, r: (i, jnp.maximum(r * rb - 1, 0), j)),
                  pl.BlockSpec((1, tr, ct), lambda i, j, r: (i, r, j)),
                  pl.BlockSpec((1, halo, ct), lambda i, j, r: (i, jnp.minimum((r + 1) * rb, last_halo), j)),
                  pl.BlockSpec((k_w, ct), lambda i, j, r: (0, j)),
                  pl.BlockSpec((1, ct), lambda i, j, r: (0, j))],
        out_specs=pl.BlockSpec((1, tr, ct), lambda i, j, r: (i, r, j)),
        out_shape=jax.ShapeDtypeStruct((b, t, c), x.dtype),
        compiler_params=_params(("parallel", "parallel", "parallel")),
        name="dwconv_tokens",
    )(x, x, x, w.astype(F32), bias.astype(F32).reshape(1, c))


def _bf16_terms(v):
    hi = v.astype(BF16)
    r = v - hi.astype(F32)
    mid = r.astype(BF16)
    return hi, mid, (r - mid.astype(F32)).astype(BF16)


def _dot_f32_rhs(m, v):
    return sum(jnp.dot(m, t, preferred_element_type=F32) for t in _bf16_terms(v))


def _dot_f32_lhs(v, m):
    return sum(jnp.dot(t, m, preferred_element_type=F32) for t in _bf16_terms(v))


def _dot_split(a, b):
    a_hi, a_lo, _ = _bf16_terms(a)
    b_hi, b_lo, _ = _bf16_terms(b)
    return (jnp.dot(a_hi, b_hi, preferred_element_type=F32) + jnp.dot(a_lo, b_hi, preferred_element_type=F32)
            + jnp.dot(a_hi, b_lo, preferred_element_type=F32))


def _ssd_scan_kernel(xbc_ref, dt_ref, dtt_ref, dtb_ref, dtbt_ref, a_ref, at_ref, o_ref, h_ref):
    nh, q, hp = SSD_N_HEADS, SSD_CHUNK, SSD_HEAD_DIM
    d = pl.program_id(1)
    fwd = d == 0

    @pl.when(pl.program_id(2) == 0)
    def _():
        h_ref[...] = jnp.zeros(h_ref.shape, F32)

    dt2 = jax.nn.softplus(dt_ref[0] + dtb_ref[...])
    dt = jnp.where(fwd, dt2[:, :nh], dt2[:, nh:])
    a = dt * jnp.where(fwd, a_ref[:, :nh], a_ref[:, nh:])
    dtt2 = jax.nn.softplus(dtt_ref[0] + dtbt_ref[...])
    a_t = jnp.where(fwd, dtt2[:nh], dtt2[nh:]) * jnp.where(fwd, at_ref[:nh], at_ref[nh:])

    r_i = lax.broadcasted_iota(jnp.int32, (q, q), 0)
    c_i = lax.broadcasted_iota(jnp.int32, (q, q), 1)
    ahead = jnp.where(fwd, c_i - r_i, r_i - c_i)
    seen = ahead <= 0
    seen_t = ahead >= 0
    cum = _dot_f32_rhs(seen.astype(BF16), a)
    cum_t = _dot_f32_lhs(a_t, seen_t.astype(BF16))
    total = jnp.where(fwd, cum[q - 1:q, :], cum[0:1, :])
    ecum = jnp.exp(cum)
    dtdec = dt * jnp.exp(total - cum)
    etot = jnp.exp(total)
    dt_t = jnp.where(fwd, dtt2[:nh], dtt2[nh:])

    pw = 2 * hp
    left = lax.broadcasted_iota(jnp.int32, (q, pw), 1) < hp
    left_row = lax.broadcasted_iota(jnp.int32, (1, pw), 1) < hp
    pairs = SSD_HEADS_PER_GROUP // 2
    for g in range(SSD_N_GROUPS):
        b_g = xbc_ref[0, :, SSD_D_INNER + g * SSD_D_STATE:SSD_D_INNER + (g + 1) * SSD_D_STATE].astype(BF16)
        c_lo = SSD_D_INNER + SSD_GN + g * SSD_D_STATE
        c_g = xbc_ref[0, :, c_lo:c_lo + SSD_D_STATE].astype(BF16)
        cb = lax.dot_general(c_g, b_g, (((1,), (1,)), ((), ())), preferred_element_type=F32)
        h_g = h_ref[g]
        ch = jnp.dot(c_g, h_g.astype(BF16), preferred_element_type=F32)
        xdecs, etots = [], []
        for pr in range(pairs):
            e1 = g * SSD_HEADS_PER_GROUP + 2 * pr
            e2 = e1 + 1
            x_pair = xbc_ref[0, :, e1 * hp:e1 * hp + pw].astype(F32)
            ws = []
            for e in (e1, e2):
                seg = cum[:, e:e + 1] - cum_t[e:e + 1, :]
                ws.append((cb * jnp.exp(jnp.where(seen, seg, -jnp.inf)) * dt_t[e:e + 1, :]).astype(BF16))
            rhs = jnp.concatenate([jnp.where(left, x_pair, 0.0), jnp.where(left, 0.0, x_pair)], axis=0).astype(BF16)
            y_diag = jnp.dot(jnp.concatenate(ws, axis=1), rhs, preferred_element_type=F32)
            y_off = ch[:, pr * pw:(pr + 1) * pw] * jnp.where(left, ecum[:, e1:e1 + 1], ecum[:, e2:e2 + 1])
            o_ref[0, 0, :, e1 * hp:e1 * hp + pw] = (y_diag + y_off).astype(o_ref.dtype)
            xdecs.append((x_pair * jnp.where(left, dtdec[:, e1:e1 + 1], dtdec[:, e2:e2 + 1])).astype(BF16))
            etots.append(jnp.where(left_row, etot[:, e1:e1 + 1], etot[:, e2:e2 + 1]))
        s_g = lax.dot_general(b_g, jnp.concatenate(xdecs, axis=1), (((0,), (0,)), ((), ())),
                              preferred_element_type=F32)
        h_ref[g] = h_g * jnp.concatenate(etots, axis=1) + s_g


def ssd_scan(xbc, dt_raw, dt_bias, a_log, n_lead_chunks):
    b, t, _ = xbc.shape
    q, nh = SSD_CHUNK, SSD_N_HEADS
    nc = t // q
    dt_t = jnp.swapaxes(dt_raw, 1, 2)
    a_neg = -jnp.exp(a_log.astype(F32)).reshape(1, 2 * nh)
    dtb = dt_bias.astype(F32).reshape(1, 2 * nh)

    def chunk_of(d, c):
        back = jnp.where(c < n_lead_chunks, n_lead_chunks - 1 - c, nc - 1 + n_lead_chunks - c)
        return jnp.where(d == 0, c, back)

    return pl.pallas_call(
        _ssd_scan_kernel,
        grid=(b, 2, nc),
        in_specs=[pl.BlockSpec((1, q, SSD_CONV_DIM), lambda i, d, c: (i, chunk_of(d, c), 0)),
                  pl.BlockSpec((1, q, 2 * nh), lambda i, d, c: (i, chunk_of(d, c), 0)),
                  pl.BlockSpec((1, 2 * nh, q), lambda i, d, c: (i, 0, chunk_of(d, c))),
                  _resident((1, 2 * nh)), _resident((2 * nh, 1)),
                  _resident((1, 2 * nh)), _resident((2 * nh, 1))],
        out_specs=pl.BlockSpec((1, 1, q, SSD_D_INNER), lambda i, d, c: (i, d, chunk_of(d, c), 0)),
        out_shape=jax.ShapeDtypeStruct((b, 2, t, SSD_D_INNER), BF16),
        scratch_shapes=[pltpu.VMEM((SSD_N_GROUPS, SSD_D_STATE, SSD_GROUP_WIDTH), F32)],
        compiler_params=_params(("parallel", "parallel", "arbitrary")),
        name="ssd_scan",
    )(xbc, dt_raw, dt_t, dtb, dtb.reshape(2 * nh, 1), a_neg, a_neg.reshape(2 * nh, 1))


def _ssd_out_kernel(x_ref, y0_ref, y1_ref, xs0_ref, xs1_ref, z_ref, dsk_ref, ng_ref, gt_ref, w_ref, o_ref):
    xs = jnp.concatenate([xs0_ref[0], xs1_ref[0]], axis=1).astype(F32)
    y = y0_ref[0, 0].astype(F32) + y1_ref[0, 0].astype(F32) + dsk_ref[...] * xs
    gy = y * _silu(z_ref[0].astype(F32))
    parts = []
    for g in range(SSD_N_GROUPS):
        blk = gy[:, g * SSD_GROUP_WIDTH:(g + 1) * SSD_GROUP_WIDTH]
        parts.append(blk * lax.rsqrt(jnp.mean(blk * blk, axis=-1, keepdims=True) + EPS))
    a = (jnp.concatenate(parts, axis=1) * ng_ref[...]).astype(BF16)
    o_ref[0] = x_ref[0] + gt_ref[0, 0] * jnp.dot(a, w_ref[...], preferred_element_type=F32)


def ssd_gate_project(x, y2, xbc, z, d_skip, norm_g, gate, w_out, src_block, tm=256):
    b, t, d = x.shape
    half = SSD_D_INNER // 2
    dsk = jnp.repeat(d_skip.astype(F32), SSD_HEAD_DIM).reshape(1, SSD_D_INNER)
    return pl.pallas_call(
        _ssd_out_kernel,
        grid=(b, t // tm),
        in_specs=[pl.BlockSpec((1, tm, d), lambda i, j: (i, j, 0)),
                  pl.BlockSpec((1, 1, tm, SSD_D_INNER), lambda i, j: (i, 0, j + src_block, 0)),
                  pl.BlockSpec((1, 1, tm, SSD_D_INNER), lambda i, j: (i, 1, j + src_block, 0)),
                  pl.BlockSpec((1, tm, half), lambda i, j: (i, j + src_block, 0)),
                  pl.BlockSpec((1, tm, half), lambda i, j: (i, j + src_block, 1)),
                  pl.BlockSpec((1, tm, SSD_D_INNER), lambda i, j: (i, j + src_block, 0)),
                  _resident((1, SSD_D_INNER)), _resident((1, SSD_D_INNER)),
                  _mod_spec(d),
                  _resident((SSD_D_INNER, d))],
        out_specs=pl.BlockSpec((1, tm, d), lambda i, j: (i, j, 0)),
        out_shape=jax.ShapeDtypeStruct((b, t, d), F32),
        compiler_params=_params(("parallel", "parallel")),
        name="ssd_gate_project",
    )(x, y2, y2, xbc, xbc, z, dsk, norm_g.astype(F32).reshape(1, SSD_D_INNER), gate, w_out)


def ssd_mixer(x, h, m_lat, m_ctx, g1, w_in, conv_w, conv_b, a_log, dt_bias, d_skip, norm_g, w_out, need_ctx_out, tm=256):
    lc, l = h.shape[1], x.shape[1]
    assert lc % tm == 0
    n0 = lc // tm
    w_bf = w_in.astype(BF16)
    ws = [w_bf[:, :SSD_D_INNER], w_bf[:, SSD_D_INNER:SSD_D_INNER + SSD_CONV_DIM], w_bf[:, SSD_D_INNER + SSD_CONV_DIM:]]
    dts = [BF16, BF16, F32]
    parts = modulate_project(x, m_lat[0], m_lat[1], g1, ws, tm=tm, out_dtypes=dts, out_rows=lc + l, row_block=n0)
    z, xbc_raw, dt_raw = modulate_project(h, m_ctx[0], m_ctx[1], g1, ws, tm=tm, out_dtypes=dts, out_rows=lc + l, dest=parts)
    xbc = dwconv_tokens(xbc_raw, conv_w, conv_b, seg_rows=lc, silu=True)
    y2 = ssd_scan(xbc, dt_raw, dt_bias, a_log, lc // SSD_CHUNK)
    w_out = w_out.astype(BF16)
    x_new = ssd_gate_project(x, y2, xbc, z, d_skip, norm_g, m_lat[2], w_out, n0, tm=tm)
    h_new = ssd_gate_project(h, y2, xbc, z, d_skip, norm_g, m_ctx[2], w_out, 0, tm=tm) if need_ctx_out else None
    return x_new, h_new


POOL_CHUNK = 256
POOL_MAX_HALF = max(POOL_WINDOWS) // 2


def _pool_band_tables(width):
    import numpy as np
    pos = np.arange(POOL_CHUNK)
    line, col = pos // width, pos % width
    out = []
    for w in POOL_WINDOWS:
        inside = (col[None, :] >= col[:, None] - w // 2) & (col[None, :] <= col[:, None] + (w - w // 2) - 1)
        out.append((inside & (line[None, :] == line[:, None])).astype(np.float32))
    return np.stack(out)


def _pool_kernel(u_ref, band_ref, o_ref, cs_ref, *, width, n_lines):
    l, c = u_ref.shape
    n_chunks = l // POOL_CHUNK
    assert POOL_WINDOWS == tuple(2 << g for g in range(POOL_GROUPS)) and width & (width - 1) == 0
    half = lax.shift_left(jnp.int32(1), pl.program_id(1) // (POOL_GROUP_DIM // LANES))
    log_w = width.bit_length() - 1
    pad = POOL_MAX_HALF * width if n_lines > 1 else 0
    if pad:
        cs_ref[0:pad, :] = jnp.zeros((pad, c), F32)
        cs_ref[pad + l:2 * pad + l, :] = jnp.zeros((pad, c), F32)

    def col_pass(i, carry):
        r = pl.multiple_of(i * POOL_CHUNK, POOL_CHUNK)
        u = u_ref[pl.ds(r, POOL_CHUNK), :]
        hi = u.astype(BF16)
        lo = (u - hi.astype(F32)).astype(BF16)
        cs_ref[pl.ds(pad + r, POOL_CHUNK), :] = (jnp.dot(band_ref[0], hi, preferred_element_type=F32)
                                                 + jnp.dot(band_ref[0], lo, preferred_element_type=F32))
        return carry

    lax.fori_loop(0, n_chunks, col_pass, 0)

    def out_pass(i, carry):
        r = pl.multiple_of(i * POOL_CHUNK, POOL_CHUNK)
        pos = r + lax.broadcasted_iota(jnp.int32, (POOL_CHUNK, c), 0)
        col = pos & (width - 1)
        cnt = jnp.minimum(col + half, width) - jnp.maximum(col - half, 0)
        if n_lines > 1:
            def add_line(k, acc):
                return acc + cs_ref[pl.ds(pl.multiple_of(pad + r + k * width, SUBLANES), POOL_CHUNK), :]
            s = lax.fori_loop(-half, half, add_line, jnp.zeros((POOL_CHUNK, c), F32))
            line = lax.shift_right_logical(pos, log_w)
            cnt_l = jnp.minimum(line + half, n_lines) - jnp.maximum(line - half, 0)
            mean = s / (cnt_l.astype(F32) * cnt.astype(F32))
        else:
            mean = cs_ref[pl.ds(r, POOL_CHUNK), :] / cnt.astype(F32)
        o_ref[pl.ds(r, POOL_CHUNK), :] = (mean - u_ref[pl.ds(r, POOL_CHUNK), :]).astype(o_ref.dtype)
        return carry

    lax.fori_loop(0, n_chunks, out_pass, 0)


def pool_tokens(u, width):
    b, l, d = u.shape
    n_lines = l // width
    bands = jnp.asarray(_pool_band_tables(width)).astype(BF16)
    tiles_per_group = POOL_GROUP_DIM // LANES
    pad = POOL_MAX_HALF * width if n_lines > 1 else 0
    return pl.pallas_call(
        functools.partial(_pool_kernel, width=width, n_lines=n_lines),
        grid=(b, d // LANES),
        in_specs=[pl.BlockSpec((None, l, LANES), lambda i, j: (i, 0, j)),
                  pl.BlockSpec((1, POOL_CHUNK, POOL_CHUNK), lambda i, j: (j // tiles_per_group, 0, 0))],
        out_specs=pl.BlockSpec((None, l, LANES), lambda i, j: (i, 0, j)),
        out_shape=jax.ShapeDtypeStruct((b, l, d), BF16),
        scratch_shapes=[pltpu.VMEM((l + 2 * pad, LANES), F32)],
        compiler_params=_params(("parallel", "parallel")),
        name="pool_tokens",
    )(u, bands)


def _pool_mix_kernel(x_ref, p_ref, w_ref, b_ref, sc_ref, gt_ref, o_ref):
    ys = [jnp.dot(p_ref[0, :, g * POOL_GROUP_DIM:(g + 1) * POOL_GROUP_DIM], w_ref[g], preferred_element_type=F32)
          for g in range(POOL_GROUPS)]
    y = (jnp.concatenate(ys, axis=1) + b_ref[...]) * sc_ref[...]
    o_ref[0] = x_ref[0] + gt_ref[0, 0] * y


def pool_mix_residual(x, pooled, w, bias, scale, gate, tm=512):
    b, l, d = x.shape
    tm = min(tm, l)
    return pl.pallas_call(
        _pool_mix_kernel,
        grid=(b, l // tm),
        in_specs=[pl.BlockSpec((1, tm, d), lambda i, j: (i, j, 0)),
                  pl.BlockSpec((1, tm, d), lambda i, j: (i, j, 0)),
                  _resident(w.shape), _resident((1, d)), _resident((1, d)), _mod_spec(d, 0)],
        out_specs=pl.BlockSpec((1, tm, d), lambda i, j: (i, j, 0)),
        out_shape=jax.ShapeDtypeStruct((b, l, d), F32),
        compiler_params=_params(("parallel", "parallel")),
        name="pool_mix_residual",
    )(x, pooled, w.astype(BF16), bias.astype(F32).reshape(1, d), scale.astype(F32).reshape(1, d), gate)


def pool_mixer(x, shift, scale_mod, gate, g1, w, bias, scale, width):
    (u,) = modulate_project(x, shift, scale_mod, g1, [])
    return pool_mix_residual(x, pool_tokens(u, width), w, bias, scale, gate)


SLAB_PAD = 8


def _round_up(v, m):
    return (v + m - 1) // m * m


def _fft_plan(l):
    n = 2 * l
    na = 1 << (n.bit_length() // 2)
    nb = n // na
    ns = na // 2 + 1
    n_pairs = (ns + 1) // 2
    return dict(l=l, n=n, na=na, nb=nb, ns=ns, nsp=_round_up(ns + 1, SUBLANES), pitch=2 * nb + SLAB_PAD,
                n_pairs=n_pairs, tb_unroll=min(nb, 16),
                pair_unroll=max(u for u in range(1, 12) if n_pairs % u == 0))


def _fft_tables(l):
    import numpy as np
    p = _fft_plan(l)
    n, na, nb, ns, nsp = p["n"], p["na"], p["nb"], p["ns"], p["nsp"]
    half = na // 2
    ka = np.arange(ns)[None, :, None]
    tb = np.arange(nb)[:, None, None]

    def stage1(ta):
        th = 2.0 * np.pi * (ta[None, None, :] * ka / na + tb * ka / n)
        m = np.zeros((nb, 2 * nsp, ta.shape[0]))
        m[:, :ns] = np.cos(th)
        m[:, nsp:nsp + ns] = -np.sin(th)
        return m

    f1 = stage1(np.arange(half))
    f1k = np.zeros((nb, 2 * nsp, na))
    f1k[:, :, :half] = f1
    f1k[1:, :, half:] = stage1(na - 1 - np.arange(half))[1:]
    tb0 = stage1(na - np.arange(half))[0]
    tb0[:, 0] = 0.0
    f1k[0, :, half:] = tb0

    k2 = np.arange(nb)
    ang = 2.0 * np.pi * np.outer(k2, k2) / nb
    c, s = np.cos(ang), np.sin(ang)
    f2 = np.block([[c, s], [-s, c]])
    g2 = np.block([[c, -s], [s, c]])

    ta = np.arange(half)[None, :, None]
    kk = np.arange(ns)[None, None, :]
    tbb = np.arange(nb)[:, None, None]
    ph = 2.0 * np.pi * (ta * kk / na + tbb * kk / n)
    wgt = np.where((kk == 0) | (kk == na // 2), 1.0, 2.0) / n
    g1 = np.zeros((nb, half, 2 * nsp))
    g1[:, :, :ns] = wgt * np.cos(ph)
    g1[:, :, nsp:nsp + ns] = -wgt * np.sin(ph)
    f32 = np.float32
    return p, f1.astype(f32), f1k.astype(f32), f2.astype(f32), g2.astype(f32), g1.astype(f32)


def _fft_stage1(gather, f1_ref, s_ref, p):
    nb, nsp, pitch = p["nb"], p["nsp"], p["pitch"]

    def body(tb, carry):
        a = jnp.dot(f1_ref[tb], gather(tb), preferred_element_type=F32)
        s_ref[pl.ds(tb, nsp, stride=pitch), :] = a[:nsp]
        s_ref[pl.ds(nb + tb, nsp, stride=pitch), :] = a[nsp:]
        return carry

    lax.fori_loop(0, nb, body, 0, unroll=p["tb_unroll"])


def _slab_pair(s_ref, i, p):
    nb, pitch = p["nb"], p["pitch"]
    r0 = pl.multiple_of(2 * i * pitch, SUBLANES)
    r1 = pl.multiple_of(2 * i * pitch + pitch, SUBLANES)
    return jnp.concatenate([s_ref[pl.ds(r0, 2 * nb), :], s_ref[pl.ds(r1, 2 * nb), :]], axis=1)


FILTER_ROWS = 512


def _filter_hidden_kernel(z_ref, w1_ref, b1_ref, w2_ref, b2_ref, fr_ref, o_ref):
    h = jnp.sin(fr_ref[0:1, :] * (jnp.dot(z_ref[...], w1_ref[...], preferred_element_type=F32, precision=HIGHEST)
                                  + b1_ref[...]))
    o_ref[...] = jnp.sin(fr_ref[1:2, :] * (jnp.dot(h, w2_ref[...], preferred_element_type=F32, precision=HIGHEST)
                                           + b2_ref[...]))


def _filter_out_kernel(h_ref, t_ref, w3f_ref, w3b_ref, dl_ref, o_ref, *, rows):
    l, tn = o_ref.shape[2], o_ref.shape[3]
    wf, wb, dl = w3f_ref[...], w3b_ref[...], dl_ref[...]

    def fill(i, ss):
        r = pl.multiple_of(i * rows, rows)
        h = h_ref[pl.ds(r, rows), :]
        decay = jnp.exp(-t_ref[pl.ds(r, rows), :] * dl)
        hf = _dot_split(h, wf) * decay
        hb = _dot_split(h, wb) * decay
        o_ref[0, 0, pl.ds(r, rows), :] = hf
        o_ref[0, 1, pl.ds(r, rows), :] = hb
        return ss + jnp.sum(hf * hf + hb * hb, axis=0, keepdims=True)

    ss = lax.fori_loop(0, l // rows, fill, jnp.zeros((1, tn), F32))
    scale = lax.rsqrt(ss + EPS)

    def rescale(i, carry):
        r = pl.multiple_of(i * rows, rows)
        o_ref[0, 0, pl.ds(r, rows), :] = o_ref[0, 0, pl.ds(r, rows), :] * scale
        o_ref[0, 1, pl.ds(r, rows), :] = o_ref[0, 1, pl.ds(r, rows), :] * scale
        return carry

    lax.fori_loop(0, l // rows, rescale, 0)


def hyena_filters(l, fw1, fb1, fw2, fb2, fw3, ffreq, tn=LANES):
    d = D_MODEL
    pos = jnp.arange(l, dtype=F32)
    t = jnp.linspace(0.0, 1.0, l, dtype=F32)
    wpos = 2.0 * math.pi * pos / l
    f = jnp.linspace(1e-4, HY_BANDS - 1, HY_BANDS, dtype=F32)
    ang = wpos[:, None] * f[None, :]
    emb_pad = _round_up(HY_EMB_DIM, SUBLANES)
    z = jnp.concatenate([t[:, None], jnp.cos(ang), -jnp.sin(ang), jnp.zeros((l, emb_pad - HY_EMB_DIM), F32)], axis=-1)
    w1 = jnp.concatenate([fw1.astype(F32), jnp.zeros((emb_pad - HY_EMB_DIM, fw1.shape[1]), F32)], axis=0)
    deltas = jnp.abs(jnp.linspace(HY_MIN_DECAY, HY_MAX_DECAY, d, dtype=F32)).reshape(1, d)
    hid = fw2.shape[0]
    rows = min(FILTER_ROWS, l)
    hidden = pl.pallas_call(
        _filter_hidden_kernel,
        grid=(l // rows,),
        in_specs=[pl.BlockSpec((rows, emb_pad), lambda r: (r, 0)), _resident((emb_pad, hid)), _resident((1, hid)),
                  _resident((hid, hid)), _resident((1, hid)), _resident((2, hid))],
        out_specs=pl.BlockSpec((rows, hid), lambda r: (r, 0)),
        out_shape=jax.ShapeDtypeStruct((l, hid), F32),
        compiler_params=_params(("parallel",)),
        name="hyena_filter_hidden",
    )(z, w1, fb1.astype(F32).reshape(1, hid), fw2.astype(F32), fb2.astype(F32).reshape(1, hid), ffreq.astype(F32))
    nt = d // tn
    w3 = fw3.astype(F32)
    return pl.pallas_call(
        functools.partial(_filter_out_kernel, rows=rows),
        grid=(HY_ORDER, nt),
        in_specs=[_resident((l, hid)), _resident((l, 1)),
                  pl.BlockSpec((hid, tn), lambda o, j: (0, (2 * o) * nt + j)),
                  pl.BlockSpec((hid, tn), lambda o, j: (0, (2 * o + 1) * nt + j)),
                  pl.BlockSpec((1, tn), lambda o, j: (0, j))],
        out_specs=pl.BlockSpec((1, 2, l, tn), lambda o, j: (o, 0, 0, j)),
        out_shape=jax.ShapeDtypeStruct((HY_ORDER, 2, l, d), F32),
        compiler_params=_params(("parallel", "parallel")),
        name="hyena_filters",
    )(hidden, t.reshape(l, 1), w3, w3, deltas)


def _filter_spectrum_kernel(hf_ref, hb_ref, f1_ref, f2_ref, o_ref, s_ref, *, p):
    na, nb = p["na"], p["nb"]
    half = na // 2

    def gather(tb):
        fwd = hf_ref[pl.ds(tb, half, stride=nb), :]
        bwd = hb_ref[pl.ds(jnp.where(tb == 0, 0, nb - tb), half, stride=nb), :]
        return jnp.concatenate([fwd, bwd], axis=0).astype(BF16)

    _fft_stage1(gather, f1_ref, s_ref, p)

    def body(i, carry):
        spec = jnp.dot(f2_ref[...], _slab_pair(s_ref, i, p).astype(BF16), preferred_element_type=F32)
        r = pl.multiple_of(i * 4 * nb, SUBLANES)
        o_ref[pl.ds(r, 2 * nb), :] = spec[:, :LANES].astype(BF16)
        o_ref[pl.ds(r + 2 * nb, 2 * nb), :] = spec[:, LANES:].astype(BF16)
        return carry

    lax.fori_loop(0, p["n_pairs"], body, 0, unroll=p["pair_unroll"])


def hyena_filter_spectrum(filt, tables):
    p, _, f1k, f2, _, _ = tables
    order, _, l, d = filt.shape
    rows = 2 * p["n_pairs"] * 2 * p["nb"]
    return pl.pallas_call(
        functools.partial(_filter_spectrum_kernel, p=p),
        grid=(order, d // LANES),
        in_specs=[pl.BlockSpec((None, None, l, LANES), lambda o, j: (o, 0, 0, j)),
                  pl.BlockSpec((None, None, l, LANES), lambda o, j: (o, 1, 0, j)),
                  _resident(f1k.shape), _resident(f2.shape)],
        out_specs=pl.BlockSpec((None, rows, LANES), lambda o, j: (o, 0, j)),
        out_shape=jax.ShapeDtypeStruct((order, rows, d), BF16),
        scratch_shapes=[pltpu.VMEM((p["nsp"] * p["pitch"], LANES), F32)],
        compiler_params=_params(("parallel", "parallel")),
        name="hyena_filter_spectrum",
    )(filt, filt, jnp.asarray(f1k).astype(BF16), jnp.asarray(f2).astype(BF16))


def _longconv_kernel(a_ref, m_ref, k_ref, bias_ref, f1_ref, f2_ref, g2_ref, g1_ref, o_ref, s_ref, *, p, row_chunk):
    na, nb, nsp, pitch = p["na"], p["nb"], p["nsp"], p["pitch"]
    half = na // 2

    _fft_stage1(lambda tb: a_ref[pl.ds(tb, half, stride=nb), :].astype(BF16), f1_ref, s_ref, p)

    def mid(i, carry):
        x = jnp.dot(f2_ref[...], _slab_pair(s_ref, i, p).astype(BF16), preferred_element_type=F32)
        r = pl.multiple_of(i * 4 * nb, SUBLANES)
        kk = jnp.concatenate([k_ref[pl.ds(r, 2 * nb), :], k_ref[pl.ds(r + 2 * nb, 2 * nb), :]], axis=1).astype(F32)
        xr, xi, kr, ki = x[:nb], x[nb:], kk[:nb], kk[nb:]
        y = jnp.concatenate([xr * kr - xi * ki, xr * ki + xi * kr], axis=0).astype(BF16)
        bq = jnp.dot(g2_ref[...], y, preferred_element_type=F32)
        r0 = pl.multiple_of(2 * i * pitch, SUBLANES)
        r1 = pl.multiple_of(2 * i * pitch + pitch, SUBLANES)
        s_ref[pl.ds(r0, 2 * nb), :] = bq[:, :LANES]
        s_ref[pl.ds(r1, 2 * nb), :] = bq[:, LANES:]
        return carry

    lax.fori_loop(0, p["n_pairs"], mid, 0, unroll=p["pair_unroll"])

    def last(tb, carry):
        bq = jnp.concatenate([s_ref[pl.ds(tb, nsp, stride=pitch), :], s_ref[pl.ds(nb + tb, nsp, stride=pitch), :]], axis=0)
        o_ref[pl.ds(tb, half, stride=nb), :] = jnp.dot(g1_ref[tb], bq.astype(BF16), preferred_element_type=F32)
        return carry

    lax.fori_loop(0, nb, last, 0, unroll=p["tb_unroll"])

    def finish(i, carry):
        r = pl.multiple_of(i * row_chunk, SUBLANES)
        a = a_ref[pl.ds(r, row_chunk), :]
        o_ref[pl.ds(r, row_chunk), :] = m_ref[pl.ds(r, row_chunk), :] * (o_ref[pl.ds(r, row_chunk), :] + bias_ref[...] * a)
        return carry

    lax.fori_loop(0, p["l"] // row_chunk, finish, 0)


def hyena_longconv(a, a_col, m, m_col, kspec, bias, tables):
    p, f1, _, f2, g2, g1 = tables
    b, l, _ = a.shape
    d = D_MODEL
    nt = d // LANES
    rows = kspec.shape[0]
    tabs = [jnp.asarray(t).astype(BF16) for t in (f1, f2, g2, g1)]
    return pl.pallas_call(
        functools.partial(_longconv_kernel, p=p, row_chunk=min(512, l)),
        grid=(nt, b),
        in_specs=[pl.BlockSpec((None, l, LANES), lambda j, i: (i, 0, a_col * nt + j)),
                  pl.BlockSpec((None, l, LANES), lambda j, i: (i, 0, m_col * nt + j)),
                  pl.BlockSpec((rows, LANES), lambda j, i: (0, j)),
                  pl.BlockSpec((1, LANES), lambda j, i: (0, j))] + [_resident(t.shape) for t in tabs],
        out_specs=pl.BlockSpec((None, l, LANES), lambda j, i: (i, 0, j)),
        out_shape=jax.ShapeDtypeStruct((b, l, d), F32),
        scratch_shapes=[pltpu.VMEM((p["nsp"] * p["pitch"], LANES), F32)],
        compiler_params=_params(("parallel", "parallel")),
        name="hyena_longconv",
    )(a, m, kspec, bias.astype(F32).reshape(1, d), *tabs)


def hyena_core(p_raw, conv_w, conv_b, fw1, fb1, fw2, fb2, fw3, ffreq, hbias):
    l = p_raw.shape[1]
    tables = _fft_tables(l)
    filt = hyena_filters(l, fw1, fb1, fw2, fb2, fw3, ffreq)
    kspec = hyena_filter_spectrum(filt, tables)
    pc = dwconv_tokens(p_raw, conv_w, conv_b)
    z = hyena_longconv(pc, 0, pc, 1, kspec[0], hbias[0], tables)
    return hyena_longconv(z, 0, pc, 2, kspec[1], hbias[1], tables)


def kernel(x, c, ctx, c_ctx, ada_w, ada_b, norm_g, ffn_w_gate, ffn_w_up, ffn_w_down, ssd_w_in, ssd_conv_w, ssd_conv_b, ssd_a_log, ssd_dt_bias, ssd_d, ssd_norm_g, ssd_w_out, pool_w, pool_b, pool_scale, hy_w_in, hy_conv_w, hy_conv_b, hy_filt_w1, hy_filt_b1, hy_filt_w2, hy_filt_b2, hy_filt_w3, hy_filt_freq, hy_bias, hy_w_out, final_g):
    batch = x.shape[0]
    d = D_MODEL
    h = ctx

    s = jnp.concatenate([jax.nn.silu(c), jax.nn.silu(c_ctx)[None], jnp.zeros((7 - batch, d), F32)], axis=0)
    mods = ada_modulation(s, ada_w, ada_b).reshape(DEPTH, 8, N_MOD, d)

    wg_bf, wu_bf, wd_bf = ffn_w_gate.astype(BF16), ffn_w_up.astype(BF16), ffn_w_down.astype(BF16)
    fg = final_g.reshape(1, d)

    for i in range(DEPTH):
        kind = i % N_MIXERS
        j = i // N_MIXERS
        last = i == DEPTH - 1
        ctx_in_needed = (not last) or kind == 0
        m = [mods[i, :batch, k].reshape(batch, 1, 1, d) for k in range(N_MOD)]
        mc = [jnp.broadcast_to(mods[i, batch, k].reshape(1, 1, 1, d), (batch, 1, 1, d)) for k in range(N_MOD)]
        g0, g1, g2 = (norm_g[i, k].reshape(1, d) for k in range(3))

        x = ffn_step(x, m[0], m[1], m[2], g0, wg_bf[i, 0], wu_bf[i, 0], wd_bf[i, 0])
        if ctx_in_needed:
            h = ffn_step(h, mc[0], mc[1], mc[2], g0, wg_bf[i, 0], wu_bf[i, 0], wd_bf[i, 0])

        if kind == 0:
            x, h_new = ssd_mixer(x, h, (m[3], m[4], m[5]), (mc[3], mc[4], mc[5]), g1, ssd_w_in[j], ssd_conv_w[j],
                                 ssd_conv_b[j], ssd_a_log[j], ssd_dt_bias[j], ssd_d[j], ssd_norm_g[j], ssd_w_out[j], not last)
            if not last:
                h = h_new
        elif kind == 1:
            x = pool_mixer(x, m[3], m[4], m[5], g1, pool_w[j], pool_b[j].reshape(-1), pool_scale[j], GRID_W)
            if not last:
                h = pool_mixer(h, mc[3], mc[4], mc[5], g1, pool_w[j], pool_b[j].reshape(-1), pool_scale[j], h.shape[1])
        else:
            w_in = hy_w_in[j].astype(BF16)
            w_out = hy_w_out[j].astype(BF16)
            filt = (hy_filt_w1[j], hy_filt_b1[j], hy_filt_w2[j], hy_filt_b2[j], hy_filt_w3[j], hy_filt_freq[j])
            (p_lat,) = modulate_project(x, m[3], m[4], g1, [w_in])
            x = project_residual(x, hyena_core(p_lat, hy_conv_w[j], hy_conv_b[j], *filt, hy_bias[j]), m[5], w_out)
            if not last:
                (p_ctx,) = modulate_project(h, mc[3], mc[4], g1, [w_in])
                h = project_residual(h, hyena_core(p_ctx, hy_conv_w[j], hy_conv_b[j], *filt, hy_bias[j]), mc[5], w_out)

        x = ffn_step(x, m[6], m[7], m[8], g2, wg_bf[i, 1], wu_bf[i, 1], wd_bf[i, 1],
                     final_g=fg if last else None)
        if not last:
            h = ffn_step(h, mc[6], mc[7], mc[8], g2, wg_bf[i, 1], wu_bf[i, 1], wd_bf[i, 1])
    return x
```

```python
import functools
import math

import jax
import jax.numpy as jnp
from jax import lax
from jax.experimental import pallas as pl
from jax.experimental.pallas import tpu as pltpu

F32 = jnp.float32
BF16 = jnp.bfloat16
HIGHEST = lax.Precision.HIGHEST

D_MODEL = 1024
DEPTH = 4
GRID_W = 64
N_MIXERS = 3
D_FF = 2816
N_MOD = 9
EPS = 1e-6
LOG2_E = 1.4426950408889634

SSD_D_INNER = 2 * D_MODEL
SSD_HEAD_DIM = 64
SSD_N_HEADS = SSD_D_INNER // SSD_HEAD_DIM
SSD_N_GROUPS = 4
SSD_HEADS_PER_GROUP = SSD_N_HEADS // SSD_N_GROUPS
SSD_D_STATE = 128
SSD_CHUNK = 128
SSD_GN = SSD_N_GROUPS * SSD_D_STATE
SSD_CONV_DIM = SSD_D_INNER + 2 * SSD_GN
SSD_GROUP_WIDTH = SSD_D_INNER // SSD_N_GROUPS

POOL_WINDOWS = (2, 4, 8, 16)
POOL_GROUPS = 4
POOL_GROUP_DIM = D_MODEL // POOL_GROUPS

HY_ORDER = 2
HY_EMB_DIM = 33
HY_BANDS = (HY_EMB_DIM - 1) // 2
HY_MAX_DECAY = math.log(1e-2) / 0.3
HY_MIN_DECAY = math.log(1e-2) / 1.5

VMEM_LIMIT_BYTES = 56 * 1024 * 1024
SUBLANES = 8
LANES = 128


def _params(sem):
    return pltpu.CompilerParams(dimension_semantics=sem, vmem_limit_bytes=VMEM_LIMIT_BYTES)


def _resident(shape):
    nd = len(shape)
    return pl.BlockSpec(shape, lambda *_: (0,) * nd, pipeline_mode=pl.Buffered(1))


def _mod_spec(d, _unused=0):
    return pl.BlockSpec((1, 1, 1, d), lambda i, j: (i, 0, 0, 0))


def _modulated(x, g, shift, scale):
    ms = jnp.mean(x * x, axis=-1, keepdims=True)
    return (x * lax.rsqrt(ms + EPS)) * g * (1.0 + scale) + shift


def _silu(v):
    return v * jax.nn.sigmoid(v)


def _ada_kernel(s_ref, w_ref, b_ref, o_ref):
    o_ref[0] = jnp.dot(s_ref[...], w_ref[0], preferred_element_type=F32, precision=HIGHEST) + b_ref[0]


def ada_modulation(s, ada_w, ada_b, tn=1024):
    depth, d, n = ada_w.shape
    r = s.shape[0]
    return pl.pallas_call(
        _ada_kernel,
        grid=(depth, n // tn),
        in_specs=[pl.BlockSpec((r, d), lambda i, j: (0, 0)),
                  pl.BlockSpec((1, d, tn), lambda i, j: (i, 0, j)),
                  pl.BlockSpec((1, 1, tn), lambda i, j: (i, 0, j))],
        out_specs=pl.BlockSpec((1, r, tn), lambda i, j: (i, 0, j)),
        out_shape=jax.ShapeDtypeStruct((depth, r, n), F32),
        compiler_params=_params(("parallel", "parallel")),
        name="ada_modulation",
    )(s, ada_w, ada_b.reshape(depth, 1, n))


def _ffn_kernel(x_ref, sh_ref, sc_ref, gt_ref, g_ref, wg_ref, wu_ref, wd_ref, *rest, f_chunk, final):
    if final:
        fg_ref, o_ref = rest
    else:
        (o_ref,) = rest
    x = x_ref[0]
    u = _modulated(x, g_ref[...], sh_ref[0, 0], sc_ref[0, 0]).astype(BF16)
    acc = jnp.zeros(x.shape, F32)
    d_ff = wg_ref.shape[1]
    for f0 in range(0, d_ff, f_chunk):
        a = jnp.dot(u, wg_ref[:, f0:f0 + f_chunk], preferred_element_type=F32)
        b = jnp.dot(u, wu_ref[:, f0:f0 + f_chunk], preferred_element_type=F32)
        h = (_silu(a) * b).astype(BF16)
        acc = acc + jnp.dot(h, wd_ref[f0:f0 + f_chunk, :], preferred_element_type=F32)
    y = x + (0.5 * gt_ref[0, 0]) * acc
    if final:
        ms = jnp.mean(y * y, axis=-1, keepdims=True)
        y = y * lax.rsqrt(ms + EPS) * fg_ref[...]
    o_ref[0] = y


def ffn_step(x, shift, scale, gate, g, wg, wu, wd, final_g=None, tm=512, f_chunk=256):
    b, l, d = x.shape
    tm = min(tm, l)
    f = wg.shape[1]
    final = final_g is not None
    mod_spec = _mod_spec(d, 0)
    in_specs = [pl.BlockSpec((1, tm, d), lambda i, j: (i, j, 0)),
                mod_spec, mod_spec, mod_spec,
                _resident((1, d)), _resident((d, f)), _resident((d, f)), _resident((f, d))]
    args = [x, shift, scale, gate, g, wg, wu, wd]
    if final:
        in_specs.append(_resident((1, d)))
        args.append(final_g)
    return pl.pallas_call(
        functools.partial(_ffn_kernel, f_chunk=f_chunk, final=final),
        grid=(b, l // tm),
        in_specs=in_specs,
        out_specs=pl.BlockSpec((1, tm, d), lambda i, j: (i, j, 0)),
        out_shape=jax.ShapeDtypeStruct((b, l, d), F32),
        compiler_params=_params(("parallel", "parallel")),
        name="ffn_step",
    )(*args)


def _modproj_kernel(x_ref, sh_ref, sc_ref, g_ref, *rest, n_w, n_dest):
    w_refs, o_refs = rest[:n_w], rest[n_w + n_dest:]
    u = _modulated(x_ref[0], g_ref[...], sh_ref[0, 0], sc_ref[0, 0])
    if n_w == 0:
        o_refs[0][0] = u
        return
    u = u.astype(BF16)
    for w_ref, o_ref in zip(w_refs, o_refs):
        o_ref[0] = jnp.dot(u, w_ref[...], preferred_element_type=F32).astype(o_ref.dtype)


def modulate_project(x, shift, scale, g, weights, tm=256, out_dtypes=None, out_rows=None, row_block=0, dest=None):
    b, l, d = x.shape
    tm = min(tm, l)
    mod_spec = _mod_spec(d)
    in_specs = [pl.BlockSpec((1, tm, d), lambda i, j: (i, j, 0)), mod_spec, mod_spec, _resident((1, d))]
    in_specs += [_resident(w.shape) for w in weights]
    widths = [w.shape[1] for w in weights] or [d]
    out_dtypes = out_dtypes or [F32] * len(widths)
    dest = list(dest or [])
    in_specs += [pl.BlockSpec(memory_space=pl.ANY)] * len(dest)
    n_in = 4 + len(weights)
    return pl.pallas_call(
        functools.partial(_modproj_kernel, n_w=len(weights), n_dest=len(dest)),
        grid=(b, l // tm),
        in_specs=in_specs,
        out_specs=[pl.BlockSpec((1, tm, n), lambda i, j: (i, j + row_block, 0)) for n in widths],
        out_shape=[jax.ShapeDtypeStruct((b, out_rows or l, n), dt) for n, dt in zip(widths, out_dtypes)],
        input_output_aliases={n_in + k: k for k in range(len(dest))},
        compiler_params=_params(("parallel", "parallel")),
        name="modulate_project",
    )(x, shift, scale, g, *weights, *dest)


def _outproj_kernel(x_ref, a_ref, gt_ref, w_ref, o_ref):
    y = jnp.dot(a_ref[0].astype(BF16), w_ref[...], preferred_element_type=F32)
    o_ref[0] = x_ref[0] + gt_ref[0, 0] * y


def project_residual(x, a, gate, w, tm=512):
    b, l, d = x.shape
    k = a.shape[-1]
    tm = min(tm, l)
    return pl.pallas_call(
        _outproj_kernel,
        grid=(b, l // tm),
        in_specs=[pl.BlockSpec((1, tm, d), lambda i, j: (i, j, 0)),
                  pl.BlockSpec((1, tm, k), lambda i, j: (i, j, 0)),
                  _mod_spec(d, 0),
                  _resident((k, d))],
        out_specs=pl.BlockSpec((1, tm, d), lambda i, j: (i, j, 0)),
        out_shape=jax.ShapeDtypeStruct((b, l, d), F32),
        compiler_params=_params(("parallel", "parallel")),
        name="project_residual",
    )(x, a, gate, w)


def _dwconv_kernel(prev_ref, x_ref, next_ref, w_ref, b_ref, o_ref, *, k_w, n_rows, seg_rows, silu):
    tr, c = x_ref.shape[1], x_ref.shape[2]
    halo = prev_ref.shape[1]
    p = k_w // 2
    act = _silu if silu else (lambda v: v)
    ext = jnp.concatenate([prev_ref[0], x_ref[0], next_ref[0]], axis=0).astype(F32)
    acc = jnp.broadcast_to(b_ref[...], (tr, c))
    for k in range(k_w):
        tap = ext if k == p else pltpu.roll(ext, (p - k) % (tr + 2 * halo), axis=0)
        acc = acc + w_ref[k:k + 1, :] * tap[halo:halo + tr, :]
    o_ref[0] = act(acc).astype(o_ref.dtype)

    edges = sorted({0, n_rows} | ({seg_rows} if seg_rows else set()))
    n_tiles = n_rows // halo
    special = sorted({t for e in edges for t in ((e - p) // halo, (e + p - 1) // halo) if 0 <= t < n_tiles})
    tiles_per_block = tr // halo
    for tile in special:
        lo = (tile % tiles_per_block) * halo

        @pl.when(pl.program_id(2) == tile // tiles_per_block)
        def _(tile=tile, lo=lo):
            row = tile * halo + lax.broadcasted_iota(jnp.int32, (halo, c), 0)
            acc_t = jnp.broadcast_to(b_ref[...], (halo, c))
            for k in range(k_w):
                src = row + (k - p)
                valid = (src >= 0) & (src < n_rows)
                for e in edges[1:-1]:
                    valid = valid & jnp.logical_not((jnp.minimum(row, src) < e) & (jnp.maximum(row, src) >= e))
                tap = ext[halo + lo + k - p:2 * halo + lo + k - p, :]
                acc_t = acc_t + w_ref[k:k + 1, :] * jnp.where(valid, tap, 0.0)
            o_ref[0, lo:lo + halo, :] = act(acc_t).astype(o_ref.dtype)


def dwconv_tokens(x, w, bias, seg_rows=0, silu=False, tr=1024, ct=512):
    b, t, c = x.shape
    k_w = w.shape[0]
    halo = SUBLANES * (4 // x.dtype.itemsize)
    tr = next(r for r in range(min(tr, t), 0, -halo) if t % r == 0)
    ct = min(ct, c)
    rb = tr // halo
    last_halo = t // halo - 1
    return pl.pallas_call(
        functools.partial(_dwconv_kernel, k_w=k_w, n_rows=t, seg_rows=seg_rows, silu=silu),
        grid=(b, c // ct, t // tr),
        in_specs=[pl.BlockSpec((1, halo, ct), lambda i, j, r: (i, jnp.maximum(r * rb - 1, 0), j)),
                  pl.BlockSpec((1, tr, ct), lambda i, j, r: (i, r, j)),
                  pl.BlockSpec((1, halo, ct), lambda i, j, r: (i, jnp.minimum((r + 1) * rb, last_halo), j)),
                  pl.BlockSpec((k_w, ct), lambda i, j, r: (0, j)),
                  pl.BlockSpec((1, ct), lambda i, j, r: (0, j))],
        out_specs=pl.BlockSpec((1, tr, ct), lambda i, j, r: (i, r, j)),
        out_shape=jax.ShapeDtypeStruct((b, t, c), x.dtype),
        compiler_params=_params(("parallel", "parallel", "parallel")),
        name="dwconv_tokens",
    )(x, x, x, w.astype(F32), bias.astype(F32).reshape(1, c))


def _bf16_terms(v):
    hi = v.astype(BF16)
    r = v - hi.astype(F32)
    mid = r.astype(BF16)
    return hi, mid, (r - mid.astype(F32)).astype(BF16)


def _dot_f32_rhs(m, v):
    return sum(jnp.dot(m, t, preferred_element_type=F32) for t in _bf16_terms(v))


def _dot_f32_lhs(v, m):
    return sum(jnp.dot(t, m, preferred_element_type=F32) for t in _bf16_terms(v))


def _dot_split(a, b):
    a_hi, a_lo, _ = _bf16_terms(a)
    b_hi, b_lo, _ = _bf16_terms(b)
    return (jnp.dot(a_hi, b_hi, preferred_element_type=F32) + jnp.dot(a_lo, b_hi, preferred_element_type=F32)
            + jnp.dot(a_hi, b_lo, preferred_element_type=F32))


def _ssd_scan_kernel(xbc_ref, dt_ref, dtt_ref, dtb_ref, dtbt_ref, a_ref, at_ref, ex_ref, o_ref, h_ref):
    nh, q, hp = SSD_N_HEADS, SSD_CHUNK, SSD_HEAD_DIM
    d = pl.program_id(1)
    fwd = d == 0

    @pl.when(pl.program_id(2) == 0)
    def _():
        h_ref[...] = jnp.zeros(h_ref.shape, F32)

    dt2 = jax.nn.softplus(dt_ref[0] + dtb_ref[...])
    dt = jnp.where(fwd, dt2[:, :nh], dt2[:, nh:])
    a = dt * jnp.where(fwd, a_ref[:, :nh], a_ref[:, nh:])
    dtt2 = jax.nn.softplus(dtt_ref[0] + dtbt_ref[...])
    a_t = jnp.where(fwd, dtt2[:nh], dtt2[nh:]) * jnp.where(fwd, at_ref[:nh], at_ref[nh:])

    r_i = lax.broadcasted_iota(jnp.int32, (q, q), 0)
    c_i = lax.broadcasted_iota(jnp.int32, (q, q), 1)
    ahead = jnp.where(fwd, c_i - r_i, r_i - c_i)
    seen = ahead <= 0
    seen_t = ahead >= 0
    cum = _dot_f32_rhs(seen.astype(BF16), a)
    cum_t = _dot_f32_lhs(a_t, seen_t.astype(BF16))
    total = jnp.where(fwd, cum[q - 1:q, :], cum[0:1, :])
    ecum = jnp.exp(cum).astype(BF16)
    dtdec = (dt * jnp.exp(total - cum)).astype(BF16)
    etot = jnp.exp(total)
    cum2 = cum * LOG2_E
    row2 = cum_t * LOG2_E - jnp.log2(jnp.where(fwd, dtt2[:nh], dtt2[nh:]))

    pw = 2 * hp
    left = lax.broadcasted_iota(jnp.int32, (q, pw), 1) < hp
    left_row = lax.broadcasted_iota(jnp.int32, (1, pw), 1) < hp
    pairs = SSD_HEADS_PER_GROUP // 2
    for g in range(SSD_N_GROUPS):
        b_g = xbc_ref[0, :, SSD_D_INNER + g * SSD_D_STATE:SSD_D_INNER + (g + 1) * SSD_D_STATE].astype(BF16)
        c_lo = SSD_D_INNER + SSD_GN + g * SSD_D_STATE
        c_g = xbc_ref[0, :, c_lo:c_lo + SSD_D_STATE].astype(BF16)
        cb = lax.dot_general(c_g, b_g, (((1,), (1,)), ((), ())), preferred_element_type=F32)
        h_g = h_ref[g]
        ex_g = ex_ref[:, g * SSD_GROUP_WIDTH:(g + 1) * SSD_GROUP_WIDTH]
        ch = (jnp.dot(c_g, h_g.astype(BF16), preferred_element_type=F32)
              * jnp.dot(ecum, ex_g, preferred_element_type=F32))
        dtdec_g = jnp.dot(dtdec, ex_g, preferred_element_type=F32)
        xdecs, etots = [], []
        for pr in range(pairs):
            e1 = g * SSD_HEADS_PER_GROUP + 2 * pr
            e2 = e1 + 1
            x_pair = xbc_ref[0, :, e1 * hp:e1 * hp + pw].astype(F32)
            ws = []
            for e in (e1, e2):
                seg2 = cum2[:, e:e + 1] - row2[e:e + 1, :]
                ws.append((cb * jnp.exp2(jnp.where(seen, seg2, -jnp.inf))).astype(BF16))
            rhs = jnp.concatenate([jnp.where(left, x_pair, 0.0), jnp.where(left, 0.0, x_pair)], axis=0).astype(BF16)
            y_diag = jnp.dot(jnp.concatenate(ws, axis=1), rhs, preferred_element_type=F32)
            y_off = ch[:, pr * pw:(pr + 1) * pw]
            o_ref[0, 0, :, e1 * hp:e1 * hp + pw] = (y_diag + y_off).astype(o_ref.dtype)
            xdecs.append((x_pair * dtdec_g[:, pr * pw:(pr + 1) * pw]).astype(BF16))
            etots.append(jnp.where(left_row, etot[:, e1:e1 + 1], etot[:, e2:e2 + 1]))
        s_g = lax.dot_general(b_g, jnp.concatenate(xdecs, axis=1), (((0,), (0,)), ((), ())),
                              preferred_element_type=F32)
        h_ref[g] = h_g * jnp.concatenate(etots, axis=1) + s_g


def ssd_scan(xbc, dt_raw, dt_bias, a_log, n_lead_chunks):
    b, t, _ = xbc.shape
    q, nh = SSD_CHUNK, SSD_N_HEADS
    nc = t // q
    dt_t = jnp.swapaxes(dt_raw, 1, 2)
    a_neg = -jnp.exp(a_log.astype(F32)).reshape(1, 2 * nh)
    dtb = dt_bias.astype(F32).reshape(1, 2 * nh)
    expand = jnp.repeat(jnp.eye(nh, dtype=BF16), SSD_HEAD_DIM, axis=1)

    def chunk_of(d, c):
        back = jnp.where(c < n_lead_chunks, n_lead_chunks - 1 - c, nc - 1 + n_lead_chunks - c)
        return jnp.where(d == 0, c, back)

    return pl.pallas_call(
        _ssd_scan_kernel,
        grid=(b, 2, nc),
        in_specs=[pl.BlockSpec((1, q, SSD_CONV_DIM), lambda i, d, c: (i, chunk_of(d, c), 0)),
                  pl.BlockSpec((1, q, 2 * nh), lambda i, d, c: (i, chunk_of(d, c), 0)),
                  pl.BlockSpec((1, 2 * nh, q), lambda i, d, c: (i, 0, chunk_of(d, c))),
                  _resident((1, 2 * nh)), _resident((2 * nh, 1)),
                  _resident((1, 2 * nh)), _resident((2 * nh, 1)), _resident((nh, SSD_D_INNER))],
        out_specs=pl.BlockSpec((1, 1, q, SSD_D_INNER), lambda i, d, c: (i, d, chunk_of(d, c), 0)),
        out_shape=jax.ShapeDtypeStruct((b, 2, t, SSD_D_INNER), BF16),
        scratch_shapes=[pltpu.VMEM((SSD_N_GROUPS, SSD_D_STATE, SSD_GROUP_WIDTH), F32)],
        compiler_params=_params(("parallel", "parallel", "arbitrary")),
        name="ssd_scan",
    )(xbc, dt_raw, dt_t, dtb, dtb.reshape(2 * nh, 1), a_neg, a_neg.reshape(2 * nh, 1), expand)


def _ssd_out_kernel(x_ref, y0_ref, y1_ref, xs0_ref, xs1_ref, z_ref, dsk_ref, ng_ref, gt_ref, w_ref, o_ref):
    xs = jnp.concatenate([xs0_ref[0], xs1_ref[0]], axis=1).astype(F32)
    y = y0_ref[0, 0].astype(F32) + y1_ref[0, 0].astype(F32) + dsk_ref[...] * xs
    gy = y * _silu(z_ref[0].astype(F32))
    parts = []
    for g in range(SSD_N_GROUPS):
        blk = gy[:, g * SSD_GROUP_WIDTH:(g + 1) * SSD_GROUP_WIDTH]
        parts.append(blk * lax.rsqrt(jnp.mean(blk * blk, axis=-1, keepdims=True) + EPS))
    a = (jnp.concatenate(parts, axis=1) * ng_ref[...]).astype(BF16)
    o_ref[0] = x_ref[0] + gt_ref[0, 0] * jnp.dot(a, w_ref[...], preferred_element_type=F32)


def ssd_gate_project(x, y2, xbc, z, d_skip, norm_g, gate, w_out, src_block, tm=256):
    b, t, d = x.shape
    half = SSD_D_INNER // 2
    dsk = jnp.repeat(d_skip.astype(F32), SSD_HEAD_DIM).reshape(1, SSD_D_INNER)
    return pl.pallas_call(
        _ssd_out_kernel,
        grid=(b, t // tm),
        in_specs=[pl.BlockSpec((1, tm, d), lambda i, j: (i, j, 0)),
                  pl.BlockSpec((1, 1, tm, SSD_D_INNER), lambda i, j: (i, 0, j + src_block, 0)),
                  pl.BlockSpec((1, 1, tm, SSD_D_INNER), lambda i, j: (i, 1, j + src_block, 0)),
                  pl.BlockSpec((1, tm, half), lambda i, j: (i, j + src_block, 0)),
                  pl.BlockSpec((1, tm, half), lambda i, j: (i, j + src_block, 1)),
                  pl.BlockSpec((1, tm, SSD_D_INNER), lambda i, j: (i, j + src_block, 0)),
                  _resident((1, SSD_D_INNER)), _resident((1, SSD_D_INNER)),
                  _mod_spec(d),
                  _resident((SSD_D_INNER, d))],
        out_specs=pl.BlockSpec((1, tm, d), lambda i, j: (i, j, 0)),
        out_shape=jax.ShapeDtypeStruct((b, t, d), F32),
        compiler_params=_params(("parallel", "parallel")),
        name="ssd_gate_project",
    )(x, y2, y2, xbc, xbc, z, dsk, norm_g.astype(F32).reshape(1, SSD_D_INNER), gate, w_out)


def ssd_mixer(x, h, m_lat, m_ctx, g1, w_in, conv_w, conv_b, a_log, dt_bias, d_skip, norm_g, w_out, need_ctx_out, tm=256):
    lc, l = h.shape[1], x.shape[1]
    assert lc % tm == 0
    n0 = lc // tm
    w_bf = w_in.astype(BF16)
    ws = [w_bf[:, :SSD_D_INNER], w_bf[:, SSD_D_INNER:SSD_D_INNER + SSD_CONV_DIM], w_bf[:, SSD_D_INNER + SSD_CONV_DIM:]]
    dts = [BF16, BF16, F32]
    parts = modulate_project(x, m_lat[0], m_lat[1], g1, ws, tm=tm, out_dtypes=dts, out_rows=lc + l, row_block=n0)
    z, xbc_raw, dt_raw = modulate_project(h, m_ctx[0], m_ctx[1], g1, ws, tm=tm, out_dtypes=dts, out_rows=lc + l, dest=parts)
    xbc = dwconv_tokens(xbc_raw, conv_w, conv_b, seg_rows=lc, silu=True)
    y2 = ssd_scan(xbc, dt_raw, dt_bias, a_log, lc // SSD_CHUNK)
    w_out = w_out.astype(BF16)
    x_new = ssd_gate_project(x, y2, xbc, z, d_skip, norm_g, m_lat[2], w_out, n0, tm=tm)
    h_new = ssd_gate_project(h, y2, xbc, z, d_skip, norm_g, m_ctx[2], w_out, 0, tm=tm) if need_ctx_out else None
    return x_new, h_new


POOL_CHUNK = 256
POOL_MAX_HALF = max(POOL_WINDOWS) // 2


def _pool_band_tables(width):
    import numpy as np
    pos = np.arange(POOL_CHUNK)
    line, col = pos // width, pos % width
    out = []
    for w in POOL_WINDOWS:
        inside = (col[None, :] >= col[:, None] - w // 2) & (col[None, :] <= col[:, None] + (w - w // 2) - 1)
        out.append((inside & (line[None, :] == line[:, None])).astype(np.float32))
    return np.stack(out)


def _pool_kernel(u_ref, band_ref, o_ref, cs_ref, *, width, n_lines):
    l, c = u_ref.shape
    n_chunks = l // POOL_CHUNK
    assert POOL_WINDOWS == tuple(2 << g for g in range(POOL_GROUPS)) and width & (width - 1) == 0
    half = lax.shift_left(jnp.int32(1), pl.program_id(1) // (POOL_GROUP_DIM // LANES))
    log_w = width.bit_length() - 1
    pad = POOL_MAX_HALF * width if n_lines > 1 else 0
    if pad:
        cs_ref[0:pad, :] = jnp.zeros((pad, c), F32)
        cs_ref[pad + l:2 * pad + l, :] = jnp.zeros((pad, c), F32)

    def col_pass(i, carry):
        r = pl.multiple_of(i * POOL_CHUNK, POOL_CHUNK)
        u = u_ref[pl.ds(r, POOL_CHUNK), :]
        hi = u.astype(BF16)
        lo = (u - hi.astype(F32)).astype(BF16)
        both = jnp.dot(band_ref[0], jnp.concatenate([hi, lo], axis=1), preferred_element_type=F32)
        cs_ref[pl.ds(pad + r, POOL_CHUNK), :] = both[:, :c] + both[:, c:]
        return carry

    lax.fori_loop(0, n_chunks, col_pass, 0)

    def out_pass(i, carry):
        r = pl.multiple_of(i * POOL_CHUNK, POOL_CHUNK)
        pos = r + lax.broadcasted_iota(jnp.int32, (POOL_CHUNK, c), 0)
        col = pos & (width - 1)
        cnt = jnp.minimum(col + half, width) - jnp.maximum(col - half, 0)
        if n_lines > 1:
            def add_line(k, acc):
                return acc + cs_ref[pl.ds(pl.multiple_of(pad + r + k * width, SUBLANES), POOL_CHUNK), :]
            s = lax.fori_loop(-half, half, add_line, jnp.zeros((POOL_CHUNK, c), F32))
            line = lax.shift_right_logical(pos, log_w)
            cnt_l = jnp.minimum(line + half, n_lines) - jnp.maximum(line - half, 0)
            mean = s / (cnt_l.astype(F32) * cnt.astype(F32))
        else:
            mean = cs_ref[pl.ds(r, POOL_CHUNK), :] / cnt.astype(F32)
        o_ref[pl.ds(r, POOL_CHUNK), :] = (mean - u_ref[pl.ds(r, POOL_CHUNK), :]).astype(o_ref.dtype)
        return carry

    lax.fori_loop(0, n_chunks, out_pass, 0)


def pool_tokens(u, width):
    b, l, d = u.shape
    n_lines = l // width
    bands = jnp.asarray(_pool_band_tables(width)).astype(BF16)
    tiles_per_group = POOL_GROUP_DIM // LANES
    pad = POOL_MAX_HALF * width if n_lines > 1 else 0
    return pl.pallas_call(
        functools.partial(_pool_kernel, width=width, n_lines=n_lines),
        grid=(b, d // LANES),
        in_specs=[pl.BlockSpec((None, l, LANES), lambda i, j: (i, 0, j)),
                  pl.BlockSpec((1, POOL_CHUNK, POOL_CHUNK), lambda i, j: (j // tiles_per_group, 0, 0))],
        out_specs=pl.BlockSpec((None, l, LANES), lambda i, j: (i, 0, j)),
        out_shape=jax.ShapeDtypeStruct((b, l, d), BF16),
        scratch_shapes=[pltpu.VMEM((l + 2 * pad, LANES), F32)],
        compiler_params=_params(("parallel", "parallel")),
        name="pool_tokens",
    )(u, bands)


def _pool_mix_kernel(x_ref, p_ref, w_ref, b_ref, sc_ref, gt_ref, o_ref):
    ys = [jnp.dot(p_ref[0, :, g * POOL_GROUP_DIM:(g + 1) * POOL_GROUP_DIM], w_ref[g], preferred_element_type=F32)
          for g in range(POOL_GROUPS)]
    y = (jnp.concatenate(ys, axis=1) + b_ref[...]) * sc_ref[...]
    o_ref[0] = x_ref[0] + gt_ref[0, 0] * y


def pool_mix_residual(x, pooled, w, bias, scale, gate, tm=512):
    b, l, d = x.shape
    tm = min(tm, l)
    return pl.pallas_call(
        _pool_mix_kernel,
        grid=(b, l // tm),
        in_specs=[pl.BlockSpec((1, tm, d), lambda i, j: (i, j, 0)),
                  pl.BlockSpec((1, tm, d), lambda i, j: (i, j, 0)),
                  _resident(w.shape), _resident((1, d)), _resident((1, d)), _mod_spec(d, 0)],
        out_specs=pl.BlockSpec((1, tm, d), lambda i, j: (i, j, 0)),
        out_shape=jax.ShapeDtypeStruct((b, l, d), F32),
        compiler_params=_params(("parallel", "parallel")),
        name="pool_mix_residual",
    )(x, pooled, w.astype(BF16), bias.astype(F32).reshape(1, d), scale.astype(F32).reshape(1, d), gate)


def pool_mixer(x, shift, scale_mod, gate, g1, w, bias, scale, width):
    (u,) = modulate_project(x, shift, scale_mod, g1, [])
    return pool_mix_residual(x, pool_tokens(u, width), w, bias, scale, gate)


SLAB_PAD = 8


def _round_up(v, m):
    return (v + m - 1) // m * m


def _fft_plan(l):
    n = 2 * l
    na = 1 << (n.bit_length() // 2)
    nb = n // na
    ns = na // 2 + 1
    n_pairs = (ns + 1) // 2
    return dict(l=l, n=n, na=na, nb=nb, ns=ns, nsp=_round_up(ns + 1, SUBLANES), pitch=2 * nb + SLAB_PAD,
                n_pairs=n_pairs, tb_unroll=min(nb, 16),
                pair_unroll=max(u for u in range(1, 12) if n_pairs % u == 0))


def _fft_tables(l):
    import numpy as np
    p = _fft_plan(l)
    n, na, nb, ns, nsp = p["n"], p["na"], p["nb"], p["ns"], p["nsp"]
    half = na // 2
    ka = np.arange(ns)[None, :, None]
    tb = np.arange(nb)[:, None, None]

    def stage1(ta):
        th = 2.0 * np.pi * (ta[None, None, :] * ka / na + tb * ka / n)
        m = np.zeros((nb, 2 * nsp, ta.shape[0]))
        m[:, :ns] = np.cos(th)
        m[:, nsp:nsp + ns] = -np.sin(th)
        return m

    f1 = stage1(np.arange(half))
    f1k = np.zeros((nb, 2 * nsp, na))
    f1k[:, :, :half] = f1
    f1k[1:, :, half:] = stage1(na - 1 - np.arange(half))[1:]
    tb0 = stage1(na - np.arange(half))[0]
    tb0[:, 0] = 0.0
    f1k[0, :, half:] = tb0

    k2 = np.arange(nb)
    ang = 2.0 * np.pi * np.outer(k2, k2) / nb
    c, s = np.cos(ang), np.sin(ang)
    f2 = np.block([[c, s], [-s, c]])
    g2 = np.block([[c, -s], [s, c]])

    ta = np.arange(half)[None, :, None]
    kk = np.arange(ns)[None, None, :]
    tbb = np.arange(nb)[:, None, None]
    ph = 2.0 * np.pi * (ta * kk / na + tbb * kk / n)
    wgt = np.where((kk == 0) | (kk == na // 2), 1.0, 2.0) / n
    g1 = np.zeros((nb, half, 2 * nsp))
    g1[:, :, :ns] = wgt * np.cos(ph)
    g1[:, :, nsp:nsp + ns] = -wgt * np.sin(ph)
    f32 = np.float32
    return p, f1.astype(f32), f1k.astype(f32), f2.astype(f32), g2.astype(f32), g1.astype(f32)


def _fft_stage1(gather, f1_ref, s_ref, p):
    nb, nsp, pitch = p["nb"], p["nsp"], p["pitch"]

    def body(tb, carry):
        a = jnp.dot(f1_ref[tb], gather(tb), preferred_element_type=F32)
        s_ref[pl.ds(tb, nsp, stride=pitch), :] = a[:nsp]
        s_ref[pl.ds(nb + tb, nsp, stride=pitch), :] = a[nsp:]
        return carry

    lax.fori_loop(0, nb, body, 0, unroll=p["tb_unroll"])


def _slab_pair(s_ref, i, p):
    nb, pitch = p["nb"], p["pitch"]
    r0 = pl.multiple_of(2 * i * pitch, SUBLANES)
    r1 = pl.multiple_of(2 * i * pitch + pitch, SUBLANES)
    return jnp.concatenate([s_ref[pl.ds(r0, 2 * nb), :], s_ref[pl.ds(r1, 2 * nb), :]], axis=1)


FILTER_ROWS = 512


def _filter_hidden_kernel(z_ref, w1_ref, b1_ref, w2_ref, b2_ref, fr_ref, o_ref):
    h = jnp.sin(fr_ref[0:1, :] * (jnp.dot(z_ref[...], w1_ref[...], preferred_element_type=F32, precision=HIGHEST)
                                  + b1_ref[...]))
    o_ref[...] = jnp.sin(fr_ref[1:2, :] * (jnp.dot(h, w2_ref[...], preferred_element_type=F32, precision=HIGHEST)
                                           + b2_ref[...]))


def _filter_out_kernel(h_ref, t_ref, w3f_ref, w3b_ref, dl_ref, o_ref, *, rows):
    l, tn = o_ref.shape[2], o_ref.shape[3]
    wf, wb, dl = w3f_ref[...], w3b_ref[...], dl_ref[...]

    def fill(i, ss):
        r = pl.multiple_of(i * rows, rows)
        h = h_ref[pl.ds(r, rows), :]
        decay = jnp.exp(-t_ref[pl.ds(r, rows), :] * dl)
        hf = _dot_split(h, wf) * decay
        hb = _dot_split(h, wb) * decay
        o_ref[0, 0, pl.ds(r, rows), :] = hf
        o_ref[0, 1, pl.ds(r, rows), :] = hb
        return ss + jnp.sum(hf * hf + hb * hb, axis=0, keepdims=True)

    ss = lax.fori_loop(0, l // rows, fill, jnp.zeros((1, tn), F32))
    scale = lax.rsqrt(ss + EPS)

    def rescale(i, carry):
        r = pl.multiple_of(i * rows, rows)
        o_ref[0, 0, pl.ds(r, rows), :] = o_ref[0, 0, pl.ds(r, rows), :] * scale
        o_ref[0, 1, pl.ds(r, rows), :] = o_ref[0, 1, pl.ds(r, rows), :] * scale
        return carry

    lax.fori_loop(0, l // rows, rescale, 0)


def hyena_filters(l, fw1, fb1, fw2, fb2, fw3, ffreq, tn=LANES):
    d = D_MODEL
    pos = jnp.arange(l, dtype=F32)
    t = jnp.linspace(0.0, 1.0, l, dtype=F32)
    wpos = 2.0 * math.pi * pos / l
    f = jnp.linspace(1e-4, HY_BANDS - 1, HY_BANDS, dtype=F32)
    ang = wpos[:, None] * f[None, :]
    emb_pad = _round_up(HY_EMB_DIM, SUBLANES)
    z = jnp.concatenate([t[:, None], jnp.cos(ang), -jnp.sin(ang), jnp.zeros((l, emb_pad - HY_EMB_DIM), F32)], axis=-1)
    w1 = jnp.concatenate([fw1.astype(F32), jnp.zeros((emb_pad - HY_EMB_DIM, fw1.shape[1]), F32)], axis=0)
    deltas = jnp.abs(jnp.linspace(HY_MIN_DECAY, HY_MAX_DECAY, d, dtype=F32)).reshape(1, d)
    hid = fw2.shape[0]
    rows = min(FILTER_ROWS, l)
    hidden = pl.pallas_call(
        _filter_hidden_kernel,
        grid=(l // rows,),
        in_specs=[pl.BlockSpec((rows, emb_pad), lambda r: (r, 0)), _resident((emb_pad, hid)), _resident((1, hid)),
                  _resident((hid, hid)), _resident((1, hid)), _resident((2, hid))],
        out_specs=pl.BlockSpec((rows, hid), lambda r: (r, 0)),
        out_shape=jax.ShapeDtypeStruct((l, hid), F32),
        compiler_params=_params(("parallel",)),
        name="hyena_filter_hidden",
    )(z, w1, fb1.astype(F32).reshape(1, hid), fw2.astype(F32), fb2.astype(F32).reshape(1, hid), ffreq.astype(F32))
    nt = d // tn
    w3 = fw3.astype(F32)
    return pl.pallas_call(
        functools.partial(_filter_out_kernel, rows=rows),
        grid=(HY_ORDER, nt),
        in_specs=[_resident((l, hid)), _resident((l, 1)),
                  pl.BlockSpec((hid, tn), lambda o, j: (0, (2 * o) * nt + j)),
                  pl.BlockSpec((hid, tn), lambda o, j: (0, (2 * o + 1) * nt + j)),
                  pl.BlockSpec((1, tn), lambda o, j: (0, j))],
        out_specs=pl.BlockSpec((1, 2, l, tn), lambda o, j: (o, 0, 0, j)),
        out_shape=jax.ShapeDtypeStruct((HY_ORDER, 2, l, d), F32),
        compiler_params=_params(("parallel", "parallel")),
        name="hyena_filters",
    )(hidden, t.reshape(l, 1), w3, w3, deltas)


def _filter_spectrum_kernel(hf_ref, hb_ref, f1_ref, f2_ref, o_ref, s_ref, *, p):
    na, nb = p["na"], p["nb"]
    half = na // 2

    def gather(tb):
        fwd = hf_ref[pl.ds(tb, half, stride=nb), :]
        bwd = hb_ref[pl.ds(jnp.where(tb == 0, 0, nb - tb), half, stride=nb), :]
        return jnp.concatenate([fwd, bwd], axis=0).astype(BF16)

    _fft_stage1(gather, f1_ref, s_ref, p)

    def body(i, carry):
        spec = jnp.dot(f2_ref[...], _slab_pair(s_ref, i, p).astype(BF16), preferred_element_type=F32)
        r = pl.multiple_of(i * 4 * nb, SUBLANES)
        o_ref[pl.ds(r, 2 * nb), :] = spec[:, :LANES].astype(BF16)
        o_ref[pl.ds(r + 2 * nb, 2 * nb), :] = spec[:, LANES:].astype(BF16)
        return carry

    lax.fori_loop(0, p["n_pairs"], body, 0, unroll=p["pair_unroll"])


def hyena_filter_spectrum(filt, tables):
    p, _, f1k, f2, _, _ = tables
    order, _, l, d = filt.shape
    rows = 2 * p["n_pairs"] * 2 * p["nb"]
    return pl.pallas_call(
        functools.partial(_filter_spectrum_kernel, p=p),
        grid=(order, d // LANES),
        in_specs=[pl.BlockSpec((None, None, l, LANES), lambda o, j: (o, 0, 0, j)),
                  pl.BlockSpec((None, None, l, LANES), lambda o, j: (o, 1, 0, j)),
                  _resident(f1k.shape), _resident(f2.shape)],
        out_specs=pl.BlockSpec((None, rows, LANES), lambda o, j: (o, 0, j)),
        out_shape=jax.ShapeDtypeStruct((order, rows, d), BF16),
        scratch_shapes=[pltpu.VMEM((p["nsp"] * p["pitch"], LANES), F32)],
        compiler_params=_params(("parallel", "parallel")),
        name="hyena_filter_spectrum",
    )(filt, filt, jnp.asarray(f1k).astype(BF16), jnp.asarray(f2).astype(BF16))


def _longconv_kernel(a_ref, m_ref, k_ref, bias_ref, f1_ref, f2_ref, g2_ref, g1_ref, o_ref, s_ref, *, p, row_chunk):
    na, nb, nsp, pitch = p["na"], p["nb"], p["nsp"], p["pitch"]
    half = na // 2

    _fft_stage1(lambda tb: a_ref[pl.ds(tb, half, stride=nb), :].astype(BF16), f1_ref, s_ref, p)

    def mid(i, carry):
        x = jnp.dot(f2_ref[...], _slab_pair(s_ref, i, p).astype(BF16), preferred_element_type=F32)
        r = pl.multiple_of(i * 4 * nb, SUBLANES)
        kk = jnp.concatenate([k_ref[pl.ds(r, 2 * nb), :], k_ref[pl.ds(r + 2 * nb, 2 * nb), :]], axis=1).astype(F32)
        xr, xi, kr, ki = x[:nb], x[nb:], kk[:nb], kk[nb:]
        y = jnp.concatenate([xr * kr - xi * ki, xr * ki + xi * kr], axis=0).astype(BF16)
        bq = jnp.dot(g2_ref[...], y, preferred_element_type=F32)
        r0 = pl.multiple_of(2 * i * pitch, SUBLANES)
        r1 = pl.multiple_of(2 * i * pitch + pitch, SUBLANES)
        s_ref[pl.ds(r0, 2 * nb), :] = bq[:, :LANES]
        s_ref[pl.ds(r1, 2 * nb), :] = bq[:, LANES:]
        return carry

    lax.fori_loop(0, p["n_pairs"], mid, 0, unroll=p["pair_unroll"])

    def last(tb, carry):
        bq = jnp.concatenate([s_ref[pl.ds(tb, nsp, stride=pitch), :], s_ref[pl.ds(nb + tb, nsp, stride=pitch), :]], axis=0)
        o_ref[pl.ds(tb, half, stride=nb), :] = jnp.dot(g1_ref[tb], bq.astype(BF16), preferred_element_type=F32)
        return carry

    lax.fori_loop(0, nb, last, 0, unroll=p["tb_unroll"])

    def finish(i, carry):
        r = pl.multiple_of(i * row_chunk, SUBLANES)
        a = a_ref[pl.ds(r, row_chunk), :]
        o_ref[pl.ds(r, row_chunk), :] = m_ref[pl.ds(r, row_chunk), :] * (o_ref[pl.ds(r, row_chunk), :] + bias_ref[...] * a)
        return carry

    lax.fori_loop(0, p["l"] // row_chunk, finish, 0)


def hyena_longconv(a, a_col, m, m_col, kspec, bias, tables):
    p, f1, _, f2, g2, g1 = tables
    b, l, _ = a.shape
    d = D_MODEL
    nt = d // LANES
    rows = kspec.shape[0]
    tabs = [jnp.asarray(t).astype(BF16) for t in (f1, f2, g2, g1)]
    return pl.pallas_call(
        functools.partial(_longconv_kernel, p=p, row_chunk=min(512, l)),
        grid=(nt, b),
        in_specs=[pl.BlockSpec((None, l, LANES), lambda j, i: (i, 0, a_col * nt + j)),
                  pl.BlockSpec((None, l, LANES), lambda j, i: (i, 0, m_col * nt + j)),
                  pl.BlockSpec((rows, LANES), lambda j, i: (0, j)),
                  pl.BlockSpec((1, LANES), lambda j, i: (0, j))] + [_resident(t.shape) for t in tabs],
        out_specs=pl.BlockSpec((None, l, LANES), lambda j, i: (i, 0, j)),
        out_shape=jax.ShapeDtypeStruct((b, l, d), F32),
        scratch_shapes=[pltpu.VMEM((p["nsp"] * p["pitch"], LANES), F32)],
        compiler_params=_params(("parallel", "parallel")),
        name="hyena_longconv",
    )(a, m, kspec, bias.astype(F32).reshape(1, d), *tabs)


def hyena_core(p_raw, conv_w, conv_b, fw1, fb1, fw2, fb2, fw3, ffreq, hbias):
    l = p_raw.shape[1]
    tables = _fft_tables(l)
    filt = hyena_filters(l, fw1, fb1, fw2, fb2, fw3, ffreq)
    kspec = hyena_filter_spectrum(filt, tables)
    pc = dwconv_tokens(p_raw, conv_w, conv_b)
    z = hyena_longconv(pc, 0, pc, 1, kspec[0], hbias[0], tables)
    return hyena_longconv(z, 0, pc, 2, kspec[1], hbias[1], tables)


def kernel(x, c, ctx, c_ctx, ada_w, ada_b, norm_g, ffn_w_gate, ffn_w_up, ffn_w_down, ssd_w_in, ssd_conv_w, ssd_conv_b, ssd_a_log, ssd_dt_bias, ssd_d, ssd_norm_g, ssd_w_out, pool_w, pool_b, pool_scale, hy_w_in, hy_conv_w, hy_conv_b, hy_filt_w1, hy_filt_b1, hy_filt_w2, hy_filt_b2, hy_filt_w3, hy_filt_freq, hy_bias, hy_w_out, final_g):
    batch = x.shape[0]
    d = D_MODEL
    h = ctx

    s = jnp.concatenate([jax.nn.silu(c), jax.nn.silu(c_ctx)[None], jnp.zeros((7 - batch, d), F32)], axis=0)
    mods = ada_modulation(s, ada_w, ada_b).reshape(DEPTH, 8, N_MOD, d)

    wg_bf, wu_bf, wd_bf = ffn_w_gate.astype(BF16), ffn_w_up.astype(BF16), ffn_w_down.astype(BF16)
    fg = final_g.reshape(1, d)

    for i in range(DEPTH):
        kind = i % N_MIXERS
        j = i // N_MIXERS
        last = i == DEPTH - 1
        ctx_in_needed = (not last) or kind == 0
        m = [mods[i, :batch, k].reshape(batch, 1, 1, d) for k in range(N_MOD)]
        mc = [jnp.broadcast_to(mods[i, batch, k].reshape(1, 1, 1, d), (batch, 1, 1, d)) for k in range(N_MOD)]
        g0, g1, g2 = (norm_g[i, k].reshape(1, d) for k in range(3))

        x = ffn_step(x, m[0], m[1], m[2], g0, wg_bf[i, 0], wu_bf[i, 0], wd_bf[i, 0])
        if ctx_in_needed:
            h = ffn_step(h, mc[0], mc[1], mc[2], g0, wg_bf[i, 0], wu_bf[i, 0], wd_bf[i, 0])

        if kind == 0:
            x, h_new = ssd_mixer(x, h, (m[3], m[4], m[5]), (mc[3], mc[4], mc[5]), g1, ssd_w_in[j], ssd_conv_w[j],
                                 ssd_conv_b[j], ssd_a_log[j], ssd_dt_bias[j], ssd_d[j], ssd_norm_g[j], ssd_w_out[j], not last)
            if not last:
                h = h_new
        elif kind == 1:
            x = pool_mixer(x, m[3], m[4], m[5], g1, pool_w[j], pool_b[j].reshape(-1), pool_scale[j], GRID_W)
            if not last:
                h = pool_mixer(h, mc[3], mc[4], mc[5], g1, pool_w[j], pool_b[j].reshape(-1), pool_scale[j], h.shape[1])
        else:
            w_in = hy_w_in[j].astype(BF16)
            w_out = hy_w_out[j].astype(BF16)
            filt = (hy_filt_w1[j], hy_filt_b1[j], hy_filt_w2[j], hy_filt_b2[j], hy_filt_w3[j], hy_filt_freq[j])
            (p_lat,) = modulate_project(x, m[3], m[4], g1, [w_in])
            x = project_residual(x, hyena_core(p_lat, hy_conv_w[j], hy_conv_b[j], *filt, hy_bias[j]), m[5], w_out)
            if not last:
                (p_ctx,) = modulate_project(h, mc[3], mc[4], g1, [w_in])
                h = project_residual(h, hyena_core(p_ctx, hy_conv_w[j], hy_conv_b[j], *filt, hy_bias[j]), mc[5], w_out)

        x = ffn_step(x, m[6], m[7], m[8], g2, wg_bf[i, 1], wu_bf[i, 1], wd_bf[i, 1],
                     final_g=fg if last else None)
        if not last:
            h = ffn_step(h, mc[6], mc[7], mc[8], g2, wg_bf[i, 1], wu_bf[i, 1], wd_bf[i, 1])
    return x
```

```python
import functools
import math

import jax
import jax.numpy as jnp
from jax import lax
from jax.experimental import pallas as pl
from jax.experimental.pallas import tpu as pltpu

F32 = jnp.float32
BF16 = jnp.bfloat16
HIGHEST = lax.Precision.HIGHEST

D_MODEL = 1024
DEPTH = 4
GRID_W = 64
N_MIXERS = 3
D_FF = 2816
N_MOD = 9
EPS = 1e-6
LOG2_E = 1.4426950408889634

SSD_D_INNER = 2 * D_MODEL
SSD_HEAD_DIM = 64
SSD_N_HEADS = SSD_D_INNER // SSD_HEAD_DIM
SSD_N_GROUPS = 4
SSD_HEADS_PER_GROUP = SSD_N_HEADS // SSD_N_GROUPS
SSD_D_STATE = 128
SSD_CHUNK = 128
SSD_GN = SSD_N_GROUPS * SSD_D_STATE
SSD_CONV_DIM = SSD_D_INNER + 2 * SSD_GN
SSD_GROUP_WIDTH = SSD_D_INNER // SSD_N_GROUPS

POOL_WINDOWS = (2, 4, 8, 16)
POOL_GROUPS = 4
POOL_GROUP_DIM = D_MODEL // POOL_GROUPS

HY_ORDER = 2
HY_EMB_DIM = 33
HY_BANDS = (HY_EMB_DIM - 1) // 2
HY_MAX_DECAY = math.log(1e-2) / 0.3
HY_MIN_DECAY = math.log(1e-2) / 1.5

VMEM_LIMIT_BYTES = 56 * 1024 * 1024
SUBLANES = 8
LANES = 128


def _params(sem):
    return pltpu.CompilerParams(dimension_semantics=sem, vmem_limit_bytes=VMEM_LIMIT_BYTES)


def _resident(shape):
    nd = len(shape)
    return pl.BlockSpec(shape, lambda *_: (0,) * nd, pipeline_mode=pl.Buffered(1))


def _mod_spec(d, _unused=0):
    return pl.BlockSpec((1, 1, 1, d), lambda i, j: (i, 0, 0, 0))


def _modulated(x, g, shift, scale):
    ms = jnp.mean(x * x, axis=-1, keepdims=True)
    return (x * lax.rsqrt(ms + EPS)) * g * (1.0 + scale) + shift


def _silu(v):
    return v * jax.nn.sigmoid(v)


def _ada_kernel(s_ref, w_ref, b_ref, o_ref):
    o_ref[0] = jnp.dot(s_ref[...], w_ref[0], preferred_element_type=F32, precision=HIGHEST) + b_ref[0]


def ada_modulation(s, ada_w, ada_b, tn=1024):
    depth, d, n = ada_w.shape
    r = s.shape[0]
    return pl.pallas_call(
        _ada_kernel,
        grid=(depth, n // tn),
        in_specs=[pl.BlockSpec((r, d), lambda i, j: (0, 0)),
                  pl.BlockSpec((1, d, tn), lambda i, j: (i, 0, j)),
                  pl.BlockSpec((1, 1, tn), lambda i, j: (i, 0, j))],
        out_specs=pl.BlockSpec((1, r, tn), lambda i, j: (i, 0, j)),
        out_shape=jax.ShapeDtypeStruct((depth, r, n), F32),
        compiler_params=_params(("parallel", "parallel")),
        name="ada_modulation",
    )(s, ada_w, ada_b.reshape(depth, 1, n))


def _ffn_kernel(x_ref, sh_ref, sc_ref, gt_ref, g_ref, wg_ref, wu_ref, wd_ref, *rest, f_chunk, final):
    if final:
        fg_ref, o_ref = rest
    else:
        (o_ref,) = rest
    x = x_ref[0]
    u = _modulated(x, g_ref[...], sh_ref[0, 0], sc_ref[0, 0]).astype(BF16)
    acc = jnp.zeros(x.shape, F32)
    d_ff = wg_ref.shape[1]
    for f0 in range(0, d_ff, f_chunk):
        a = jnp.dot(u, wg_ref[:, f0:f0 + f_chunk], preferred_element_type=F32)
        b = jnp.dot(u, wu_ref[:, f0:f0 + f_chunk], preferred_element_type=F32)
        h = (_silu(a) * b).astype(BF16)
        acc = acc + jnp.dot(h, wd_ref[f0:f0 + f_chunk, :], preferred_element_type=F32)
    y = x + (0.5 * gt_ref[0, 0]) * acc
    if final:
        ms = jnp.mean(y * y, axis=-1, keepdims=True)
        y = y * lax.rsqrt(ms + EPS) * fg_ref[...]
    o_ref[0] = y


def ffn_step(x, shift, scale, gate, g, wg, wu, wd, final_g=None, tm=512, f_chunk=256):
    b, l, d = x.shape
    tm = min(tm, l)
    f = wg.shape[1]
    final = final_g is not None
    mod_spec = _mod_spec(d, 0)
    in_specs = [pl.BlockSpec((1, tm, d), lambda i, j: (i, j, 0)),
                mod_spec, mod_spec, mod_spec,
                _resident((1, d)), _resident((d, f)), _resident((d, f)), _resident((f, d))]
    args = [x, shift, scale, gate, g, wg, wu, wd]
    if final:
        in_specs.append(_resident((1, d)))
        args.append(final_g)
    return pl.pallas_call(
        functools.partial(_ffn_kernel, f_chunk=f_chunk, final=final),
        grid=(b, l // tm),
        in_specs=in_specs,
        out_specs=pl.BlockSpec((1, tm, d), lambda i, j: (i, j, 0)),
        out_shape=jax.ShapeDtypeStruct((b, l, d), F32),
        compiler_params=_params(("parallel", "parallel")),
        name="ffn_step",
    )(*args)


CONV_COLS = 512


def _modproj_kernel(*refs, n_w, n_dest, conv, n_tiles):
    if conv:
        prev_ref, x_ref, next_ref, sh_ref, sc_ref, g_ref, *rest = refs
    else:
        x_ref, sh_ref, sc_ref, g_ref, *rest = refs
    w_refs = rest[:n_w]
    cw_refs = rest[n_w:n_w + 2 * len(conv)]
    o_refs = rest[n_w + 2 * len(conv) + n_dest:]
    g, sh, sc = g_ref[...], sh_ref[0, 0], sc_ref[0, 0]
    u = _modulated(x_ref[0], g, sh, sc)
    if n_w == 0:
        o_refs[0][0] = u
        return
    u = u.astype(BF16)
    tm = x_ref.shape[1]
    if conv:
        ue = _modulated(jnp.concatenate([prev_ref[0], x_ref[0], next_ref[0]], axis=0), g, sh, sc).astype(BF16)
        j = pl.program_id(1)
        top = (j > 0).astype(F32)
        bot = (j < n_tiles - 1).astype(F32)
    for idx, (w_ref, o_ref) in enumerate(zip(w_refs, o_refs)):
        if idx not in conv:
            o_ref[0] = jnp.dot(u, w_ref[...], preferred_element_type=F32).astype(o_ref.dtype)
            continue
        k_w, silu = conv[idx]
        cw_ref, cb_ref = cw_refs[2 * list(conv).index(idx)], cw_refs[2 * list(conv).index(idx) + 1]
        p = k_w // 2
        rows = tm + 2 * SUBLANES
        for c0 in range(0, w_ref.shape[1], CONV_COLS):
            ce = jnp.dot(ue, w_ref[:, c0:c0 + CONV_COLS], preferred_element_type=F32)
            ce = jnp.concatenate([ce[:SUBLANES] * top, ce[SUBLANES:SUBLANES + tm], ce[SUBLANES + tm:] * bot], axis=0)
            acc = jnp.broadcast_to(cb_ref[:, c0:c0 + CONV_COLS], (tm, CONV_COLS))
            for k in range(k_w):
                tap = ce if k == p else pltpu.roll(ce, (p - k) % rows, axis=0)
                acc = acc + cw_ref[k:k + 1, c0:c0 + CONV_COLS] * tap[SUBLANES:SUBLANES + tm, :]
            o_ref[0, :, c0:c0 + CONV_COLS] = (_silu(acc) if silu else acc).astype(o_ref.dtype)


def modulate_project(x, shift, scale, g, weights, tm=256, out_dtypes=None, out_rows=None, row_block=0, dest=None,
                     conv=None):
    b, l, d = x.shape
    tm = min(tm, l)
    conv = dict(conv or {})
    mod_spec = _mod_spec(d)
    x_spec = pl.BlockSpec((1, tm, d), lambda i, j: (i, j, 0))
    args = [x]
    in_specs = [x_spec]
    if conv:
        rb, last = tm // SUBLANES, l // SUBLANES - 1
        in_specs = [pl.BlockSpec((1, SUBLANES, d), lambda i, j: (i, jnp.maximum(j * rb - 1, 0), 0)), x_spec,
                    pl.BlockSpec((1, SUBLANES, d), lambda i, j: (i, jnp.minimum((j + 1) * rb, last), 0))]
        args = [x, x, x]
    in_specs += [mod_spec, mod_spec, _resident((1, d))] + [_resident(w.shape) for w in weights]
    args += [shift, scale, g, *weights]
    for i in conv:
        cw, cb, _ = conv[i]
        assert weights[i].shape[1] % CONV_COLS == 0
        in_specs += [_resident(cw.shape), _resident((1, cb.shape[0]))]
        args += [cw.astype(F32), cb.astype(F32).reshape(1, -1)]
    widths = [w.shape[1] for w in weights] or [d]
    out_dtypes = out_dtypes or [F32] * len(widths)
    dest = list(dest or [])
    in_specs += [pl.BlockSpec(memory_space=pl.ANY)] * len(dest)
    n_in = len(args)
    return pl.pallas_call(
        functools.partial(_modproj_kernel, n_w=len(weights), n_dest=len(dest),
                          conv={i: (c[0].shape[0], c[2]) for i, c in conv.items()}, n_tiles=l // tm),
        grid=(b, l // tm),
        in_specs=in_specs,
        out_specs=[pl.BlockSpec((1, tm, n), lambda i, j: (i, j + row_block, 0)) for n in widths],
        out_shape=[jax.ShapeDtypeStruct((b, out_rows or l, n), dt) for n, dt in zip(widths, out_dtypes)],
        input_output_aliases={n_in + k: k for k in range(len(dest))},
        compiler_params=_params(("parallel", "parallel")),
        name="modulate_project",
    )(*args, *dest)


def _outproj_kernel(x_ref, a_ref, gt_ref, w_ref, o_ref):
    y = jnp.dot(a_ref[0].astype(BF16), w_ref[...], preferred_element_type=F32)
    o_ref[0] = x_ref[0] + gt_ref[0, 0] * y


def project_residual(x, a, gate, w, tm=512):
    b, l, d = x.shape
    k = a.shape[-1]
    tm = min(tm, l)
    return pl.pallas_call(
        _outproj_kernel,
        grid=(b, l // tm),
        in_specs=[pl.BlockSpec((1, tm, d), lambda i, j: (i, j, 0)),
                  pl.BlockSpec((1, tm, k), lambda i, j: (i, j, 0)),
                  _mod_spec(d, 0),
                  _resident((k, d))],
        out_specs=pl.BlockSpec((1, tm, d), lambda i, j: (i, j, 0)),
        out_shape=jax.ShapeDtypeStruct((b, l, d), F32),
        compiler_params=_params(("parallel", "parallel")),
        name="project_residual",
    )(x, a, gate, w)


def _bf16_terms(v):
    hi = v.astype(BF16)
    r = v - hi.astype(F32)
    mid = r.astype(BF16)
    return hi, mid, (r - mid.astype(F32)).astype(BF16)


def _dot_f32_rhs(m, v):
    return sum(jnp.dot(m, t, preferred_element_type=F32) for t in _bf16_terms(v))


def _dot_f32_lhs(v, m):
    return sum(jnp.dot(t, m, preferred_element_type=F32) for t in _bf16_terms(v))


def _dot_split(a, b):
    a_hi, a_lo, _ = _bf16_terms(a)
    b_hi, b_lo, _ = _bf16_terms(b)
    return (jnp.dot(a_hi, b_hi, preferred_element_type=F32) + jnp.dot(a_lo, b_hi, preferred_element_type=F32)
            + jnp.dot(a_hi, b_lo, preferred_element_type=F32))


def _ssd_scan_kernel(xbc_ref, dt_ref, dtt_ref, dtb_ref, dtbt_ref, a_ref, at_ref, ex_ref, o_ref, h_ref):
    nh, q, hp = SSD_N_HEADS, SSD_CHUNK, SSD_HEAD_DIM
    d = pl.program_id(1)
    fwd = d == 0

    @pl.when(pl.program_id(2) == 0)
    def _():
        h_ref[...] = jnp.zeros(h_ref.shape, F32)

    dt2 = jax.nn.softplus(dt_ref[0] + dtb_ref[...])
    dt = jnp.where(fwd, dt2[:, :nh], dt2[:, nh:])
    a = dt * jnp.where(fwd, a_ref[:, :nh], a_ref[:, nh:])
    dtt2 = jax.nn.softplus(dtt_ref[0] + dtbt_ref[...])
    a_t = jnp.where(fwd, dtt2[:nh], dtt2[nh:]) * jnp.where(fwd, at_ref[:nh], at_ref[nh:])

    r_i = lax.broadcasted_iota(jnp.int32, (q, q), 0)
    c_i = lax.broadcasted_iota(jnp.int32, (q, q), 1)
    ahead = jnp.where(fwd, c_i - r_i, r_i - c_i)
    seen = ahead <= 0
    seen_t = ahead >= 0
    cum = _dot_f32_rhs(seen.astype(BF16), a)
    cum_t = _dot_f32_lhs(a_t, seen_t.astype(BF16))
    total = jnp.where(fwd, cum[q - 1:q, :], cum[0:1, :])
    ecum = jnp.exp(cum).astype(BF16)
    dtdec = (dt * jnp.exp(total - cum)).astype(BF16)
    etot = jnp.exp(total)
    cum2 = cum * LOG2_E
    row2 = cum_t * LOG2_E - jnp.log2(jnp.where(fwd, dtt2[:nh], dtt2[nh:]))

    pw = 2 * hp
    left = lax.broadcasted_iota(jnp.int32, (q, pw), 1) < hp
    left_row = lax.broadcasted_iota(jnp.int32, (1, pw), 1) < hp
    pairs = SSD_HEADS_PER_GROUP // 2
    for g in range(SSD_N_GROUPS):
        b_g = xbc_ref[0, :, SSD_D_INNER + g * SSD_D_STATE:SSD_D_INNER + (g + 1) * SSD_D_STATE].astype(BF16)
        c_lo = SSD_D_INNER + SSD_GN + g * SSD_D_STATE
        c_g = xbc_ref[0, :, c_lo:c_lo + SSD_D_STATE].astype(BF16)
        cb = lax.dot_general(c_g, b_g, (((1,), (1,)), ((), ())), preferred_element_type=F32)
        h_g = h_ref[g]
        ex_g = ex_ref[:, g * SSD_GROUP_WIDTH:(g + 1) * SSD_GROUP_WIDTH]
        ch = (jnp.dot(c_g, h_g.astype(BF16), preferred_element_type=F32)
              * jnp.dot(ecum, ex_g, preferred_element_type=F32))
        dtdec_g = jnp.dot(dtdec, ex_g, preferred_element_type=F32)
        xdecs, etots = [], []
        for pr in range(pairs):
            e1 = g * SSD_HEADS_PER_GROUP + 2 * pr
            e2 = e1 + 1
            x_pair = xbc_ref[0, :, e1 * hp:e1 * hp + pw].astype(F32)
            ws = []
            for e in (e1, e2):
                seg2 = cum2[:, e:e + 1] - row2[e:e + 1, :]
                ws.append((cb * jnp.exp2(jnp.where(seen, seg2, -jnp.inf))).astype(BF16))
            rhs = jnp.concatenate([jnp.where(left, x_pair, 0.0), jnp.where(left, 0.0, x_pair)], axis=0).astype(BF16)
            y_diag = jnp.dot(jnp.concatenate(ws, axis=1), rhs, preferred_element_type=F32)
            y_off = ch[:, pr * pw:(pr + 1) * pw]
            o_ref[0, 0, :, e1 * hp:e1 * hp + pw] = (y_diag + y_off).astype(o_ref.dtype)
            xdecs.append((x_pair * dtdec_g[:, pr * pw:(pr + 1) * pw]).astype(BF16))
            etots.append(jnp.where(left_row, etot[:, e1:e1 + 1], etot[:, e2:e2 + 1]))
        s_g = lax.dot_general(b_g, jnp.concatenate(xdecs, axis=1), (((0,), (0,)), ((), ())),
                              preferred_element_type=F32)
        h_ref[g] = h_g * jnp.concatenate(etots, axis=1) + s_g


def ssd_scan(xbc, dt_raw, dt_bias, a_log, n_lead_chunks):
    b, t, _ = xbc.shape
    q, nh = SSD_CHUNK, SSD_N_HEADS
    nc = t // q
    dt_t = jnp.swapaxes(dt_raw, 1, 2)
    a_neg = -jnp.exp(a_log.astype(F32)).reshape(1, 2 * nh)
    dtb = dt_bias.astype(F32).reshape(1, 2 * nh)
    expand = jnp.repeat(jnp.eye(nh, dtype=BF16), SSD_HEAD_DIM, axis=1)

    def chunk_of(d, c):
        back = jnp.where(c < n_lead_chunks, n_lead_chunks - 1 - c, nc - 1 + n_lead_chunks - c)
        return jnp.where(d == 0, c, back)

    return pl.pallas_call(
        _ssd_scan_kernel,
        grid=(b, 2, nc),
        in_specs=[pl.BlockSpec((1, q, SSD_CONV_DIM), lambda i, d, c: (i, chunk_of(d, c), 0)),
                  pl.BlockSpec((1, q, 2 * nh), lambda i, d, c: (i, chunk_of(d, c), 0)),
                  pl.BlockSpec((1, 2 * nh, q), lambda i, d, c: (i, 0, chunk_of(d, c))),
                  _resident((1, 2 * nh)), _resident((2 * nh, 1)),
                  _resident((1, 2 * nh)), _resident((2 * nh, 1)), _resident((nh, SSD_D_INNER))],
        out_specs=pl.BlockSpec((1, 1, q, SSD_D_INNER), lambda i, d, c: (i, d, chunk_of(d, c), 0)),
        out_shape=jax.ShapeDtypeStruct((b, 2, t, SSD_D_INNER), BF16),
        scratch_shapes=[pltpu.VMEM((SSD_N_GROUPS, SSD_D_STATE, SSD_GROUP_WIDTH), F32)],
        compiler_params=_params(("parallel", "parallel", "arbitrary")),
        name="ssd_scan",
    )(xbc, dt_raw, dt_t, dtb, dtb.reshape(2 * nh, 1), a_neg, a_neg.reshape(2 * nh, 1), expand)


def _ssd_out_kernel(x_ref, y0_ref, y1_ref, xs0_ref, xs1_ref, z_ref, dsk_ref, ng_ref, gt_ref, w_ref, o_ref):
    xs = jnp.concatenate([xs0_ref[0], xs1_ref[0]], axis=1).astype(F32)
    y = y0_ref[0, 0].astype(F32) + y1_ref[0, 0].astype(F32) + dsk_ref[...] * xs
    gy = y * _silu(z_ref[0].astype(F32))
    parts = []
    for g in range(SSD_N_GROUPS):
        blk = gy[:, g * SSD_GROUP_WIDTH:(g + 1) * SSD_GROUP_WIDTH]
        parts.append(blk * lax.rsqrt(jnp.mean(blk * blk, axis=-1, keepdims=True) + EPS))
    a = (jnp.concatenate(parts, axis=1) * ng_ref[...]).astype(BF16)
    o_ref[0] = x_ref[0] + gt_ref[0, 0] * jnp.dot(a, w_ref[...], preferred_element_type=F32)


def ssd_gate_project(x, y2, xbc, z, d_skip, norm_g, gate, w_out, src_block, tm=256):
    b, t, d = x.shape
    half = SSD_D_INNER // 2
    dsk = jnp.repeat(d_skip.astype(F32), SSD_HEAD_DIM).reshape(1, SSD_D_INNER)
    return pl.pallas_call(
        _ssd_out_kernel,
        grid=(b, t // tm),
        in_specs=[pl.BlockSpec((1, tm, d), lambda i, j: (i, j, 0)),
                  pl.BlockSpec((1, 1, tm, SSD_D_INNER), lambda i, j: (i, 0, j + src_block, 0)),
                  pl.BlockSpec((1, 1, tm, SSD_D_INNER), lambda i, j: (i, 1, j + src_block, 0)),
                  pl.BlockSpec((1, tm, half), lambda i, j: (i, j + src_block, 0)),
                  pl.BlockSpec((1, tm, half), lambda i, j: (i, j + src_block, 1)),
                  pl.BlockSpec((1, tm, SSD_D_INNER), lambda i, j: (i, j + src_block, 0)),
                  _resident((1, SSD_D_INNER)), _resident((1, SSD_D_INNER)),
                  _mod_spec(d),
                  _resident((SSD_D_INNER, d))],
        out_specs=pl.BlockSpec((1, tm, d), lambda i, j: (i, j, 0)),
        out_shape=jax.ShapeDtypeStruct((b, t, d), F32),
        compiler_params=_params(("parallel", "parallel")),
        name="ssd_gate_project",
    )(x, y2, y2, xbc, xbc, z, dsk, norm_g.astype(F32).reshape(1, SSD_D_INNER), gate, w_out)


def ssd_mixer(x, h, m_lat, m_ctx, g1, w_in, conv_w, conv_b, a_log, dt_bias, d_skip, norm_g, w_out, need_ctx_out, tm=256):
    lc, l = h.shape[1], x.shape[1]
    assert lc % tm == 0
    n0 = lc // tm
    w_bf = w_in.astype(BF16)
    ws = [w_bf[:, :SSD_D_INNER], w_bf[:, SSD_D_INNER:SSD_D_INNER + SSD_CONV_DIM], w_bf[:, SSD_D_INNER + SSD_CONV_DIM:]]
    dts = [BF16, BF16, F32]
    conv = {1: (conv_w, conv_b, True)}
    parts = modulate_project(x, m_lat[0], m_lat[1], g1, ws, tm=tm, out_dtypes=dts, out_rows=lc + l, row_block=n0, conv=conv)
    z, xbc, dt_raw = modulate_project(h, m_ctx[0], m_ctx[1], g1, ws, tm=tm, out_dtypes=dts, out_rows=lc + l, dest=parts,
                                      conv=conv)
    y2 = ssd_scan(xbc, dt_raw, dt_bias, a_log, lc // SSD_CHUNK)
    w_out = w_out.astype(BF16)
    x_new = ssd_gate_project(x, y2, xbc, z, d_skip, norm_g, m_lat[2], w_out, n0, tm=tm)
    h_new = ssd_gate_project(h, y2, xbc, z, d_skip, norm_g, m_ctx[2], w_out, 0, tm=tm) if need_ctx_out else None
    return x_new, h_new


POOL_CHUNK = 256
POOL_MAX_HALF = max(POOL_WINDOWS) // 2


def _pool_band_tables(width):
    import numpy as np
    pos = np.arange(POOL_CHUNK)
    line, col = pos // width, pos % width
    out = []
    for w in POOL_WINDOWS:
        inside = (col[None, :] >= col[:, None] - w // 2) & (col[None, :] <= col[:, None] + (w - w // 2) - 1)
        out.append((inside & (line[None, :] == line[:, None])).astype(np.float32))
    return np.stack(out)


def _pool_kernel(u_ref, band_ref, o_ref, cs_ref, *, width, n_lines):
    l, c = u_ref.shape
    n_chunks = l // POOL_CHUNK
    assert POOL_WINDOWS == tuple(2 << g for g in range(POOL_GROUPS)) and width & (width - 1) == 0
    half = lax.shift_left(jnp.int32(1), pl.program_id(1) // (POOL_GROUP_DIM // LANES))
    log_w = width.bit_length() - 1
    pad = POOL_MAX_HALF * width if n_lines > 1 else 0
    if pad:
        cs_ref[0:pad, :] = jnp.zeros((pad, c), F32)
        cs_ref[pad + l:2 * pad + l, :] = jnp.zeros((pad, c), F32)

    def col_pass(i, carry):
        r = pl.multiple_of(i * POOL_CHUNK, POOL_CHUNK)
        u = u_ref[pl.ds(r, POOL_CHUNK), :]
        hi = u.astype(BF16)
        lo = (u - hi.astype(F32)).astype(BF16)
        both = jnp.dot(band_ref[0], jnp.concatenate([hi, lo], axis=1), preferred_element_type=F32)
        cs_ref[pl.ds(pad + r, POOL_CHUNK), :] = both[:, :c] + both[:, c:]
        return carry

    lax.fori_loop(0, n_chunks, col_pass, 0)

    def out_pass(i, carry):
        r = pl.multiple_of(i * POOL_CHUNK, POOL_CHUNK)
        pos = r + lax.broadcasted_iota(jnp.int32, (POOL_CHUNK, c), 0)
        col = pos & (width - 1)
        cnt = jnp.minimum(col + half, width) - jnp.maximum(col - half, 0)
        if n_lines > 1:
            def add_line(k, acc):
                return acc + cs_ref[pl.ds(pl.multiple_of(pad + r + k * width, SUBLANES), POOL_CHUNK), :]
            s = lax.fori_loop(-half, half, add_line, jnp.zeros((POOL_CHUNK, c), F32))
            line = lax.shift_right_logical(pos, log_w)
            cnt_l = jnp.minimum(line + half, n_lines) - jnp.maximum(line - half, 0)
            mean = s / (cnt_l.astype(F32) * cnt.astype(F32))
        else:
            mean = cs_ref[pl.ds(r, POOL_CHUNK), :] / cnt.astype(F32)
        o_ref[pl.ds(r, POOL_CHUNK), :] = (mean - u_ref[pl.ds(r, POOL_CHUNK), :]).astype(o_ref.dtype)
        return carry

    lax.fori_loop(0, n_chunks, out_pass, 0)


def pool_tokens(u, width):
    b, l, d = u.shape
    n_lines = l // width
    bands = jnp.asarray(_pool_band_tables(width)).astype(BF16)
    tiles_per_group = POOL_GROUP_DIM // LANES
    pad = POOL_MAX_HALF * width if n_lines > 1 else 0
    return pl.pallas_call(
        functools.partial(_pool_kernel, width=width, n_lines=n_lines),
        grid=(b, d // LANES),
        in_specs=[pl.BlockSpec((None, l, LANES), lambda i, j: (i, 0, j)),
                  pl.BlockSpec((1, POOL_CHUNK, POOL_CHUNK), lambda i, j: (j // tiles_per_group, 0, 0))],
        out_specs=pl.BlockSpec((None, l, LANES), lambda i, j: (i, 0, j)),
        out_shape=jax.ShapeDtypeStruct((b, l, d), BF16),
        scratch_shapes=[pltpu.VMEM((l + 2 * pad, LANES), F32)],
        compiler_params=_params(("parallel", "parallel")),
        name="pool_tokens",
    )(u, bands)


def _pool_mix_kernel(x_ref, p_ref, w_ref, b_ref, sc_ref, gt_ref, o_ref):
    ys = [jnp.dot(p_ref[0, :, g * POOL_GROUP_DIM:(g + 1) * POOL_GROUP_DIM], w_ref[g], preferred_element_type=F32)
          for g in range(POOL_GROUPS)]
    y = (jnp.concatenate(ys, axis=1) + b_ref[...]) * sc_ref[...]
    o_ref[0] = x_ref[0] + gt_ref[0, 0] * y


def pool_mix_residual(x, pooled, w, bias, scale, gate, tm=512):
    b, l, d = x.shape
    tm = min(tm, l)
    return pl.pallas_call(
        _pool_mix_kernel,
        grid=(b, l // tm),
        in_specs=[pl.BlockSpec((1, tm, d), lambda i, j: (i, j, 0)),
                  pl.BlockSpec((1, tm, d), lambda i, j: (i, j, 0)),
                  _resident(w.shape), _resident((1, d)), _resident((1, d)), _mod_spec(d, 0)],
        out_specs=pl.BlockSpec((1, tm, d), lambda i, j: (i, j, 0)),
        out_shape=jax.ShapeDtypeStruct((b, l, d), F32),
        compiler_params=_params(("parallel", "parallel")),
        name="pool_mix_residual",
    )(x, pooled, w.astype(BF16), bias.astype(F32).reshape(1, d), scale.astype(F32).reshape(1, d), gate)


def pool_mixer(x, shift, scale_mod, gate, g1, w, bias, scale, width):
    (u,) = modulate_project(x, shift, scale_mod, g1, [])
    return pool_mix_residual(x, pool_tokens(u, width), w, bias, scale, gate)


SLAB_PAD = 8


def _round_up(v, m):
    return (v + m - 1) // m * m


def _fft_plan(l):
    n = 2 * l
    na = 1 << (n.bit_length() // 2)
    nb = n // na
    ns = na // 2 + 1
    n_pairs = (ns + 1) // 2
    return dict(l=l, n=n, na=na, nb=nb, ns=ns, nsp=_round_up(ns + 1, SUBLANES), pitch=2 * nb + SLAB_PAD,
                n_pairs=n_pairs, tb_unroll=min(nb, 16),
                pair_unroll=max(u for u in range(1, 12) if n_pairs % u == 0))


def _fft_tables(l):
    import numpy as np
    p = _fft_plan(l)
    n, na, nb, ns, nsp = p["n"], p["na"], p["nb"], p["ns"], p["nsp"]
    half = na // 2
    ka = np.arange(ns)[None, :, None]
    tb = np.arange(nb)[:, None, None]

    def stage1(ta):
        th = 2.0 * np.pi * (ta[None, None, :] * ka / na + tb * ka / n)
        m = np.zeros((nb, 2 * nsp, ta.shape[0]))
        m[:, :ns] = np.cos(th)
        m[:, nsp:nsp + ns] = -np.sin(th)
        return m

    f1 = stage1(np.arange(half))
    f1k = np.zeros((nb, 2 * nsp, na))
    f1k[:, :, :half] = f1
    f1k[1:, :, half:] = stage1(na - 1 - np.arange(half))[1:]
    tb0 = stage1(na - np.arange(half))[0]
    tb0[:, 0] = 0.0
    f1k[0, :, half:] = tb0

    k2 = np.arange(nb)
    ang = 2.0 * np.pi * np.outer(k2, k2) / nb
    c, s = np.cos(ang), np.sin(ang)
    f2 = np.block([[c, s], [-s, c]])
    g2 = np.block([[c, -s], [s, c]])

    ta = np.arange(half)[None, :, None]
    kk = np.arange(ns)[None, None, :]
    tbb = np.arange(nb)[:, None, None]
    ph = 2.0 * np.pi * (ta * kk / na + tbb * kk / n)
    wgt = np.where((kk == 0) | (kk == na // 2), 1.0, 2.0) / n
    g1 = np.zeros((nb, half, 2 * nsp))
    g1[:, :, :ns] = wgt * np.cos(ph)
    g1[:, :, nsp:nsp + ns] = -wgt * np.sin(ph)
    f32 = np.float32
    return p, f1.astype(f32), f1k.astype(f32), f2.astype(f32), g2.astype(f32), g1.astype(f32)


def _fft_stage1(gather, f1_ref, s_ref, p):
    nb, nsp, pitch = p["nb"], p["nsp"], p["pitch"]

    def body(tb, carry):
        a = jnp.dot(f1_ref[tb], gather(tb), preferred_element_type=F32)
        s_ref[pl.ds(tb, nsp, stride=pitch), :] = a[:nsp]
        s_ref[pl.ds(nb + tb, nsp, stride=pitch), :] = a[nsp:]
        return carry

    lax.fori_loop(0, nb, body, 0, unroll=p["tb_unroll"])


def _slab_pair(s_ref, i, p):
    nb, pitch = p["nb"], p["pitch"]
    r0 = pl.multiple_of(2 * i * pitch, SUBLANES)
    r1 = pl.multiple_of(2 * i * pitch + pitch, SUBLANES)
    return jnp.concatenate([s_ref[pl.ds(r0, 2 * nb), :], s_ref[pl.ds(r1, 2 * nb), :]], axis=1)


FILTER_ROWS = 512


def _filter_hidden_kernel(z_ref, w1_ref, b1_ref, w2_ref, b2_ref, fr_ref, o_ref):
    h = jnp.sin(fr_ref[0:1, :] * (jnp.dot(z_ref[...], w1_ref[...], preferred_element_type=F32, precision=HIGHEST)
                                  + b1_ref[...]))
    o_ref[...] = jnp.sin(fr_ref[1:2, :] * (jnp.dot(h, w2_ref[...], preferred_element_type=F32, precision=HIGHEST)
                                           + b2_ref[...]))


def _filter_out_kernel(h_ref, t_ref, w3f_ref, w3b_ref, dl_ref, o_ref, *, rows):
    l, tn = o_ref.shape[2], o_ref.shape[3]
    wf, wb, dl = w3f_ref[...], w3b_ref[...], dl_ref[...]

    def fill(i, ss):
        r = pl.multiple_of(i * rows, rows)
        h = h_ref[pl.ds(r, rows), :]
        decay = jnp.exp(-t_ref[pl.ds(r, rows), :] * dl)
        hf = _dot_split(h, wf) * decay
        hb = _dot_split(h, wb) * decay
        o_ref[0, 0, pl.ds(r, rows), :] = hf
        o_ref[0, 1, pl.ds(r, rows), :] = hb
        return ss + jnp.sum(hf * hf + hb * hb, axis=0, keepdims=True)

    ss = lax.fori_loop(0, l // rows, fill, jnp.zeros((1, tn), F32))
    scale = lax.rsqrt(ss + EPS)

    def rescale(i, carry):
        r = pl.multiple_of(i * rows, rows)
        o_ref[0, 0, pl.ds(r, rows), :] = o_ref[0, 0, pl.ds(r, rows), :] * scale
        o_ref[0, 1, pl.ds(r, rows), :] = o_ref[0, 1, pl.ds(r, rows), :] * scale
        return carry

    lax.fori_loop(0, l // rows, rescale, 0)


def hyena_filters(l, fw1, fb1, fw2, fb2, fw3, ffreq, tn=LANES):
    d = D_MODEL
    pos = jnp.arange(l, dtype=F32)
    t = jnp.linspace(0.0, 1.0, l, dtype=F32)
    wpos = 2.0 * math.pi * pos / l
    f = jnp.linspace(1e-4, HY_BANDS - 1, HY_BANDS, dtype=F32)
    ang = wpos[:, None] * f[None, :]
    emb_pad = _round_up(HY_EMB_DIM, SUBLANES)
    z = jnp.concatenate([t[:, None], jnp.cos(ang), -jnp.sin(ang), jnp.zeros((l, emb_pad - HY_EMB_DIM), F32)], axis=-1)
    w1 = jnp.concatenate([fw1.astype(F32), jnp.zeros((emb_pad - HY_EMB_DIM, fw1.shape[1]), F32)], axis=0)
    deltas = jnp.abs(jnp.linspace(HY_MIN_DECAY, HY_MAX_DECAY, d, dtype=F32)).reshape(1, d)
    hid = fw2.shape[0]
    rows = min(FILTER_ROWS, l)
    hidden = pl.pallas_call(
        _filter_hidden_kernel,
        grid=(l // rows,),
        in_specs=[pl.BlockSpec((rows, emb_pad), lambda r: (r, 0)), _resident((emb_pad, hid)), _resident((1, hid)),
                  _resident((hid, hid)), _resident((1, hid)), _resident((2, hid))],
        out_specs=pl.BlockSpec((rows, hid), lambda r: (r, 0)),
        out_shape=jax.ShapeDtypeStruct((l, hid), F32),
        compiler_params=_params(("parallel",)),
        name="hyena_filter_hidden",
    )(z, w1, fb1.astype(F32).reshape(1, hid), fw2.astype(F32), fb2.astype(F32).reshape(1, hid), ffreq.astype(F32))
    nt = d // tn
    w3 = fw3.astype(F32)
    return pl.pallas_call(
        functools.partial(_filter_out_kernel, rows=rows),
        grid=(HY_ORDER, nt),
        in_specs=[_resident((l, hid)), _resident((l, 1)),
                  pl.BlockSpec((hid, tn), lambda o, j: (0, (2 * o) * nt + j)),
                  pl.BlockSpec((hid, tn), lambda o, j: (0, (2 * o + 1) * nt + j)),
                  pl.BlockSpec((1, tn), lambda o, j: (0, j))],
        out_specs=pl.BlockSpec((1, 2, l, tn), lambda o, j: (o, 0, 0, j)),
        out_shape=jax.ShapeDtypeStruct((HY_ORDER, 2, l, d), F32),
        compiler_params=_params(("parallel", "parallel")),
        name="hyena_filters",
    )(hidden, t.reshape(l, 1), w3, w3, deltas)


def _filter_spectrum_kernel(hf_ref, hb_ref, f1_ref, f2_ref, o_ref, s_ref, *, p):
    na, nb = p["na"], p["nb"]
    half = na // 2

    def gather(tb):
        fwd = hf_ref[pl.ds(tb, half, stride=nb), :]
        bwd = hb_ref[pl.ds(jnp.where(tb == 0, 0, nb - tb), half, stride=nb), :]
        return jnp.concatenate([fwd, bwd], axis=0).astype(BF16)

    _fft_stage1(gather, f1_ref, s_ref, p)

    def body(i, carry):
        spec = jnp.dot(f2_ref[...], _slab_pair(s_ref, i, p).astype(BF16), preferred_element_type=F32)
        r = pl.multiple_of(i * 4 * nb, SUBLANES)
        o_ref[pl.ds(r, 2 * nb), :] = spec[:, :LANES].astype(BF16)
        o_ref[pl.ds(r + 2 * nb, 2 * nb), :] = spec[:, LANES:].astype(BF16)
        return carry

    lax.fori_loop(0, p["n_pairs"], body, 0, unroll=p["pair_unroll"])


def hyena_filter_spectrum(filt, tables):
    p, _, f1k, f2, _, _ = tables
    order, _, l, d = filt.shape
    rows = 2 * p["n_pairs"] * 2 * p["nb"]
    return pl.pallas_call(
        functools.partial(_filter_spectrum_kernel, p=p),
        grid=(order, d // LANES),
        in_specs=[pl.BlockSpec((None, None, l, LANES), lambda o, j: (o, 0, 0, j)),
                  pl.BlockSpec((None, None, l, LANES), lambda o, j: (o, 1, 0, j)),
                  _resident(f1k.shape), _resident(f2.shape)],
        out_specs=pl.BlockSpec((None, rows, LANES), lambda o, j: (o, 0, j)),
        out_shape=jax.ShapeDtypeStruct((order, rows, d), BF16),
        scratch_shapes=[pltpu.VMEM((p["nsp"] * p["pitch"], LANES), F32)],
        compiler_params=_params(("parallel", "parallel")),
        name="hyena_filter_spectrum",
    )(filt, filt, jnp.asarray(f1k).astype(BF16), jnp.asarray(f2).astype(BF16))


def _longconv_kernel(a_ref, m_ref, k_ref, bias_ref, f1_ref, f2_ref, g2_ref, g1_ref, o_ref, s_ref, *, p, row_chunk):
    na, nb, nsp, pitch = p["na"], p["nb"], p["nsp"], p["pitch"]
    half = na // 2

    _fft_stage1(lambda tb: a_ref[pl.ds(tb, half, stride=nb), :].astype(BF16), f1_ref, s_ref, p)

    def mid(i, carry):
        x = jnp.dot(f2_ref[...], _slab_pair(s_ref, i, p).astype(BF16), preferred_element_type=F32)
        r = pl.multiple_of(i * 4 * nb, SUBLANES)
        kk = jnp.concatenate([k_ref[pl.ds(r, 2 * nb), :], k_ref[pl.ds(r + 2 * nb, 2 * nb), :]], axis=1).astype(F32)
        xr, xi, kr, ki = x[:nb], x[nb:], kk[:nb], kk[nb:]
        y = jnp.concatenate([xr * kr - xi * ki, xr * ki + xi * kr], axis=0).astype(BF16)
        bq = jnp.dot(g2_ref[...], y, preferred_element_type=F32)
        r0 = pl.multiple_of(2 * i * pitch, SUBLANES)
        r1 = pl.multiple_of(2 * i * pitch + pitch, SUBLANES)
        s_ref[pl.ds(r0, 2 * nb), :] = bq[:, :LANES]
        s_ref[pl.ds(r1, 2 * nb), :] = bq[:, LANES:]
        return carry

    lax.fori_loop(0, p["n_pairs"], mid, 0, unroll=p["pair_unroll"])

    def last(tb, carry):
        bq = jnp.concatenate([s_ref[pl.ds(tb, nsp, stride=pitch), :], s_ref[pl.ds(nb + tb, nsp, stride=pitch), :]], axis=0)
        o_ref[pl.ds(tb, half, stride=nb), :] = jnp.dot(g1_ref[tb], bq.astype(BF16), preferred_element_type=F32)
        return carry

    lax.fori_loop(0, nb, last, 0, unroll=p["tb_unroll"])

    def finish(i, carry):
        r = pl.multiple_of(i * row_chunk, SUBLANES)
        a = a_ref[pl.ds(r, row_chunk), :]
        o_ref[pl.ds(r, row_chunk), :] = m_ref[pl.ds(r, row_chunk), :] * (o_ref[pl.ds(r, row_chunk), :] + bias_ref[...] * a)
        return carry

    lax.fori_loop(0, p["l"] // row_chunk, finish, 0)


def hyena_longconv(a, a_col, m, m_col, kspec, bias, tables):
    p, f1, _, f2, g2, g1 = tables
    b, l, _ = a.shape
    d = D_MODEL
    nt = d // LANES
    rows = kspec.shape[0]
    tabs = [jnp.asarray(t).astype(BF16) for t in (f1, f2, g2, g1)]
    return pl.pallas_call(
        functools.partial(_longconv_kernel, p=p, row_chunk=min(512, l)),
        grid=(nt, b),
        in_specs=[pl.BlockSpec((None, l, LANES), lambda j, i: (i, 0, a_col * nt + j)),
                  pl.BlockSpec((None, l, LANES), lambda j, i: (i, 0, m_col * nt + j)),
                  pl.BlockSpec((rows, LANES), lambda j, i: (0, j)),
                  pl.BlockSpec((1, LANES), lambda j, i: (0, j))] + [_resident(t.shape) for t in tabs],
        out_specs=pl.BlockSpec((None, l, LANES), lambda j, i: (i, 0, j)),
        out_shape=jax.ShapeDtypeStruct((b, l, d), F32),
        scratch_shapes=[pltpu.VMEM((p["nsp"] * p["pitch"], LANES), F32)],
        compiler_params=_params(("parallel", "parallel")),
        name="hyena_longconv",
    )(a, m, kspec, bias.astype(F32).reshape(1, d), *tabs)


def hyena_core(pc, fw1, fb1, fw2, fb2, fw3, ffreq, hbias):
    l = pc.shape[1]
    tables = _fft_tables(l)
    filt = hyena_filters(l, fw1, fb1, fw2, fb2, fw3, ffreq)
    kspec = hyena_filter_spectrum(filt, tables)
    z = hyena_longconv(pc, 0, pc, 1, kspec[0], hbias[0], tables)
    return hyena_longconv(z, 0, pc, 2, kspec[1], hbias[1], tables)


def kernel(x, c, ctx, c_ctx, ada_w, ada_b, norm_g, ffn_w_gate, ffn_w_up, ffn_w_down, ssd_w_in, ssd_conv_w, ssd_conv_b, ssd_a_log, ssd_dt_bias, ssd_d, ssd_norm_g, ssd_w_out, pool_w, pool_b, pool_scale, hy_w_in, hy_conv_w, hy_conv_b, hy_filt_w1, hy_filt_b1, hy_filt_w2, hy_filt_b2, hy_filt_w3, hy_filt_freq, hy_bias, hy_w_out, final_g):
    batch = x.shape[0]
    d = D_MODEL
    h = ctx

    s = jnp.concatenate([jax.nn.silu(c), jax.nn.silu(c_ctx)[None], jnp.zeros((7 - batch, d), F32)], axis=0)
    mods = ada_modulation(s, ada_w, ada_b).reshape(DEPTH, 8, N_MOD, d)

    wg_bf, wu_bf, wd_bf = ffn_w_gate.astype(BF16), ffn_w_up.astype(BF16), ffn_w_down.astype(BF16)
    fg = final_g.reshape(1, d)

    for i in range(DEPTH):
        kind = i % N_MIXERS
        j = i // N_MIXERS
        last = i == DEPTH - 1
        ctx_in_needed = (not last) or kind == 0
        m = [mods[i, :batch, k].reshape(batch, 1, 1, d) for k in range(N_MOD)]
        mc = [jnp.broadcast_to(mods[i, batch, k].reshape(1, 1, 1, d), (batch, 1, 1, d)) for k in range(N_MOD)]
        g0, g1, g2 = (norm_g[i, k].reshape(1, d) for k in range(3))

        x = ffn_step(x, m[0], m[1], m[2], g0, wg_bf[i, 0], wu_bf[i, 0], wd_bf[i, 0])
        if ctx_in_needed:
            h = ffn_step(h, mc[0], mc[1], mc[2], g0, wg_bf[i, 0], wu_bf[i, 0], wd_bf[i, 0])

        if kind == 0:
            x, h_new = ssd_mixer(x, h, (m[3], m[4], m[5]), (mc[3], mc[4], mc[5]), g1, ssd_w_in[j], ssd_conv_w[j],
                                 ssd_conv_b[j], ssd_a_log[j], ssd_dt_bias[j], ssd_d[j], ssd_norm_g[j], ssd_w_out[j], not last)
            if not last:
                h = h_new
        elif kind == 1:
            x = pool_mixer(x, m[3], m[4], m[5], g1, pool_w[j], pool_b[j].reshape(-1), pool_scale[j], GRID_W)
            if not last:
                h = pool_mixer(h, mc[3], mc[4], mc[5], g1, pool_w[j], pool_b[j].reshape(-1), pool_scale[j], h.shape[1])
        else:
            w_in = hy_w_in[j].astype(BF16)
            w_out = hy_w_out[j].astype(BF16)
            filt = (hy_filt_w1[j], hy_filt_b1[j], hy_filt_w2[j], hy_filt_b2[j], hy_filt_w3[j], hy_filt_freq[j])
            conv = {0: (hy_conv_w[j], hy_conv_b[j], False)}
            (p_lat,) = modulate_project(x, m[3], m[4], g1, [w_in], conv=conv)
            x = project_residual(x, hyena_core(p_lat, *filt, hy_bias[j]), m[5], w_out)
            if not last:
                (p_ctx,) = modulate_project(h, mc[3], mc[4], g1, [w_in], conv=conv)
                h = project_residual(h, hyena_core(p_ctx, *filt, hy_bias[j]), mc[5], w_out)

        x = ffn_step(x, m[6], m[7], m[8], g2, wg_bf[i, 1], wu_bf[i, 1], wd_bf[i, 1],
                     final_g=fg if last else None)
        if not last:
            h = ffn_step(h, mc[6], mc[7], mc[8], g2, wg_bf[i, 1], wu_bf[i, 1], wd_bf[i, 1])
    return x
```

```python
import functools
import math

import jax
import jax.numpy as jnp
from jax import lax
from jax.experimental import pallas as pl
from jax.experimental.pallas import tpu as pltpu

F32 = jnp.float32
BF16 = jnp.bfloat16
HIGHEST = lax.Precision.HIGHEST

D_MODEL = 1024
DEPTH = 4
GRID_W = 64
N_MIXERS = 3
D_FF = 2816
N_MOD = 9
EPS = 1e-6
LOG2_E = 1.4426950408889634

SSD_D_INNER = 2 * D_MODEL
SSD_HEAD_DIM = 64
SSD_N_HEADS = SSD_D_INNER // SSD_HEAD_DIM
SSD_N_GROUPS = 4
SSD_HEADS_PER_GROUP = SSD_N_HEADS // SSD_N_GROUPS
SSD_D_STATE = 128
SSD_CHUNK = 128
SSD_GN = SSD_N_GROUPS * SSD_D_STATE
SSD_CONV_DIM = SSD_D_INNER + 2 * SSD_GN
SSD_GROUP_WIDTH = SSD_D_INNER // SSD_N_GROUPS

POOL_WINDOWS = (2, 4, 8, 16)
POOL_GROUPS = 4
POOL_GROUP_DIM = D_MODEL // POOL_GROUPS

HY_ORDER = 2
HY_EMB_DIM = 33
HY_BANDS = (HY_EMB_DIM - 1) // 2
HY_MAX_DECAY = math.log(1e-2) / 0.3
HY_MIN_DECAY = math.log(1e-2) / 1.5

VMEM_LIMIT_BYTES = 56 * 1024 * 1024
SUBLANES = 8
LANES = 128


def _params(sem):
    return pltpu.CompilerParams(dimension_semantics=sem, vmem_limit_bytes=VMEM_LIMIT_BYTES)


def _resident(shape):
    nd = len(shape)
    return pl.BlockSpec(shape, lambda *_: (0,) * nd, pipeline_mode=pl.Buffered(1))


def _mod_spec(d, _unused=0):
    return pl.BlockSpec((1, 1, 1, d), lambda i, j: (i, 0, 0, 0))


def _modulated(x, g, shift, scale):
    ms = jnp.mean(x * x, axis=-1, keepdims=True)
    return (x * lax.rsqrt(ms + EPS)) * g * (1.0 + scale) + shift


def _silu(v):
    return v * jax.nn.sigmoid(v)


def _ada_kernel(s_ref, w_ref, b_ref, o_ref):
    o_ref[0] = jnp.dot(s_ref[...], w_ref[0], preferred_element_type=F32, precision=HIGHEST) + b_ref[0]


def ada_modulation(s, ada_w, ada_b, tn=1024):
    depth, d, n = ada_w.shape
    r = s.shape[0]
    return pl.pallas_call(
        _ada_kernel,
        grid=(depth, n // tn),
        in_specs=[pl.BlockSpec((r, d), lambda i, j: (0, 0)),
                  pl.BlockSpec((1, d, tn), lambda i, j: (i, 0, j)),
                  pl.BlockSpec((1, 1, tn), lambda i, j: (i, 0, j))],
        out_specs=pl.BlockSpec((1, r, tn), lambda i, j: (i, 0, j)),
        out_shape=jax.ShapeDtypeStruct((depth, r, n), F32),
        compiler_params=_params(("parallel", "parallel")),
        name="ada_modulation",
    )(s, ada_w, ada_b.reshape(depth, 1, n))


def _ffn_kernel(x_ref, sh_ref, sc_ref, gt_ref, g_ref, wg_ref, wu_ref, wd_ref, *rest, f_chunk, final):
    if final:
        fg_ref, o_ref = rest
    else:
        (o_ref,) = rest
    x = x_ref[0]
    u = _modulated(x, g_ref[...], sh_ref[0, 0], sc_ref[0, 0]).astype(BF16)
    acc = jnp.zeros(x.shape, F32)
    d_ff = wg_ref.shape[1]
    for f0 in range(0, d_ff, f_chunk):
        a = jnp.dot(u, wg_ref[:, f0:f0 + f_chunk], preferred_element_type=F32)
        b = jnp.dot(u, wu_ref[:, f0:f0 + f_chunk], preferred_element_type=F32)
        h = (_silu(a) * b).astype(BF16)
        acc = acc + jnp.dot(h, wd_ref[f0:f0 + f_chunk, :], preferred_element_type=F32)
    y = x + (0.5 * gt_ref[0, 0]) * acc
    if final:
        ms = jnp.mean(y * y, axis=-1, keepdims=True)
        y = y * lax.rsqrt(ms + EPS) * fg_ref[...]
    o_ref[0] = y


def ffn_step(x, shift, scale, gate, g, wg, wu, wd, final_g=None, tm=512, f_chunk=256):
    b, l, d = x.shape
    tm = min(tm, l)
    f = wg.shape[1]
    final = final_g is not None
    mod_spec = _mod_spec(d, 0)
    in_specs = [pl.BlockSpec((1, tm, d), lambda i, j: (i, j, 0)),
                mod_spec, mod_spec, mod_spec,
                _resident((1, d)), _resident((d, f)), _resident((d, f)), _resident((f, d))]
    args = [x, shift, scale, gate, g, wg, wu, wd]
    if final:
        in_specs.append(_resident((1, d)))
        args.append(final_g)
    return pl.pallas_call(
        functools.partial(_ffn_kernel, f_chunk=f_chunk, final=final),
        grid=(b, l // tm),
        in_specs=in_specs,
        out_specs=pl.BlockSpec((1, tm, d), lambda i, j: (i, j, 0)),
        out_shape=jax.ShapeDtypeStruct((b, l, d), F32),
        compiler_params=_params(("parallel", "parallel")),
        name="ffn_step",
    )(*args)


CONV_COLS = 512


def _modproj_kernel(*refs, n_w, n_dest, conv, n_tiles):
    if conv:
        prev_ref, x_ref, next_ref, sh_ref, sc_ref, g_ref, *rest = refs
    else:
        x_ref, sh_ref, sc_ref, g_ref, *rest = refs
    w_refs = rest[:n_w]
    cw_refs = rest[n_w:n_w + 2 * len(conv)]
    o_refs = rest[n_w + 2 * len(conv) + n_dest:]
    g, sh, sc = g_ref[...], sh_ref[0, 0], sc_ref[0, 0]
    u = _modulated(x_ref[0], g, sh, sc)
    if n_w == 0:
        o_refs[0][0] = u
        return
    u = u.astype(BF16)
    tm = x_ref.shape[1]
    if conv:
        ue = _modulated(jnp.concatenate([prev_ref[0], x_ref[0], next_ref[0]], axis=0), g, sh, sc).astype(BF16)
        j = pl.program_id(1)
        top = (j > 0).astype(F32)
        bot = (j < n_tiles - 1).astype(F32)
    for idx, (w_ref, o_ref) in enumerate(zip(w_refs, o_refs)):
        if idx not in conv:
            o_ref[0] = jnp.dot(u, w_ref[...], preferred_element_type=F32).astype(o_ref.dtype)
            continue
        k_w, silu = conv[idx]
        cw_ref, cb_ref = cw_refs[2 * list(conv).index(idx)], cw_refs[2 * list(conv).index(idx) + 1]
        p = k_w // 2
        rows = tm + 2 * SUBLANES
        for c0 in range(0, w_ref.shape[1], CONV_COLS):
            ce = jnp.dot(ue, w_ref[:, c0:c0 + CONV_COLS], preferred_element_type=F32)
            ce = jnp.concatenate([ce[:SUBLANES] * top, ce[SUBLANES:SUBLANES + tm], ce[SUBLANES + tm:] * bot], axis=0)
            acc = jnp.broadcast_to(cb_ref[:, c0:c0 + CONV_COLS], (tm, CONV_COLS))
            for k in range(k_w):
                tap = ce if k == p else pltpu.roll(ce, (p - k) % rows, axis=0)
                acc = acc + cw_ref[k:k + 1, c0:c0 + CONV_COLS] * tap[SUBLANES:SUBLANES + tm, :]
            o_ref[0, :, c0:c0 + CONV_COLS] = (_silu(acc) if silu else acc).astype(o_ref.dtype)


def modulate_project(x, shift, scale, g, weights, tm=256, out_dtypes=None, out_rows=None, row_block=0, dest=None,
                     conv=None):
    b, l, d = x.shape
    tm = min(tm, l)
    conv = dict(conv or {})
    mod_spec = _mod_spec(d)
    x_spec = pl.BlockSpec((1, tm, d), lambda i, j: (i, j, 0))
    args = [x]
    in_specs = [x_spec]
    if conv:
        rb, last = tm // SUBLANES, l // SUBLANES - 1
        in_specs = [pl.BlockSpec((1, SUBLANES, d), lambda i, j: (i, jnp.maximum(j * rb - 1, 0), 0)), x_spec,
                    pl.BlockSpec((1, SUBLANES, d), lambda i, j: (i, jnp.minimum((j + 1) * rb, last), 0))]
        args = [x, x, x]
    in_specs += [mod_spec, mod_spec, _resident((1, d))] + [_resident(w.shape) for w in weights]
    args += [shift, scale, g, *weights]
    for i in conv:
        cw, cb, _ = conv[i]
        assert weights[i].shape[1] % CONV_COLS == 0
        in_specs += [_resident(cw.shape), _resident((1, cb.shape[0]))]
        args += [cw.astype(F32), cb.astype(F32).reshape(1, -1)]
    widths = [w.shape[1] for w in weights] or [d]
    out_dtypes = out_dtypes or [F32] * len(widths)
    dest = list(dest or [])
    in_specs += [pl.BlockSpec(memory_space=pl.ANY)] * len(dest)
    n_in = len(args)
    return pl.pallas_call(
        functools.partial(_modproj_kernel, n_w=len(weights), n_dest=len(dest),
                          conv={i: (c[0].shape[0], c[2]) for i, c in conv.items()}, n_tiles=l // tm),
        grid=(b, l // tm),
        in_specs=in_specs,
        out_specs=[pl.BlockSpec((1, tm, n), lambda i, j: (i, j + row_block, 0)) for n in widths],
        out_shape=[jax.ShapeDtypeStruct((b, out_rows or l, n), dt) for n, dt in zip(widths, out_dtypes)],
        input_output_aliases={n_in + k: k for k in range(len(dest))},
        compiler_params=_params(("parallel", "parallel")),
        name="modulate_project",
    )(*args, *dest)


def _outproj_kernel(x_ref, a_ref, gt_ref, w_ref, o_ref):
    y = jnp.dot(a_ref[0].astype(BF16), w_ref[...], preferred_element_type=F32)
    o_ref[0] = x_ref[0] + gt_ref[0, 0] * y


def project_residual(x, a, gate, w, tm=512):
    b, l, d = x.shape
    k = a.shape[-1]
    tm = min(tm, l)
    return pl.pallas_call(
        _outproj_kernel,
        grid=(b, l // tm),
        in_specs=[pl.BlockSpec((1, tm, d), lambda i, j: (i, j, 0)),
                  pl.BlockSpec((1, tm, k), lambda i, j: (i, j, 0)),
                  _mod_spec(d, 0),
                  _resident((k, d))],
        out_specs=pl.BlockSpec((1, tm, d), lambda i, j: (i, j, 0)),
        out_shape=jax.ShapeDtypeStruct((b, l, d), F32),
        compiler_params=_params(("parallel", "parallel")),
        name="project_residual",
    )(x, a, gate, w)


def _bf16_terms(v):
    hi = v.astype(BF16)
    r = v - hi.astype(F32)
    mid = r.astype(BF16)
    return hi, mid, (r - mid.astype(F32)).astype(BF16)


def _dot_f32_rhs(m, v):
    return sum(jnp.dot(m, t, preferred_element_type=F32) for t in _bf16_terms(v))


def _dot_f32_lhs(v, m):
    return sum(jnp.dot(t, m, preferred_element_type=F32) for t in _bf16_terms(v))


def _dot_split(a, b):
    a_hi, a_lo, _ = _bf16_terms(a)
    b_hi, b_lo, _ = _bf16_terms(b)
    return (jnp.dot(a_hi, b_hi, preferred_element_type=F32) + jnp.dot(a_lo, b_hi, preferred_element_type=F32)
            + jnp.dot(a_hi, b_lo, preferred_element_type=F32))


def _ssd_scan_kernel(xf_ref, dtf_ref, dttf_ref, xb_ref, dtb_in_ref, dttb_ref, dtb_ref, dtbt_ref, a_ref, at_ref, ex_ref,
                     of_ref, ob_ref, h_ref):
    @pl.when(pl.program_id(1) == 0)
    def _():
        h_ref[...] = jnp.zeros(h_ref.shape, F32)

    consts = (dtb_ref, dtbt_ref, a_ref, at_ref, ex_ref)
    _ssd_chunk(xf_ref, dtf_ref, dttf_ref, *consts, of_ref, h_ref.at[0], True)
    _ssd_chunk(xb_ref, dtb_in_ref, dttb_ref, *consts, ob_ref, h_ref.at[1], False)


def _ssd_chunk(xbc_ref, dt_ref, dtt_ref, dtb_ref, dtbt_ref, a_ref, at_ref, ex_ref, o_ref, h_ref, fwd):
    nh, q, hp = SSD_N_HEADS, SSD_CHUNK, SSD_HEAD_DIM
    heads = slice(0, nh) if fwd else slice(nh, 2 * nh)
    dt = jax.nn.softplus(dt_ref[0][:, heads] + dtb_ref[:, heads])
    a = dt * a_ref[:, heads]
    dt_t = jax.nn.softplus(dtt_ref[0][heads] + dtbt_ref[heads])
    a_t = dt_t * at_ref[heads]

    r_i = lax.broadcasted_iota(jnp.int32, (q, q), 0)
    c_i = lax.broadcasted_iota(jnp.int32, (q, q), 1)
    seen = (c_i <= r_i) if fwd else (c_i >= r_i)
    seen_t = (r_i <= c_i) if fwd else (r_i >= c_i)
    cum = _dot_f32_rhs(seen.astype(BF16), a)
    cum_t = _dot_f32_lhs(a_t, seen_t.astype(BF16))
    total = cum[q - 1:q, :] if fwd else cum[0:1, :]
    ecum = jnp.exp(cum).astype(BF16)
    dtdec = (dt * jnp.exp(total - cum)).astype(BF16)
    etot = jnp.exp(total)
    cum2 = cum * LOG2_E
    row2 = cum_t * LOG2_E - jnp.log2(dt_t)

    pw = 2 * hp
    left = lax.broadcasted_iota(jnp.int32, (q, pw), 1) < hp
    left_row = lax.broadcasted_iota(jnp.int32, (1, pw), 1) < hp
    pairs = SSD_HEADS_PER_GROUP // 2
    for g in range(SSD_N_GROUPS):
        b_g = xbc_ref[0, :, SSD_D_INNER + g * SSD_D_STATE:SSD_D_INNER + (g + 1) * SSD_D_STATE].astype(BF16)
        c_lo = SSD_D_INNER + SSD_GN + g * SSD_D_STATE
        c_g = xbc_ref[0, :, c_lo:c_lo + SSD_D_STATE].astype(BF16)
        cb = lax.dot_general(c_g, b_g, (((1,), (1,)), ((), ())), preferred_element_type=F32)
        h_g = h_ref[g]
        ex_g = ex_ref[:, g * SSD_GROUP_WIDTH:(g + 1) * SSD_GROUP_WIDTH]
        ch = (jnp.dot(c_g, h_g.astype(BF16), preferred_element_type=F32)
              * jnp.dot(ecum, ex_g, preferred_element_type=F32))
        dtdec_g = jnp.dot(dtdec, ex_g, preferred_element_type=F32)
        xdecs, etots = [], []
        for pr in range(pairs):
            e1 = g * SSD_HEADS_PER_GROUP + 2 * pr
            e2 = e1 + 1
            x_pair = xbc_ref[0, :, e1 * hp:e1 * hp + pw].astype(F32)
            ws = []
            for e in (e1, e2):
                seg2 = cum2[:, e:e + 1] - row2[e:e + 1, :]
                ws.append((cb * jnp.exp2(jnp.where(seen, seg2, -jnp.inf))).astype(BF16))
            rhs = jnp.concatenate([jnp.where(left, x_pair, 0.0), jnp.where(left, 0.0, x_pair)], axis=0).astype(BF16)
            y_diag = jnp.dot(jnp.concatenate(ws, axis=1), rhs, preferred_element_type=F32)
            y_off = ch[:, pr * pw:(pr + 1) * pw]
            o_ref[0, 0, :, e1 * hp:e1 * hp + pw] = (y_diag + y_off).astype(o_ref.dtype)
            xdecs.append((x_pair * dtdec_g[:, pr * pw:(pr + 1) * pw]).astype(BF16))
            etots.append(jnp.where(left_row, etot[:, e1:e1 + 1], etot[:, e2:e2 + 1]))
        s_g = lax.dot_general(b_g, jnp.concatenate(xdecs, axis=1), (((0,), (0,)), ((), ())),
                              preferred_element_type=F32)
        h_ref[g] = h_g * jnp.concatenate(etots, axis=1) + s_g


def ssd_scan(xbc, dt_raw, dt_bias, a_log, n_lead_chunks):
    b, t, _ = xbc.shape
    q, nh = SSD_CHUNK, SSD_N_HEADS
    nc = t // q
    dt_t = jnp.swapaxes(dt_raw, 1, 2)
    a_neg = -jnp.exp(a_log.astype(F32)).reshape(1, 2 * nh)
    dtb = dt_bias.astype(F32).reshape(1, 2 * nh)
    expand = jnp.repeat(jnp.eye(nh, dtype=BF16), SSD_HEAD_DIM, axis=1)

    n_main = nc - n_lead_chunks

    def chunk_of(d, c):
        if d:
            return jnp.where(c < n_lead_chunks, nc - 1 - c, n_main - 1 - (c - n_lead_chunks))
        return jnp.where(c < n_lead_chunks, n_main + c, c - n_lead_chunks)

    def token_specs(d):
        return [pl.BlockSpec((1, q, SSD_CONV_DIM), lambda i, c: (i, chunk_of(d, c), 0)),
                pl.BlockSpec((1, q, 2 * nh), lambda i, c: (i, chunk_of(d, c), 0)),
                pl.BlockSpec((1, 2 * nh, q), lambda i, c: (i, 0, chunk_of(d, c)))]

    y_f, y_b = pl.pallas_call(
        _ssd_scan_kernel,
        grid=(b, nc),
        in_specs=token_specs(0) + token_specs(1) + [
            _resident((1, 2 * nh)), _resident((2 * nh, 1)),
            _resident((1, 2 * nh)), _resident((2 * nh, 1)), _resident((nh, SSD_D_INNER))],
        out_specs=[pl.BlockSpec((1, 1, q, SSD_D_INNER), lambda i, c, d=d: (i, 0, chunk_of(d, c), 0)) for d in (0, 1)],
        out_shape=[jax.ShapeDtypeStruct((b, 1, t, SSD_D_INNER), BF16)] * 2,
        scratch_shapes=[pltpu.VMEM((2, SSD_N_GROUPS, SSD_D_STATE, SSD_GROUP_WIDTH), F32)],
        compiler_params=_params(("parallel", "arbitrary")),
        name="ssd_scan",
    )(xbc, dt_raw, dt_t, xbc, dt_raw, dt_t, dtb, dtb.reshape(2 * nh, 1), a_neg, a_neg.reshape(2 * nh, 1), expand)
    return y_f, y_b


def _ssd_out_kernel(x_ref, y0_ref, y1_ref, xs0_ref, xs1_ref, z_ref, dsk_ref, ng_ref, gt_ref, w_ref, o_ref):
    xs = jnp.concatenate([xs0_ref[0], xs1_ref[0]], axis=1).astype(F32)
    y = y0_ref[0, 0].astype(F32) + y1_ref[0, 0].astype(F32) + dsk_ref[...] * xs
    gy = y * _silu(z_ref[0].astype(F32))
    parts = []
    for g in range(SSD_N_GROUPS):
        blk = gy[:, g * SSD_GROUP_WIDTH:(g + 1) * SSD_GROUP_WIDTH]
        parts.append(blk * lax.rsqrt(jnp.mean(blk * blk, axis=-1, keepdims=True) + EPS))
    a = (jnp.concatenate(parts, axis=1) * ng_ref[...]).astype(BF16)
    o_ref[0] = x_ref[0] + gt_ref[0, 0] * jnp.dot(a, w_ref[...], preferred_element_type=F32)


def ssd_gate_project(x, y2, xbc, z, d_skip, norm_g, gate, w_out, src_block, tm=256):
    b, t, d = x.shape
    half = SSD_D_INNER // 2
    dsk = jnp.repeat(d_skip.astype(F32), SSD_HEAD_DIM).reshape(1, SSD_D_INNER)
    return pl.pallas_call(
        _ssd_out_kernel,
        grid=(b, t // tm),
        in_specs=[pl.BlockSpec((1, tm, d), lambda i, j: (i, j, 0)),
                  pl.BlockSpec((1, 1, tm, SSD_D_INNER), lambda i, j: (i, 0, j + src_block, 0)),
                  pl.BlockSpec((1, 1, tm, SSD_D_INNER), lambda i, j: (i, 0, j + src_block, 0)),
                  pl.BlockSpec((1, tm, half), lambda i, j: (i, j + src_block, 0)),
                  pl.BlockSpec((1, tm, half), lambda i, j: (i, j + src_block, 1)),
                  pl.BlockSpec((1, tm, SSD_D_INNER), lambda i, j: (i, j + src_block, 0)),
                  _resident((1, SSD_D_INNER)), _resident((1, SSD_D_INNER)),
                  _mod_spec(d),
                  _resident((SSD_D_INNER, d))],
        out_specs=pl.BlockSpec((1, tm, d), lambda i, j: (i, j, 0)),
        out_shape=jax.ShapeDtypeStruct((b, t, d), F32),
        compiler_params=_params(("parallel", "parallel")),
        name="ssd_gate_project",
    )(x, y2[0], y2[1], xbc, xbc, z, dsk, norm_g.astype(F32).reshape(1, SSD_D_INNER), gate, w_out)


def ssd_mixer(x, h, m_lat, m_ctx, g1, w_in, conv_w, conv_b, a_log, dt_bias, d_skip, norm_g, w_out, need_ctx_out,
              tm_proj=512):
    lc, l = h.shape[1], x.shape[1]
    tm_ctx = min(tm_proj, lc)
    assert l % tm_proj == 0 and l % tm_ctx == 0
    w_bf = w_in.astype(BF16)
    ws = [w_bf[:, :SSD_D_INNER], w_bf[:, SSD_D_INNER:SSD_D_INNER + SSD_CONV_DIM], w_bf[:, SSD_D_INNER + SSD_CONV_DIM:]]
    dts = [BF16, BF16, F32]
    conv = {1: (conv_w, conv_b, True)}
    parts = modulate_project(x, m_lat[0], m_lat[1], g1, ws, tm=tm_proj, out_dtypes=dts, out_rows=lc + l, conv=conv)
    z, xbc, dt_raw = modulate_project(h, m_ctx[0], m_ctx[1], g1, ws, tm=tm_ctx, out_dtypes=dts, out_rows=lc + l,
                                      row_block=l // tm_ctx, dest=parts, conv=conv)
    y2 = ssd_scan(xbc, dt_raw, dt_bias, a_log, lc // SSD_CHUNK)
    w_out = w_out.astype(BF16)
    x_new = ssd_gate_project(x, y2, xbc, z, d_skip, norm_g, m_lat[2], w_out, 0, tm=tm_proj)
    h_new = (ssd_gate_project(h, y2, xbc, z, d_skip, norm_g, m_ctx[2], w_out, l // tm_ctx, tm=tm_ctx)
             if need_ctx_out else None)
    return x_new, h_new


POOL_CHUNK = 256
POOL_MAX_HALF = max(POOL_WINDOWS) // 2


def _pool_band_tables(width):
    import numpy as np
    pos = np.arange(POOL_CHUNK)
    line, col = pos // width, pos % width
    out = []
    for w in POOL_WINDOWS:
        inside = (col[None, :] >= col[:, None] - w // 2) & (col[None, :] <= col[:, None] + (w - w // 2) - 1)
        out.append((inside & (line[None, :] == line[:, None])).astype(np.float32))
    return np.stack(out)


def _pool_kernel(u_ref, band_ref, o_ref, cs_ref, *, width, n_lines):
    l, c = u_ref.shape
    n_chunks = l // POOL_CHUNK
    assert POOL_WINDOWS == tuple(2 << g for g in range(POOL_GROUPS)) and width & (width - 1) == 0
    half = lax.shift_left(jnp.int32(1), pl.program_id(1) // (POOL_GROUP_DIM // LANES))
    log_w = width.bit_length() - 1
    pad = POOL_MAX_HALF * width if n_lines > 1 else 0
    if pad:
        cs_ref[0:pad, :] = jnp.zeros((pad, c), F32)
        cs_ref[pad + l:2 * pad + l, :] = jnp.zeros((pad, c), F32)

    def col_pass(i, carry):
        r = pl.multiple_of(i * POOL_CHUNK, POOL_CHUNK)
        u = u_ref[pl.ds(r, POOL_CHUNK), :]
        hi = u.astype(BF16)
        lo = (u - hi.astype(F32)).astype(BF16)
        both = jnp.dot(band_ref[0], jnp.concatenate([hi, lo], axis=1), preferred_element_type=F32)
        cs_ref[pl.ds(pad + r, POOL_CHUNK), :] = both[:, :c] + both[:, c:]
        return carry

    lax.fori_loop(0, n_chunks, col_pass, 0)

    def out_pass(i, carry):
        r = pl.multiple_of(i * POOL_CHUNK, POOL_CHUNK)
        pos = r + lax.broadcasted_iota(jnp.int32, (POOL_CHUNK, c), 0)
        col = pos & (width - 1)
        cnt = jnp.minimum(col + half, width) - jnp.maximum(col - half, 0)
        if n_lines > 1:
            def add_line(k, acc):
                return acc + cs_ref[pl.ds(pl.multiple_of(pad + r + k * width, SUBLANES), POOL_CHUNK), :]
            s = lax.fori_loop(-half, half, add_line, jnp.zeros((POOL_CHUNK, c), F32))
            line = lax.shift_right_logical(pos, log_w)
            cnt_l = jnp.minimum(line + half, n_lines) - jnp.maximum(line - half, 0)
            mean = s / (cnt_l.astype(F32) * cnt.astype(F32))
        else:
            mean = cs_ref[pl.ds(r, POOL_CHUNK), :] / cnt.astype(F32)
        o_ref[pl.ds(r, POOL_CHUNK), :] = (mean - u_ref[pl.ds(r, POOL_CHUNK), :]).astype(o_ref.dtype)
        return carry

    lax.fori_loop(0, n_chunks, out_pass, 0)


def pool_tokens(u, width):
    b, l, d = u.shape
    n_lines = l // width
    bands = jnp.asarray(_pool_band_tables(width)).astype(BF16)
    tiles_per_group = POOL_GROUP_DIM // LANES
    pad = POOL_MAX_HALF * width if n_lines > 1 else 0
    return pl.pallas_call(
        functools.partial(_pool_kernel, width=width, n_lines=n_lines),
        grid=(b, d // LANES),
        in_specs=[pl.BlockSpec((None, l, LANES), lambda i, j: (i, 0, j)),
                  pl.BlockSpec((1, POOL_CHUNK, POOL_CHUNK), lambda i, j: (j // tiles_per_group, 0, 0))],
        out_specs=pl.BlockSpec((None, l, LANES), lambda i, j: (i, 0, j)),
        out_shape=jax.ShapeDtypeStruct((b, l, d), BF16),
        scratch_shapes=[pltpu.VMEM((l + 2 * pad, LANES), F32)],
        compiler_params=_params(("parallel", "parallel")),
        name="pool_tokens",
    )(u, bands)


def _pool_mix_kernel(x_ref, p_ref, w_ref, b_ref, sc_ref, gt_ref, o_ref):
    ys = [jnp.dot(p_ref[0, :, g * POOL_GROUP_DIM:(g + 1) * POOL_GROUP_DIM], w_ref[g], preferred_element_type=F32)
          for g in range(POOL_GROUPS)]
    y = (jnp.concatenate(ys, axis=1) + b_ref[...]) * sc_ref[...]
    o_ref[0] = x_ref[0] + gt_ref[0, 0] * y


def pool_mix_residual(x, pooled, w, bias, scale, gate, tm=512):
    b, l, d = x.shape
    tm = min(tm, l)
    return pl.pallas_call(
        _pool_mix_kernel,
        grid=(b, l // tm),
        in_specs=[pl.BlockSpec((1, tm, d), lambda i, j: (i, j, 0)),
                  pl.BlockSpec((1, tm, d), lambda i, j: (i, j, 0)),
                  _resident(w.shape), _resident((1, d)), _resident((1, d)), _mod_spec(d, 0)],
        out_specs=pl.BlockSpec((1, tm, d), lambda i, j: (i, j, 0)),
        out_shape=jax.ShapeDtypeStruct((b, l, d), F32),
        compiler_params=_params(("parallel", "parallel")),
        name="pool_mix_residual",
    )(x, pooled, w.astype(BF16), bias.astype(F32).reshape(1, d), scale.astype(F32).reshape(1, d), gate)


def pool_mixer(x, shift, scale_mod, gate, g1, w, bias, scale, width):
    (u,) = modulate_project(x, shift, scale_mod, g1, [])
    return pool_mix_residual(x, pool_tokens(u, width), w, bias, scale, gate)


SLAB_PAD = 8


def _round_up(v, m):
    return (v + m - 1) // m * m


def _fft_plan(l):
    n = 2 * l
    na = 1 << (n.bit_length() // 2)
    nb = n // na
    ns = na // 2 + 1
    n_pairs = (ns + 1) // 2
    return dict(l=l, n=n, na=na, nb=nb, ns=ns, nsp=_round_up(ns + 1, SUBLANES), pitch=2 * nb + SLAB_PAD,
                n_pairs=n_pairs, tb_unroll=min(nb, 16),
                pair_unroll=max(u for u in range(1, 12) if n_pairs % u == 0))


def _fft_tables(l):
    import numpy as np
    p = _fft_plan(l)
    n, na, nb, ns, nsp = p["n"], p["na"], p["nb"], p["ns"], p["nsp"]
    half = na // 2
    ka = np.arange(ns)[None, :, None]
    tb = np.arange(nb)[:, None, None]

    def stage1(ta):
        th = 2.0 * np.pi * (ta[None, None, :] * ka / na + tb * ka / n)
        m = np.zeros((nb, 2 * nsp, ta.shape[0]))
        m[:, :ns] = np.cos(th)
        m[:, nsp:nsp + ns] = -np.sin(th)
        return m

    f1 = stage1(np.arange(half))
    f1k = np.zeros((nb, 2 * nsp, na))
    f1k[:, :, :half] = f1
    f1k[1:, :, half:] = stage1(na - 1 - np.arange(half))[1:]
    tb0 = stage1(na - np.arange(half))[0]
    tb0[:, 0] = 0.0
    f1k[0, :, half:] = tb0

    k2 = np.arange(nb)
    ang = 2.0 * np.pi * np.outer(k2, k2) / nb
    c, s = np.cos(ang), np.sin(ang)
    f2 = np.block([[c, s], [-s, c]])
    g2 = np.block([[c, -s], [s, c]])

    ta = np.arange(half)[None, :, None]
    kk = np.arange(ns)[None, None, :]
    tbb = np.arange(nb)[:, None, None]
    ph = 2.0 * np.pi * (ta * kk / na + tbb * kk / n)
    wgt = np.where((kk == 0) | (kk == na // 2), 1.0, 2.0) / n
    g1 = np.zeros((nb, half, 2 * nsp))
    g1[:, :, :ns] = wgt * np.cos(ph)
    g1[:, :, nsp:nsp + ns] = -wgt * np.sin(ph)
    f32 = np.float32
    return p, f1.astype(f32), f1k.astype(f32), f2.astype(f32), g2.astype(f32), g1.astype(f32)


def _fft_stage1(gather, f1_ref, s_ref, p):
    nb, nsp, pitch = p["nb"], p["nsp"], p["pitch"]

    def body(tb, carry):
        a = jnp.dot(f1_ref[tb], gather(tb), preferred_element_type=F32)
        s_ref[pl.ds(tb, nsp, stride=pitch), :] = a[:nsp]
        s_ref[pl.ds(nb + tb, nsp, stride=pitch), :] = a[nsp:]
        return carry

    lax.fori_loop(0, nb, body, 0, unroll=p["tb_unroll"])


def _slab_pair(s_ref, i, p):
    nb, pitch = p["nb"], p["pitch"]
    r0 = pl.multiple_of(2 * i * pitch, SUBLANES)
    r1 = pl.multiple_of(2 * i * pitch + pitch, SUBLANES)
    return jnp.concatenate([s_ref[pl.ds(r0, 2 * nb), :], s_ref[pl.ds(r1, 2 * nb), :]], axis=1)


FILTER_ROWS = 512


def _filter_hidden_kernel(z_ref, w1_ref, b1_ref, w2_ref, b2_ref, fr_ref, o_ref):
    h = jnp.sin(fr_ref[0:1, :] * (jnp.dot(z_ref[...], w1_ref[...], preferred_element_type=F32, precision=HIGHEST)
                                  + b1_ref[...]))
    o_ref[...] = jnp.sin(fr_ref[1:2, :] * (jnp.dot(h, w2_ref[...], preferred_element_type=F32, precision=HIGHEST)
                                           + b2_ref[...]))


def _filter_out_kernel(h_ref, t_ref, w3f_ref, w3b_ref, dl_ref, o_ref, *, rows):
    l, tn = o_ref.shape[2], o_ref.shape[3]
    wf, wb, dl = w3f_ref[...], w3b_ref[...], dl_ref[...]

    def fill(i, ss):
        r = pl.multiple_of(i * rows, rows)
        h = h_ref[pl.ds(r, rows), :]
        decay = jnp.exp(-t_ref[pl.ds(r, rows), :] * dl)
        hf = _dot_split(h, wf) * decay
        hb = _dot_split(h, wb) * decay
        o_ref[0, 0, pl.ds(r, rows), :] = hf
        o_ref[0, 1, pl.ds(r, rows), :] = hb
        return ss + jnp.sum(hf * hf + hb * hb, axis=0, keepdims=True)

    ss = lax.fori_loop(0, l // rows, fill, jnp.zeros((1, tn), F32))
    scale = lax.rsqrt(ss + EPS)

    def rescale(i, carry):
        r = pl.multiple_of(i * rows, rows)
        o_ref[0, 0, pl.ds(r, rows), :] = o_ref[0, 0, pl.ds(r, rows), :] * scale
        o_ref[0, 1, pl.ds(r, rows), :] = o_ref[0, 1, pl.ds(r, rows), :] * scale
        return carry

    lax.fori_loop(0, l // rows, rescale, 0)


def hyena_filters(l, fw1, fb1, fw2, fb2, fw3, ffreq, tn=LANES):
    d = D_MODEL
    pos = jnp.arange(l, dtype=F32)
    t = jnp.linspace(0.0, 1.0, l, dtype=F32)
    wpos = 2.0 * math.pi * pos / l
    f = jnp.linspace(1e-4, HY_BANDS - 1, HY_BANDS, dtype=F32)
    ang = wpos[:, None] * f[None, :]
    emb_pad = _round_up(HY_EMB_DIM, SUBLANES)
    z = jnp.concatenate([t[:, None], jnp.cos(ang), -jnp.sin(ang), jnp.zeros((l, emb_pad - HY_EMB_DIM), F32)], axis=-1)
    w1 = jnp.concatenate([fw1.astype(F32), jnp.zeros((emb_pad - HY_EMB_DIM, fw1.shape[1]), F32)], axis=0)
    deltas = jnp.abs(jnp.linspace(HY_MIN_DECAY, HY_MAX_DECAY, d, dtype=F32)).reshape(1, d)
    hid = fw2.shape[0]
    rows = min(FILTER_ROWS, l)
    hidden = pl.pallas_call(
        _filter_hidden_kernel,
        grid=(l // rows,),
        in_specs=[pl.BlockSpec((rows, emb_pad), lambda r: (r, 0)), _resident((emb_pad, hid)), _resident((1, hid)),
                  _resident((hid, hid)), _resident((1, hid)), _resident((2, hid))],
        out_specs=pl.BlockSpec((rows, hid), lambda r: (r, 0)),
        out_shape=jax.ShapeDtypeStruct((l, hid), F32),
        compiler_params=_params(("parallel",)),
        name="hyena_filter_hidden",
    )(z, w1, fb1.astype(F32).reshape(1, hid), fw2.astype(F32), fb2.astype(F32).reshape(1, hid), ffreq.astype(F32))
    nt = d // tn
    w3 = fw3.astype(F32)
    return pl.pallas_call(
        functools.partial(_filter_out_kernel, rows=rows),
        grid=(HY_ORDER, nt),
        in_specs=[_resident((l, hid)), _resident((l, 1)),
                  pl.BlockSpec((hid, tn), lambda o, j: (0, (2 * o) * nt + j)),
                  pl.BlockSpec((hid, tn), lambda o, j: (0, (2 * o + 1) * nt + j)),
                  pl.BlockSpec((1, tn), lambda o, j: (0, j))],
        out_specs=pl.BlockSpec((1, 2, l, tn), lambda o, j: (o, 0, 0, j)),
        out_shape=jax.ShapeDtypeStruct((HY_ORDER, 2, l, d), F32),
        compiler_params=_params(("parallel", "parallel")),
        name="hyena_filters",
    )(hidden, t.reshape(l, 1), w3, w3, deltas)


def _filter_spectrum_kernel(hf_ref, hb_ref, f1_ref, f2_ref, o_ref, s_ref, *, p):
    na, nb = p["na"], p["nb"]
    half = na // 2

    def gather(tb):
        fwd = hf_ref[pl.ds(tb, half, stride=nb), :]
        bwd = hb_ref[pl.ds(jnp.where(tb == 0, 0, nb - tb), half, stride=nb), :]
        return jnp.concatenate([fwd, bwd], axis=0).astype(BF16)

    _fft_stage1(gather, f1_ref, s_ref, p)

    def body(i, carry):
        spec = jnp.dot(f2_ref[...], _slab_pair(s_ref, i, p).astype(BF16), preferred_element_type=F32)
        r = pl.multiple_of(i * 4 * nb, SUBLANES)
        o_ref[pl.ds(r, 2 * nb), :] = spec[:, :LANES].astype(BF16)
        o_ref[pl.ds(r + 2 * nb, 2 * nb), :] = spec[:, LANES:].astype(BF16)
        return carry

    lax.fori_loop(0, p["n_pairs"], body, 0, unroll=p["pair_unroll"])


def hyena_filter_spectrum(filt, tables):
    p, _, f1k, f2, _, _ = tables
    order, _, l, d = filt.shape
    rows = 2 * p["n_pairs"] * 2 * p["nb"]
    return pl.pallas_call(
        functools.partial(_filter_spectrum_kernel, p=p),
        grid=(order, d // LANES),
        in_specs=[pl.BlockSpec((None, None, l, LANES), lambda o, j: (o, 0, 0, j)),
                  pl.BlockSpec((None, None, l, LANES), lambda o, j: (o, 1, 0, j)),
                  _resident(f1k.shape), _resident(f2.shape)],
        out_specs=pl.BlockSpec((None, rows, LANES), lambda o, j: (o, 0, j)),
        out_shape=jax.ShapeDtypeStruct((order, rows, d), BF16),
        scratch_shapes=[pltpu.VMEM((p["nsp"] * p["pitch"], LANES), F32)],
        compiler_params=_params(("parallel", "parallel")),
        name="hyena_filter_spectrum",
    )(filt, filt, jnp.asarray(f1k).astype(BF16), jnp.asarray(f2).astype(BF16))


def _longconv_kernel(a_ref, m_ref, k_ref, bias_ref, f1_ref, f2_ref, g2_ref, g1_ref, o_ref, s_ref, *, p, row_chunk):
    na, nb, nsp, pitch = p["na"], p["nb"], p["nsp"], p["pitch"]
    half = na // 2

    _fft_stage1(lambda tb: a_ref[pl.ds(tb, half, stride=nb), :].astype(BF16), f1_ref, s_ref, p)

    def mid(i, carry):
        x = jnp.dot(f2_ref[...], _slab_pair(s_ref, i, p).astype(BF16), preferred_element_type=F32)
        r = pl.multiple_of(i * 4 * nb, SUBLANES)
        kk = jnp.concatenate([k_ref[pl.ds(r, 2 * nb), :], k_ref[pl.ds(r + 2 * nb, 2 * nb), :]], axis=1).astype(F32)
        xr, xi, kr, ki = x[:nb], x[nb:], kk[:nb], kk[nb:]
        y = jnp.concatenate([xr * kr - xi * ki, xr * ki + xi * kr], axis=0).astype(BF16)
        bq = jnp.dot(g2_ref[...], y, preferred_element_type=F32)
        r0 = pl.multiple_of(2 * i * pitch, SUBLANES)
        r1 = pl.multiple_of(2 * i * pitch + pitch, SUBLANES)
        s_ref[pl.ds(r0, 2 * nb), :] = bq[:, :LANES]
        s_ref[pl.ds(r1, 2 * nb), :] = bq[:, LANES:]
        return carry

    lax.fori_loop(0, p["n_pairs"], mid, 0, unroll=p["pair_unroll"])

    def last(tb, carry):
        bq = jnp.concatenate([s_ref[pl.ds(tb, nsp, stride=pitch), :], s_ref[pl.ds(nb + tb, nsp, stride=pitch), :]], axis=0)
        o_ref[pl.ds(tb, half, stride=nb), :] = jnp.dot(g1_ref[tb], bq.astype(BF16), preferred_element_type=F32)
        return carry

    lax.fori_loop(0, nb, last, 0, unroll=p["tb_unroll"])

    def finish(i, carry):
        r = pl.multiple_of(i * row_chunk, SUBLANES)
        a = a_ref[pl.ds(r, row_chunk), :]
        o_ref[pl.ds(r, row_chunk), :] = m_ref[pl.ds(r, row_chunk), :] * (o_ref[pl.ds(r, row_chunk), :] + bias_ref[...] * a)
        return carry

    lax.fori_loop(0, p["l"] // row_chunk, finish, 0)


def hyena_longconv(a, a_col, m, m_col, kspec, bias, tables):
    p, f1, _, f2, g2, g1 = tables
    b, l, _ = a.shape
    d = D_MODEL
    nt = d // LANES
    rows = kspec.shape[0]
    tabs = [jnp.asarray(t).astype(BF16) for t in (f1, f2, g2, g1)]
    return pl.pallas_call(
        functools.partial(_longconv_kernel, p=p, row_chunk=min(512, l)),
        grid=(nt, b),
        in_specs=[pl.BlockSpec((None, l, LANES), lambda j, i: (i, 0, a_col * nt + j)),
                  pl.BlockSpec((None, l, LANES), lambda j, i: (i, 0, m_col * nt + j)),
                  pl.BlockSpec((rows, LANES), lambda j, i: (0, j)),
                  pl.BlockSpec((1, LANES), lambda j, i: (0, j))] + [_resident(t.shape) for t in tabs],
        out_specs=pl.BlockSpec((None, l, LANES), lambda j, i: (i, 0, j)),
        out_shape=jax.ShapeDtypeStruct((b, l, d), F32),
        scratch_shapes=[pltpu.VMEM((p["nsp"] * p["pitch"], LANES), F32)],
        compiler_params=_params(("parallel", "parallel")),
        name="hyena_longconv",
    )(a, m, kspec, bias.astype(F32).reshape(1, d), *tabs)


def hyena_core(pc, fw1, fb1, fw2, fb2, fw3, ffreq, hbias):
    l = pc.shape[1]
    tables = _fft_tables(l)
    filt = hyena_filters(l, fw1, fb1, fw2, fb2, fw3, ffreq)
    kspec = hyena_filter_spectrum(filt, tables)
    z = hyena_longconv(pc, 0, pc, 1, kspec[0], hbias[0], tables)
    return hyena_longconv(z, 0, pc, 2, kspec[1], hbias[1], tables)


def kernel(x, c, ctx, c_ctx, ada_w, ada_b, norm_g, ffn_w_gate, ffn_w_up, ffn_w_down, ssd_w_in, ssd_conv_w, ssd_conv_b, ssd_a_log, ssd_dt_bias, ssd_d, ssd_norm_g, ssd_w_out, pool_w, pool_b, pool_scale, hy_w_in, hy_conv_w, hy_conv_b, hy_filt_w1, hy_filt_b1, hy_filt_w2, hy_filt_b2, hy_filt_w3, hy_filt_freq, hy_bias, hy_w_out, final_g):
    batch = x.shape[0]
    d = D_MODEL
    h = ctx

    s = jnp.concatenate([jax.nn.silu(c), jax.nn.silu(c_ctx)[None], jnp.zeros((7 - batch, d), F32)], axis=0)
    mods = ada_modulation(s, ada_w, ada_b).reshape(DEPTH, 8, N_MOD, d)

    wg_bf, wu_bf, wd_bf = ffn_w_gate.astype(BF16), ffn_w_up.astype(BF16), ffn_w_down.astype(BF16)
    fg = final_g.reshape(1, d)

    for i in range(DEPTH):
        kind = i % N_MIXERS
        j = i // N_MIXERS
        last = i == DEPTH - 1
        ctx_in_needed = (not last) or kind == 0
        m = [mods[i, :batch, k].reshape(batch, 1, 1, d) for k in range(N_MOD)]
        mc = [jnp.broadcast_to(mods[i, batch, k].reshape(1, 1, 1, d), (batch, 1, 1, d)) for k in range(N_MOD)]
        g0, g1, g2 = (norm_g[i, k].reshape(1, d) for k in range(3))

        x = ffn_step(x, m[0], m[1], m[2], g0, wg_bf[i, 0], wu_bf[i, 0], wd_bf[i, 0])
        if ctx_in_needed:
            h = ffn_step(h, mc[0], mc[1], mc[2], g0, wg_bf[i, 0], wu_bf[i, 0], wd_bf[i, 0])

        if kind == 0:
            x, h_new = ssd_mixer(x, h, (m[3], m[4], m[5]), (mc[3], mc[4], mc[5]), g1, ssd_w_in[j], ssd_conv_w[j],
                                 ssd_conv_b[j], ssd_a_log[j], ssd_dt_bias[j], ssd_d[j], ssd_norm_g[j], ssd_w_out[j], not last)
            if not last:
                h = h_new
        elif kind == 1:
            x = pool_mixer(x, m[3], m[4], m[5], g1, pool_w[j], pool_b[j].reshape(-1), pool_scale[j], GRID_W)
            if not last:
                h = pool_mixer(h, mc[3], mc[4], mc[5], g1, pool_w[j], pool_b[j].reshape(-1), pool_scale[j], h.shape[1])
        else:
            w_in = hy_w_in[j].astype(BF16)
            w_out = hy_w_out[j].astype(BF16)
            filt = (hy_filt_w1[j], hy_filt_b1[j], hy_filt_w2[j], hy_filt_b2[j], hy_filt_w3[j], hy_filt_freq[j])
            conv = {0: (hy_conv_w[j], hy_conv_b[j], False)}
            (p_lat,) = modulate_project(x, m[3], m[4], g1, [w_in], tm=512, conv=conv)
            x = project_residual(x, hyena_core(p_lat, *filt, hy_bias[j]), m[5], w_out)
            if not last:
                (p_ctx,) = modulate_project(h, mc[3], mc[4], g1, [w_in], conv=conv)
                h = project_residual(h, hyena_core(p_ctx, *filt, hy_bias[j]), mc[5], w_out)

        x = ffn_step(x, m[6], m[7], m[8], g2, wg_bf[i, 1], wu_bf[i, 1], wd_bf[i, 1],
                     final_g=fg if last else None)
        if not last:
            h = ffn_step(h, mc[6], mc[7], mc[8], g2, wg_bf[i, 1], wu_bf[i, 1], wd_bf[i, 1])
    return x
```

```python
import functools
import math

import jax
import jax.numpy as jnp
from jax import lax
from jax.experimental import pallas as pl
from jax.experimental.pallas import tpu as pltpu

F32 = jnp.float32
BF16 = jnp.bfloat16
HIGHEST = lax.Precision.HIGHEST

D_MODEL = 1024
DEPTH = 4
GRID_W = 64
N_MIXERS = 3
D_FF = 2816
N_MOD = 9
EPS = 1e-6
LOG2_E = 1.4426950408889634

SSD_D_INNER = 2 * D_MODEL
SSD_HEAD_DIM = 64
SSD_N_HEADS = SSD_D_INNER // SSD_HEAD_DIM
SSD_N_GROUPS = 4
SSD_HEADS_PER_GROUP = SSD_N_HEADS // SSD_N_GROUPS
SSD_D_STATE = 128
SSD_CHUNK = 128
SSD_GN = SSD_N_GROUPS * SSD_D_STATE
SSD_CONV_DIM = SSD_D_INNER + 2 * SSD_GN
SSD_GROUP_WIDTH = SSD_D_INNER // SSD_N_GROUPS

POOL_WINDOWS = (2, 4, 8, 16)
POOL_GROUPS = 4
POOL_GROUP_DIM = D_MODEL // POOL_GROUPS

HY_ORDER = 2
HY_EMB_DIM = 33
HY_BANDS = (HY_EMB_DIM - 1) // 2
HY_MAX_DECAY = math.log(1e-2) / 0.3
HY_MIN_DECAY = math.log(1e-2) / 1.5

VMEM_LIMIT_BYTES = 56 * 1024 * 1024
SUBLANES = 8
LANES = 128


def _params(sem):
    return pltpu.CompilerParams(dimension_semantics=sem, vmem_limit_bytes=VMEM_LIMIT_BYTES)


def _resident(shape):
    nd = len(shape)
    return pl.BlockSpec(shape, lambda *_: (0,) * nd, pipeline_mode=pl.Buffered(1))


def _mod_spec(d, _unused=0):
    return pl.BlockSpec((1, 1, 1, d), lambda i, j: (i, 0, 0, 0))


def _modulated(x, g, shift, scale):
    ms = jnp.mean(x * x, axis=-1, keepdims=True)
    return (x * lax.rsqrt(ms + EPS)) * g * (1.0 + scale) + shift


def _silu(v):
    return v * jax.nn.sigmoid(v)


def _ada_kernel(s_ref, w_ref, b_ref, o_ref):
    o_ref[0] = jnp.dot(s_ref[...], w_ref[0], preferred_element_type=F32, precision=HIGHEST) + b_ref[0]


def ada_modulation(s, ada_w, ada_b, tn=2304):
    depth, d, n = ada_w.shape
    r = s.shape[0]
    return pl.pallas_call(
        _ada_kernel,
        grid=(depth, n // tn),
        in_specs=[pl.BlockSpec((r, d), lambda i, j: (0, 0)),
                  pl.BlockSpec((1, d, tn), lambda i, j: (i, 0, j)),
                  pl.BlockSpec((1, 1, tn), lambda i, j: (i, 0, j))],
        out_specs=pl.BlockSpec((1, r, tn), lambda i, j: (i, 0, j)),
        out_shape=jax.ShapeDtypeStruct((depth, r, n), F32),
        compiler_params=_params(("parallel", "parallel")),
        name="ada_modulation",
    )(s, ada_w, ada_b.reshape(depth, 1, n))


def _ffn_kernel(x_ref, sh_ref, sc_ref, gt_ref, g_ref, wg_ref, wu_ref, wd_ref, *rest, f_chunk, final):
    if final:
        fg_ref, o_ref = rest
    else:
        (o_ref,) = rest
    x = x_ref[0]
    u = _modulated(x, g_ref[...], sh_ref[0, 0], sc_ref[0, 0]).astype(BF16)
    acc = jnp.zeros(x.shape, F32)
    d_ff = wg_ref.shape[1]
    for f0 in range(0, d_ff, f_chunk):
        a = jnp.dot(u, wg_ref[:, f0:f0 + f_chunk], preferred_element_type=F32)
        b = jnp.dot(u, wu_ref[:, f0:f0 + f_chunk], preferred_element_type=F32)
        h = (_silu(a) * b).astype(BF16)
        acc = acc + jnp.dot(h, wd_ref[f0:f0 + f_chunk, :], preferred_element_type=F32)
    y = x + (0.5 * gt_ref[0, 0]) * acc
    if final:
        ms = jnp.mean(y * y, axis=-1, keepdims=True)
        y = y * lax.rsqrt(ms + EPS) * fg_ref[...]
    o_ref[0] = y


def ffn_step(x, shift, scale, gate, g, wg, wu, wd, final_g=None, tm=512, f_chunk=256):
    b, l, d = x.shape
    tm = min(tm, l)
    f = wg.shape[1]
    final = final_g is not None
    mod_spec = _mod_spec(d, 0)
    in_specs = [pl.BlockSpec((1, tm, d), lambda i, j: (i, j, 0)),
                mod_spec, mod_spec, mod_spec,
                _resident((1, d)), _resident((d, f)), _resident((d, f)), _resident((f, d))]
    args = [x, shift, scale, gate, g, wg, wu, wd]
    if final:
        in_specs.append(_resident((1, d)))
        args.append(final_g)
    return pl.pallas_call(
        functools.partial(_ffn_kernel, f_chunk=f_chunk, final=final),
        grid=(b, l // tm),
        in_specs=in_specs,
        out_specs=pl.BlockSpec((1, tm, d), lambda i, j: (i, j, 0)),
        out_shape=jax.ShapeDtypeStruct((b, l, d), F32),
        compiler_params=_params(("parallel", "parallel")),
        name="ffn_step",
    )(*args)


CONV_COLS = 512


def _modproj_kernel(*refs, n_w, n_dest, conv, n_tiles):
    if conv:
        prev_ref, x_ref, next_ref, sh_ref, sc_ref, g_ref, *rest = refs
    else:
        x_ref, sh_ref, sc_ref, g_ref, *rest = refs
    w_refs = rest[:n_w]
    cw_refs = rest[n_w:n_w + 2 * len(conv)]
    o_refs = rest[n_w + 2 * len(conv) + n_dest:]
    g, sh, sc = g_ref[...], sh_ref[0, 0], sc_ref[0, 0]
    u = _modulated(x_ref[0], g, sh, sc)
    if n_w == 0:
        o_refs[0][0] = u
        return
    u = u.astype(BF16)
    tm = x_ref.shape[1]
    if conv:
        ue = _modulated(jnp.concatenate([prev_ref[0], x_ref[0], next_ref[0]], axis=0), g, sh, sc).astype(BF16)
        j = pl.program_id(1)
        top = (j > 0).astype(F32)
        bot = (j < n_tiles - 1).astype(F32)
    for idx, (w_ref, o_ref) in enumerate(zip(w_refs, o_refs)):
        if idx not in conv:
            o_ref[0] = jnp.dot(u, w_ref[...], preferred_element_type=F32).astype(o_ref.dtype)
            continue
        k_w, silu = conv[idx]
        cw_ref, cb_ref = cw_refs[2 * list(conv).index(idx)], cw_refs[2 * list(conv).index(idx) + 1]
        p = k_w // 2
        rows = tm + 2 * SUBLANES
        for c0 in range(0, w_ref.shape[1], CONV_COLS):
            ce = jnp.dot(ue, w_ref[:, c0:c0 + CONV_COLS], preferred_element_type=F32)
            ce = jnp.concatenate([ce[:SUBLANES] * top, ce[SUBLANES:SUBLANES + tm], ce[SUBLANES + tm:] * bot], axis=0)
            acc = jnp.broadcast_to(cb_ref[:, c0:c0 + CONV_COLS], (tm, CONV_COLS))
            for k in range(k_w):
                tap = ce if k == p else pltpu.roll(ce, (p - k) % rows, axis=0)
                acc = acc + cw_ref[k:k + 1, c0:c0 + CONV_COLS] * tap[SUBLANES:SUBLANES + tm, :]
            o_ref[0, :, c0:c0 + CONV_COLS] = (_silu(acc) if silu else acc).astype(o_ref.dtype)


def modulate_project(x, shift, scale, g, weights, tm=256, out_dtypes=None, out_rows=None, row_block=0, dest=None,
                     conv=None):
    b, l, d = x.shape
    tm = min(tm, l)
    conv = dict(conv or {})
    mod_spec = _mod_spec(d)
    x_spec = pl.BlockSpec((1, tm, d), lambda i, j: (i, j, 0))
    args = [x]
    in_specs = [x_spec]
    if conv:
        rb, last = tm // SUBLANES, l // SUBLANES - 1
        in_specs = [pl.BlockSpec((1, SUBLANES, d), lambda i, j: (i, jnp.maximum(j * rb - 1, 0), 0)), x_spec,
                    pl.BlockSpec((1, SUBLANES, d), lambda i, j: (i, jnp.minimum((j + 1) * rb, last), 0))]
        args = [x, x, x]
    in_specs += [mod_spec, mod_spec, _resident((1, d))] + [_resident(w.shape) for w in weights]
    args += [shift, scale, g, *weights]
    for i in conv:
        cw, cb, _ = conv[i]
        assert weights[i].shape[1] % CONV_COLS == 0
        in_specs += [_resident(cw.shape), _resident((1, cb.shape[0]))]
        args += [cw.astype(F32), cb.astype(F32).reshape(1, -1)]
    widths = [w.shape[1] for w in weights] or [d]
    out_dtypes = out_dtypes or [F32] * len(widths)
    dest = list(dest or [])
    in_specs += [pl.BlockSpec(memory_space=pl.ANY)] * len(dest)
    n_in = len(args)
    return pl.pallas_call(
        functools.partial(_modproj_kernel, n_w=len(weights), n_dest=len(dest),
                          conv={i: (c[0].shape[0], c[2]) for i, c in conv.items()}, n_tiles=l // tm),
        grid=(b, l // tm),
        in_specs=in_specs,
        out_specs=[pl.BlockSpec((1, tm, n), lambda i, j: (i, j + row_block, 0)) for n in widths],
        out_shape=[jax.ShapeDtypeStruct((b, out_rows or l, n), dt) for n, dt in zip(widths, out_dtypes)],
        input_output_aliases={n_in + k: k for k in range(len(dest))},
        compiler_params=_params(("parallel", "parallel")),
        name="modulate_project",
    )(*args, *dest)


def _outproj_kernel(x_ref, a_ref, gt_ref, w_ref, o_ref):
    y = jnp.dot(a_ref[0].astype(BF16), w_ref[...], preferred_element_type=F32)
    o_ref[0] = x_ref[0] + gt_ref[0, 0] * y


def project_residual(x, a, gate, w, tm=512):
    b, l, d = x.shape
    k = a.shape[-1]
    tm = min(tm, l)
    return pl.pallas_call(
        _outproj_kernel,
        grid=(b, l // tm),
        in_specs=[pl.BlockSpec((1, tm, d), lambda i, j: (i, j, 0)),
                  pl.BlockSpec((1, tm, k), lambda i, j: (i, j, 0)),
                  _mod_spec(d, 0),
                  _resident((k, d))],
        out_specs=pl.BlockSpec((1, tm, d), lambda i, j: (i, j, 0)),
        out_shape=jax.ShapeDtypeStruct((b, l, d), F32),
        compiler_params=_params(("parallel", "parallel")),
        name="project_residual",
    )(x, a, gate, w)


def _bf16_terms(v):
    hi = v.astype(BF16)
    r = v - hi.astype(F32)
    mid = r.astype(BF16)
    return hi, mid, (r - mid.astype(F32)).astype(BF16)


def _dot_f32_rhs(m, v):
    return sum(jnp.dot(m, t, preferred_element_type=F32) for t in _bf16_terms(v))


def _dot_f32_lhs(v, m):
    return sum(jnp.dot(t, m, preferred_element_type=F32) for t in _bf16_terms(v))


def _dot_split(a, b):
    a_hi, a_lo, _ = _bf16_terms(a)
    b_hi, b_lo, _ = _bf16_terms(b)
    return (jnp.dot(a_hi, b_hi, preferred_element_type=F32) + jnp.dot(a_lo, b_hi, preferred_element_type=F32)
            + jnp.dot(a_hi, b_lo, preferred_element_type=F32))


def _ssd_scan_kernel(xf_ref, dtf_ref, dttf_ref, xb_ref, dtb_in_ref, dttb_ref, dtb_ref, dtbt_ref, a_ref, at_ref, ex_ref,
                     of_ref, ob_ref, h_ref):
    @pl.when(pl.program_id(1) == 0)
    def _():
        h_ref[...] = jnp.zeros(h_ref.shape, F32)

    consts = (dtb_ref, dtbt_ref, a_ref, at_ref, ex_ref)
    _ssd_chunk(xf_ref, dtf_ref, dttf_ref, *consts, of_ref, h_ref.at[0], True)
    _ssd_chunk(xb_ref, dtb_in_ref, dttb_ref, *consts, ob_ref, h_ref.at[1], False)


def _ssd_chunk(xbc_ref, dt_ref, dtt_ref, dtb_ref, dtbt_ref, a_ref, at_ref, ex_ref, o_ref, h_ref, fwd):
    nh, q, hp = SSD_N_HEADS, SSD_CHUNK, SSD_HEAD_DIM
    heads = slice(0, nh) if fwd else slice(nh, 2 * nh)
    dt = jax.nn.softplus(dt_ref[0][:, heads] + dtb_ref[:, heads])
    a = dt * a_ref[:, heads]
    dt_t = jax.nn.softplus(dtt_ref[0][heads] + dtbt_ref[heads])
    a_t = dt_t * at_ref[heads]

    r_i = lax.broadcasted_iota(jnp.int32, (q, q), 0)
    c_i = lax.broadcasted_iota(jnp.int32, (q, q), 1)
    seen = (c_i <= r_i) if fwd else (c_i >= r_i)
    seen_t = (r_i <= c_i) if fwd else (r_i >= c_i)
    cum = _dot_f32_rhs(seen.astype(BF16), a)
    cum_t = _dot_f32_lhs(a_t, seen_t.astype(BF16))
    total = cum[q - 1:q, :] if fwd else cum[0:1, :]
    ecum = jnp.exp(cum).astype(BF16)
    dtdec = (dt * jnp.exp(total - cum)).astype(BF16)
    etot = jnp.exp(total)
    cum2 = cum * LOG2_E
    row2 = cum_t * LOG2_E - jnp.log2(dt_t)

    pw = 2 * hp
    left = lax.broadcasted_iota(jnp.int32, (q, pw), 1) < hp
    left_row = lax.broadcasted_iota(jnp.int32, (1, pw), 1) < hp
    pairs = SSD_HEADS_PER_GROUP // 2
    for g in range(SSD_N_GROUPS):
        b_g = xbc_ref[0, :, SSD_D_INNER + g * SSD_D_STATE:SSD_D_INNER + (g + 1) * SSD_D_STATE].astype(BF16)
        c_lo = SSD_D_INNER + SSD_GN + g * SSD_D_STATE
        c_g = xbc_ref[0, :, c_lo:c_lo + SSD_D_STATE].astype(BF16)
        cb = lax.dot_general(c_g, b_g, (((1,), (1,)), ((), ())), preferred_element_type=F32)
        h_g = h_ref[g]
        ex_g = ex_ref[:, g * SSD_GROUP_WIDTH:(g + 1) * SSD_GROUP_WIDTH]
        ch = (jnp.dot(c_g, h_g.astype(BF16), preferred_element_type=F32)
              * jnp.dot(ecum, ex_g, preferred_element_type=F32))
        dtdec_g = jnp.dot(dtdec, ex_g, preferred_element_type=F32)
        xdecs, etots = [], []
        for pr in range(pairs):
            e1 = g * SSD_HEADS_PER_GROUP + 2 * pr
            e2 = e1 + 1
            x_pair = xbc_ref[0, :, e1 * hp:e1 * hp + pw].astype(F32)
            ws = []
            for e in (e1, e2):
                seg2 = cum2[:, e:e + 1] - row2[e:e + 1, :]
                ws.append((cb * jnp.exp2(jnp.where(seen, seg2, -jnp.inf))).astype(BF16))
            rhs = jnp.concatenate([jnp.where(left, x_pair, 0.0), jnp.where(left, 0.0, x_pair)], axis=0).astype(BF16)
            y_diag = jnp.dot(jnp.concatenate(ws, axis=1), rhs, preferred_element_type=F32)
            y_off = ch[:, pr * pw:(pr + 1) * pw]
            o_ref[0, 0, :, e1 * hp:e1 * hp + pw] = (y_diag + y_off).astype(o_ref.dtype)
            xdecs.append((x_pair * dtdec_g[:, pr * pw:(pr + 1) * pw]).astype(BF16))
            etots.append(jnp.where(left_row, etot[:, e1:e1 + 1], etot[:, e2:e2 + 1]))
        s_g = lax.dot_general(b_g, jnp.concatenate(xdecs, axis=1), (((0,), (0,)), ((), ())),
                              preferred_element_type=F32)
        h_ref[g] = h_g * jnp.concatenate(etots, axis=1) + s_g


def ssd_scan(xbc, dt_raw, dt_bias, a_log, n_lead_chunks):
    b, t, _ = xbc.shape
    q, nh = SSD_CHUNK, SSD_N_HEADS
    nc = t // q
    dt_t = jnp.swapaxes(dt_raw, 1, 2)
    a_neg = -jnp.exp(a_log.astype(F32)).reshape(1, 2 * nh)
    dtb = dt_bias.astype(F32).reshape(1, 2 * nh)
    expand = jnp.repeat(jnp.eye(nh, dtype=BF16), SSD_HEAD_DIM, axis=1)

    n_main = nc - n_lead_chunks

    def chunk_of(d, c):
        if d:
            return jnp.where(c < n_lead_chunks, nc - 1 - c, n_main - 1 - (c - n_lead_chunks))
        return jnp.where(c < n_lead_chunks, n_main + c, c - n_lead_chunks)

    def token_specs(d):
        return [pl.BlockSpec((1, q, SSD_CONV_DIM), lambda i, c: (i, chunk_of(d, c), 0)),
                pl.BlockSpec((1, q, 2 * nh), lambda i, c: (i, chunk_of(d, c), 0)),
                pl.BlockSpec((1, 2 * nh, q), lambda i, c: (i, 0, chunk_of(d, c)))]

    y_f, y_b = pl.pallas_call(
        _ssd_scan_kernel,
        grid=(b, nc),
        in_specs=token_specs(0) + token_specs(1) + [
            _resident((1, 2 * nh)), _resident((2 * nh, 1)),
            _resident((1, 2 * nh)), _resident((2 * nh, 1)), _resident((nh, SSD_D_INNER))],
        out_specs=[pl.BlockSpec((1, 1, q, SSD_D_INNER), lambda i, c, d=d: (i, 0, chunk_of(d, c), 0)) for d in (0, 1)],
        out_shape=[jax.ShapeDtypeStruct((b, 1, t, SSD_D_INNER), BF16)] * 2,
        scratch_shapes=[pltpu.VMEM((2, SSD_N_GROUPS, SSD_D_STATE, SSD_GROUP_WIDTH), F32)],
        compiler_params=_params(("parallel", "arbitrary")),
        name="ssd_scan",
    )(xbc, dt_raw, dt_t, xbc, dt_raw, dt_t, dtb, dtb.reshape(2 * nh, 1), a_neg, a_neg.reshape(2 * nh, 1), expand)
    return y_f, y_b


def _ssd_out_kernel(x_ref, y0_ref, y1_ref, xs0_ref, xs1_ref, z_ref, dsk_ref, ng_ref, gt_ref, w_ref, o_ref):
    xs = jnp.concatenate([xs0_ref[0], xs1_ref[0]], axis=1).astype(F32)
    y = y0_ref[0, 0].astype(F32) + y1_ref[0, 0].astype(F32) + dsk_ref[...] * xs
    gy = y * _silu(z_ref[0].astype(F32))
    parts = []
    for g in range(SSD_N_GROUPS):
        blk = gy[:, g * SSD_GROUP_WIDTH:(g + 1) * SSD_GROUP_WIDTH]
        parts.append(blk * lax.rsqrt(jnp.mean(blk * blk, axis=-1, keepdims=True) + EPS))
    a = (jnp.concatenate(parts, axis=1) * ng_ref[...]).astype(BF16)
    o_ref[0] = x_ref[0] + gt_ref[0, 0] * jnp.dot(a, w_ref[...], preferred_element_type=F32)


def ssd_gate_project(x, y2, xbc, z, d_skip, norm_g, gate, w_out, src_block, tm=256):
    b, t, d = x.shape
    half = SSD_D_INNER // 2
    dsk = jnp.repeat(d_skip.astype(F32), SSD_HEAD_DIM).reshape(1, SSD_D_INNER)
    return pl.pallas_call(
        _ssd_out_kernel,
        grid=(b, t // tm),
        in_specs=[pl.BlockSpec((1, tm, d), lambda i, j: (i, j, 0)),
                  pl.BlockSpec((1, 1, tm, SSD_D_INNER), lambda i, j: (i, 0, j + src_block, 0)),
                  pl.BlockSpec((1, 1, tm, SSD_D_INNER), lambda i, j: (i, 0, j + src_block, 0)),
                  pl.BlockSpec((1, tm, half), lambda i, j: (i, j + src_block, 0)),
                  pl.BlockSpec((1, tm, half), lambda i, j: (i, j + src_block, 1)),
                  pl.BlockSpec((1, tm, SSD_D_INNER), lambda i, j: (i, j + src_block, 0)),
                  _resident((1, SSD_D_INNER)), _resident((1, SSD_D_INNER)),
                  _mod_spec(d),
                  _resident((SSD_D_INNER, d))],
        out_specs=pl.BlockSpec((1, tm, d), lambda i, j: (i, j, 0)),
        out_shape=jax.ShapeDtypeStruct((b, t, d), F32),
        compiler_params=_params(("parallel", "parallel")),
        name="ssd_gate_project",
    )(x, y2[0], y2[1], xbc, xbc, z, dsk, norm_g.astype(F32).reshape(1, SSD_D_INNER), gate, w_out)


def ssd_mixer(x, h, m_lat, m_ctx, g1, w_in, conv_w, conv_b, a_log, dt_bias, d_skip, norm_g, w_out, need_ctx_out,
              tm_proj=512):
    lc, l = h.shape[1], x.shape[1]
    tm_ctx = min(tm_proj, lc)
    assert l % tm_proj == 0 and l % tm_ctx == 0
    w_bf = w_in.astype(BF16)
    ws = [w_bf[:, :SSD_D_INNER], w_bf[:, SSD_D_INNER:SSD_D_INNER + SSD_CONV_DIM], w_bf[:, SSD_D_INNER + SSD_CONV_DIM:]]
    dts = [BF16, BF16, F32]
    conv = {1: (conv_w, conv_b, True)}
    parts = modulate_project(x, m_lat[0], m_lat[1], g1, ws, tm=tm_proj, out_dtypes=dts, out_rows=lc + l, conv=conv)
    z, xbc, dt_raw = modulate_project(h, m_ctx[0], m_ctx[1], g1, ws, tm=tm_ctx, out_dtypes=dts, out_rows=lc + l,
                                      row_block=l // tm_ctx, dest=parts, conv=conv)
    y2 = ssd_scan(xbc, dt_raw, dt_bias, a_log, lc // SSD_CHUNK)
    w_out = w_out.astype(BF16)
    x_new = ssd_gate_project(x, y2, xbc, z, d_skip, norm_g, m_lat[2], w_out, 0, tm=tm_proj)
    h_new = (ssd_gate_project(h, y2, xbc, z, d_skip, norm_g, m_ctx[2], w_out, l // tm_ctx, tm=tm_ctx)
             if need_ctx_out else None)
    return x_new, h_new


POOL_CHUNK = 256
POOL_MAX_HALF = max(POOL_WINDOWS) // 2


def _pool_band_tables(width):
    import numpy as np
    pos = np.arange(POOL_CHUNK)
    line, col = pos // width, pos % width
    out = []
    for w in POOL_WINDOWS:
        inside = (col[None, :] >= col[:, None] - w // 2) & (col[None, :] <= col[:, None] + (w - w // 2) - 1)
        out.append((inside & (line[None, :] == line[:, None])).astype(np.float32))
    return np.stack(out)


def _pool_kernel(u_ref, band_ref, o_ref, cs_ref, *, width, n_lines):
    l, c = u_ref.shape
    n_chunks = l // POOL_CHUNK
    assert POOL_WINDOWS == tuple(2 << g for g in range(POOL_GROUPS)) and width & (width - 1) == 0
    half = lax.shift_left(jnp.int32(1), pl.program_id(1) // (POOL_GROUP_DIM // LANES))
    log_w = width.bit_length() - 1
    pad = POOL_MAX_HALF * width if n_lines > 1 else 0
    if pad:
        cs_ref[0:pad, :] = jnp.zeros((pad, c), F32)
        cs_ref[pad + l:2 * pad + l, :] = jnp.zeros((pad, c), F32)

    def col_pass(i, carry):
        r = pl.multiple_of(i * POOL_CHUNK, POOL_CHUNK)
        u = u_ref[pl.ds(r, POOL_CHUNK), :]
        hi = u.astype(BF16)
        lo = (u - hi.astype(F32)).astype(BF16)
        both = jnp.dot(band_ref[0], jnp.concatenate([hi, lo], axis=1), preferred_element_type=F32)
        cs_ref[pl.ds(pad + r, POOL_CHUNK), :] = both[:, :c] + both[:, c:]
        return carry

    lax.fori_loop(0, n_chunks, col_pass, 0)

    pos0 = lax.broadcasted_iota(jnp.int32, (POOL_CHUNK, c), 0)
    col = pos0 & (width - 1)
    inv_cnt = 1.0 / (jnp.minimum(col + half, width) - jnp.maximum(col - half, 0)).astype(F32)
    line0 = lax.shift_right_logical(pos0, log_w)

    def out_pass(i, carry):
        r = pl.multiple_of(i * POOL_CHUNK, POOL_CHUNK)
        if n_lines > 1:
            def add_line(k, acc):
                return acc + cs_ref[pl.ds(pl.multiple_of(pad + r + k * width, SUBLANES), POOL_CHUNK), :]
            s = lax.fori_loop(-half, half, add_line, jnp.zeros((POOL_CHUNK, c), F32))
            line = line0 + lax.shift_right_logical(r, log_w)
            cnt_l = jnp.minimum(line + half, n_lines) - jnp.maximum(line - half, 0)
            mean = s * inv_cnt / cnt_l.astype(F32)
        else:
            mean = cs_ref[pl.ds(r, POOL_CHUNK), :] * inv_cnt
        o_ref[pl.ds(r, POOL_CHUNK), :] = (mean - u_ref[pl.ds(r, POOL_CHUNK), :]).astype(o_ref.dtype)
        return carry

    lax.fori_loop(0, n_chunks, out_pass, 0)


def pool_tokens(u, width):
    b, l, d = u.shape
    n_lines = l // width
    bands = jnp.asarray(_pool_band_tables(width)).astype(BF16)
    tiles_per_group = POOL_GROUP_DIM // LANES
    pad = POOL_MAX_HALF * width if n_lines > 1 else 0
    return pl.pallas_call(
        functools.partial(_pool_kernel, width=width, n_lines=n_lines),
        grid=(b, d // LANES),
        in_specs=[pl.BlockSpec((None, l, LANES), lambda i, j: (i, 0, j)),
                  pl.BlockSpec((1, POOL_CHUNK, POOL_CHUNK), lambda i, j: (j // tiles_per_group, 0, 0))],
        out_specs=pl.BlockSpec((None, l, LANES), lambda i, j: (i, 0, j)),
        out_shape=jax.ShapeDtypeStruct((b, l, d), BF16),
        scratch_shapes=[pltpu.VMEM((l + 2 * pad, LANES), F32)],
        compiler_params=_params(("parallel", "parallel")),
        name="pool_tokens",
    )(u, bands)


def _pool_mix_kernel(x_ref, p_ref, w_ref, b_ref, sc_ref, gt_ref, o_ref):
    ys = [jnp.dot(p_ref[0, :, g * POOL_GROUP_DIM:(g + 1) * POOL_GROUP_DIM], w_ref[g], preferred_element_type=F32)
          for g in range(POOL_GROUPS)]
    y = (jnp.concatenate(ys, axis=1) + b_ref[...]) * sc_ref[...]
    o_ref[0] = x_ref[0] + gt_ref[0, 0] * y


def pool_mix_residual(x, pooled, w, bias, scale, gate, tm=512):
    b, l, d = x.shape
    tm = min(tm, l)
    return pl.pallas_call(
        _pool_mix_kernel,
        grid=(b, l // tm),
        in_specs=[pl.BlockSpec((1, tm, d), lambda i, j: (i, j, 0)),
                  pl.BlockSpec((1, tm, d), lambda i, j: (i, j, 0)),
                  _resident(w.shape), _resident((1, d)), _resident((1, d)), _mod_spec(d, 0)],
        out_specs=pl.BlockSpec((1, tm, d), lambda i, j: (i, j, 0)),
        out_shape=jax.ShapeDtypeStruct((b, l, d), F32),
        compiler_params=_params(("parallel", "parallel")),
        name="pool_mix_residual",
    )(x, pooled, w.astype(BF16), bias.astype(F32).reshape(1, d), scale.astype(F32).reshape(1, d), gate)


def pool_mixer(x, shift, scale_mod, gate, g1, w, bias, scale, width):
    (u,) = modulate_project(x, shift, scale_mod, g1, [])
    return pool_mix_residual(x, pool_tokens(u, width), w, bias, scale, gate)


SLAB_PAD = 8


def _round_up(v, m):
    return (v + m - 1) // m * m


def _fft_plan(l):
    n = 2 * l
    na = 1 << (n.bit_length() // 2)
    nb = n // na
    ns = na // 2 + 1
    n_pairs = (ns + 1) // 2
    return dict(l=l, n=n, na=na, nb=nb, ns=ns, nsp=_round_up(ns + 1, SUBLANES), pitch=2 * nb + SLAB_PAD,
                n_pairs=n_pairs, tb_unroll=min(nb, 16),
                pair_unroll=max(u for u in range(1, 12) if n_pairs % u == 0))


def _fft_tables(l):
    import numpy as np
    p = _fft_plan(l)
    n, na, nb, ns, nsp = p["n"], p["na"], p["nb"], p["ns"], p["nsp"]
    half = na // 2
    ka = np.arange(ns)[None, :, None]
    tb = np.arange(nb)[:, None, None]

    def stage1(ta):
        th = 2.0 * np.pi * (ta[None, None, :] * ka / na + tb * ka / n)
        m = np.zeros((nb, 2 * nsp, ta.shape[0]))
        m[:, :ns] = np.cos(th)
        m[:, nsp:nsp + ns] = -np.sin(th)
        return m

    f1 = stage1(np.arange(half))
    f1k = np.zeros((nb, 2 * nsp, na))
    f1k[:, :, :half] = f1
    f1k[1:, :, half:] = stage1(na - 1 - np.arange(half))[1:]
    tb0 = stage1(na - np.arange(half))[0]
    tb0[:, 0] = 0.0
    f1k[0, :, half:] = tb0

    k2 = np.arange(nb)
    ang = 2.0 * np.pi * np.outer(k2, k2) / nb
    c, s = np.cos(ang), np.sin(ang)
    f2 = np.block([[c, s], [-s, c]])
    g2 = np.block([[c, -s], [s, c]])

    ta = np.arange(half)[None, :, None]
    kk = np.arange(ns)[None, None, :]
    tbb = np.arange(nb)[:, None, None]
    ph = 2.0 * np.pi * (ta * kk / na + tbb * kk / n)
    wgt = np.where((kk == 0) | (kk == na // 2), 1.0, 2.0) / n
    g1 = np.zeros((nb, half, 2 * nsp))
    g1[:, :, :ns] = wgt * np.cos(ph)
    g1[:, :, nsp:nsp + ns] = -wgt * np.sin(ph)
    f32 = np.float32
    return p, f1.astype(f32), f1k.astype(f32), f2.astype(f32), g2.astype(f32), g1.astype(f32)


def _fft_stage1(gather, f1_ref, s_ref, p):
    nb, nsp, pitch = p["nb"], p["nsp"], p["pitch"]

    def body(tb, carry):
        a = jnp.dot(f1_ref[tb], gather(tb), preferred_element_type=F32)
        s_ref[pl.ds(tb, nsp, stride=pitch), :] = a[:nsp]
        s_ref[pl.ds(nb + tb, nsp, stride=pitch), :] = a[nsp:]
        return carry

    lax.fori_loop(0, nb, body, 0, unroll=p["tb_unroll"])


def _slab_pair(s_ref, i, p):
    nb, pitch = p["nb"], p["pitch"]
    r0 = pl.multiple_of(2 * i * pitch, SUBLANES)
    r1 = pl.multiple_of(2 * i * pitch + pitch, SUBLANES)
    return jnp.concatenate([s_ref[pl.ds(r0, 2 * nb), :], s_ref[pl.ds(r1, 2 * nb), :]], axis=1)


FILTER_ROWS = 512


def _filter_hidden_kernel(z_ref, w1_ref, b1_ref, w2_ref, b2_ref, fr_ref, o_ref):
    h = jnp.sin(fr_ref[0:1, :] * (jnp.dot(z_ref[...], w1_ref[...], preferred_element_type=F32, precision=HIGHEST)
                                  + b1_ref[...]))
    o_ref[...] = jnp.sin(fr_ref[1:2, :] * (jnp.dot(h, w2_ref[...], preferred_element_type=F32, precision=HIGHEST)
                                           + b2_ref[...]))


def _filter_out_kernel(h_ref, t_ref, w3f_ref, w3b_ref, dl_ref, o_ref, *, rows):
    l, tn = o_ref.shape[2], o_ref.shape[3]
    wf, wb, dl = w3f_ref[...], w3b_ref[...], dl_ref[...]

    def fill(i, ss):
        r = pl.multiple_of(i * rows, rows)
        h = h_ref[pl.ds(r, rows), :]
        decay = jnp.exp(-t_ref[pl.ds(r, rows), :] * dl)
        hf = _dot_split(h, wf) * decay
        hb = _dot_split(h, wb) * decay
        o_ref[0, 0, pl.ds(r, rows), :] = hf
        o_ref[0, 1, pl.ds(r, rows), :] = hb
        return ss + jnp.sum(hf * hf + hb * hb, axis=0, keepdims=True)

    ss = lax.fori_loop(0, l // rows, fill, jnp.zeros((1, tn), F32))
    scale = lax.rsqrt(ss + EPS)

    def rescale(i, carry):
        r = pl.multiple_of(i * rows, rows)
        o_ref[0, 0, pl.ds(r, rows), :] = o_ref[0, 0, pl.ds(r, rows), :] * scale
        o_ref[0, 1, pl.ds(r, rows), :] = o_ref[0, 1, pl.ds(r, rows), :] * scale
        return carry

    lax.fori_loop(0, l // rows, rescale, 0)


def hyena_filters(l, fw1, fb1, fw2, fb2, fw3, ffreq, tn=LANES):
    d = D_MODEL
    pos = jnp.arange(l, dtype=F32)
    t = jnp.linspace(0.0, 1.0, l, dtype=F32)
    wpos = 2.0 * math.pi * pos / l
    f = jnp.linspace(1e-4, HY_BANDS - 1, HY_BANDS, dtype=F32)
    ang = wpos[:, None] * f[None, :]
    emb_pad = _round_up(HY_EMB_DIM, SUBLANES)
    z = jnp.concatenate([t[:, None], jnp.cos(ang), -jnp.sin(ang), jnp.zeros((l, emb_pad - HY_EMB_DIM), F32)], axis=-1)
    w1 = jnp.concatenate([fw1.astype(F32), jnp.zeros((emb_pad - HY_EMB_DIM, fw1.shape[1]), F32)], axis=0)
    deltas = jnp.abs(jnp.linspace(HY_MIN_DECAY, HY_MAX_DECAY, d, dtype=F32)).reshape(1, d)
    hid = fw2.shape[0]
    rows = min(FILTER_ROWS, l)
    hidden = pl.pallas_call(
        _filter_hidden_kernel,
        grid=(l // rows,),
        in_specs=[pl.BlockSpec((rows, emb_pad), lambda r: (r, 0)), _resident((emb_pad, hid)), _resident((1, hid)),
                  _resident((hid, hid)), _resident((1, hid)), _resident((2, hid))],
        out_specs=pl.BlockSpec((rows, hid), lambda r: (r, 0)),
        out_shape=jax.ShapeDtypeStruct((l, hid), F32),
        compiler_params=_params(("parallel",)),
        name="hyena_filter_hidden",
    )(z, w1, fb1.astype(F32).reshape(1, hid), fw2.astype(F32), fb2.astype(F32).reshape(1, hid), ffreq.astype(F32))
    nt = d // tn
    w3 = fw3.astype(F32)
    return pl.pallas_call(
        functools.partial(_filter_out_kernel, rows=rows),
        grid=(HY_ORDER, nt),
        in_specs=[_resident((l, hid)), _resident((l, 1)),
                  pl.BlockSpec((hid, tn), lambda o, j: (0, (2 * o) * nt + j)),
                  pl.BlockSpec((hid, tn), lambda o, j: (0, (2 * o + 1) * nt + j)),
                  pl.BlockSpec((1, tn), lambda o, j: (0, j))],
        out_specs=pl.BlockSpec((1, 2, l, tn), lambda o, j: (o, 0, 0, j)),
        out_shape=jax.ShapeDtypeStruct((HY_ORDER, 2, l, d), F32),
        compiler_params=_params(("parallel", "parallel")),
        name="hyena_filters",
    )(hidden, t.reshape(l, 1), w3, w3, deltas)


def _filter_spectrum_kernel(hf_ref, hb_ref, f1_ref, f2_ref, o_ref, s_ref, *, p):
    na, nb = p["na"], p["nb"]
    half = na // 2

    def gather(tb):
        fwd = hf_ref[pl.ds(tb, half, stride=nb), :]
        bwd = hb_ref[pl.ds(jnp.where(tb == 0, 0, nb - tb), half, stride=nb), :]
        return jnp.concatenate([fwd, bwd], axis=0).astype(BF16)

    _fft_stage1(gather, f1_ref, s_ref, p)

    def body(i, carry):
        spec = jnp.dot(f2_ref[...], _slab_pair(s_ref, i, p).astype(BF16), preferred_element_type=F32)
        r = pl.multiple_of(i * 4 * nb, SUBLANES)
        o_ref[pl.ds(r, 2 * nb), :] = spec[:, :LANES].astype(BF16)
        o_ref[pl.ds(r + 2 * nb, 2 * nb), :] = spec[:, LANES:].astype(BF16)
        return carry

    lax.fori_loop(0, p["n_pairs"], body, 0, unroll=p["pair_unroll"])


def hyena_filter_spectrum(filt, tables):
    p, _, f1k, f2, _, _ = tables
    order, _, l, d = filt.shape
    rows = 2 * p["n_pairs"] * 2 * p["nb"]
    return pl.pallas_call(
        functools.partial(_filter_spectrum_kernel, p=p),
        grid=(order, d // LANES),
        in_specs=[pl.BlockSpec((None, None, l, LANES), lambda o, j: (o, 0, 0, j)),
                  pl.BlockSpec((None, None, l, LANES), lambda o, j: (o, 1, 0, j)),
                  _resident(f1k.shape), _resident(f2.shape)],
        out_specs=pl.BlockSpec((None, rows, LANES), lambda o, j: (o, 0, j)),
        out_shape=jax.ShapeDtypeStruct((order, rows, d), BF16),
        scratch_shapes=[pltpu.VMEM((p["nsp"] * p["pitch"], LANES), F32)],
        compiler_params=_params(("parallel", "parallel")),
        name="hyena_filter_spectrum",
    )(filt, filt, jnp.asarray(f1k).astype(BF16), jnp.asarray(f2).astype(BF16))


def _longconv_kernel(a_ref, m_ref, k_ref, bias_ref, f1_ref, f2_ref, g2_ref, g1_ref, o_ref, s_ref, y_ref, *, p):
    na, nb, nsp, pitch = p["na"], p["nb"], p["nsp"], p["pitch"]
    half = na // 2

    _fft_stage1(lambda tb: a_ref[pl.ds(tb, half, stride=nb), :].astype(BF16), f1_ref, s_ref, p)

    def mid(i, carry):
        x = jnp.dot(f2_ref[...], _slab_pair(s_ref, i, p).astype(BF16), preferred_element_type=F32)
        r = pl.multiple_of(i * 4 * nb, SUBLANES)
        kk = jnp.concatenate([k_ref[pl.ds(r, 2 * nb), :], k_ref[pl.ds(r + 2 * nb, 2 * nb), :]], axis=1).astype(F32)
        xr, xi, kr, ki = x[:nb], x[nb:], kk[:nb], kk[nb:]
        y = jnp.concatenate([xr * kr - xi * ki, xr * ki + xi * kr], axis=0).astype(BF16)
        bq = jnp.dot(g2_ref[...], y, preferred_element_type=F32)
        r0 = pl.multiple_of(2 * i * pitch, SUBLANES)
        r1 = pl.multiple_of(2 * i * pitch + pitch, SUBLANES)
        s_ref[pl.ds(r0, 2 * nb), :] = bq[:, :LANES]
        s_ref[pl.ds(r1, 2 * nb), :] = bq[:, LANES:]
        return carry

    lax.fori_loop(0, p["n_pairs"], mid, 0, unroll=p["pair_unroll"])

    def last(tb, carry):
        bq = jnp.concatenate([s_ref[pl.ds(tb, nsp, stride=pitch), :], s_ref[pl.ds(nb + tb, nsp, stride=pitch), :]], axis=0)
        y_ref[pl.ds(tb, half, stride=nb + SLAB_PAD), :] = jnp.dot(g1_ref[tb], bq.astype(BF16), preferred_element_type=F32)
        return carry

    lax.fori_loop(0, nb, last, 0, unroll=p["tb_unroll"])

    def finish(ta, carry):
        r = pl.multiple_of(ta * nb, SUBLANES)
        conv = y_ref[pl.ds(pl.multiple_of(ta * (nb + SLAB_PAD), SUBLANES), nb), :]
        o_ref[pl.ds(r, nb), :] = (m_ref[pl.ds(r, nb), :] * (conv + bias_ref[...] * a_ref[pl.ds(r, nb), :])).astype(o_ref.dtype)
        return carry

    lax.fori_loop(0, half, finish, 0, unroll=min(half, 8))


def hyena_longconv(a, a_col, m, m_col, kspec, bias, tables, out_dtype=F32):
    p, f1, _, f2, g2, g1 = tables
    b, l, _ = a.shape
    d = D_MODEL
    nt = d // LANES
    rows = kspec.shape[0]
    tabs = [jnp.asarray(t).astype(BF16) for t in (f1, f2, g2, g1)]
    return pl.pallas_call(
        functools.partial(_longconv_kernel, p=p),
        grid=(nt, b),
        in_specs=[pl.BlockSpec((None, l, LANES), lambda j, i: (i, 0, a_col * nt + j)),
                  pl.BlockSpec((None, l, LANES), lambda j, i: (i, 0, m_col * nt + j)),
                  pl.BlockSpec((rows, LANES), lambda j, i: (0, j)),
                  pl.BlockSpec((1, LANES), lambda j, i: (0, j))] + [_resident(t.shape) for t in tabs],
        out_specs=pl.BlockSpec((None, l, LANES), lambda j, i: (i, 0, j)),
        out_shape=jax.ShapeDtypeStruct((b, l, d), out_dtype),
        scratch_shapes=[pltpu.VMEM((p["nsp"] * p["pitch"], LANES), F32),
                        pltpu.VMEM((p["na"] // 2 * (p["nb"] + SLAB_PAD), LANES), F32)],
        compiler_params=_params(("parallel", "parallel")),
        name="hyena_longconv",
    )(a, m, kspec, bias.astype(F32).reshape(1, d), *tabs)


def hyena_core(pc, fw1, fb1, fw2, fb2, fw3, ffreq, hbias):
    l = pc.shape[1]
    tables = _fft_tables(l)
    filt = hyena_filters(l, fw1, fb1, fw2, fb2, fw3, ffreq)
    kspec = hyena_filter_spectrum(filt, tables)
    z = hyena_longconv(pc, 0, pc, 1, kspec[0], hbias[0], tables)
    return hyena_longconv(z, 0, pc, 2, kspec[1], hbias[1], tables, out_dtype=BF16)


def kernel(x, c, ctx, c_ctx, ada_w, ada_b, norm_g, ffn_w_gate, ffn_w_up, ffn_w_down, ssd_w_in, ssd_conv_w, ssd_conv_b, ssd_a_log, ssd_dt_bias, ssd_d, ssd_norm_g, ssd_w_out, pool_w, pool_b, pool_scale, hy_w_in, hy_conv_w, hy_conv_b, hy_filt_w1, hy_filt_b1, hy_filt_w2, hy_filt_b2, hy_filt_w3, hy_filt_freq, hy_bias, hy_w_out, final_g):
    batch = x.shape[0]
    d = D_MODEL
    h = ctx

    s = jnp.concatenate([jax.nn.silu(c), jax.nn.silu(c_ctx)[None], jnp.zeros((7 - batch, d), F32)], axis=0)
    mods = ada_modulation(s, ada_w, ada_b).reshape(DEPTH, 8, N_MOD, d)

    wg_bf, wu_bf, wd_bf = ffn_w_gate.astype(BF16), ffn_w_up.astype(BF16), ffn_w_down.astype(BF16)
    fg = final_g.reshape(1, d)

    for i in range(DEPTH):
        kind = i % N_MIXERS
        j = i // N_MIXERS
        last = i == DEPTH - 1
        ctx_in_needed = (not last) or kind == 0
        m = [mods[i, :batch, k].reshape(batch, 1, 1, d) for k in range(N_MOD)]
        mc = [jnp.broadcast_to(mods[i, batch, k].reshape(1, 1, 1, d), (batch, 1, 1, d)) for k in range(N_MOD)]
        g0, g1, g2 = (norm_g[i, k].reshape(1, d) for k in range(3))

        x = ffn_step(x, m[0], m[1], m[2], g0, wg_bf[i, 0], wu_bf[i, 0], wd_bf[i, 0])
        if ctx_in_needed:
            h = ffn_step(h, mc[0], mc[1], mc[2], g0, wg_bf[i, 0], wu_bf[i, 0], wd_bf[i, 0])

        if kind == 0:
            x, h_new = ssd_mixer(x, h, (m[3], m[4], m[5]), (mc[3], mc[4], mc[5]), g1, ssd_w_in[j], ssd_conv_w[j],
                                 ssd_conv_b[j], ssd_a_log[j], ssd_dt_bias[j], ssd_d[j], ssd_norm_g[j], ssd_w_out[j], not last)
            if not last:
                h = h_new
        elif kind == 1:
            x = pool_mixer(x, m[3], m[4], m[5], g1, pool_w[j], pool_b[j].reshape(-1), pool_scale[j], GRID_W)
            if not last:
                h = pool_mixer(h, mc[3], mc[4], mc[5], g1, pool_w[j], pool_b[j].reshape(-1), pool_scale[j], h.shape[1])
        else:
            w_in = hy_w_in[j].astype(BF16)
            w_out = hy_w_out[j].astype(BF16)
            filt = (hy_filt_w1[j], hy_filt_b1[j], hy_filt_w2[j], hy_filt_b2[j], hy_filt_w3[j], hy_filt_freq[j])
            conv = {0: (hy_conv_w[j], hy_conv_b[j], False)}
            (p_lat,) = modulate_project(x, m[3], m[4], g1, [w_in], tm=512, conv=conv)
            x = project_residual(x, hyena_core(p_lat, *filt, hy_bias[j]), m[5], w_out)
            if not last:
                (p_ctx,) = modulate_project(h, mc[3], mc[4], g1, [w_in], conv=conv)
                h = project_residual(h, hyena_core(p_ctx, *filt, hy_bias[j]), mc[5], w_out)

        x = ffn_step(x, m[6], m[7], m[8], g2, wg_bf[i, 1], wu_bf[i, 1], wd_bf[i, 1],
                     final_g=fg if last else None)
        if not last:
            h = ffn_step(h, mc[6], mc[7], mc[8], g2, wg_bf[i, 1], wu_bf[i, 1], wd_bf[i, 1])
    return x
```

```python
import functools
import math

import jax
import jax.numpy as jnp
from jax import lax
from jax.experimental import pallas as pl
from jax.experimental.pallas import tpu as pltpu

F32 = jnp.float32
BF16 = jnp.bfloat16
HIGHEST = lax.Precision.HIGHEST

D_MODEL = 1024
DEPTH = 4
GRID_W = 64
N_MIXERS = 3
D_FF = 2816
N_MOD = 9
EPS = 1e-6
LOG2_E = 1.4426950408889634

SSD_D_INNER = 2 * D_MODEL
SSD_HEAD_DIM = 64
SSD_N_HEADS = SSD_D_INNER // SSD_HEAD_DIM
SSD_N_GROUPS = 4
SSD_HEADS_PER_GROUP = SSD_N_HEADS // SSD_N_GROUPS
SSD_D_STATE = 128
SSD_CHUNK = 128
SSD_GN = SSD_N_GROUPS * SSD_D_STATE
SSD_CONV_DIM = SSD_D_INNER + 2 * SSD_GN
SSD_GROUP_WIDTH = SSD_D_INNER // SSD_N_GROUPS

POOL_WINDOWS = (2, 4, 8, 16)
POOL_GROUPS = 4
POOL_GROUP_DIM = D_MODEL // POOL_GROUPS

HY_ORDER = 2
HY_EMB_DIM = 33
HY_BANDS = (HY_EMB_DIM - 1) // 2
HY_MAX_DECAY = math.log(1e-2) / 0.3
HY_MIN_DECAY = math.log(1e-2) / 1.5

VMEM_LIMIT_BYTES = 56 * 1024 * 1024
SUBLANES = 8
LANES = 128


def _params(sem):
    return pltpu.CompilerParams(dimension_semantics=sem, vmem_limit_bytes=VMEM_LIMIT_BYTES)


def _resident(shape):
    nd = len(shape)
    return pl.BlockSpec(shape, lambda *_: (0,) * nd, pipeline_mode=pl.Buffered(1))


def _mod_spec(d, _unused=0):
    return pl.BlockSpec((1, 1, 1, d), lambda i, j: (i, 0, 0, 0))


def _modulated(x, g, shift, scale):
    ms = jnp.mean(x * x, axis=-1, keepdims=True)
    return (x * lax.rsqrt(ms + EPS)) * (g * (1.0 + scale)) + shift


def _silu(v):
    return v * jax.nn.sigmoid(v)


def _ada_kernel(s_ref, w_ref, b_ref, o_ref):
    o_ref[0] = jnp.dot(s_ref[...], w_ref[0], preferred_element_type=F32, precision=HIGHEST) + b_ref[0]


def ada_modulation(s, ada_w, ada_b, tn=2304):
    depth, d, n = ada_w.shape
    r = s.shape[0]
    return pl.pallas_call(
        _ada_kernel,
        grid=(depth, n // tn),
        in_specs=[pl.BlockSpec((r, d), lambda i, j: (0, 0)),
                  pl.BlockSpec((1, d, tn), lambda i, j: (i, 0, j)),
                  pl.BlockSpec((1, 1, tn), lambda i, j: (i, 0, j))],
        out_specs=pl.BlockSpec((1, r, tn), lambda i, j: (i, 0, j)),
        out_shape=jax.ShapeDtypeStruct((depth, r, n), F32),
        compiler_params=_params(("parallel", "parallel")),
        name="ada_modulation",
    )(s, ada_w, ada_b.reshape(depth, 1, n))


def _ffn_kernel(x_ref, sh_ref, sc_ref, gt_ref, g_ref, wg_ref, wu_ref, wd_ref, *rest, f_chunk, final):
    if final:
        fg_ref, o_ref = rest
    else:
        (o_ref,) = rest
    x = x_ref[0]
    u = _modulated(x, g_ref[...], sh_ref[0, 0], sc_ref[0, 0]).astype(BF16)
    acc = jnp.zeros(x.shape, F32)
    d_ff = wg_ref.shape[1]
    for f0 in range(0, d_ff, f_chunk):
        a = jnp.dot(u, wg_ref[:, f0:f0 + f_chunk], preferred_element_type=F32)
        b = jnp.dot(u, wu_ref[:, f0:f0 + f_chunk], preferred_element_type=F32)
        h = (_silu(a) * b).astype(BF16)
        acc = acc + jnp.dot(h, wd_ref[f0:f0 + f_chunk, :], preferred_element_type=F32)
    y = x + (0.5 * gt_ref[0, 0]) * acc
    if final:
        ms = jnp.mean(y * y, axis=-1, keepdims=True)
        y = y * lax.rsqrt(ms + EPS) * fg_ref[...]
    o_ref[0] = y


def ffn_step(x, shift, scale, gate, g, wg, wu, wd, final_g=None, tm=512, f_chunk=256):
    b, l, d = x.shape
    tm = min(tm, l)
    f = wg.shape[1]
    final = final_g is not None
    mod_spec = _mod_spec(d, 0)
    in_specs = [pl.BlockSpec((1, tm, d), lambda i, j: (i, j, 0)),
                mod_spec, mod_spec, mod_spec,
                _resident((1, d)), _resident((d, f)), _resident((d, f)), _resident((f, d))]
    args = [x, shift, scale, gate, g, wg, wu, wd]
    if final:
        in_specs.append(_resident((1, d)))
        args.append(final_g)
    return pl.pallas_call(
        functools.partial(_ffn_kernel, f_chunk=f_chunk, final=final),
        grid=(b, l // tm),
        in_specs=in_specs,
        out_specs=pl.BlockSpec((1, tm, d), lambda i, j: (i, j, 0)),
        out_shape=jax.ShapeDtypeStruct((b, l, d), F32),
        compiler_params=_params(("parallel", "parallel")),
        name="ffn_step",
    )(*args)


CONV_COLS = 512


def _modproj_kernel(*refs, n_w, n_dest, conv, n_tiles):
    if conv:
        prev_ref, x_ref, next_ref, sh_ref, sc_ref, g_ref, *rest = refs
    else:
        x_ref, sh_ref, sc_ref, g_ref, *rest = refs
    w_refs = rest[:n_w]
    cw_refs = rest[n_w:n_w + 2 * len(conv)]
    o_refs = rest[n_w + 2 * len(conv) + n_dest:]
    g, sh, sc = g_ref[...], sh_ref[0, 0], sc_ref[0, 0]
    u = _modulated(x_ref[0], g, sh, sc)
    if n_w == 0:
        o_refs[0][0] = u
        return
    u = u.astype(BF16)
    tm = x_ref.shape[1]
    if conv:
        ue = _modulated(jnp.concatenate([prev_ref[0], x_ref[0], next_ref[0]], axis=0), g, sh, sc).astype(BF16)
        j = pl.program_id(1)
        top = (j > 0).astype(F32)
        bot = (j < n_tiles - 1).astype(F32)
    for idx, (w_ref, o_ref) in enumerate(zip(w_refs, o_refs)):
        if idx not in conv:
            o_ref[0] = jnp.dot(u, w_ref[...], preferred_element_type=F32).astype(o_ref.dtype)
            continue
        k_w, silu = conv[idx]
        cw_ref, cb_ref = cw_refs[2 * list(conv).index(idx)], cw_refs[2 * list(conv).index(idx) + 1]
        p = k_w // 2
        rows = tm + 2 * SUBLANES
        for c0 in range(0, w_ref.shape[1], CONV_COLS):
            ce = jnp.dot(ue, w_ref[:, c0:c0 + CONV_COLS], preferred_element_type=F32)
            ce = jnp.concatenate([ce[:SUBLANES] * top, ce[SUBLANES:SUBLANES + tm], ce[SUBLANES + tm:] * bot], axis=0)
            acc = jnp.broadcast_to(cb_ref[:, c0:c0 + CONV_COLS], (tm, CONV_COLS))
            for k in range(k_w):
                tap = ce if k == p else pltpu.roll(ce, (p - k) % rows, axis=0)
                acc = acc + cw_ref[k:k + 1, c0:c0 + CONV_COLS] * tap[SUBLANES:SUBLANES + tm, :]
            o_ref[0, :, c0:c0 + CONV_COLS] = (_silu(acc) if silu else acc).astype(o_ref.dtype)


def modulate_project(x, shift, scale, g, weights, tm=256, out_dtypes=None, out_rows=None, row_block=0, dest=None,
                     conv=None):
    b, l, d = x.shape
    tm = min(tm, l)
    conv = dict(conv or {})
    mod_spec = _mod_spec(d)
    x_spec = pl.BlockSpec((1, tm, d), lambda i, j: (i, j, 0))
    args = [x]
    in_specs = [x_spec]
    if conv:
        rb, last = tm // SUBLANES, l // SUBLANES - 1
        in_specs = [pl.BlockSpec((1, SUBLANES, d), lambda i, j: (i, jnp.maximum(j * rb - 1, 0), 0)), x_spec,
                    pl.BlockSpec((1, SUBLANES, d), lambda i, j: (i, jnp.minimum((j + 1) * rb, last), 0))]
        args = [x, x, x]
    in_specs += [mod_spec, mod_spec, _resident((1, d))] + [_resident(w.shape) for w in weights]
    args += [shift, scale, g, *weights]
    for i in conv:
        cw, cb, _ = conv[i]
        assert weights[i].shape[1] % CONV_COLS == 0
        in_specs += [_resident(cw.shape), _resident((1, cb.shape[0]))]
        args += [cw.astype(F32), cb.astype(F32).reshape(1, -1)]
    widths = [w.shape[1] for w in weights] or [d]
    out_dtypes = out_dtypes or [F32] * len(widths)
    dest = list(dest or [])
    in_specs += [pl.BlockSpec(memory_space=pl.ANY)] * len(dest)
    n_in = len(args)
    return pl.pallas_call(
        functools.partial(_modproj_kernel, n_w=len(weights), n_dest=len(dest),
                          conv={i: (c[0].shape[0], c[2]) for i, c in conv.items()}, n_tiles=l // tm),
        grid=(b, l // tm),
        in_specs=in_specs,
        out_specs=[pl.BlockSpec((1, tm, n), lambda i, j: (i, j + row_block, 0)) for n in widths],
        out_shape=[jax.ShapeDtypeStruct((b, out_rows or l, n), dt) for n, dt in zip(widths, out_dtypes)],
        input_output_aliases={n_in + k: k for k in range(len(dest))},
        compiler_params=_params(("parallel", "parallel")),
        name="modulate_project",
    )(*args, *dest)


def _outproj_kernel(x_ref, a_ref, gt_ref, w_ref, o_ref):
    y = jnp.dot(a_ref[0].astype(BF16), w_ref[...], preferred_element_type=F32)
    o_ref[0] = x_ref[0] + gt_ref[0, 0] * y


def project_residual(x, a, gate, w, tm=512):
    b, l, d = x.shape
    k = a.shape[-1]
    tm = min(tm, l)
    return pl.pallas_call(
        _outproj_kernel,
        grid=(b, l // tm),
        in_specs=[pl.BlockSpec((1, tm, d), lambda i, j: (i, j, 0)),
                  pl.BlockSpec((1, tm, k), lambda i, j: (i, j, 0)),
                  _mod_spec(d, 0),
                  _resident((k, d))],
        out_specs=pl.BlockSpec((1, tm, d), lambda i, j: (i, j, 0)),
        out_shape=jax.ShapeDtypeStruct((b, l, d), F32),
        compiler_params=_params(("parallel", "parallel")),
        name="project_residual",
    )(x, a, gate, w)


def _bf16_terms(v):
    hi = v.astype(BF16)
    r = v - hi.astype(F32)
    mid = r.astype(BF16)
    return hi, mid, (r - mid.astype(F32)).astype(BF16)


def _dot_f32_rhs(m, v):
    return sum(jnp.dot(m, t, preferred_element_type=F32) for t in _bf16_terms(v))


def _dot_f32_lhs(v, m):
    return sum(jnp.dot(t, m, preferred_element_type=F32) for t in _bf16_terms(v))


def _dot_split(a, b):
    a_hi, a_lo, _ = _bf16_terms(a)
    b_hi, b_lo, _ = _bf16_terms(b)
    return (jnp.dot(a_hi, b_hi, preferred_element_type=F32) + jnp.dot(a_lo, b_hi, preferred_element_type=F32)
            + jnp.dot(a_hi, b_lo, preferred_element_type=F32))


def _ssd_scan_kernel(xf_ref, dtf_ref, dttf_ref, xb_ref, dtb_in_ref, dttb_ref, dtb_ref, dtbt_ref, a_ref, at_ref, ex_ref,
                     of_ref, ob_ref, h_ref):
    @pl.when(pl.program_id(1) == 0)
    def _():
        h_ref[...] = jnp.zeros(h_ref.shape, F32)

    consts = (dtb_ref, dtbt_ref, a_ref, at_ref, ex_ref)
    _ssd_chunk(xf_ref, dtf_ref, dttf_ref, *consts, of_ref, h_ref.at[0], True)
    _ssd_chunk(xb_ref, dtb_in_ref, dttb_ref, *consts, ob_ref, h_ref.at[1], False)


def _ssd_chunk(xbc_ref, dt_ref, dtt_ref, dtb_ref, dtbt_ref, a_ref, at_ref, ex_ref, o_ref, h_ref, fwd):
    nh, q, hp = SSD_N_HEADS, SSD_CHUNK, SSD_HEAD_DIM
    heads = slice(0, nh) if fwd else slice(nh, 2 * nh)
    dt = jax.nn.softplus(dt_ref[0][:, heads] + dtb_ref[:, heads])
    a = dt * a_ref[:, heads]
    dt_t = jax.nn.softplus(dtt_ref[0][heads] + dtbt_ref[heads])
    a_t = dt_t * at_ref[heads]

    r_i = lax.broadcasted_iota(jnp.int32, (q, q), 0)
    c_i = lax.broadcasted_iota(jnp.int32, (q, q), 1)
    seen = (c_i <= r_i) if fwd else (c_i >= r_i)
    seen_t = (r_i <= c_i) if fwd else (r_i >= c_i)
    cum = _dot_f32_rhs(seen.astype(BF16), a)
    cum_t = _dot_f32_lhs(a_t, seen_t.astype(BF16))
    total = cum[q - 1:q, :] if fwd else cum[0:1, :]
    ecum = jnp.exp(cum).astype(BF16)
    dtdec = (dt * jnp.exp(total - cum)).astype(BF16)
    etot = jnp.exp(total)
    cum2 = cum * LOG2_E
    row2 = cum_t * LOG2_E - jnp.log2(dt_t)

    pw = 2 * hp
    left = lax.broadcasted_iota(jnp.int32, (q, pw), 1) < hp
    left_row = lax.broadcasted_iota(jnp.int32, (1, pw), 1) < hp
    pairs = SSD_HEADS_PER_GROUP // 2
    for g in range(SSD_N_GROUPS):
        b_g = xbc_ref[0, :, SSD_D_INNER + g * SSD_D_STATE:SSD_D_INNER + (g + 1) * SSD_D_STATE].astype(BF16)
        c_lo = SSD_D_INNER + SSD_GN + g * SSD_D_STATE
        c_g = xbc_ref[0, :, c_lo:c_lo + SSD_D_STATE].astype(BF16)
        cb = lax.dot_general(c_g, b_g, (((1,), (1,)), ((), ())), preferred_element_type=F32)
        h_g = h_ref[g]
        ex_g = ex_ref[:, g * SSD_GROUP_WIDTH:(g + 1) * SSD_GROUP_WIDTH]
        ch = (jnp.dot(c_g, h_g.astype(BF16), preferred_element_type=F32)
              * jnp.dot(ecum, ex_g, preferred_element_type=F32))
        dtdec_g = jnp.dot(dtdec, ex_g, preferred_element_type=F32)
        xdecs, etots = [], []
        for pr in range(pairs):
            e1 = g * SSD_HEADS_PER_GROUP + 2 * pr
            e2 = e1 + 1
            x_pair = xbc_ref[0, :, e1 * hp:e1 * hp + pw].astype(F32)
            ws = []
            for e in (e1, e2):
                seg2 = cum2[:, e:e + 1] - row2[e:e + 1, :]
                ws.append((cb * jnp.exp2(jnp.where(seen, seg2, -jnp.inf))).astype(BF16))
            rhs = jnp.concatenate([jnp.where(left, x_pair, 0.0), jnp.where(left, 0.0, x_pair)], axis=0).astype(BF16)
            y_diag = jnp.dot(jnp.concatenate(ws, axis=1), rhs, preferred_element_type=F32)
            y_off = ch[:, pr * pw:(pr + 1) * pw]
            o_ref[0, 0, :, e1 * hp:e1 * hp + pw] = (y_diag + y_off).astype(o_ref.dtype)
            xdecs.append((x_pair * dtdec_g[:, pr * pw:(pr + 1) * pw]).astype(BF16))
            etots.append(jnp.where(left_row, etot[:, e1:e1 + 1], etot[:, e2:e2 + 1]))
        s_g = lax.dot_general(b_g, jnp.concatenate(xdecs, axis=1), (((0,), (0,)), ((), ())),
                              preferred_element_type=F32)
        h_ref[g] = h_g * jnp.concatenate(etots, axis=1) + s_g


def ssd_scan(xbc, dt_raw, dt_bias, a_log, n_lead_chunks):
    b, t, _ = xbc.shape
    q, nh = SSD_CHUNK, SSD_N_HEADS
    nc = t // q
    dt_t = jnp.swapaxes(dt_raw, 1, 2)
    a_neg = -jnp.exp(a_log.astype(F32)).reshape(1, 2 * nh)
    dtb = dt_bias.astype(F32).reshape(1, 2 * nh)
    expand = jnp.repeat(jnp.eye(nh, dtype=BF16), SSD_HEAD_DIM, axis=1)

    n_main = nc - n_lead_chunks

    def chunk_of(d, c):
        if d:
            return jnp.where(c < n_lead_chunks, nc - 1 - c, n_main - 1 - (c - n_lead_chunks))
        return jnp.where(c < n_lead_chunks, n_main + c, c - n_lead_chunks)

    def token_specs(d):
        return [pl.BlockSpec((1, q, SSD_CONV_DIM), lambda i, c: (i, chunk_of(d, c), 0)),
                pl.BlockSpec((1, q, 2 * nh), lambda i, c: (i, chunk_of(d, c), 0)),
                pl.BlockSpec((1, 2 * nh, q), lambda i, c: (i, 0, chunk_of(d, c)))]

    y_f, y_b = pl.pallas_call(
        _ssd_scan_kernel,
        grid=(b, nc),
        in_specs=token_specs(0) + token_specs(1) + [
            _resident((1, 2 * nh)), _resident((2 * nh, 1)),
            _resident((1, 2 * nh)), _resident((2 * nh, 1)), _resident((nh, SSD_D_INNER))],
        out_specs=[pl.BlockSpec((1, 1, q, SSD_D_INNER), lambda i, c, d=d: (i, 0, chunk_of(d, c), 0)) for d in (0, 1)],
        out_shape=[jax.ShapeDtypeStruct((b, 1, t, SSD_D_INNER), BF16)] * 2,
        scratch_shapes=[pltpu.VMEM((2, SSD_N_GROUPS, SSD_D_STATE, SSD_GROUP_WIDTH), F32)],
        compiler_params=_params(("parallel", "arbitrary")),
        name="ssd_scan",
    )(xbc, dt_raw, dt_t, xbc, dt_raw, dt_t, dtb, dtb.reshape(2 * nh, 1), a_neg, a_neg.reshape(2 * nh, 1), expand)
    return y_f, y_b


def _ssd_out_kernel(x_ref, y0_ref, y1_ref, xs0_ref, xs1_ref, z_ref, dsk_ref, ng_ref, gt_ref, w_ref, o_ref):
    xs = jnp.concatenate([xs0_ref[0], xs1_ref[0]], axis=1).astype(F32)
    y = y0_ref[0, 0].astype(F32) + y1_ref[0, 0].astype(F32) + dsk_ref[...] * xs
    gy = y * _silu(z_ref[0].astype(F32))
    parts = []
    for g in range(SSD_N_GROUPS):
        blk = gy[:, g * SSD_GROUP_WIDTH:(g + 1) * SSD_GROUP_WIDTH]
        parts.append(blk * lax.rsqrt(jnp.mean(blk * blk, axis=-1, keepdims=True) + EPS))
    a = (jnp.concatenate(parts, axis=1) * ng_ref[...]).astype(BF16)
    o_ref[0] = x_ref[0] + gt_ref[0, 0] * jnp.dot(a, w_ref[...], preferred_element_type=F32)


def ssd_gate_project(x, y2, xbc, z, d_skip, norm_g, gate, w_out, src_block, tm=256):
    b, t, d = x.shape
    half = SSD_D_INNER // 2
    dsk = jnp.repeat(d_skip.astype(F32), SSD_HEAD_DIM).reshape(1, SSD_D_INNER)
    return pl.pallas_call(
        _ssd_out_kernel,
        grid=(b, t // tm),
        in_specs=[pl.BlockSpec((1, tm, d), lambda i, j: (i, j, 0)),
                  pl.BlockSpec((1, 1, tm, SSD_D_INNER), lambda i, j: (i, 0, j + src_block, 0)),
                  pl.BlockSpec((1, 1, tm, SSD_D_INNER), lambda i, j: (i, 0, j + src_block, 0)),
                  pl.BlockSpec((1, tm, half), lambda i, j: (i, j + src_block, 0)),
                  pl.BlockSpec((1, tm, half), lambda i, j: (i, j + src_block, 1)),
                  pl.BlockSpec((1, tm, SSD_D_INNER), lambda i, j: (i, j + src_block, 0)),
                  _resident((1, SSD_D_INNER)), _resident((1, SSD_D_INNER)),
                  _mod_spec(d),
                  _resident((SSD_D_INNER, d))],
        out_specs=pl.BlockSpec((1, tm, d), lambda i, j: (i, j, 0)),
        out_shape=jax.ShapeDtypeStruct((b, t, d), F32),
        compiler_params=_params(("parallel", "parallel")),
        name="ssd_gate_project",
    )(x, y2[0], y2[1], xbc, xbc, z, dsk, norm_g.astype(F32).reshape(1, SSD_D_INNER), gate, w_out)


def ssd_mixer(x, h, m_lat, m_ctx, g1, w_in, conv_w, conv_b, a_log, dt_bias, d_skip, norm_g, w_out, need_ctx_out,
              tm_proj=512):
    lc, l = h.shape[1], x.shape[1]
    tm_ctx = min(tm_proj, lc)
    assert l % tm_proj == 0 and l % tm_ctx == 0
    w_bf = w_in.astype(BF16)
    ws = [w_bf[:, :SSD_D_INNER], w_bf[:, SSD_D_INNER:SSD_D_INNER + SSD_CONV_DIM], w_bf[:, SSD_D_INNER + SSD_CONV_DIM:]]
    dts = [BF16, BF16, F32]
    conv = {1: (conv_w, conv_b, True)}
    parts = modulate_project(x, m_lat[0], m_lat[1], g1, ws, tm=tm_proj, out_dtypes=dts, out_rows=lc + l, conv=conv)
    z, xbc, dt_raw = modulate_project(h, m_ctx[0], m_ctx[1], g1, ws, tm=tm_ctx, out_dtypes=dts, out_rows=lc + l,
                                      row_block=l // tm_ctx, dest=parts, conv=conv)
    y2 = ssd_scan(xbc, dt_raw, dt_bias, a_log, lc // SSD_CHUNK)
    w_out = w_out.astype(BF16)
    x_new = ssd_gate_project(x, y2, xbc, z, d_skip, norm_g, m_lat[2], w_out, 0, tm=tm_proj)
    h_new = (ssd_gate_project(h, y2, xbc, z, d_skip, norm_g, m_ctx[2], w_out, l // tm_ctx, tm=tm_ctx)
             if need_ctx_out else None)
    return x_new, h_new


POOL_CHUNK = 256
POOL_MAX_HALF = max(POOL_WINDOWS) // 2


def _pool_band_tables(width):
    import numpy as np
    pos = np.arange(POOL_CHUNK)
    line, col = pos // width, pos % width
    out = []
    for w in POOL_WINDOWS:
        inside = (col[None, :] >= col[:, None] - w // 2) & (col[None, :] <= col[:, None] + (w - w // 2) - 1)
        out.append((inside & (line[None, :] == line[:, None])).astype(np.float32))
    return np.stack(out)


def _pool_kernel(u_ref, band_ref, o_ref, cs_ref, *, width, n_lines):
    l, c = u_ref.shape
    n_chunks = l // POOL_CHUNK
    assert POOL_WINDOWS == tuple(2 << g for g in range(POOL_GROUPS)) and width & (width - 1) == 0
    half = lax.shift_left(jnp.int32(1), pl.program_id(1) // (POOL_GROUP_DIM // LANES))
    log_w = width.bit_length() - 1
    pad = POOL_MAX_HALF * width if n_lines > 1 else 0
    if pad:
        cs_ref[0:pad, :] = jnp.zeros((pad, c), F32)
        cs_ref[pad + l:2 * pad + l, :] = jnp.zeros((pad, c), F32)

    def col_pass(i, carry):
        r = pl.multiple_of(i * POOL_CHUNK, POOL_CHUNK)
        u = u_ref[pl.ds(r, POOL_CHUNK), :]
        hi = u.astype(BF16)
        lo = (u - hi.astype(F32)).astype(BF16)
        both = jnp.dot(band_ref[0], jnp.concatenate([hi, lo], axis=1), preferred_element_type=F32)
        cs_ref[pl.ds(pad + r, POOL_CHUNK), :] = both[:, :c] + both[:, c:]
        return carry

    lax.fori_loop(0, n_chunks, col_pass, 0)

    pos0 = lax.broadcasted_iota(jnp.int32, (POOL_CHUNK, c), 0)
    col = pos0 & (width - 1)
    inv_cnt = 1.0 / (jnp.minimum(col + half, width) - jnp.maximum(col - half, 0)).astype(F32)
    line0 = lax.shift_right_logical(pos0, log_w)

    def out_pass(i, carry):
        r = pl.multiple_of(i * POOL_CHUNK, POOL_CHUNK)
        if n_lines > 1:
            def add_line(k, acc):
                return acc + cs_ref[pl.ds(pl.multiple_of(pad + r + k * width, SUBLANES), POOL_CHUNK), :]
            s = lax.fori_loop(-half, half, add_line, jnp.zeros((POOL_CHUNK, c), F32))
            line = line0 + lax.shift_right_logical(r, log_w)
            cnt_l = jnp.minimum(line + half, n_lines) - jnp.maximum(line - half, 0)
            mean = s * inv_cnt / cnt_l.astype(F32)
        else:
            mean = cs_ref[pl.ds(r, POOL_CHUNK), :] * inv_cnt
        o_ref[pl.ds(r, POOL_CHUNK), :] = (mean - u_ref[pl.ds(r, POOL_CHUNK), :]).astype(o_ref.dtype)
        return carry

    lax.fori_loop(0, n_chunks, out_pass, 0)


def pool_tokens(u, width):
    b, l, d = u.shape
    n_lines = l // width
    bands = jnp.asarray(_pool_band_tables(width)).astype(BF16)
    tiles_per_group = POOL_GROUP_DIM // LANES
    pad = POOL_MAX_HALF * width if n_lines > 1 else 0
    return pl.pallas_call(
        functools.partial(_pool_kernel, width=width, n_lines=n_lines),
        grid=(b, d // LANES),
        in_specs=[pl.BlockSpec((None, l, LANES), lambda i, j: (i, 0, j)),
                  pl.BlockSpec((1, POOL_CHUNK, POOL_CHUNK), lambda i, j: (j // tiles_per_group, 0, 0))],
        out_specs=pl.BlockSpec((None, l, LANES), lambda i, j: (i, 0, j)),
        out_shape=jax.ShapeDtypeStruct((b, l, d), BF16),
        scratch_shapes=[pltpu.VMEM((l + 2 * pad, LANES), F32)],
        compiler_params=_params(("parallel", "parallel")),
        name="pool_tokens",
    )(u, bands)


def _pool_mix_kernel(x_ref, p_ref, w_ref, b_ref, sc_ref, gt_ref, o_ref):
    ys = [jnp.dot(p_ref[0, :, g * POOL_GROUP_DIM:(g + 1) * POOL_GROUP_DIM], w_ref[g], preferred_element_type=F32)
          for g in range(POOL_GROUPS)]
    y = (jnp.concatenate(ys, axis=1) + b_ref[...]) * sc_ref[...]
    o_ref[0] = x_ref[0] + gt_ref[0, 0] * y


def pool_mix_residual(x, pooled, w, bias, scale, gate, tm=512):
    b, l, d = x.shape
    tm = min(tm, l)
    return pl.pallas_call(
        _pool_mix_kernel,
        grid=(b, l // tm),
        in_specs=[pl.BlockSpec((1, tm, d), lambda i, j: (i, j, 0)),
                  pl.BlockSpec((1, tm, d), lambda i, j: (i, j, 0)),
                  _resident(w.shape), _resident((1, d)), _resident((1, d)), _mod_spec(d, 0)],
        out_specs=pl.BlockSpec((1, tm, d), lambda i, j: (i, j, 0)),
        out_shape=jax.ShapeDtypeStruct((b, l, d), F32),
        compiler_params=_params(("parallel", "parallel")),
        name="pool_mix_residual",
    )(x, pooled, w.astype(BF16), bias.astype(F32).reshape(1, d), scale.astype(F32).reshape(1, d), gate)


def pool_mixer(x, shift, scale_mod, gate, g1, w, bias, scale, width):
    (u,) = modulate_project(x, shift, scale_mod, g1, [])
    return pool_mix_residual(x, pool_tokens(u, width), w, bias, scale, gate)


SLAB_PAD = 8


def _round_up(v, m):
    return (v + m - 1) // m * m


def _fft_plan(l):
    n = 2 * l
    na = 1 << (n.bit_length() // 2)
    nb = n // na
    ns = na // 2 + 1
    n_pairs = (ns + 1) // 2
    return dict(l=l, n=n, na=na, nb=nb, ns=ns, nsp=_round_up(ns + 1, SUBLANES), pitch=2 * nb + SLAB_PAD,
                n_pairs=n_pairs, tb_unroll=min(nb, 16),
                pair_unroll=max(u for u in range(1, 12) if n_pairs % u == 0))


def _fft_tables(l):
    import numpy as np
    p = _fft_plan(l)
    n, na, nb, ns, nsp = p["n"], p["na"], p["nb"], p["ns"], p["nsp"]
    half = na // 2
    ka = np.arange(ns)[None, :, None]
    tb = np.arange(nb)[:, None, None]

    def stage1(ta):
        th = 2.0 * np.pi * (ta[None, None, :] * ka / na + tb * ka / n)
        m = np.zeros((nb, 2 * nsp, ta.shape[0]))
        m[:, :ns] = np.cos(th)
        m[:, nsp:nsp + ns] = -np.sin(th)
        return m

    f1 = stage1(np.arange(half))
    f1k = np.zeros((nb, 2 * nsp, na))
    f1k[:, :, :half] = f1
    f1k[1:, :, half:] = stage1(na - 1 - np.arange(half))[1:]
    tb0 = stage1(na - np.arange(half))[0]
    tb0[:, 0] = 0.0
    f1k[0, :, half:] = tb0

    k2 = np.arange(nb)
    ang = 2.0 * np.pi * np.outer(k2, k2) / nb
    c, s = np.cos(ang), np.sin(ang)
    f2 = np.block([[c, s], [-s, c]])
    g2 = np.block([[c, -s], [s, c]])

    ta = np.arange(half)[None, :, None]
    kk = np.arange(ns)[None, None, :]
    tbb = np.arange(nb)[:, None, None]
    ph = 2.0 * np.pi * (ta * kk / na + tbb * kk / n)
    wgt = np.where((kk == 0) | (kk == na // 2), 1.0, 2.0) / n
    g1 = np.zeros((nb, half, 2 * nsp))
    g1[:, :, :ns] = wgt * np.cos(ph)
    g1[:, :, nsp:nsp + ns] = -wgt * np.sin(ph)
    f32 = np.float32
    return p, f1.astype(f32), f1k.astype(f32), f2.astype(f32), g2.astype(f32), g1.astype(f32)


def _fft_stage1(gather, f1_ref, s_ref, p):
    nb, nsp, pitch = p["nb"], p["nsp"], p["pitch"]

    def body(tb, carry):
        a = jnp.dot(f1_ref[tb], gather(tb), preferred_element_type=F32)
        s_ref[pl.ds(tb, nsp, stride=pitch), :] = a[:nsp]
        s_ref[pl.ds(nb + tb, nsp, stride=pitch), :] = a[nsp:]
        return carry

    lax.fori_loop(0, nb, body, 0, unroll=p["tb_unroll"])


def _slab_pair(s_ref, i, p):
    nb, pitch = p["nb"], p["pitch"]
    r0 = pl.multiple_of(2 * i * pitch, SUBLANES)
    r1 = pl.multiple_of(2 * i * pitch + pitch, SUBLANES)
    return jnp.concatenate([s_ref[pl.ds(r0, 2 * nb), :], s_ref[pl.ds(r1, 2 * nb), :]], axis=1)


FILTER_ROWS = 512


def _filter_hidden_kernel(z_ref, w1_ref, b1_ref, w2_ref, b2_ref, fr_ref, o_ref):
    h = jnp.sin(fr_ref[0:1, :] * (jnp.dot(z_ref[...], w1_ref[...], preferred_element_type=F32, precision=HIGHEST)
                                  + b1_ref[...]))
    o_ref[...] = jnp.sin(fr_ref[1:2, :] * (jnp.dot(h, w2_ref[...], preferred_element_type=F32, precision=HIGHEST)
                                           + b2_ref[...]))


def _filter_out_kernel(h_ref, t_ref, w3f_ref, w3b_ref, dl_ref, o_ref, *, rows):
    l, tn = o_ref.shape[2], o_ref.shape[3]
    wf, wb, dl = w3f_ref[...], w3b_ref[...], dl_ref[...]

    def fill(i, ss):
        r = pl.multiple_of(i * rows, rows)
        h = h_ref[pl.ds(r, rows), :]
        decay = jnp.exp(-t_ref[pl.ds(r, rows), :] * dl)
        hf = _dot_split(h, wf) * decay
        hb = _dot_split(h, wb) * decay
        o_ref[0, 0, pl.ds(r, rows), :] = hf
        o_ref[0, 1, pl.ds(r, rows), :] = hb
        return ss + jnp.sum(hf * hf + hb * hb, axis=0, keepdims=True)

    ss = lax.fori_loop(0, l // rows, fill, jnp.zeros((1, tn), F32))
    scale = lax.rsqrt(ss + EPS)

    def rescale(i, carry):
        r = pl.multiple_of(i * rows, rows)
        o_ref[0, 0, pl.ds(r, rows), :] = o_ref[0, 0, pl.ds(r, rows), :] * scale
        o_ref[0, 1, pl.ds(r, rows), :] = o_ref[0, 1, pl.ds(r, rows), :] * scale
        return carry

    lax.fori_loop(0, l // rows, rescale, 0)


def hyena_filters(l, fw1, fb1, fw2, fb2, fw3, ffreq, tn=2 * LANES):
    d = D_MODEL
    pos = jnp.arange(l, dtype=F32)
    t = jnp.linspace(0.0, 1.0, l, dtype=F32)
    wpos = 2.0 * math.pi * pos / l
    f = jnp.linspace(1e-4, HY_BANDS - 1, HY_BANDS, dtype=F32)
    ang = wpos[:, None] * f[None, :]
    emb_pad = _round_up(HY_EMB_DIM, SUBLANES)
    z = jnp.concatenate([t[:, None], jnp.cos(ang), -jnp.sin(ang), jnp.zeros((l, emb_pad - HY_EMB_DIM), F32)], axis=-1)
    w1 = jnp.concatenate([fw1.astype(F32), jnp.zeros((emb_pad - HY_EMB_DIM, fw1.shape[1]), F32)], axis=0)
    deltas = jnp.abs(jnp.linspace(HY_MIN_DECAY, HY_MAX_DECAY, d, dtype=F32)).reshape(1, d)
    hid = fw2.shape[0]
    rows = min(FILTER_ROWS, l)
    hidden = pl.pallas_call(
        _filter_hidden_kernel,
        grid=(l // rows,),
        in_specs=[pl.BlockSpec((rows, emb_pad), lambda r: (r, 0)), _resident((emb_pad, hid)), _resident((1, hid)),
                  _resident((hid, hid)), _resident((1, hid)), _resident((2, hid))],
        out_specs=pl.BlockSpec((rows, hid), lambda r: (r, 0)),
        out_shape=jax.ShapeDtypeStruct((l, hid), F32),
        compiler_params=_params(("parallel",)),
        name="hyena_filter_hidden",
    )(z, w1, fb1.astype(F32).reshape(1, hid), fw2.astype(F32), fb2.astype(F32).reshape(1, hid), ffreq.astype(F32))
    nt = d // tn
    w3 = fw3.astype(F32)
    return pl.pallas_call(
        functools.partial(_filter_out_kernel, rows=rows),
        grid=(HY_ORDER, nt),
        in_specs=[_resident((l, hid)), _resident((l, 1)),
                  pl.BlockSpec((hid, tn), lambda o, j: (0, (2 * o) * nt + j)),
                  pl.BlockSpec((hid, tn), lambda o, j: (0, (2 * o + 1) * nt + j)),
                  pl.BlockSpec((1, tn), lambda o, j: (0, j))],
        out_specs=pl.BlockSpec((1, 2, l, tn), lambda o, j: (o, 0, 0, j)),
        out_shape=jax.ShapeDtypeStruct((HY_ORDER, 2, l, d), F32),
        compiler_params=_params(("parallel", "parallel")),
        name="hyena_filters",
    )(hidden, t.reshape(l, 1), w3, w3, deltas)


def _filter_spectrum_kernel(hf_ref, hb_ref, f1_ref, f2_ref, o_ref, s_ref, *, p):
    na, nb = p["na"], p["nb"]
    half = na // 2

    def gather(tb):
        fwd = hf_ref[pl.ds(tb, half, stride=nb), :]
        bwd = hb_ref[pl.ds(jnp.where(tb == 0, 0, nb - tb), half, stride=nb), :]
        return jnp.concatenate([fwd, bwd], axis=0).astype(BF16)

    _fft_stage1(gather, f1_ref, s_ref, p)

    def body(i, carry):
        spec = jnp.dot(f2_ref[...], _slab_pair(s_ref, i, p).astype(BF16), preferred_element_type=F32)
        r = pl.multiple_of(i * 4 * nb, SUBLANES)
        o_ref[pl.ds(r, 2 * nb), :] = spec[:, :LANES].astype(BF16)
        o_ref[pl.ds(r + 2 * nb, 2 * nb), :] = spec[:, LANES:].astype(BF16)
        return carry

    lax.fori_loop(0, p["n_pairs"], body, 0, unroll=p["pair_unroll"])


def hyena_filter_spectrum(filt, tables):
    p, _, f1k, f2, _, _ = tables
    order, _, l, d = filt.shape
    rows = 2 * p["n_pairs"] * 2 * p["nb"]
    return pl.pallas_call(
        functools.partial(_filter_spectrum_kernel, p=p),
        grid=(order, d // LANES),
        in_specs=[pl.BlockSpec((None, None, l, LANES), lambda o, j: (o, 0, 0, j)),
                  pl.BlockSpec((None, None, l, LANES), lambda o, j: (o, 1, 0, j)),
                  _resident(f1k.shape), _resident(f2.shape)],
        out_specs=pl.BlockSpec((None, rows, LANES), lambda o, j: (o, 0, j)),
        out_shape=jax.ShapeDtypeStruct((order, rows, d), BF16),
        scratch_shapes=[pltpu.VMEM((p["nsp"] * p["pitch"], LANES), F32)],
        compiler_params=_params(("parallel", "parallel")),
        name="hyena_filter_spectrum",
    )(filt, filt, jnp.asarray(f1k).astype(BF16), jnp.asarray(f2).astype(BF16))


def _longconv_kernel(a_ref, m_ref, k_ref, bias_ref, f1_ref, f2_ref, g2_ref, g1_ref, o_ref, s_ref, y_ref, *, p):
    na, nb, nsp, pitch = p["na"], p["nb"], p["nsp"], p["pitch"]
    half = na // 2

    _fft_stage1(lambda tb: a_ref[pl.ds(tb, half, stride=nb), :].astype(BF16), f1_ref, s_ref, p)

    def mid(i, carry):
        x = jnp.dot(f2_ref[...], _slab_pair(s_ref, i, p).astype(BF16), preferred_element_type=F32)
        r = pl.multiple_of(i * 4 * nb, SUBLANES)
        kk = jnp.concatenate([k_ref[pl.ds(r, 2 * nb), :], k_ref[pl.ds(r + 2 * nb, 2 * nb), :]], axis=1).astype(F32)
        xr, xi, kr, ki = x[:nb], x[nb:], kk[:nb], kk[nb:]
        y = jnp.concatenate([xr * kr - xi * ki, xr * ki + xi * kr], axis=0).astype(BF16)
        bq = jnp.dot(g2_ref[...], y, preferred_element_type=F32)
        r0 = pl.multiple_of(2 * i * pitch, SUBLANES)
        r1 = pl.multiple_of(2 * i * pitch + pitch, SUBLANES)
        s_ref[pl.ds(r0, 2 * nb), :] = bq[:, :LANES]
        s_ref[pl.ds(r1, 2 * nb), :] = bq[:, LANES:]
        return carry

    lax.fori_loop(0, p["n_pairs"], mid, 0, unroll=p["pair_unroll"])

    def last(tb, carry):
        bq = jnp.concatenate([s_ref[pl.ds(tb, nsp, stride=pitch), :], s_ref[pl.ds(nb + tb, nsp, stride=pitch), :]], axis=0)
        y_ref[pl.ds(tb, half, stride=nb + SLAB_PAD), :] = jnp.dot(g1_ref[tb], bq.astype(BF16), preferred_element_type=F32)
        return carry

    lax.fori_loop(0, nb, last, 0, unroll=p["tb_unroll"])

    def finish(ta, carry):
        r = pl.multiple_of(ta * nb, SUBLANES)
        conv = y_ref[pl.ds(pl.multiple_of(ta * (nb + SLAB_PAD), SUBLANES), nb), :]
        o_ref[pl.ds(r, nb), :] = (m_ref[pl.ds(r, nb), :] * (conv + bias_ref[...] * a_ref[pl.ds(r, nb), :])).astype(o_ref.dtype)
        return carry

    lax.fori_loop(0, half, finish, 0, unroll=min(half, 8))


def hyena_longconv(a, a_col, m, m_col, kspec, bias, tables, out_dtype=F32):
    p, f1, _, f2, g2, g1 = tables
    b, l, _ = a.shape
    d = D_MODEL
    nt = d // LANES
    rows = kspec.shape[0]
    tabs = [jnp.asarray(t).astype(BF16) for t in (f1, f2, g2, g1)]
    return pl.pallas_call(
        functools.partial(_longconv_kernel, p=p),
        grid=(nt, b),
        in_specs=[pl.BlockSpec((None, l, LANES), lambda j, i: (i, 0, a_col * nt + j)),
                  pl.BlockSpec((None, l, LANES), lambda j, i: (i, 0, m_col * nt + j)),
                  pl.BlockSpec((rows, LANES), lambda j, i: (0, j)),
                  pl.BlockSpec((1, LANES), lambda j, i: (0, j))] + [_resident(t.shape) for t in tabs],
        out_specs=pl.BlockSpec((None, l, LANES), lambda j, i: (i, 0, j)),
        out_shape=jax.ShapeDtypeStruct((b, l, d), out_dtype),
        scratch_shapes=[pltpu.VMEM((p["nsp"] * p["pitch"], LANES), F32),
                        pltpu.VMEM((p["na"] // 2 * (p["nb"] + SLAB_PAD), LANES), F32)],
        compiler_params=_params(("parallel", "parallel")),
        name="hyena_longconv",
    )(a, m, kspec, bias.astype(F32).reshape(1, d), *tabs)


def hyena_core(pc, fw1, fb1, fw2, fb2, fw3, ffreq, hbias):
    l = pc.shape[1]
    tables = _fft_tables(l)
    filt = hyena_filters(l, fw1, fb1, fw2, fb2, fw3, ffreq)
    kspec = hyena_filter_spectrum(filt, tables)
    z = hyena_longconv(pc, 0, pc, 1, kspec[0], hbias[0], tables)
    return hyena_longconv(z, 0, pc, 2, kspec[1], hbias[1], tables, out_dtype=BF16)


def kernel(x, c, ctx, c_ctx, ada_w, ada_b, norm_g, ffn_w_gate, ffn_w_up, ffn_w_down, ssd_w_in, ssd_conv_w, ssd_conv_b, ssd_a_log, ssd_dt_bias, ssd_d, ssd_norm_g, ssd_w_out, pool_w, pool_b, pool_scale, hy_w_in, hy_conv_w, hy_conv_b, hy_filt_w1, hy_filt_b1, hy_filt_w2, hy_filt_b2, hy_filt_w3, hy_filt_freq, hy_bias, hy_w_out, final_g):
    batch = x.shape[0]
    d = D_MODEL
    h = ctx

    s = jnp.concatenate([jax.nn.silu(c), jax.nn.silu(c_ctx)[None], jnp.zeros((7 - batch, d), F32)], axis=0)
    mods = ada_modulation(s, ada_w, ada_b).reshape(DEPTH, 8, N_MOD, d)

    wg_bf, wu_bf, wd_bf = ffn_w_gate.astype(BF16), ffn_w_up.astype(BF16), ffn_w_down.astype(BF16)
    fg = final_g.reshape(1, d)

    for i in range(DEPTH):
        kind = i % N_MIXERS
        j = i // N_MIXERS
        last = i == DEPTH - 1
        ctx_in_needed = (not last) or kind == 0
        m = [mods[i, :batch, k].reshape(batch, 1, 1, d) for k in range(N_MOD)]
        mc = [jnp.broadcast_to(mods[i, batch, k].reshape(1, 1, 1, d), (batch, 1, 1, d)) for k in range(N_MOD)]
        g0, g1, g2 = (norm_g[i, k].reshape(1, d) for k in range(3))

        x = ffn_step(x, m[0], m[1], m[2], g0, wg_bf[i, 0], wu_bf[i, 0], wd_bf[i, 0])
        if ctx_in_needed:
            h = ffn_step(h, mc[0], mc[1], mc[2], g0, wg_bf[i, 0], wu_bf[i, 0], wd_bf[i, 0])

        if kind == 0:
            x, h_new = ssd_mixer(x, h, (m[3], m[4], m[5]), (mc[3], mc[4], mc[5]), g1, ssd_w_in[j], ssd_conv_w[j],
                                 ssd_conv_b[j], ssd_a_log[j], ssd_dt_bias[j], ssd_d[j], ssd_norm_g[j], ssd_w_out[j], not last)
            if not last:
                h = h_new
        elif kind == 1:
            x = pool_mixer(x, m[3], m[4], m[5], g1, pool_w[j], pool_b[j].reshape(-1), pool_scale[j], GRID_W)
            if not last:
                h = pool_mixer(h, mc[3], mc[4], mc[5], g1, pool_w[j], pool_b[j].reshape(-1), pool_scale[j], h.shape[1])
        else:
            w_in = hy_w_in[j].astype(BF16)
            w_out = hy_w_out[j].astype(BF16)
            filt = (hy_filt_w1[j], hy_filt_b1[j], hy_filt_w2[j], hy_filt_b2[j], hy_filt_w3[j], hy_filt_freq[j])
            conv = {0: (hy_conv_w[j], hy_conv_b[j], False)}
            (p_lat,) = modulate_project(x, m[3], m[4], g1, [w_in], tm=512, conv=conv)
            x = project_residual(x, hyena_core(p_lat, *filt, hy_bias[j]), m[5], w_out)
            if not last:
                (p_ctx,) = modulate_project(h, mc[3], mc[4], g1, [w_in], conv=conv)
                h = project_residual(h, hyena_core(p_ctx, *filt, hy_bias[j]), mc[5], w_out)

        x = ffn_step(x, m[6], m[7], m[8], g2, wg_bf[i, 1], wu_bf[i, 1], wd_bf[i, 1],
                     final_g=fg if last else None)
        if not last:
            h = ffn_step(h, mc[6], mc[7], mc[8], g2, wg_bf[i, 1], wu_bf[i, 1], wd_bf[i, 1])
    return x
```

```python
import functools
import math

import jax
import jax.numpy as jnp
from jax import lax
from jax.experimental import pallas as pl
from jax.experimental.pallas import tpu as pltpu

F32 = jnp.float32
BF16 = jnp.bfloat16
HIGHEST = lax.Precision.HIGHEST

D_MODEL = 1024
DEPTH = 4
GRID_W = 64
N_MIXERS = 3
D_FF = 2816
N_MOD = 9
EPS = 1e-6
LOG2_E = 1.4426950408889634

SSD_D_INNER = 2 * D_MODEL
SSD_HEAD_DIM = 64
SSD_N_HEADS = SSD_D_INNER // SSD_HEAD_DIM
SSD_N_GROUPS = 4
SSD_HEADS_PER_GROUP = SSD_N_HEADS // SSD_N_GROUPS
SSD_D_STATE = 128
SSD_CHUNK = 128
SSD_GN = SSD_N_GROUPS * SSD_D_STATE
SSD_CONV_DIM = SSD_D_INNER + 2 * SSD_GN
SSD_GROUP_WIDTH = SSD_D_INNER // SSD_N_GROUPS

POOL_WINDOWS = (2, 4, 8, 16)
POOL_GROUPS = 4
POOL_GROUP_DIM = D_MODEL // POOL_GROUPS

HY_ORDER = 2
HY_EMB_DIM = 33
HY_BANDS = (HY_EMB_DIM - 1) // 2
HY_MAX_DECAY = math.log(1e-2) / 0.3
HY_MIN_DECAY = math.log(1e-2) / 1.5

VMEM_LIMIT_BYTES = 56 * 1024 * 1024
SUBLANES = 8
LANES = 128


def _params(sem):
    return pltpu.CompilerParams(dimension_semantics=sem, vmem_limit_bytes=VMEM_LIMIT_BYTES)


def _resident(shape):
    nd = len(shape)
    return pl.BlockSpec(shape, lambda *_: (0,) * nd, pipeline_mode=pl.Buffered(1))


def _mod_spec(d, _unused=0):
    return pl.BlockSpec((1, 1, 1, d), lambda i, j: (i, 0, 0, 0))


def _modulated(x, g, shift, scale):
    ms = jnp.mean(x * x, axis=-1, keepdims=True)
    return (x * lax.rsqrt(ms + EPS)) * (g * (1.0 + scale)) + shift


def _silu(v):
    return v * jax.nn.sigmoid(v)


def _ada_kernel(s_ref, w_ref, b_ref, o_ref):
    o_ref[0] = jnp.dot(s_ref[...], w_ref[0], preferred_element_type=F32, precision=HIGHEST) + b_ref[0]


def ada_modulation(s, ada_w, ada_b, tn=2304):
    depth, d, n = ada_w.shape
    r = s.shape[0]
    return pl.pallas_call(
        _ada_kernel,
        grid=(depth, n // tn),
        in_specs=[pl.BlockSpec((r, d), lambda i, j: (0, 0)),
                  pl.BlockSpec((1, d, tn), lambda i, j: (i, 0, j)),
                  pl.BlockSpec((1, 1, tn), lambda i, j: (i, 0, j))],
        out_specs=pl.BlockSpec((1, r, tn), lambda i, j: (i, 0, j)),
        out_shape=jax.ShapeDtypeStruct((depth, r, n), F32),
        compiler_params=_params(("parallel", "parallel")),
        name="ada_modulation",
    )(s, ada_w, ada_b.reshape(depth, 1, n))


def _ffn_kernel(x_ref, sh_ref, sc_ref, gt_ref, g_ref, wg_ref, wu_ref, wd_ref, *rest, f_chunk, final):
    if final:
        fg_ref, o_ref = rest
    else:
        (o_ref,) = rest
    x = x_ref[0]
    u = _modulated(x, g_ref[...], sh_ref[0, 0], sc_ref[0, 0]).astype(BF16)
    acc = jnp.zeros(x.shape, F32)
    d_ff = wg_ref.shape[1]
    for f0 in range(0, d_ff, f_chunk):
        a = jnp.dot(u, wg_ref[:, f0:f0 + f_chunk], preferred_element_type=F32)
        b = jnp.dot(u, wu_ref[:, f0:f0 + f_chunk], preferred_element_type=F32)
        h = (_silu(a) * b).astype(BF16)
        acc = acc + jnp.dot(h, wd_ref[f0:f0 + f_chunk, :], preferred_element_type=F32)
    y = x + (0.5 * gt_ref[0, 0]) * acc
    if final:
        ms = jnp.mean(y * y, axis=-1, keepdims=True)
        y = y * lax.rsqrt(ms + EPS) * fg_ref[...]
    o_ref[0] = y


def ffn_step(x, shift, scale, gate, g, wg, wu, wd, final_g=None, tm=512, f_chunk=256):
    b, l, d = x.shape
    tm = min(tm, l)
    f = wg.shape[1]
    final = final_g is not None
    mod_spec = _mod_spec(d, 0)
    in_specs = [pl.BlockSpec((1, tm, d), lambda i, j: (i, j, 0)),
                mod_spec, mod_spec, mod_spec,
                _resident((1, d)), _resident((d, f)), _resident((d, f)), _resident((f, d))]
    args = [x, shift, scale, gate, g, wg, wu, wd]
    if final:
        in_specs.append(_resident((1, d)))
        args.append(final_g)
    return pl.pallas_call(
        functools.partial(_ffn_kernel, f_chunk=f_chunk, final=final),
        grid=(b, l // tm),
        in_specs=in_specs,
        out_specs=pl.BlockSpec((1, tm, d), lambda i, j: (i, j, 0)),
        out_shape=jax.ShapeDtypeStruct((b, l, d), F32),
        compiler_params=_params(("parallel", "parallel")),
        name="ffn_step",
    )(*args)


CONV_COLS = 512


def _modproj_kernel(*refs, n_w, n_dest, conv, n_tiles):
    if conv:
        prev_ref, x_ref, next_ref, sh_ref, sc_ref, g_ref, *rest = refs
    else:
        x_ref, sh_ref, sc_ref, g_ref, *rest = refs
    w_refs = rest[:n_w]
    cw_refs = rest[n_w:n_w + 2 * len(conv)]
    o_refs = rest[n_w + 2 * len(conv) + n_dest:]
    g, sh, sc = g_ref[...], sh_ref[0, 0], sc_ref[0, 0]
    u = _modulated(x_ref[0], g, sh, sc)
    if n_w == 0:
        o_refs[0][0] = u
        return
    u = u.astype(BF16)
    tm = x_ref.shape[1]
    if conv:
        ue = _modulated(jnp.concatenate([prev_ref[0], x_ref[0], next_ref[0]], axis=0), g, sh, sc).astype(BF16)
        j = pl.program_id(1)
        top = (j > 0).astype(F32)
        bot = (j < n_tiles - 1).astype(F32)
    for idx, (w_ref, o_ref) in enumerate(zip(w_refs, o_refs)):
        if idx not in conv:
            o_ref[0] = jnp.dot(u, w_ref[...], preferred_element_type=F32).astype(o_ref.dtype)
            continue
        k_w, silu = conv[idx]
        cw_ref, cb_ref = cw_refs[2 * list(conv).index(idx)], cw_refs[2 * list(conv).index(idx) + 1]
        p = k_w // 2
        rows = tm + 2 * SUBLANES
        for c0 in range(0, w_ref.shape[1], CONV_COLS):
            ce = jnp.dot(ue, w_ref[:, c0:c0 + CONV_COLS], preferred_element_type=F32)
            ce = jnp.concatenate([ce[:SUBLANES] * top, ce[SUBLANES:SUBLANES + tm], ce[SUBLANES + tm:] * bot], axis=0)
            acc = jnp.broadcast_to(cb_ref[:, c0:c0 + CONV_COLS], (tm, CONV_COLS))
            for k in range(k_w):
                tap = ce if k == p else pltpu.roll(ce, (p - k) % rows, axis=0)
                acc = acc + cw_ref[k:k + 1, c0:c0 + CONV_COLS] * tap[SUBLANES:SUBLANES + tm, :]
            o_ref[0, :, c0:c0 + CONV_COLS] = (_silu(acc) if silu else acc).astype(o_ref.dtype)


def modulate_project(x, shift, scale, g, weights, tm=256, out_dtypes=None, out_rows=None, row_block=0, dest=None,
                     conv=None):
    b, l, d = x.shape
    tm = min(tm, l)
    conv = dict(conv or {})
    mod_spec = _mod_spec(d)
    x_spec = pl.BlockSpec((1, tm, d), lambda i, j: (i, j, 0))
    args = [x]
    in_specs = [x_spec]
    if conv:
        rb, last = tm // SUBLANES, l // SUBLANES - 1
        in_specs = [pl.BlockSpec((1, SUBLANES, d), lambda i, j: (i, jnp.maximum(j * rb - 1, 0), 0)), x_spec,
                    pl.BlockSpec((1, SUBLANES, d), lambda i, j: (i, jnp.minimum((j + 1) * rb, last), 0))]
        args = [x, x, x]
    in_specs += [mod_spec, mod_spec, _resident((1, d))] + [_resident(w.shape) for w in weights]
    args += [shift, scale, g, *weights]
    for i in conv:
        cw, cb, _ = conv[i]
        assert weights[i].shape[1] % CONV_COLS == 0
        in_specs += [_resident(cw.shape), _resident((1, cb.shape[0]))]
        args += [cw.astype(F32), cb.astype(F32).reshape(1, -1)]
    widths = [w.shape[1] for w in weights] or [d]
    out_dtypes = out_dtypes or [F32] * len(widths)
    dest = list(dest or [])
    in_specs += [pl.BlockSpec(memory_space=pl.ANY)] * len(dest)
    n_in = len(args)
    return pl.pallas_call(
        functools.partial(_modproj_kernel, n_w=len(weights), n_dest=len(dest),
                          conv={i: (c[0].shape[0], c[2]) for i, c in conv.items()}, n_tiles=l // tm),
        grid=(b, l // tm),
        in_specs=in_specs,
        out_specs=[pl.BlockSpec((1, tm, n), lambda i, j: (i, j + row_block, 0)) for n in widths],
        out_shape=[jax.ShapeDtypeStruct((b, out_rows or l, n), dt) for n, dt in zip(widths, out_dtypes)],
        input_output_aliases={n_in + k: k for k in range(len(dest))},
        compiler_params=_params(("parallel", "parallel")),
        name="modulate_project",
    )(*args, *dest)


def _outproj_kernel(x_ref, a_ref, gt_ref, w_ref, o_ref):
    y = jnp.dot(a_ref[0].astype(BF16), w_ref[...], preferred_element_type=F32)
    o_ref[0] = x_ref[0] + gt_ref[0, 0] * y


def project_residual(x, a, gate, w, tm=512):
    b, l, d = x.shape
    k = a.shape[-1]
    tm = min(tm, l)
    return pl.pallas_call(
        _outproj_kernel,
        grid=(b, l // tm),
        in_specs=[pl.BlockSpec((1, tm, d), lambda i, j: (i, j, 0)),
                  pl.BlockSpec((1, tm, k), lambda i, j: (i, j, 0)),
                  _mod_spec(d, 0),
                  _resident((k, d))],
        out_specs=pl.BlockSpec((1, tm, d), lambda i, j: (i, j, 0)),
        out_shape=jax.ShapeDtypeStruct((b, l, d), F32),
        compiler_params=_params(("parallel", "parallel")),
        name="project_residual",
    )(x, a, gate, w)


def _bf16_terms(v):
    hi = v.astype(BF16)
    r = v - hi.astype(F32)
    mid = r.astype(BF16)
    return hi, mid, (r - mid.astype(F32)).astype(BF16)


def _dot_f32_rhs(m, v):
    return sum(jnp.dot(m, t, preferred_element_type=F32) for t in _bf16_terms(v))


def _dot_f32_lhs(v, m):
    return sum(jnp.dot(t, m, preferred_element_type=F32) for t in _bf16_terms(v))


def _dot_split(a, b):
    a_hi, a_lo, _ = _bf16_terms(a)
    b_hi, b_lo, _ = _bf16_terms(b)
    return (jnp.dot(a_hi, b_hi, preferred_element_type=F32) + jnp.dot(a_lo, b_hi, preferred_element_type=F32)
            + jnp.dot(a_hi, b_lo, preferred_element_type=F32))


def _ssd_scan_kernel(xf_ref, dtf_ref, dttf_ref, xb_ref, dtb_in_ref, dttb_ref, dtb_ref, dtbt_ref, a_ref, at_ref, ex_ref,
                     of_ref, ob_ref, h_ref):
    @pl.when(pl.program_id(1) == 0)
    def _():
        h_ref[...] = jnp.zeros(h_ref.shape, F32)

    consts = (dtb_ref, dtbt_ref, a_ref, at_ref, ex_ref)
    _ssd_chunk(xf_ref, dtf_ref, dttf_ref, *consts, of_ref, h_ref.at[0], True)
    _ssd_chunk(xb_ref, dtb_in_ref, dttb_ref, *consts, ob_ref, h_ref.at[1], False)


def _ssd_chunk(xbc_ref, dt_ref, dtt_ref, dtb_ref, dtbt_ref, a_ref, at_ref, ex_ref, o_ref, h_ref, fwd):
    nh, q, hp = SSD_N_HEADS, SSD_CHUNK, SSD_HEAD_DIM
    heads = slice(0, nh) if fwd else slice(nh, 2 * nh)
    dt = jax.nn.softplus(dt_ref[0][:, heads] + dtb_ref[:, heads])
    a = dt * a_ref[:, heads]
    dt_t = jax.nn.softplus(dtt_ref[0][heads] + dtbt_ref[heads])
    a_t = dt_t * at_ref[heads]

    r_i = lax.broadcasted_iota(jnp.int32, (q, q), 0)
    c_i = lax.broadcasted_iota(jnp.int32, (q, q), 1)
    seen = (c_i <= r_i) if fwd else (c_i >= r_i)
    seen_t = (r_i <= c_i) if fwd else (r_i >= c_i)
    cum = _dot_f32_rhs(seen.astype(BF16), a)
    cum_t = _dot_f32_lhs(a_t, seen_t.astype(BF16))
    total = cum[q - 1:q, :] if fwd else cum[0:1, :]
    ecum = jnp.exp(cum).astype(BF16)
    dtdec = (dt * jnp.exp(total - cum)).astype(BF16)
    etot = jnp.exp(total)
    cum2 = cum * LOG2_E
    row2 = cum_t * LOG2_E - jnp.log2(dt_t)

    pw = 2 * hp
    left = lax.broadcasted_iota(jnp.int32, (q, pw), 1) < hp
    left_row = lax.broadcasted_iota(jnp.int32, (1, pw), 1) < hp
    pairs = SSD_HEADS_PER_GROUP // 2
    for g in range(SSD_N_GROUPS):
        b_g = xbc_ref[0, :, SSD_D_INNER + g * SSD_D_STATE:SSD_D_INNER + (g + 1) * SSD_D_STATE].astype(BF16)
        c_lo = SSD_D_INNER + SSD_GN + g * SSD_D_STATE
        c_g = xbc_ref[0, :, c_lo:c_lo + SSD_D_STATE].astype(BF16)
        cb = lax.dot_general(c_g, b_g, (((1,), (1,)), ((), ())), preferred_element_type=F32)
        h_g = h_ref[g]
        ex_g = ex_ref[:, g * SSD_GROUP_WIDTH:(g + 1) * SSD_GROUP_WIDTH]
        ch = (jnp.dot(c_g, h_g.astype(BF16), preferred_element_type=F32)
              * jnp.dot(ecum, ex_g, preferred_element_type=F32))
        dtdec_g = jnp.dot(dtdec, ex_g, preferred_element_type=F32)
        xdecs, etots = [], []
        for pr in range(pairs):
            e1 = g * SSD_HEADS_PER_GROUP + 2 * pr
            e2 = e1 + 1
            x_pair = xbc_ref[0, :, e1 * hp:e1 * hp + pw].astype(F32)
            ws = []
            for e in (e1, e2):
                seg2 = cum2[:, e:e + 1] - row2[e:e + 1, :]
                ws.append((cb * jnp.exp2(jnp.where(seen, seg2, -jnp.inf))).astype(BF16))
            rhs = jnp.concatenate([jnp.where(left, x_pair, 0.0), jnp.where(left, 0.0, x_pair)], axis=0).astype(BF16)
            y_diag = jnp.dot(jnp.concatenate(ws, axis=1), rhs, preferred_element_type=F32)
            y_off = ch[:, pr * pw:(pr + 1) * pw]
            o_ref[0, 0, :, e1 * hp:e1 * hp + pw] = (y_diag + y_off).astype(o_ref.dtype)
            xdecs.append((x_pair * dtdec_g[:, pr * pw:(pr + 1) * pw]).astype(BF16))
            etots.append(jnp.where(left_row, etot[:, e1:e1 + 1], etot[:, e2:e2 + 1]))
        s_g = lax.dot_general(b_g, jnp.concatenate(xdecs, axis=1), (((0,), (0,)), ((), ())),
                              preferred_element_type=F32)
        h_ref[g] = h_g * jnp.concatenate(etots, axis=1) + s_g


def ssd_scan(xbc, dt_raw, dt_bias, a_log, n_lead_chunks):
    b, t, _ = xbc.shape
    q, nh = SSD_CHUNK, SSD_N_HEADS
    nc = t // q
    dt_t = jnp.swapaxes(dt_raw, 1, 2)
    a_neg = -jnp.exp(a_log.astype(F32)).reshape(1, 2 * nh)
    dtb = dt_bias.astype(F32).reshape(1, 2 * nh)
    expand = jnp.repeat(jnp.eye(nh, dtype=BF16), SSD_HEAD_DIM, axis=1)

    n_main = nc - n_lead_chunks

    def chunk_of(d, c):
        if d:
            return jnp.where(c < n_lead_chunks, nc - 1 - c, n_main - 1 - (c - n_lead_chunks))
        return jnp.where(c < n_lead_chunks, n_main + c, c - n_lead_chunks)

    def token_specs(d):
        return [pl.BlockSpec((1, q, SSD_CONV_DIM), lambda i, c: (i, chunk_of(d, c), 0)),
                pl.BlockSpec((1, q, 2 * nh), lambda i, c: (i, chunk_of(d, c), 0)),
                pl.BlockSpec((1, 2 * nh, q), lambda i, c: (i, 0, chunk_of(d, c)))]

    y_f, y_b = pl.pallas_call(
        _ssd_scan_kernel,
        grid=(b, nc),
        in_specs=token_specs(0) + token_specs(1) + [
            _resident((1, 2 * nh)), _resident((2 * nh, 1)),
            _resident((1, 2 * nh)), _resident((2 * nh, 1)), _resident((nh, SSD_D_INNER))],
        out_specs=[pl.BlockSpec((1, 1, q, SSD_D_INNER), lambda i, c, d=d: (i, 0, chunk_of(d, c), 0)) for d in (0, 1)],
        out_shape=[jax.ShapeDtypeStruct((b, 1, t, SSD_D_INNER), BF16)] * 2,
        scratch_shapes=[pltpu.VMEM((2, SSD_N_GROUPS, SSD_D_STATE, SSD_GROUP_WIDTH), F32)],
        compiler_params=_params(("parallel", "arbitrary")),
        name="ssd_scan",
    )(xbc, dt_raw, dt_t, xbc, dt_raw, dt_t, dtb, dtb.reshape(2 * nh, 1), a_neg, a_neg.reshape(2 * nh, 1), expand)
    return y_f, y_b


def _ssd_out_kernel(x_ref, y0_ref, y1_ref, xs0_ref, xs1_ref, z_ref, dsk_ref, ng_ref, gt_ref, w_ref, o_ref):
    gw = SSD_GROUP_WIDTH
    half_groups = SSD_N_GROUPS // 2
    parts = []
    for g in range(SSD_N_GROUPS):
        cols = slice(g * gw, (g + 1) * gw)
        xs_ref, lo = (xs0_ref, g * gw) if g < half_groups else (xs1_ref, (g - half_groups) * gw)
        y = (y0_ref[0, 0, :, cols].astype(F32) + y1_ref[0, 0, :, cols].astype(F32)
             + dsk_ref[:, cols] * xs_ref[0, :, lo:lo + gw].astype(F32))
        blk = y * _silu(z_ref[0, :, cols].astype(F32))
        blk = blk * lax.rsqrt(jnp.mean(blk * blk, axis=-1, keepdims=True) + EPS)
        parts.append((blk * ng_ref[:, cols]).astype(BF16))
    a = jnp.concatenate(parts, axis=1)
    o_ref[0] = x_ref[0] + gt_ref[0, 0] * jnp.dot(a, w_ref[...], preferred_element_type=F32)


def ssd_gate_project(x, y2, xbc, z, d_skip, norm_g, gate, w_out, src_block, tm=256):
    b, t, d = x.shape
    half = SSD_D_INNER // 2
    dsk = jnp.repeat(d_skip.astype(F32), SSD_HEAD_DIM).reshape(1, SSD_D_INNER)
    return pl.pallas_call(
        _ssd_out_kernel,
        grid=(b, t // tm),
        in_specs=[pl.BlockSpec((1, tm, d), lambda i, j: (i, j, 0)),
                  pl.BlockSpec((1, 1, tm, SSD_D_INNER), lambda i, j: (i, 0, j + src_block, 0)),
                  pl.BlockSpec((1, 1, tm, SSD_D_INNER), lambda i, j: (i, 0, j + src_block, 0)),
                  pl.BlockSpec((1, tm, half), lambda i, j: (i, j + src_block, 0)),
                  pl.BlockSpec((1, tm, half), lambda i, j: (i, j + src_block, 1)),
                  pl.BlockSpec((1, tm, SSD_D_INNER), lambda i, j: (i, j + src_block, 0)),
                  _resident((1, SSD_D_INNER)), _resident((1, SSD_D_INNER)),
                  _mod_spec(d),
                  _resident((SSD_D_INNER, d))],
        out_specs=pl.BlockSpec((1, tm, d), lambda i, j: (i, j, 0)),
        out_shape=jax.ShapeDtypeStruct((b, t, d), F32),
        compiler_params=_params(("parallel", "parallel")),
        name="ssd_gate_project",
    )(x, y2[0], y2[1], xbc, xbc, z, dsk, norm_g.astype(F32).reshape(1, SSD_D_INNER), gate, w_out)


def ssd_mixer(x, h, m_lat, m_ctx, g1, w_in, conv_w, conv_b, a_log, dt_bias, d_skip, norm_g, w_out, need_ctx_out,
              tm_proj=1024, tm_gate=512):
    lc, l = h.shape[1], x.shape[1]
    tm_proj, tm_gate = min(tm_proj, l), min(tm_gate, l)
    tm_ctx = min(tm_gate, lc)
    assert l % tm_proj == 0 and l % tm_gate == 0 and l % tm_ctx == 0
    w_bf = w_in.astype(BF16)
    ws = [w_bf[:, :SSD_D_INNER], w_bf[:, SSD_D_INNER:SSD_D_INNER + SSD_CONV_DIM], w_bf[:, SSD_D_INNER + SSD_CONV_DIM:]]
    dts = [BF16, BF16, F32]
    conv = {1: (conv_w, conv_b, True)}
    parts = modulate_project(x, m_lat[0], m_lat[1], g1, ws, tm=tm_proj, out_dtypes=dts, out_rows=lc + l, conv=conv)
    z, xbc, dt_raw = modulate_project(h, m_ctx[0], m_ctx[1], g1, ws, tm=tm_ctx, out_dtypes=dts, out_rows=lc + l,
                                      row_block=l // tm_ctx, dest=parts, conv=conv)
    y2 = ssd_scan(xbc, dt_raw, dt_bias, a_log, lc // SSD_CHUNK)
    w_out = w_out.astype(BF16)
    x_new = ssd_gate_project(x, y2, xbc, z, d_skip, norm_g, m_lat[2], w_out, 0, tm=tm_gate)
    h_new = (ssd_gate_project(h, y2, xbc, z, d_skip, norm_g, m_ctx[2], w_out, l // tm_ctx, tm=tm_ctx)
             if need_ctx_out else None)
    return x_new, h_new


POOL_CHUNK = 256
POOL_MAX_HALF = max(POOL_WINDOWS) // 2


def _pool_band_tables(width):
    import numpy as np
    pos = np.arange(POOL_CHUNK)
    line, col = pos // width, pos % width
    out = []
    for w in POOL_WINDOWS:
        inside = (col[None, :] >= col[:, None] - w // 2) & (col[None, :] <= col[:, None] + (w - w // 2) - 1)
        out.append((inside & (line[None, :] == line[:, None])).astype(np.float32))
    return np.stack(out)


def _pool_kernel(u_ref, band_ref, o_ref, cs_ref, *, width, n_lines):
    l, c = u_ref.shape
    n_chunks = l // POOL_CHUNK
    assert POOL_WINDOWS == tuple(2 << g for g in range(POOL_GROUPS)) and width & (width - 1) == 0
    half = lax.shift_left(jnp.int32(1), pl.program_id(1) // (POOL_GROUP_DIM // LANES))
    log_w = width.bit_length() - 1
    pad = POOL_MAX_HALF * width if n_lines > 1 else 0
    if pad:
        cs_ref[0:pad, :] = jnp.zeros((pad, c), F32)
        cs_ref[pad + l:2 * pad + l, :] = jnp.zeros((pad, c), F32)

    def col_pass(i, carry):
        r = pl.multiple_of(i * POOL_CHUNK, POOL_CHUNK)
        u = u_ref[pl.ds(r, POOL_CHUNK), :]
        hi = u.astype(BF16)
        lo = (u - hi.astype(F32)).astype(BF16)
        both = jnp.dot(band_ref[0], jnp.concatenate([hi, lo], axis=1), preferred_element_type=F32)
        cs_ref[pl.ds(pad + r, POOL_CHUNK), :] = both[:, :c] + both[:, c:]
        return carry

    lax.fori_loop(0, n_chunks, col_pass, 0)

    pos0 = lax.broadcasted_iota(jnp.int32, (POOL_CHUNK, c), 0)
    col = pos0 & (width - 1)
    inv_cnt = 1.0 / (jnp.minimum(col + half, width) - jnp.maximum(col - half, 0)).astype(F32)
    line0 = lax.shift_right_logical(pos0, log_w)

    def out_pass(i, carry):
        r = pl.multiple_of(i * POOL_CHUNK, POOL_CHUNK)
        if n_lines > 1:
            def add_line(k, acc):
                return acc + cs_ref[pl.ds(pl.multiple_of(pad + r + k * width, SUBLANES), POOL_CHUNK), :]
            s = lax.fori_loop(-half, half, add_line, jnp.zeros((POOL_CHUNK, c), F32))
            line = line0 + lax.shift_right_logical(r, log_w)
            cnt_l = jnp.minimum(line + half, n_lines) - jnp.maximum(line - half, 0)
            mean = s * inv_cnt / cnt_l.astype(F32)
        else:
            mean = cs_ref[pl.ds(r, POOL_CHUNK), :] * inv_cnt
        o_ref[pl.ds(r, POOL_CHUNK), :] = (mean - u_ref[pl.ds(r, POOL_CHUNK), :]).astype(o_ref.dtype)
        return carry

    lax.fori_loop(0, n_chunks, out_pass, 0)


def pool_tokens(u, width):
    b, l, d = u.shape
    n_lines = l // width
    bands = jnp.asarray(_pool_band_tables(width)).astype(BF16)
    tiles_per_group = POOL_GROUP_DIM // LANES
    pad = POOL_MAX_HALF * width if n_lines > 1 else 0
    return pl.pallas_call(
        functools.partial(_pool_kernel, width=width, n_lines=n_lines),
        grid=(b, d // LANES),
        in_specs=[pl.BlockSpec((None, l, LANES), lambda i, j: (i, 0, j)),
                  pl.BlockSpec((1, POOL_CHUNK, POOL_CHUNK), lambda i, j: (j // tiles_per_group, 0, 0))],
        out_specs=pl.BlockSpec((None, l, LANES), lambda i, j: (i, 0, j)),
        out_shape=jax.ShapeDtypeStruct((b, l, d), BF16),
        scratch_shapes=[pltpu.VMEM((l + 2 * pad, LANES), F32)],
        compiler_params=_params(("parallel", "parallel")),
        name="pool_tokens",
    )(u, bands)


def _pool_mix_kernel(x_ref, p_ref, w_ref, b_ref, sc_ref, gt_ref, o_ref):
    ys = [jnp.dot(p_ref[0, :, g * POOL_GROUP_DIM:(g + 1) * POOL_GROUP_DIM], w_ref[g], preferred_element_type=F32)
          for g in range(POOL_GROUPS)]
    y = (jnp.concatenate(ys, axis=1) + b_ref[...]) * sc_ref[...]
    o_ref[0] = x_ref[0] + gt_ref[0, 0] * y


def pool_mix_residual(x, pooled, w, bias, scale, gate, tm=512):
    b, l, d = x.shape
    tm = min(tm, l)
    return pl.pallas_call(
        _pool_mix_kernel,
        grid=(b, l // tm),
        in_specs=[pl.BlockSpec((1, tm, d), lambda i, j: (i, j, 0)),
                  pl.BlockSpec((1, tm, d), lambda i, j: (i, j, 0)),
                  _resident(w.shape), _resident((1, d)), _resident((1, d)), _mod_spec(d, 0)],
        out_specs=pl.BlockSpec((1, tm, d), lambda i, j: (i, j, 0)),
        out_shape=jax.ShapeDtypeStruct((b, l, d), F32),
        compiler_params=_params(("parallel", "parallel")),
        name="pool_mix_residual",
    )(x, pooled, w.astype(BF16), bias.astype(F32).reshape(1, d), scale.astype(F32).reshape(1, d), gate)


def pool_mixer(x, shift, scale_mod, gate, g1, w, bias, scale, width):
    (u,) = modulate_project(x, shift, scale_mod, g1, [])
    return pool_mix_residual(x, pool_tokens(u, width), w, bias, scale, gate)


SLAB_PAD = 8


def _round_up(v, m):
    return (v + m - 1) // m * m


def _fft_plan(l):
    n = 2 * l
    na = 1 << (n.bit_length() // 2)
    nb = n // na
    ns = na // 2 + 1
    n_pairs = (ns + 1) // 2
    return dict(l=l, n=n, na=na, nb=nb, ns=ns, nsp=_round_up(ns + 1, SUBLANES), pitch=2 * nb + SLAB_PAD,
                n_pairs=n_pairs, tb_unroll=min(nb, 16),
                pair_unroll=max(u for u in range(1, 12) if n_pairs % u == 0))


def _fft_tables(l):
    import numpy as np
    p = _fft_plan(l)
    n, na, nb, ns, nsp = p["n"], p["na"], p["nb"], p["ns"], p["nsp"]
    half = na // 2
    ka = np.arange(ns)[None, :, None]
    tb = np.arange(nb)[:, None, None]

    def stage1(ta):
        th = 2.0 * np.pi * (ta[None, None, :] * ka / na + tb * ka / n)
        m = np.zeros((nb, 2 * nsp, ta.shape[0]))
        m[:, :ns] = np.cos(th)
        m[:, nsp:nsp + ns] = -np.sin(th)
        return m

    f1 = stage1(np.arange(half))
    f1k = np.zeros((nb, 2 * nsp, na))
    f1k[:, :, :half] = f1
    f1k[1:, :, half:] = stage1(na - 1 - np.arange(half))[1:]
    tb0 = stage1(na - np.arange(half))[0]
    tb0[:, 0] = 0.0
    f1k[0, :, half:] = tb0

    k2 = np.arange(nb)
    ang = 2.0 * np.pi * np.outer(k2, k2) / nb
    c, s = np.cos(ang), np.sin(ang)
    f2 = np.block([[c, s], [-s, c]])
    g2 = np.block([[c, -s], [s, c]])

    ta = np.arange(half)[None, :, None]
    kk = np.arange(ns)[None, None, :]
    tbb = np.arange(nb)[:, None, None]
    ph = 2.0 * np.pi * (ta * kk / na + tbb * kk / n)
    wgt = np.where((kk == 0) | (kk == na // 2), 1.0, 2.0) / n
    g1 = np.zeros((nb, half, 2 * nsp))
    g1[:, :, :ns] = wgt * np.cos(ph)
    g1[:, :, nsp:nsp + ns] = -wgt * np.sin(ph)
    f32 = np.float32
    return p, f1.astype(f32), f1k.astype(f32), f2.astype(f32), g2.astype(f32), g1.astype(f32)


def _fft_stage1(gather, f1_ref, s_ref, p):
    nb, nsp, pitch = p["nb"], p["nsp"], p["pitch"]

    def body(tb, carry):
        a = jnp.dot(f1_ref[tb], gather(tb), preferred_element_type=F32)
        s_ref[pl.ds(tb, nsp, stride=pitch), :] = a[:nsp]
        s_ref[pl.ds(nb + tb, nsp, stride=pitch), :] = a[nsp:]
        return carry

    lax.fori_loop(0, nb, body, 0, unroll=p["tb_unroll"])


def _slab_pair(s_ref, i, p):
    nb, pitch = p["nb"], p["pitch"]
    r0 = pl.multiple_of(2 * i * pitch, SUBLANES)
    r1 = pl.multiple_of(2 * i * pitch + pitch, SUBLANES)
    return jnp.concatenate([s_ref[pl.ds(r0, 2 * nb), :], s_ref[pl.ds(r1, 2 * nb), :]], axis=1)


FILTER_ROWS = 512


def _filter_hidden_kernel(z_ref, w1_ref, b1_ref, w2_ref, b2_ref, fr_ref, o_ref):
    h = jnp.sin(fr_ref[0:1, :] * (jnp.dot(z_ref[...], w1_ref[...], preferred_element_type=F32, precision=HIGHEST)
                                  + b1_ref[...]))
    o_ref[...] = jnp.sin(fr_ref[1:2, :] * (jnp.dot(h, w2_ref[...], preferred_element_type=F32, precision=HIGHEST)
                                           + b2_ref[...]))


def _filter_out_kernel(h_ref, t_ref, w3f_ref, w3b_ref, dl_ref, o_ref, *, rows):
    l, tn = o_ref.shape[2], o_ref.shape[3]
    wf, wb, dl = w3f_ref[...], w3b_ref[...], dl_ref[...]

    def fill(i, ss):
        r = pl.multiple_of(i * rows, rows)
        h = h_ref[pl.ds(r, rows), :]
        decay = jnp.exp(-t_ref[pl.ds(r, rows), :] * dl)
        hf = _dot_split(h, wf) * decay
        hb = _dot_split(h, wb) * decay
        o_ref[0, 0, pl.ds(r, rows), :] = hf
        o_ref[0, 1, pl.ds(r, rows), :] = hb
        return ss + jnp.sum(hf * hf + hb * hb, axis=0, keepdims=True)

    ss = lax.fori_loop(0, l // rows, fill, jnp.zeros((1, tn), F32))
    scale = lax.rsqrt(ss + EPS)

    def rescale(i, carry):
        r = pl.multiple_of(i * rows, rows)
        o_ref[0, 0, pl.ds(r, rows), :] = o_ref[0, 0, pl.ds(r, rows), :] * scale
        o_ref[0, 1, pl.ds(r, rows), :] = o_ref[0, 1, pl.ds(r, rows), :] * scale
        return carry

    lax.fori_loop(0, l // rows, rescale, 0)


def hyena_filters(l, fw1, fb1, fw2, fb2, fw3, ffreq, tn=2 * LANES):
    d = D_MODEL
    pos = jnp.arange(l, dtype=F32)
    t = jnp.linspace(0.0, 1.0, l, dtype=F32)
    wpos = 2.0 * math.pi * pos / l
    f = jnp.linspace(1e-4, HY_BANDS - 1, HY_BANDS, dtype=F32)
    ang = wpos[:, None] * f[None, :]
    emb_pad = _round_up(HY_EMB_DIM, SUBLANES)
    z = jnp.concatenate([t[:, None], jnp.cos(ang), -jnp.sin(ang), jnp.zeros((l, emb_pad - HY_EMB_DIM), F32)], axis=-1)
    w1 = jnp.concatenate([fw1.astype(F32), jnp.zeros((emb_pad - HY_EMB_DIM, fw1.shape[1]), F32)], axis=0)
    deltas = jnp.abs(jnp.linspace(HY_MIN_DECAY, HY_MAX_DECAY, d, dtype=F32)).reshape(1, d)
    hid = fw2.shape[0]
    rows = min(FILTER_ROWS, l)
    hidden = pl.pallas_call(
        _filter_hidden_kernel,
        grid=(l // rows,),
        in_specs=[pl.BlockSpec((rows, emb_pad), lambda r: (r, 0)), _resident((emb_pad, hid)), _resident((1, hid)),
                  _resident((hid, hid)), _resident((1, hid)), _resident((2, hid))],
        out_specs=pl.BlockSpec((rows, hid), lambda r: (r, 0)),
        out_shape=jax.ShapeDtypeStruct((l, hid), F32),
        compiler_params=_params(("parallel",)),
        name="hyena_filter_hidden",
    )(z, w1, fb1.astype(F32).reshape(1, hid), fw2.astype(F32), fb2.astype(F32).reshape(1, hid), ffreq.astype(F32))
    nt = d // tn
    w3 = fw3.astype(F32)
    return pl.pallas_call(
        functools.partial(_filter_out_kernel, rows=rows),
        grid=(HY_ORDER, nt),
        in_specs=[_resident((l, hid)), _resident((l, 1)),
                  pl.BlockSpec((hid, tn), lambda o, j: (0, (2 * o) * nt + j)),
                  pl.BlockSpec((hid, tn), lambda o, j: (0, (2 * o + 1) * nt + j)),
                  pl.BlockSpec((1, tn), lambda o, j: (0, j))],
        out_specs=pl.BlockSpec((1, 2, l, tn), lambda o, j: (o, 0, 0, j)),
        out_shape=jax.ShapeDtypeStruct((HY_ORDER, 2, l, d), F32),
        compiler_params=_params(("parallel", "parallel")),
        name="hyena_filters",
    )(hidden, t.reshape(l, 1), w3, w3, deltas)


def _filter_spectrum_kernel(hf_ref, hb_ref, f1_ref, f2_ref, o_ref, s_ref, *, p):
    na, nb = p["na"], p["nb"]
    half = na // 2

    def gather(tb):
        fwd = hf_ref[pl.ds(tb, half, stride=nb), :]
        bwd = hb_ref[pl.ds(jnp.where(tb == 0, 0, nb - tb), half, stride=nb), :]
        return jnp.concatenate([fwd, bwd], axis=0).astype(BF16)

    _fft_stage1(gather, f1_ref, s_ref, p)

    def body(i, carry):
        spec = jnp.dot(f2_ref[...], _slab_pair(s_ref, i, p).astype(BF16), preferred_element_type=F32)
        r = pl.multiple_of(i * 4 * nb, SUBLANES)
        o_ref[pl.ds(r, 2 * nb), :] = spec[:, :LANES].astype(BF16)
        o_ref[pl.ds(r + 2 * nb, 2 * nb), :] = spec[:, LANES:].astype(BF16)
        return carry

    lax.fori_loop(0, p["n_pairs"], body, 0, unroll=p["pair_unroll"])


def hyena_filter_spectrum(filt, tables):
    p, _, f1k, f2, _, _ = tables
    order, _, l, d = filt.shape
    rows = 2 * p["n_pairs"] * 2 * p["nb"]
    return pl.pallas_call(
        functools.partial(_filter_spectrum_kernel, p=p),
        grid=(order, d // LANES),
        in_specs=[pl.BlockSpec((None, None, l, LANES), lambda o, j: (o, 0, 0, j)),
                  pl.BlockSpec((None, None, l, LANES), lambda o, j: (o, 1, 0, j)),
                  _resident(f1k.shape), _resident(f2.shape)],
        out_specs=pl.BlockSpec((None, rows, LANES), lambda o, j: (o, 0, j)),
        out_shape=jax.ShapeDtypeStruct((order, rows, d), BF16),
        scratch_shapes=[pltpu.VMEM((p["nsp"] * p["pitch"], LANES), F32)],
        compiler_params=_params(("parallel", "parallel")),
        name="hyena_filter_spectrum",
    )(filt, filt, jnp.asarray(f1k).astype(BF16), jnp.asarray(f2).astype(BF16))


def _longconv_kernel(a_ref, m_ref, k_ref, bias_ref, f1_ref, f2_ref, g2_ref, g1_ref, o_ref, s_ref, y_ref, *, p):
    na, nb, nsp, pitch = p["na"], p["nb"], p["nsp"], p["pitch"]
    half = na // 2

    _fft_stage1(lambda tb: a_ref[pl.ds(tb, half, stride=nb), :].astype(BF16), f1_ref, s_ref, p)

    def mid(i, carry):
        x = jnp.dot(f2_ref[...], _slab_pair(s_ref, i, p).astype(BF16), preferred_element_type=F32)
        r = pl.multiple_of(i * 4 * nb, SUBLANES)
        kk = jnp.concatenate([k_ref[pl.ds(r, 2 * nb), :], k_ref[pl.ds(r + 2 * nb, 2 * nb), :]], axis=1).astype(F32)
        xr, xi, kr, ki = x[:nb], x[nb:], kk[:nb], kk[nb:]
        y = jnp.concatenate([xr * kr - xi * ki, xr * ki + xi * kr], axis=0).astype(BF16)
        bq = jnp.dot(g2_ref[...], y, preferred_element_type=F32)
        r0 = pl.multiple_of(2 * i * pitch, SUBLANES)
        r1 = pl.multiple_of(2 * i * pitch + pitch, SUBLANES)
        s_ref[pl.ds(r0, 2 * nb), :] = bq[:, :LANES]
        s_ref[pl.ds(r1, 2 * nb), :] = bq[:, LANES:]
        return carry

    lax.fori_loop(0, p["n_pairs"], mid, 0, unroll=p["pair_unroll"])

    def last(tb, carry):
        bq = jnp.concatenate([s_ref[pl.ds(tb, nsp, stride=pitch), :], s_ref[pl.ds(nb + tb, nsp, stride=pitch), :]], axis=0)
        y_ref[pl.ds(tb, half, stride=nb + SLAB_PAD), :] = jnp.dot(g1_ref[tb], bq.astype(BF16), preferred_element_type=F32)
        return carry

    lax.fori_loop(0, nb, last, 0, unroll=p["tb_unroll"])

    def finish(ta, carry):
        r = pl.multiple_of(ta * nb, SUBLANES)
        conv = y_ref[pl.ds(pl.multiple_of(ta * (nb + SLAB_PAD), SUBLANES), nb), :]
        o_ref[pl.ds(r, nb), :] = (m_ref[pl.ds(r, nb), :] * (conv + bias_ref[...] * a_ref[pl.ds(r, nb), :])).astype(o_ref.dtype)
        return carry

    lax.fori_loop(0, half, finish, 0, unroll=min(half, 8))


def hyena_longconv(a, a_col, m, m_col, kspec, bias, tables, out_dtype=F32):
    p, f1, _, f2, g2, g1 = tables
    b, l, _ = a.shape
    d = D_MODEL
    nt = d // LANES
    rows = kspec.shape[0]
    tabs = [jnp.asarray(t).astype(BF16) for t in (f1, f2, g2, g1)]
    return pl.pallas_call(
        functools.partial(_longconv_kernel, p=p),
        grid=(nt, b),
        in_specs=[pl.BlockSpec((None, l, LANES), lambda j, i: (i, 0, a_col * nt + j)),
                  pl.BlockSpec((None, l, LANES), lambda j, i: (i, 0, m_col * nt + j)),
                  pl.BlockSpec((rows, LANES), lambda j, i: (0, j)),
                  pl.BlockSpec((1, LANES), lambda j, i: (0, j))] + [_resident(t.shape) for t in tabs],
        out_specs=pl.BlockSpec((None, l, LANES), lambda j, i: (i, 0, j)),
        out_shape=jax.ShapeDtypeStruct((b, l, d), out_dtype),
        scratch_shapes=[pltpu.VMEM((p["nsp"] * p["pitch"], LANES), F32),
                        pltpu.VMEM((p["na"] // 2 * (p["nb"] + SLAB_PAD), LANES), F32)],
        compiler_params=_params(("parallel", "parallel")),
        name="hyena_longconv",
    )(a, m, kspec, bias.astype(F32).reshape(1, d), *tabs)


def hyena_core(pc, fw1, fb1, fw2, fb2, fw3, ffreq, hbias):
    l = pc.shape[1]
    tables = _fft_tables(l)
    filt = hyena_filters(l, fw1, fb1, fw2, fb2, fw3, ffreq)
    kspec = hyena_filter_spectrum(filt, tables)
    z = hyena_longconv(pc, 0, pc, 1, kspec[0], hbias[0], tables)
    return hyena_longconv(z, 0, pc, 2, kspec[1], hbias[1], tables, out_dtype=BF16)


def kernel(x, c, ctx, c_ctx, ada_w, ada_b, norm_g, ffn_w_gate, ffn_w_up, ffn_w_down, ssd_w_in, ssd_conv_w, ssd_conv_b, ssd_a_log, ssd_dt_bias, ssd_d, ssd_norm_g, ssd_w_out, pool_w, pool_b, pool_scale, hy_w_in, hy_conv_w, hy_conv_b, hy_filt_w1, hy_filt_b1, hy_filt_w2, hy_filt_b2, hy_filt_w3, hy_filt_freq, hy_bias, hy_w_out, final_g):
    batch = x.shape[0]
    d = D_MODEL
    h = ctx

    s = jnp.concatenate([jax.nn.silu(c), jax.nn.silu(c_ctx)[None], jnp.zeros((7 - batch, d), F32)], axis=0)
    mods = ada_modulation(s, ada_w, ada_b).reshape(DEPTH, 8, N_MOD, d)

    wg_bf, wu_bf, wd_bf = ffn_w_gate.astype(BF16), ffn_w_up.astype(BF16), ffn_w_down.astype(BF16)
    fg = final_g.reshape(1, d)

    for i in range(DEPTH):
        kind = i % N_MIXERS
        j = i // N_MIXERS
        last = i == DEPTH - 1
        ctx_in_needed = (not last) or kind == 0
        m = [mods[i, :batch, k].reshape(batch, 1, 1, d) for k in range(N_MOD)]
        mc = [jnp.broadcast_to(mods[i, batch, k].reshape(1, 1, 1, d), (batch, 1, 1, d)) for k in range(N_MOD)]
        g0, g1, g2 = (norm_g[i, k].reshape(1, d) for k in range(3))

        x = ffn_step(x, m[0], m[1], m[2], g0, wg_bf[i, 0], wu_bf[i, 0], wd_bf[i, 0])
        if ctx_in_needed:
            h = ffn_step(h, mc[0], mc[1], mc[2], g0, wg_bf[i, 0], wu_bf[i, 0], wd_bf[i, 0])

        if kind == 0:
            x, h_new = ssd_mixer(x, h, (m[3], m[4], m[5]), (mc[3], mc[4], mc[5]), g1, ssd_w_in[j], ssd_conv_w[j],
                                 ssd_conv_b[j], ssd_a_log[j], ssd_dt_bias[j], ssd_d[j], ssd_norm_g[j], ssd_w_out[j], not last)
            if not last:
                h = h_new
        elif kind == 1:
            x = pool_mixer(x, m[3], m[4], m[5], g1, pool_w[j], pool_b[j].reshape(-1), pool_scale[j], GRID_W)
            if not last:
                h = pool_mixer(h, mc[3], mc[4], mc[5], g1, pool_w[j], pool_b[j].reshape(-1), pool_scale[j], h.shape[1])
        else:
            w_in = hy_w_in[j].astype(BF16)
            w_out = hy_w_out[j].astype(BF16)
            filt = (hy_filt_w1[j], hy_filt_b1[j], hy_filt_w2[j], hy_filt_b2[j], hy_filt_w3[j], hy_filt_freq[j])
            conv = {0: (hy_conv_w[j], hy_conv_b[j], False)}
            (p_lat,) = modulate_project(x, m[3], m[4], g1, [w_in], tm=1024, conv=conv)
            x = project_residual(x, hyena_core(p_lat, *filt, hy_bias[j]), m[5], w_out)
            if not last:
                (p_ctx,) = modulate_project(h, mc[3], mc[4], g1, [w_in], conv=conv)
                h = project_residual(h, hyena_core(p_ctx, *filt, hy_bias[j]), mc[5], w_out)

        x = ffn_step(x, m[6], m[7], m[8], g2, wg_bf[i, 1], wu_bf[i, 1], wd_bf[i, 1],
                     final_g=fg if last else None)
        if not last:
            h = ffn_step(h, mc[6], mc[7], mc[8], g2, wg_bf[i, 1], wu_bf[i, 1], wd_bf[i, 1])
    return x
```

```python
import functools
import math

import jax
import jax.numpy as jnp
from jax import lax
from jax.experimental import pallas as pl
from jax.experimental.pallas import tpu as pltpu

F32 = jnp.float32
BF16 = jnp.bfloat16
HIGHEST = lax.Precision.HIGHEST

D_MODEL = 1024
DEPTH = 4
GRID_W = 64
N_MIXERS = 3
D_FF = 2816
N_MOD = 9
EPS = 1e-6
LOG2_E = 1.4426950408889634

SSD_D_INNER = 2 * D_MODEL
SSD_HEAD_DIM = 64
SSD_N_HEADS = SSD_D_INNER // SSD_HEAD_DIM
SSD_N_GROUPS = 4
SSD_HEADS_PER_GROUP = SSD_N_HEADS // SSD_N_GROUPS
SSD_D_STATE = 128
SSD_CHUNK = 128
SSD_GN = SSD_N_GROUPS * SSD_D_STATE
SSD_CONV_DIM = SSD_D_INNER + 2 * SSD_GN
SSD_GROUP_WIDTH = SSD_D_INNER // SSD_N_GROUPS

POOL_WINDOWS = (2, 4, 8, 16)
POOL_GROUPS = 4
POOL_GROUP_DIM = D_MODEL // POOL_GROUPS

HY_ORDER = 2
HY_EMB_DIM = 33
HY_BANDS = (HY_EMB_DIM - 1) // 2
HY_MAX_DECAY = math.log(1e-2) / 0.3
HY_MIN_DECAY = math.log(1e-2) / 1.5

VMEM_LIMIT_BYTES = 56 * 1024 * 1024
SUBLANES = 8
LANES = 128


def _params(sem):
    return pltpu.CompilerParams(dimension_semantics=sem, vmem_limit_bytes=VMEM_LIMIT_BYTES)


def _resident(shape):
    nd = len(shape)
    return pl.BlockSpec(shape, lambda *_: (0,) * nd, pipeline_mode=pl.Buffered(1))


def _mod_spec(d, _unused=0):
    return pl.BlockSpec((1, 1, 1, d), lambda i, j: (i, 0, 0, 0))


def _modulated(x, g, shift, scale):
    ms = jnp.mean(x * x, axis=-1, keepdims=True)
    return (x * lax.rsqrt(ms + EPS)) * (g * (1.0 + scale)) + shift


def _silu(v):
    return v * jax.nn.sigmoid(v)


def _ada_kernel(s_ref, w_ref, b_ref, o_ref):
    o_ref[0] = jnp.dot(s_ref[...], w_ref[0], preferred_element_type=F32, precision=HIGHEST) + b_ref[0]


def ada_modulation(s, ada_w, ada_b, tn=2304):
    depth, d, n = ada_w.shape
    r = s.shape[0]
    return pl.pallas_call(
        _ada_kernel,
        grid=(depth, n // tn),
        in_specs=[pl.BlockSpec((r, d), lambda i, j: (0, 0)),
                  pl.BlockSpec((1, d, tn), lambda i, j: (i, 0, j)),
                  pl.BlockSpec((1, 1, tn), lambda i, j: (i, 0, j))],
        out_specs=pl.BlockSpec((1, r, tn), lambda i, j: (i, 0, j)),
        out_shape=jax.ShapeDtypeStruct((depth, r, n), F32),
        compiler_params=_params(("parallel", "parallel")),
        name="ada_modulation",
    )(s, ada_w, ada_b.reshape(depth, 1, n))


def _ffn_kernel(x_ref, sh_ref, sc_ref, gt_ref, g_ref, wg_ref, wu_ref, wd_ref, *rest, f_chunk, final):
    if final:
        fg_ref, o_ref = rest
    else:
        (o_ref,) = rest
    x = x_ref[0]
    u = _modulated(x, g_ref[...], sh_ref[0, 0], sc_ref[0, 0]).astype(BF16)
    acc = jnp.zeros(x.shape, F32)
    d_ff = wg_ref.shape[1]
    for f0 in range(0, d_ff, f_chunk):
        a = jnp.dot(u, wg_ref[:, f0:f0 + f_chunk], preferred_element_type=F32)
        b = jnp.dot(u, wu_ref[:, f0:f0 + f_chunk], preferred_element_type=F32)
        h = (_silu(a) * b).astype(BF16)
        acc = acc + jnp.dot(h, wd_ref[f0:f0 + f_chunk, :], preferred_element_type=F32)
    y = x + (0.5 * gt_ref[0, 0]) * acc
    if final:
        ms = jnp.mean(y * y, axis=-1, keepdims=True)
        y = y * lax.rsqrt(ms + EPS) * fg_ref[...]
    o_ref[0] = y


def ffn_step(x, shift, scale, gate, g, wg, wu, wd, final_g=None, tm=512, f_chunk=256):
    b, l, d = x.shape
    tm = min(tm, l)
    f = wg.shape[1]
    final = final_g is not None
    mod_spec = _mod_spec(d, 0)
    in_specs = [pl.BlockSpec((1, tm, d), lambda i, j: (i, j, 0)),
                mod_spec, mod_spec, mod_spec,
                _resident((1, d)), _resident((d, f)), _resident((d, f)), _resident((f, d))]
    args = [x, shift, scale, gate, g, wg, wu, wd]
    if final:
        in_specs.append(_resident((1, d)))
        args.append(final_g)
    return pl.pallas_call(
        functools.partial(_ffn_kernel, f_chunk=f_chunk, final=final),
        grid=(b, l // tm),
        in_specs=in_specs,
        out_specs=pl.BlockSpec((1, tm, d), lambda i, j: (i, j, 0)),
        out_shape=jax.ShapeDtypeStruct((b, l, d), F32),
        compiler_params=_params(("parallel", "parallel")),
        name="ffn_step",
    )(*args)


CONV_COLS = 512


def _modproj_kernel(*refs, n_w, n_dest, conv, n_tiles, splits):
    if conv:
        prev_ref, x_ref, next_ref, sh_ref, sc_ref, g_ref, *rest = refs
    else:
        x_ref, sh_ref, sc_ref, g_ref, *rest = refs
    w_refs = rest[:n_w]
    cw_refs = rest[n_w:n_w + 2 * len(conv)]
    o_refs = rest[n_w + 2 * len(conv) + n_dest:]
    g, sh, sc = g_ref[...], sh_ref[0, 0], sc_ref[0, 0]
    u = _modulated(x_ref[0], g, sh, sc)
    if n_w == 0:
        o_refs[0][0] = u
        return
    u = u.astype(BF16)
    tm = x_ref.shape[1]
    if conv:
        ue = _modulated(jnp.concatenate([prev_ref[0], x_ref[0], next_ref[0]], axis=0), g, sh, sc).astype(BF16)
        j = pl.program_id(1)
        top = (j > 0).astype(F32)
        bot = (j < n_tiles - 1).astype(F32)
    views = ([(w_refs[0], lo, n_cols) for lo, n_cols in splits] if splits
             else [(w_ref, 0, w_ref.shape[1]) for w_ref in w_refs])
    for idx, ((w_ref, lo, n_cols), o_ref) in enumerate(zip(views, o_refs)):
        if idx not in conv:
            o_ref[0] = jnp.dot(u, w_ref[:, lo:lo + n_cols], preferred_element_type=F32).astype(o_ref.dtype)
            continue
        k_w, silu = conv[idx]
        cw_ref, cb_ref = cw_refs[2 * list(conv).index(idx)], cw_refs[2 * list(conv).index(idx) + 1]
        p = k_w // 2
        rows = tm + 2 * SUBLANES
        for c0 in range(0, n_cols, CONV_COLS):
            ce = jnp.dot(ue, w_ref[:, lo + c0:lo + c0 + CONV_COLS], preferred_element_type=F32)
            ce = jnp.concatenate([ce[:SUBLANES] * top, ce[SUBLANES:SUBLANES + tm], ce[SUBLANES + tm:] * bot], axis=0)
            acc = jnp.broadcast_to(cb_ref[:, c0:c0 + CONV_COLS], (tm, CONV_COLS))
            for k in range(k_w):
                tap = ce if k == p else pltpu.roll(ce, (p - k) % rows, axis=0)
                acc = acc + cw_ref[k:k + 1, c0:c0 + CONV_COLS] * tap[SUBLANES:SUBLANES + tm, :]
            o_ref[0, :, c0:c0 + CONV_COLS] = (_silu(acc) if silu else acc).astype(o_ref.dtype)


def modulate_project(x, shift, scale, g, weights, tm=256, out_dtypes=None, out_rows=None, row_block=0, dest=None,
                     conv=None, splits=None):
    b, l, d = x.shape
    tm = min(tm, l)
    conv = dict(conv or {})
    mod_spec = _mod_spec(d)
    x_spec = pl.BlockSpec((1, tm, d), lambda i, j: (i, j, 0))
    args = [x]
    in_specs = [x_spec]
    if conv:
        rb, last = tm // SUBLANES, l // SUBLANES - 1
        in_specs = [pl.BlockSpec((1, SUBLANES, d), lambda i, j: (i, jnp.maximum(j * rb - 1, 0), 0)), x_spec,
                    pl.BlockSpec((1, SUBLANES, d), lambda i, j: (i, jnp.minimum((j + 1) * rb, last), 0))]
        args = [x, x, x]
    in_specs += [mod_spec, mod_spec, _resident((1, d))] + [_resident(w.shape) for w in weights]
    args += [shift, scale, g, *weights]
    for i in conv:
        cw, cb, _ = conv[i]
        in_specs += [_resident(cw.shape), _resident((1, cb.shape[0]))]
        args += [cw.astype(F32), cb.astype(F32).reshape(1, -1)]
    widths = [n_cols for _, n_cols in splits] if splits else ([w.shape[1] for w in weights] or [d])
    assert all(widths[i] % CONV_COLS == 0 for i in conv) and (not splits or len(weights) == 1)
    out_dtypes = out_dtypes or [F32] * len(widths)
    dest = list(dest or [])
    in_specs += [pl.BlockSpec(memory_space=pl.ANY)] * len(dest)
    n_in = len(args)
    return pl.pallas_call(
        functools.partial(_modproj_kernel, n_w=len(weights), n_dest=len(dest),
                          conv={i: (c[0].shape[0], c[2]) for i, c in conv.items()}, n_tiles=l // tm,
                          splits=tuple(splits) if splits else None),
        grid=(b, l // tm),
        in_specs=in_specs,
        out_specs=[pl.BlockSpec((1, tm, n), lambda i, j: (i, j + row_block, 0)) for n in widths],
        out_shape=[jax.ShapeDtypeStruct((b, out_rows or l, n), dt) for n, dt in zip(widths, out_dtypes)],
        input_output_aliases={n_in + k: k for k in range(len(dest))},
        compiler_params=_params(("parallel", "parallel")),
        name="modulate_project",
    )(*args, *dest)


def _outproj_kernel(x_ref, a_ref, gt_ref, w_ref, o_ref):
    y = jnp.dot(a_ref[0].astype(BF16), w_ref[...], preferred_element_type=F32)
    o_ref[0] = x_ref[0] + gt_ref[0, 0] * y


def project_residual(x, a, gate, w, tm=512):
    b, l, d = x.shape
    k = a.shape[-1]
    tm = min(tm, l)
    return pl.pallas_call(
        _outproj_kernel,
        grid=(b, l // tm),
        in_specs=[pl.BlockSpec((1, tm, d), lambda i, j: (i, j, 0)),
                  pl.BlockSpec((1, tm, k), lambda i, j: (i, j, 0)),
                  _mod_spec(d, 0),
                  _resident((k, d))],
        out_specs=pl.BlockSpec((1, tm, d), lambda i, j: (i, j, 0)),
        out_shape=jax.ShapeDtypeStruct((b, l, d), F32),
        compiler_params=_params(("parallel", "parallel")),
        name="project_residual",
    )(x, a, gate, w)


def _bf16_terms(v):
    hi = v.astype(BF16)
    r = v - hi.astype(F32)
    mid = r.astype(BF16)
    return hi, mid, (r - mid.astype(F32)).astype(BF16)


def _dot_f32_rhs(m, v):
    return sum(jnp.dot(m, t, preferred_element_type=F32) for t in _bf16_terms(v))


def _dot_f32_lhs(v, m):
    return sum(jnp.dot(t, m, preferred_element_type=F32) for t in _bf16_terms(v))


def _dot_split(a, b):
    a_hi, a_lo, _ = _bf16_terms(a)
    b_hi, b_lo, _ = _bf16_terms(b)
    return (jnp.dot(a_hi, b_hi, preferred_element_type=F32) + jnp.dot(a_lo, b_hi, preferred_element_type=F32)
            + jnp.dot(a_hi, b_lo, preferred_element_type=F32))


def _ssd_scan_kernel(xf_ref, dtf_ref, dttf_ref, xb_ref, dtb_in_ref, dttb_ref, dtb_ref, dtbt_ref, a_ref, at_ref, ex_ref,
                     of_ref, ob_ref, h_ref):
    @pl.when(pl.program_id(1) == 0)
    def _():
        h_ref[...] = jnp.zeros(h_ref.shape, F32)

    consts = (dtb_ref, dtbt_ref, a_ref, at_ref, ex_ref)
    _ssd_chunk(xf_ref, dtf_ref, dttf_ref, *consts, of_ref, h_ref.at[0], True)
    _ssd_chunk(xb_ref, dtb_in_ref, dttb_ref, *consts, ob_ref, h_ref.at[1], False)


def _ssd_chunk(xbc_ref, dt_ref, dtt_ref, dtb_ref, dtbt_ref, a_ref, at_ref, ex_ref, o_ref, h_ref, fwd):
    nh, q, hp = SSD_N_HEADS, SSD_CHUNK, SSD_HEAD_DIM
    heads = slice(0, nh) if fwd else slice(nh, 2 * nh)
    dt = jax.nn.softplus(dt_ref[0][:, heads] + dtb_ref[:, heads])
    a = dt * a_ref[:, heads]
    dt_t = jax.nn.softplus(dtt_ref[0][heads] + dtbt_ref[heads])
    a_t = dt_t * at_ref[heads]

    r_i = lax.broadcasted_iota(jnp.int32, (q, q), 0)
    c_i = lax.broadcasted_iota(jnp.int32, (q, q), 1)
    seen = (c_i <= r_i) if fwd else (c_i >= r_i)
    seen_t = (r_i <= c_i) if fwd else (r_i >= c_i)
    cum = _dot_f32_rhs(seen.astype(BF16), a)
    cum_t = _dot_f32_lhs(a_t, seen_t.astype(BF16))
    total = cum[q - 1:q, :] if fwd else cum[0:1, :]
    ecum = jnp.exp(cum).astype(BF16)
    dtdec = (dt * jnp.exp(total - cum)).astype(BF16)
    etot = jnp.exp(total)
    cum2 = cum * LOG2_E
    row2 = cum_t * LOG2_E - jnp.log2(dt_t)

    pw = 2 * hp
    left = lax.broadcasted_iota(jnp.int32, (q, pw), 1) < hp
    left_row = lax.broadcasted_iota(jnp.int32, (1, pw), 1) < hp
    pairs = SSD_HEADS_PER_GROUP // 2
    for g in range(SSD_N_GROUPS):
        b_g = xbc_ref[0, :, SSD_D_INNER + g * SSD_D_STATE:SSD_D_INNER + (g + 1) * SSD_D_STATE].astype(BF16)
        c_lo = SSD_D_INNER + SSD_GN + g * SSD_D_STATE
        c_g = xbc_ref[0, :, c_lo:c_lo + SSD_D_STATE].astype(BF16)
        cb = lax.dot_general(c_g, b_g, (((1,), (1,)), ((), ())), preferred_element_type=F32)
        h_g = h_ref[g]
        ex_g = ex_ref[:, g * SSD_GROUP_WIDTH:(g + 1) * SSD_GROUP_WIDTH]
        ch = (jnp.dot(c_g, h_g.astype(BF16), preferred_element_type=F32)
              * jnp.dot(ecum, ex_g, preferred_element_type=F32))
        dtdec_g = jnp.dot(dtdec, ex_g, preferred_element_type=F32)
        xdecs, etots = [], []
        for pr in range(pairs):
            e1 = g * SSD_HEADS_PER_GROUP + 2 * pr
            e2 = e1 + 1
            x_pair = xbc_ref[0, :, e1 * hp:e1 * hp + pw].astype(F32)
            ws = []
            for e in (e1, e2):
                seg2 = cum2[:, e:e + 1] - row2[e:e + 1, :]
                ws.append((cb * jnp.exp2(jnp.where(seen, seg2, -jnp.inf))).astype(BF16))
            rhs = jnp.concatenate([jnp.where(left, x_pair, 0.0), jnp.where(left, 0.0, x_pair)], axis=0).astype(BF16)
            y_diag = jnp.dot(jnp.concatenate(ws, axis=1), rhs, preferred_element_type=F32)
            y_off = ch[:, pr * pw:(pr + 1) * pw]
            o_ref[0, 0, :, e1 * hp:e1 * hp + pw] = (y_diag + y_off).astype(o_ref.dtype)
            xdecs.append((x_pair * dtdec_g[:, pr * pw:(pr + 1) * pw]).astype(BF16))
            etots.append(jnp.where(left_row, etot[:, e1:e1 + 1], etot[:, e2:e2 + 1]))
        s_g = lax.dot_general(b_g, jnp.concatenate(xdecs, axis=1), (((0,), (0,)), ((), ())),
                              preferred_element_type=F32)
        h_ref[g] = h_g * jnp.concatenate(etots, axis=1) + s_g


def ssd_scan(xbc, dt_raw, dt_bias, a_log, n_lead_chunks):
    b, t, _ = xbc.shape
    q, nh = SSD_CHUNK, SSD_N_HEADS
    nc = t // q
    dt_t = jnp.swapaxes(dt_raw, 1, 2)
    a_neg = -jnp.exp(a_log.astype(F32)).reshape(1, 2 * nh)
    dtb = dt_bias.astype(F32).reshape(1, 2 * nh)
    expand = jnp.repeat(jnp.eye(nh, dtype=BF16), SSD_HEAD_DIM, axis=1)

    n_main = nc - n_lead_chunks

    def chunk_of(d, c):
        if d:
            return jnp.where(c < n_lead_chunks, nc - 1 - c, n_main - 1 - (c - n_lead_chunks))
        return jnp.where(c < n_lead_chunks, n_main + c, c - n_lead_chunks)

    def token_specs(d):
        return [pl.BlockSpec((1, q, SSD_CONV_DIM), lambda i, c: (i, chunk_of(d, c), 0)),
                pl.BlockSpec((1, q, 2 * nh), lambda i, c: (i, chunk_of(d, c), 0)),
                pl.BlockSpec((1, 2 * nh, q), lambda i, c: (i, 0, chunk_of(d, c)))]

    y_f, y_b = pl.pallas_call(
        _ssd_scan_kernel,
        grid=(b, nc),
        in_specs=token_specs(0) + token_specs(1) + [
            _resident((1, 2 * nh)), _resident((2 * nh, 1)),
            _resident((1, 2 * nh)), _resident((2 * nh, 1)), _resident((nh, SSD_D_INNER))],
        out_specs=[pl.BlockSpec((1, 1, q, SSD_D_INNER), lambda i, c, d=d: (i, 0, chunk_of(d, c), 0)) for d in (0, 1)],
        out_shape=[jax.ShapeDtypeStruct((b, 1, t, SSD_D_INNER), BF16)] * 2,
        scratch_shapes=[pltpu.VMEM((2, SSD_N_GROUPS, SSD_D_STATE, SSD_GROUP_WIDTH), F32)],
        compiler_params=_params(("parallel", "arbitrary")),
        name="ssd_scan",
    )(xbc, dt_raw, dt_t, xbc, dt_raw, dt_t, dtb, dtb.reshape(2 * nh, 1), a_neg, a_neg.reshape(2 * nh, 1), expand)
    return y_f, y_b


def _ssd_out_kernel(x_ref, y0_ref, y1_ref, xs0_ref, xs1_ref, z_ref, dsk_ref, ng_ref, gt_ref, w_ref, o_ref):
    gw = SSD_GROUP_WIDTH
    half_groups = SSD_N_GROUPS // 2
    parts = []
    for g in range(SSD_N_GROUPS):
        cols = slice(g * gw, (g + 1) * gw)
        xs_ref, lo = (xs0_ref, g * gw) if g < half_groups else (xs1_ref, (g - half_groups) * gw)
        y = (y0_ref[0, 0, :, cols].astype(F32) + y1_ref[0, 0, :, cols].astype(F32)
             + dsk_ref[:, cols] * xs_ref[0, :, lo:lo + gw].astype(F32))
        blk = y * _silu(z_ref[0, :, cols].astype(F32))
        blk = blk * lax.rsqrt(jnp.mean(blk * blk, axis=-1, keepdims=True) + EPS)
        parts.append((blk * ng_ref[:, cols]).astype(BF16))
    a = jnp.concatenate(parts, axis=1)
    o_ref[0] = x_ref[0] + gt_ref[0, 0] * jnp.dot(a, w_ref[...], preferred_element_type=F32)


def ssd_gate_project(x, y2, xbc, z, d_skip, norm_g, gate, w_out, src_block, tm=256):
    b, t, d = x.shape
    half = SSD_D_INNER // 2
    dsk = jnp.repeat(d_skip.astype(F32), SSD_HEAD_DIM).reshape(1, SSD_D_INNER)
    return pl.pallas_call(
        _ssd_out_kernel,
        grid=(b, t // tm),
        in_specs=[pl.BlockSpec((1, tm, d), lambda i, j: (i, j, 0)),
                  pl.BlockSpec((1, 1, tm, SSD_D_INNER), lambda i, j: (i, 0, j + src_block, 0)),
                  pl.BlockSpec((1, 1, tm, SSD_D_INNER), lambda i, j: (i, 0, j + src_block, 0)),
                  pl.BlockSpec((1, tm, half), lambda i, j: (i, j + src_block, 0)),
                  pl.BlockSpec((1, tm, half), lambda i, j: (i, j + src_block, 1)),
                  pl.BlockSpec((1, tm, SSD_D_INNER), lambda i, j: (i, j + src_block, 0)),
                  _resident((1, SSD_D_INNER)), _resident((1, SSD_D_INNER)),
                  _mod_spec(d),
                  _resident((SSD_D_INNER, d))],
        out_specs=pl.BlockSpec((1, tm, d), lambda i, j: (i, j, 0)),
        out_shape=jax.ShapeDtypeStruct((b, t, d), F32),
        compiler_params=_params(("parallel", "parallel")),
        name="ssd_gate_project",
    )(x, y2[0], y2[1], xbc, xbc, z, dsk, norm_g.astype(F32).reshape(1, SSD_D_INNER), gate, w_out)


def ssd_mixer(x, h, m_lat, m_ctx, g1, w_in, conv_w, conv_b, a_log, dt_bias, d_skip, norm_g, w_out, need_ctx_out,
              tm_proj=1024, tm_gate=512):
    lc, l = h.shape[1], x.shape[1]
    tm_proj, tm_gate = min(tm_proj, l), min(tm_gate, l)
    tm_ctx = min(tm_gate, lc)
    assert l % tm_proj == 0 and l % tm_gate == 0 and l % tm_ctx == 0
    ws = [w_in.astype(BF16)]
    cols = [(0, SSD_D_INNER), (SSD_D_INNER, SSD_CONV_DIM), (SSD_D_INNER + SSD_CONV_DIM, 2 * SSD_N_HEADS)]
    dts = [BF16, BF16, F32]
    conv = {1: (conv_w, conv_b, True)}
    parts = modulate_project(x, m_lat[0], m_lat[1], g1, ws, tm=tm_proj, out_dtypes=dts, out_rows=lc + l, conv=conv,
                             splits=cols)
    z, xbc, dt_raw = modulate_project(h, m_ctx[0], m_ctx[1], g1, ws, tm=tm_ctx, out_dtypes=dts, out_rows=lc + l,
                                      row_block=l // tm_ctx, dest=parts, conv=conv, splits=cols)
    y2 = ssd_scan(xbc, dt_raw, dt_bias, a_log, lc // SSD_CHUNK)
    w_out = w_out.astype(BF16)
    x_new = ssd_gate_project(x, y2, xbc, z, d_skip, norm_g, m_lat[2], w_out, 0, tm=tm_gate)
    h_new = (ssd_gate_project(h, y2, xbc, z, d_skip, norm_g, m_ctx[2], w_out, l // tm_ctx, tm=tm_ctx)
             if need_ctx_out else None)
    return x_new, h_new


POOL_CHUNK = 256
POOL_MAX_HALF = max(POOL_WINDOWS) // 2


def _pool_band_tables(width):
    import numpy as np
    pos = np.arange(POOL_CHUNK)
    line, col = pos // width, pos % width
    out = []
    for w in POOL_WINDOWS:
        inside = (col[None, :] >= col[:, None] - w // 2) & (col[None, :] <= col[:, None] + (w - w // 2) - 1)
        out.append((inside & (line[None, :] == line[:, None])).astype(np.float32))
    return np.stack(out)


def _pool_kernel(u_ref, band_ref, o_ref, cs_ref, *, width, n_lines):
    l, c = u_ref.shape
    n_chunks = l // POOL_CHUNK
    assert POOL_WINDOWS == tuple(2 << g for g in range(POOL_GROUPS)) and width & (width - 1) == 0
    half = lax.shift_left(jnp.int32(1), pl.program_id(1) // (POOL_GROUP_DIM // LANES))
    log_w = width.bit_length() - 1
    pad = POOL_MAX_HALF * width if n_lines > 1 else 0
    if pad:
        cs_ref[0:pad, :] = jnp.zeros((pad, c), F32)
        cs_ref[pad + l:2 * pad + l, :] = jnp.zeros((pad, c), F32)

    def col_pass(i, carry):
        r = pl.multiple_of(i * POOL_CHUNK, POOL_CHUNK)
        u = u_ref[pl.ds(r, POOL_CHUNK), :]
        hi = u.astype(BF16)
        lo = (u - hi.astype(F32)).astype(BF16)
        both = jnp.dot(band_ref[0], jnp.concatenate([hi, lo], axis=1), preferred_element_type=F32)
        cs_ref[pl.ds(pad + r, POOL_CHUNK), :] = both[:, :c] + both[:, c:]
        return carry

    lax.fori_loop(0, n_chunks, col_pass, 0)

    pos0 = lax.broadcasted_iota(jnp.int32, (POOL_CHUNK, c), 0)
    col = pos0 & (width - 1)
    inv_cnt = 1.0 / (jnp.minimum(col + half, width) - jnp.maximum(col - half, 0)).astype(F32)
    line0 = lax.shift_right_logical(pos0, log_w)

    def out_pass(i, carry):
        r = pl.multiple_of(i * POOL_CHUNK, POOL_CHUNK)
        if n_lines > 1:
            def add_line(k, acc):
                return acc + cs_ref[pl.ds(pl.multiple_of(pad + r + k * width, SUBLANES), POOL_CHUNK), :]
            s = lax.fori_loop(-half, half, add_line, jnp.zeros((POOL_CHUNK, c), F32))
            line = line0 + lax.shift_right_logical(r, log_w)
            cnt_l = jnp.minimum(line + half, n_lines) - jnp.maximum(line - half, 0)
            mean = s * inv_cnt / cnt_l.astype(F32)
        else:
            mean = cs_ref[pl.ds(r, POOL_CHUNK), :] * inv_cnt
        o_ref[pl.ds(r, POOL_CHUNK), :] = (mean - u_ref[pl.ds(r, POOL_CHUNK), :]).astype(o_ref.dtype)
        return carry

    lax.fori_loop(0, n_chunks, out_pass, 0)


def pool_tokens(u, width):
    b, l, d = u.shape
    n_lines = l // width
    bands = jnp.asarray(_pool_band_tables(width)).astype(BF16)
    tiles_per_group = POOL_GROUP_DIM // LANES
    pad = POOL_MAX_HALF * width if n_lines > 1 else 0
    return pl.pallas_call(
        functools.partial(_pool_kernel, width=width, n_lines=n_lines),
        grid=(b, d // LANES),
        in_specs=[pl.BlockSpec((None, l, LANES), lambda i, j: (i, 0, j)),
                  pl.BlockSpec((1, POOL_CHUNK, POOL_CHUNK), lambda i, j: (j // tiles_per_group, 0, 0))],
        out_specs=pl.BlockSpec((None, l, LANES), lambda i, j: (i, 0, j)),
        out_shape=jax.ShapeDtypeStruct((b, l, d), BF16),
        scratch_shapes=[pltpu.VMEM((l + 2 * pad, LANES), F32)],
        compiler_params=_params(("parallel", "parallel")),
        name="pool_tokens",
    )(u, bands)


def _pool_mix_kernel(x_ref, p_ref, w_ref, b_ref, sc_ref, gt_ref, o_ref):
    ys = [jnp.dot(p_ref[0, :, g * POOL_GROUP_DIM:(g + 1) * POOL_GROUP_DIM], w_ref[g], preferred_element_type=F32)
          for g in range(POOL_GROUPS)]
    y = (jnp.concatenate(ys, axis=1) + b_ref[...]) * sc_ref[...]
    o_ref[0] = x_ref[0] + gt_ref[0, 0] * y


def pool_mix_residual(x, pooled, w, bias, scale, gate, tm=512):
    b, l, d = x.shape
    tm = min(tm, l)
    return pl.pallas_call(
        _pool_mix_kernel,
        grid=(b, l // tm),
        in_specs=[pl.BlockSpec((1, tm, d), lambda i, j: (i, j, 0)),
                  pl.BlockSpec((1, tm, d), lambda i, j: (i, j, 0)),
                  _resident(w.shape), _resident((1, d)), _resident((1, d)), _mod_spec(d, 0)],
        out_specs=pl.BlockSpec((1, tm, d), lambda i, j: (i, j, 0)),
        out_shape=jax.ShapeDtypeStruct((b, l, d), F32),
        compiler_params=_params(("parallel", "parallel")),
        name="pool_mix_residual",
    )(x, pooled, w.astype(BF16), bias.astype(F32).reshape(1, d), scale.astype(F32).reshape(1, d), gate)


def pool_mixer(x, shift, scale_mod, gate, g1, w, bias, scale, width):
    (u,) = modulate_project(x, shift, scale_mod, g1, [])
    return pool_mix_residual(x, pool_tokens(u, width), w, bias, scale, gate)


SLAB_PAD = 8


def _round_up(v, m):
    return (v + m - 1) // m * m


def _fft_plan(l):
    n = 2 * l
    na = 1 << (n.bit_length() // 2)
    nb = n // na
    ns = na // 2 + 1
    n_pairs = (ns + 1) // 2
    return dict(l=l, n=n, na=na, nb=nb, ns=ns, nsp=_round_up(ns + 1, SUBLANES), pitch=2 * nb + SLAB_PAD,
                n_pairs=n_pairs, tb_unroll=min(nb, 16),
                pair_unroll=max(u for u in range(1, 12) if n_pairs % u == 0))


def _fft_tables(l):
    import numpy as np
    p = _fft_plan(l)
    n, na, nb, ns, nsp = p["n"], p["na"], p["nb"], p["ns"], p["nsp"]
    half = na // 2
    ka = np.arange(ns)[None, :, None]
    tb = np.arange(nb)[:, None, None]

    def stage1(ta):
        th = 2.0 * np.pi * (ta[None, None, :] * ka / na + tb * ka / n)
        m = np.zeros((nb, 2 * nsp, ta.shape[0]))
        m[:, :ns] = np.cos(th)
        m[:, nsp:nsp + ns] = -np.sin(th)
        return m

    f1 = stage1(np.arange(half))
    f1k = np.zeros((nb, 2 * nsp, na))
    f1k[:, :, :half] = f1
    f1k[1:, :, half:] = stage1(na - 1 - np.arange(half))[1:]
    tb0 = stage1(na - np.arange(half))[0]
    tb0[:, 0] = 0.0
    f1k[0, :, half:] = tb0

    k2 = np.arange(nb)
    ang = 2.0 * np.pi * np.outer(k2, k2) / nb
    c, s = np.cos(ang), np.sin(ang)
    f2 = np.block([[c, s], [-s, c]])
    g2 = np.block([[c, -s], [s, c]])

    ta = np.arange(half)[None, :, None]
    kk = np.arange(ns)[None, None, :]
    tbb = np.arange(nb)[:, None, None]
    ph = 2.0 * np.pi * (ta * kk / na + tbb * kk / n)
    wgt = np.where((kk == 0) | (kk == na // 2), 1.0, 2.0) / n
    g1 = np.zeros((nb, half, 2 * nsp))
    g1[:, :, :ns] = wgt * np.cos(ph)
    g1[:, :, nsp:nsp + ns] = -wgt * np.sin(ph)
    f32 = np.float32
    return p, f1.astype(f32), f1k.astype(f32), f2.astype(f32), g2.astype(f32), g1.astype(f32)


def _fft_stage1(gather, f1_ref, s_ref, p):
    nb, nsp, pitch = p["nb"], p["nsp"], p["pitch"]

    def body(tb, carry):
        a = jnp.dot(f1_ref[tb], gather(tb), preferred_element_type=F32)
        s_ref[pl.ds(tb, nsp, stride=pitch), :] = a[:nsp]
        s_ref[pl.ds(nb + tb, nsp, stride=pitch), :] = a[nsp:]
        return carry

    lax.fori_loop(0, nb, body, 0, unroll=p["tb_unroll"])


def _slab_pair(s_ref, i, p):
    nb, pitch = p["nb"], p["pitch"]
    r0 = pl.multiple_of(2 * i * pitch, SUBLANES)
    r1 = pl.multiple_of(2 * i * pitch + pitch, SUBLANES)
    return jnp.concatenate([s_ref[pl.ds(r0, 2 * nb), :], s_ref[pl.ds(r1, 2 * nb), :]], axis=1)


FILTER_ROWS = 512


def _filter_hidden_kernel(z_ref, w1_ref, b1_ref, w2_ref, b2_ref, fr_ref, o_ref):
    h = jnp.sin(fr_ref[0:1, :] * (jnp.dot(z_ref[...], w1_ref[...], preferred_element_type=F32, precision=HIGHEST)
                                  + b1_ref[...]))
    o_ref[...] = jnp.sin(fr_ref[1:2, :] * (jnp.dot(h, w2_ref[...], preferred_element_type=F32, precision=HIGHEST)
                                           + b2_ref[...]))


def _filter_out_kernel(h_ref, t_ref, w3f_ref, w3b_ref, dl_ref, o_ref, *, rows):
    l, tn = o_ref.shape[2], o_ref.shape[3]
    wf, wb, dl = w3f_ref[...], w3b_ref[...], dl_ref[...]

    def fill(i, ss):
        r = pl.multiple_of(i * rows, rows)
        h = h_ref[pl.ds(r, rows), :]
        decay = jnp.exp(-t_ref[pl.ds(r, rows), :] * dl)
        hf = _dot_split(h, wf) * decay
        hb = _dot_split(h, wb) * decay
        o_ref[0, 0, pl.ds(r, rows), :] = hf
        o_ref[0, 1, pl.ds(r, rows), :] = hb
        return ss + jnp.sum(hf * hf + hb * hb, axis=0, keepdims=True)

    ss = lax.fori_loop(0, l // rows, fill, jnp.zeros((1, tn), F32))
    scale = lax.rsqrt(ss + EPS)

    def rescale(i, carry):
        r = pl.multiple_of(i * rows, rows)
        o_ref[0, 0, pl.ds(r, rows), :] = o_ref[0, 0, pl.ds(r, rows), :] * scale
        o_ref[0, 1, pl.ds(r, rows), :] = o_ref[0, 1, pl.ds(r, rows), :] * scale
        return carry

    lax.fori_loop(0, l // rows, rescale, 0)


def hyena_filters(l, fw1, fb1, fw2, fb2, fw3, ffreq, tn=2 * LANES):
    d = D_MODEL
    pos = jnp.arange(l, dtype=F32)
    t = jnp.linspace(0.0, 1.0, l, dtype=F32)
    wpos = 2.0 * math.pi * pos / l
    f = jnp.linspace(1e-4, HY_BANDS - 1, HY_BANDS, dtype=F32)
    ang = wpos[:, None] * f[None, :]
    emb_pad = _round_up(HY_EMB_DIM, SUBLANES)
    z = jnp.concatenate([t[:, None], jnp.cos(ang), -jnp.sin(ang), jnp.zeros((l, emb_pad - HY_EMB_DIM), F32)], axis=-1)
    w1 = jnp.concatenate([fw1.astype(F32), jnp.zeros((emb_pad - HY_EMB_DIM, fw1.shape[1]), F32)], axis=0)
    deltas = jnp.abs(jnp.linspace(HY_MIN_DECAY, HY_MAX_DECAY, d, dtype=F32)).reshape(1, d)
    hid = fw2.shape[0]
    rows = min(FILTER_ROWS, l)
    hidden = pl.pallas_call(
        _filter_hidden_kernel,
        grid=(l // rows,),
        in_specs=[pl.BlockSpec((rows, emb_pad), lambda r: (r, 0)), _resident((emb_pad, hid)), _resident((1, hid)),
                  _resident((hid, hid)), _resident((1, hid)), _resident((2, hid))],
        out_specs=pl.BlockSpec((rows, hid), lambda r: (r, 0)),
        out_shape=jax.ShapeDtypeStruct((l, hid), F32),
        compiler_params=_params(("parallel",)),
        name="hyena_filter_hidden",
    )(z, w1, fb1.astype(F32).reshape(1, hid), fw2.astype(F32), fb2.astype(F32).reshape(1, hid), ffreq.astype(F32))
    nt = d // tn
    w3 = fw3.astype(F32)
    return pl.pallas_call(
        functools.partial(_filter_out_kernel, rows=rows),
        grid=(HY_ORDER, nt),
        in_specs=[_resident((l, hid)), _resident((l, 1)),
                  pl.BlockSpec((hid, tn), lambda o, j: (0, (2 * o) * nt + j)),
                  pl.BlockSpec((hid, tn), lambda o, j: (0, (2 * o + 1) * nt + j)),
                  pl.BlockSpec((1, tn), lambda o, j: (0, j))],
        out_specs=pl.BlockSpec((1, 2, l, tn), lambda o, j: (o, 0, 0, j)),
        out_shape=jax.ShapeDtypeStruct((HY_ORDER, 2, l, d), F32),
        compiler_params=_params(("parallel", "parallel")),
        name="hyena_filters",
    )(hidden, t.reshape(l, 1), w3, w3, deltas)


def _filter_spectrum_kernel(hf_ref, hb_ref, f1_ref, f2_ref, o_ref, s_ref, *, p):
    na, nb = p["na"], p["nb"]
    half = na // 2

    def gather(tb):
        fwd = hf_ref[pl.ds(tb, half, stride=nb), :]
        bwd = hb_ref[pl.ds(jnp.where(tb == 0, 0, nb - tb), half, stride=nb), :]
        return jnp.concatenate([fwd, bwd], axis=0).astype(BF16)

    _fft_stage1(gather, f1_ref, s_ref, p)

    def body(i, carry):
        spec = jnp.dot(f2_ref[...], _slab_pair(s_ref, i, p).astype(BF16), preferred_element_type=F32)
        r = pl.multiple_of(i * 4 * nb, SUBLANES)
        o_ref[pl.ds(r, 2 * nb), :] = spec[:, :LANES].astype(BF16)
        o_ref[pl.ds(r + 2 * nb, 2 * nb), :] = spec[:, LANES:].astype(BF16)
        return carry

    lax.fori_loop(0, p["n_pairs"], body, 0, unroll=p["pair_unroll"])


def hyena_filter_spectrum(filt, tables):
    p, _, f1k, f2, _, _ = tables
    order, _, l, d = filt.shape
    rows = 2 * p["n_pairs"] * 2 * p["nb"]
    return pl.pallas_call(
        functools.partial(_filter_spectrum_kernel, p=p),
        grid=(order, d // LANES),
        in_specs=[pl.BlockSpec((None, None, l, LANES), lambda o, j: (o, 0, 0, j)),
                  pl.BlockSpec((None, None, l, LANES), lambda o, j: (o, 1, 0, j)),
                  _resident(f1k.shape), _resident(f2.shape)],
        out_specs=pl.BlockSpec((None, rows, LANES), lambda o, j: (o, 0, j)),
        out_shape=jax.ShapeDtypeStruct((order, rows, d), BF16),
        scratch_shapes=[pltpu.VMEM((p["nsp"] * p["pitch"], LANES), F32)],
        compiler_params=_params(("parallel", "parallel")),
        name="hyena_filter_spectrum",
    )(filt, filt, jnp.asarray(f1k).astype(BF16), jnp.asarray(f2).astype(BF16))


def _longconv_kernel(a_ref, m_ref, k_ref, bias_ref, f1_ref, f2_ref, g2_ref, g1_ref, o_ref, s_ref, y_ref, *, p):
    na, nb, nsp, pitch = p["na"], p["nb"], p["nsp"], p["pitch"]
    half = na // 2

    _fft_stage1(lambda tb: a_ref[pl.ds(tb, half, stride=nb), :].astype(BF16), f1_ref, s_ref, p)

    def mid(i, carry):
        x = jnp.dot(f2_ref[...], _slab_pair(s_ref, i, p).astype(BF16), preferred_element_type=F32)
        r = pl.multiple_of(i * 4 * nb, SUBLANES)
        kk = jnp.concatenate([k_ref[pl.ds(r, 2 * nb), :], k_ref[pl.ds(r + 2 * nb, 2 * nb), :]], axis=1).astype(F32)
        xr, xi, kr, ki = x[:nb], x[nb:], kk[:nb], kk[nb:]
        y = jnp.concatenate([xr * kr - xi * ki, xr * ki + xi * kr], axis=0).astype(BF16)
        bq = jnp.dot(g2_ref[...], y, preferred_element_type=F32)
        r0 = pl.multiple_of(2 * i * pitch, SUBLANES)
        r1 = pl.multiple_of(2 * i * pitch + pitch, SUBLANES)
        s_ref[pl.ds(r0, 2 * nb), :] = bq[:, :LANES]
        s_ref[pl.ds(r1, 2 * nb), :] = bq[:, LANES:]
        return carry

    lax.fori_loop(0, p["n_pairs"], mid, 0, unroll=p["pair_unroll"])

    def last(tb, carry):
        bq = jnp.concatenate([s_ref[pl.ds(tb, nsp, stride=pitch), :], s_ref[pl.ds(nb + tb, nsp, stride=pitch), :]], axis=0)
        y_ref[pl.ds(tb, half, stride=nb + SLAB_PAD), :] = jnp.dot(g1_ref[tb], bq.astype(BF16), preferred_element_type=F32)
        return carry

    lax.fori_loop(0, nb, last, 0, unroll=p["tb_unroll"])

    def finish(ta, carry):
        r = pl.multiple_of(ta * nb, SUBLANES)
        conv = y_ref[pl.ds(pl.multiple_of(ta * (nb + SLAB_PAD), SUBLANES), nb), :]
        o_ref[pl.ds(r, nb), :] = (m_ref[pl.ds(r, nb), :] * (conv + bias_ref[...] * a_ref[pl.ds(r, nb), :])).astype(o_ref.dtype)
        return carry

    lax.fori_loop(0, half, finish, 0, unroll=min(half, 8))


def hyena_longconv(a, a_col, m, m_col, kspec, bias, tables, out_dtype=F32):
    p, f1, _, f2, g2, g1 = tables
    b, l, _ = a.shape
    d = D_MODEL
    nt = d // LANES
    rows = kspec.shape[0]
    tabs = [jnp.asarray(t).astype(BF16) for t in (f1, f2, g2, g1)]
    return pl.pallas_call(
        functools.partial(_longconv_kernel, p=p),
        grid=(nt, b),
        in_specs=[pl.BlockSpec((None, l, LANES), lambda j, i: (i, 0, a_col * nt + j)),
                  pl.BlockSpec((None, l, LANES), lambda j, i: (i, 0, m_col * nt + j)),
                  pl.BlockSpec((rows, LANES), lambda j, i: (0, j)),
                  pl.BlockSpec((1, LANES), lambda j, i: (0, j))] + [_resident(t.shape) for t in tabs],
        out_specs=pl.BlockSpec((None, l, LANES), lambda j, i: (i, 0, j)),
        out_shape=jax.ShapeDtypeStruct((b, l, d), out_dtype),
        scratch_shapes=[pltpu.VMEM((p["nsp"] * p["pitch"], LANES), F32),
                        pltpu.VMEM((p["na"] // 2 * (p["nb"] + SLAB_PAD), LANES), F32)],
        compiler_params=_params(("parallel", "parallel")),
        name="hyena_longconv",
    )(a, m, kspec, bias.astype(F32).reshape(1, d), *tabs)


def hyena_core(pc, fw1, fb1, fw2, fb2, fw3, ffreq, hbias):
    l = pc.shape[1]
    tables = _fft_tables(l)
    filt = hyena_filters(l, fw1, fb1, fw2, fb2, fw3, ffreq)
    kspec = hyena_filter_spectrum(filt, tables)
    z = hyena_longconv(pc, 0, pc, 1, kspec[0], hbias[0], tables)
    return hyena_longconv(z, 0, pc, 2, kspec[1], hbias[1], tables, out_dtype=BF16)


def kernel(x, c, ctx, c_ctx, ada_w, ada_b, norm_g, ffn_w_gate, ffn_w_up, ffn_w_down, ssd_w_in, ssd_conv_w, ssd_conv_b, ssd_a_log, ssd_dt_bias, ssd_d, ssd_norm_g, ssd_w_out, pool_w, pool_b, pool_scale, hy_w_in, hy_conv_w, hy_conv_b, hy_filt_w1, hy_filt_b1, hy_filt_w2, hy_filt_b2, hy_filt_w3, hy_filt_freq, hy_bias, hy_w_out, final_g):
    batch = x.shape[0]
    d = D_MODEL
    h = ctx

    s = jnp.concatenate([jax.nn.silu(c), jax.nn.silu(c_ctx)[None], jnp.zeros((7 - batch, d), F32)], axis=0)
    mods = ada_modulation(s, ada_w, ada_b).reshape(DEPTH, 8, N_MOD, d)

    wg_bf, wu_bf, wd_bf = ffn_w_gate.astype(BF16), ffn_w_up.astype(BF16), ffn_w_down.astype(BF16)
    fg = final_g.reshape(1, d)

    for i in range(DEPTH):
        kind = i % N_MIXERS
        j = i // N_MIXERS
        last = i == DEPTH - 1
        ctx_in_needed = (not last) or kind == 0
        m = [mods[i, :batch, k].reshape(batch, 1, 1, d) for k in range(N_MOD)]
        mc = [jnp.broadcast_to(mods[i, batch, k].reshape(1, 1, 1, d), (batch, 1, 1, d)) for k in range(N_MOD)]
        g0, g1, g2 = (norm_g[i, k].reshape(1, d) for k in range(3))

        x = ffn_step(x, m[0], m[1], m[2], g0, wg_bf[i, 0], wu_bf[i, 0], wd_bf[i, 0])
        if ctx_in_needed:
            h = ffn_step(h, mc[0], mc[1], mc[2], g0, wg_bf[i, 0], wu_bf[i, 0], wd_bf[i, 0])

        if kind == 0:
            x, h_new = ssd_mixer(x, h, (m[3], m[4], m[5]), (mc[3], mc[4], mc[5]), g1, ssd_w_in[j], ssd_conv_w[j],
                                 ssd_conv_b[j], ssd_a_log[j], ssd_dt_bias[j], ssd_d[j], ssd_norm_g[j], ssd_w_out[j], not last)
            if not last:
                h = h_new
        elif kind == 1:
            x = pool_mixer(x, m[3], m[4], m[5], g1, pool_w[j], pool_b[j].reshape(-1), pool_scale[j], GRID_W)
            if not last:
                h = pool_mixer(h, mc[3], mc[4], mc[5], g1, pool_w[j], pool_b[j].reshape(-1), pool_scale[j], h.shape[1])
        else:
            w_in = hy_w_in[j].astype(BF16)
            w_out = hy_w_out[j].astype(BF16)
            filt = (hy_filt_w1[j], hy_filt_b1[j], hy_filt_w2[j], hy_filt_b2[j], hy_filt_w3[j], hy_filt_freq[j])
            conv = {0: (hy_conv_w[j], hy_conv_b[j], False)}
            (p_lat,) = modulate_project(x, m[3], m[4], g1, [w_in], tm=1024, conv=conv)
            x = project_residual(x, hyena_core(p_lat, *filt, hy_bias[j]), m[5], w_out)
            if not last:
                (p_ctx,) = modulate_project(h, mc[3], mc[4], g1, [w_in], conv=conv)
                h = project_residual(h, hyena_core(p_ctx, *filt, hy_bias[j]), mc[5], w_out)

        x = ffn_step(x, m[6], m[7], m[8], g2, wg_bf[i, 1], wu_bf[i, 1], wd_bf[i, 1],
                     final_g=fg if last else None)
        if not last:
            h = ffn_step(h, mc[6], mc[7], mc[8], g2, wg_bf[i, 1], wu_bf[i, 1], wd_bf[i, 1])
    return x
```
